```python
import math
import jax, jax.numpy as jnp
from jax import lax
import numpy as np

D_MODEL = 1024
BATCH = 4
SEQ = 8192
DEPTH = 2

D_MIX = D_MODEL
D_POOL = D_MIX // 4
D_S5 = 3 * D_MIX // 8
D_LRU = D_MIX - D_POOL - D_S5
D_IN = D_POOL + D_S5 + 2 * D_LRU

POOL_WINDOWS = (2, 4, 8, 16)
POOL_GROUP = D_POOL // len(POOL_WINDOWS)

S5_GROUP = 16
S5_NGROUPS = D_S5 // S5_GROUP
S5_STATE = 64
S5_DT_MIN = 1e-3
S5_DT_MAX = 1e-1

LRU_HEADS = 6
LRU_HEAD_DIM = D_LRU // LRU_HEADS
CONV_WIDTH = 4
RG_C = 8.0

N_EXPERTS = 32
TOP_K = 4
D_FF = D_MODEL
SWIGLU_LIMIT = 7.0
SWIGLU_ALPHA = 1.702
MOE_BLOCK = 128
EPS = 1e-5

kernel_name = "hybrid_pool_s5_rglru_moe_encoder"


def rmsnorm(x, g):
    xf = x.astype(jnp.float32)
    y = xf * lax.rsqrt(jnp.mean(xf * xf, axis=-1, keepdims=True) + EPS)
    return (y * g.astype(jnp.float32)).astype(x.dtype)


def pool_mixer(u, pool_w, pool_scale):
    b, s, _ = u.shape
    uf = u.astype(jnp.float32)
    cs = jnp.concatenate([jnp.zeros((b, 1, D_POOL), jnp.float32), jnp.cumsum(uf, axis=1)], axis=1)
    t = jnp.arange(s)
    outs = []
    for g, w in enumerate(POOL_WINDOWS):
        half = w // 2
        lo = jnp.maximum(t - half, 0)
        hi = jnp.minimum(t + half, s)
        sl = slice(g * POOL_GROUP, (g + 1) * POOL_GROUP)
        csg = cs[..., sl]
        mean = (jnp.take(csg, hi, axis=1) - jnp.take(csg, lo, axis=1)) / (hi - lo).astype(jnp.float32)[None, :, None]
        d = (mean - uf[..., sl]).astype(u.dtype)
        outs.append(jnp.einsum('bsc,cd->bsd', d, pool_w[g]))
    return jnp.concatenate(outs, axis=-1) * pool_scale


def _cplx_lin_combine(left, right):
    a1r, a1i, b1r, b1i = left
    a2r, a2i, b2r, b2i = right
    return (a2r * a1r - a2i * a1i,
            a2r * a1i + a2i * a1r,
            a2r * b1r - a2i * b1i + b2r,
            a2r * b1i + a2i * b1r + b2i)


def _lin_combine(left, right):
    a1, b1 = left
    a2, b2 = right
    return (a2 * a1, a2 * b1 + b2)


def _s5_scan(ug, lam_re, lam_im, log_step, b_re, b_im, c_re, c_im, reverse):
    lr = lam_re.astype(jnp.float32)
    li = lam_im.astype(jnp.float32)
    dt = jnp.exp(log_step.astype(jnp.float32))[:, None]
    mag = jnp.exp(lr * dt)
    abar_re = mag * jnp.cos(li * dt)
    abar_im = mag * jnp.sin(li * dt)
    den = lr * lr + li * li
    nr = abar_re - 1.0
    ni = abar_im
    q_re = (nr * lr + ni * li) / den
    q_im = (ni * lr - nr * li) / den
    br = b_re.astype(jnp.float32)
    bi = b_im.astype(jnp.float32)
    bb_re = q_re[..., None] * br - q_im[..., None] * bi
    bb_im = q_re[..., None] * bi + q_im[..., None] * br
    bu_re = jnp.einsum('bsgc,gpc->bsgp', ug, bb_re)
    bu_im = jnp.einsum('bsgc,gpc->bsgp', ug, bb_im)
    a_re = jnp.broadcast_to(abar_re, bu_re.shape)
    a_im = jnp.broadcast_to(abar_im, bu_re.shape)
    _, _, xr, xi = lax.associative_scan(_cplx_lin_combine, (a_re, a_im, bu_re, bu_im), axis=1, reverse=reverse)
    return (jnp.einsum('bsgp,gcp->bsgc', xr, c_re.astype(jnp.float32))
            - jnp.einsum('bsgp,gcp->bsgc', xi, c_im.astype(jnp.float32)))


def s5_mixer(u, lam_re, lam_im, log_step, b_re, b_im, c_re, c_im, d_skip, w_glu, b_glu):
    b, s, _ = u.shape
    uf = u.astype(jnp.float32)
    ug = uf.reshape(b, s, S5_NGROUPS, S5_GROUP)
    y = d_skip.astype(jnp.float32) * uf
    for dirn, rev in enumerate((False, True)):
        y = y + _s5_scan(ug, lam_re[dirn], lam_im[dirn], log_step[dirn], b_re[dirn], b_im[dirn],
                         c_re[dirn], c_im[dirn], rev).reshape(b, s, D_S5)
    v = jax.nn.gelu(y).astype(u.dtype)
    gl = v @ w_glu + b_glu
    return gl[..., :D_S5] * jax.nn.sigmoid(gl[..., D_S5:])


def rglru_mixer(u, gate, conv_w, conv_b, w_a, b_a, w_x, b_x, lam):
    b, s, _ = u.shape
    xc = lax.conv_general_dilated(u, conv_w[:, None, :], window_strides=(1,), padding=[(1, 2)],
                                  dimension_numbers=('NWC', 'WIO', 'NWC'),
                                  feature_group_count=D_LRU) + conv_b
    xf = xc.astype(jnp.float32)
    xh = xf.reshape(b, s, LRU_HEADS, LRU_HEAD_DIM)
    h_sum = jnp.zeros_like(xf)
    for dirn, rev in enumerate((False, True)):
        r = jax.nn.sigmoid(jnp.einsum('bshi,hij->bshj', xh, w_a[dirn].astype(jnp.float32)).reshape(b, s, D_LRU)
                           + b_a[dirn].astype(jnp.float32))
        ig = jax.nn.sigmoid(jnp.einsum('bshi,hij->bshj', xh, w_x[dirn].astype(jnp.float32)).reshape(b, s, D_LRU)
                            + b_x[dirn].astype(jnp.float32))
        log_a = -RG_C * r * jax.nn.softplus(-lam[dirn].astype(jnp.float32))
        a = jnp.exp(log_a)
        bt = jnp.sqrt(-jnp.expm1(2.0 * log_a)) * (ig * xf)
        _, hdir = lax.associative_scan(_lin_combine, (a, bt), axis=1, reverse=rev)
        h_sum = h_sum + hdir
    return (h_sum * jax.nn.gelu(gate.astype(jnp.float32))).astype(u.dtype)


def token_mixer(h, w_in, pool_w, pool_scale, s5_lam_re, s5_lam_im, s5_log_step, s5_b_re, s5_b_im,
                s5_c_re, s5_c_im, s5_d, s5_w_glu, s5_b_glu, lru_conv_w, lru_conv_b, lru_w_a, lru_b_a,
                lru_w_x, lru_b_x, lru_lam, mix_gain, w_out):
    z = h @ w_in
    o1 = D_POOL
    o2 = o1 + D_S5
    o3 = o2 + D_LRU
    y_a = pool_mixer(z[..., :o1], pool_w, pool_scale)
    y_b = s5_mixer(z[..., o1:o2], s5_lam_re, s5_lam_im, s5_log_step, s5_b_re, s5_b_im, s5_c_re, s5_c_im,
                   s5_d, s5_w_glu, s5_b_glu)
    y_c = rglru_mixer(z[..., o2:o3], z[..., o3:], lru_conv_w, lru_conv_b, lru_w_a, lru_b_a, lru_w_x, lru_b_x, lru_lam)
    o = jnp.concatenate([rmsnorm(y_a, mix_gain[:o1]),
                         rmsnorm(y_b, mix_gain[o1:o2]),
                         rmsnorm(y_c, mix_gain[o2:])], axis=-1)
    return o @ w_out


def moe_ffn(h, router_w, router_b, w_gate_up, b_gate_up, w_down, b_down):
    b, s, d = h.shape
    t = b * s
    xt = h.reshape(t, d)
    logits = (xt @ router_w + router_b).astype(jnp.float32)
    top_v, top_i = lax.top_k(logits, TOP_K)
    gates = jax.nn.softmax(top_v, axis=-1)
    n_assign = t * TOP_K
    flat_e = top_i.reshape(-1)
    order = jnp.argsort(flat_e)
    sorted_e = flat_e[order]
    tok = order // TOP_K
    gate_sorted = gates.reshape(-1)[order]
    counts = jnp.bincount(flat_e, length=N_EXPERTS)
    padded = ((counts + MOE_BLOCK - 1) // MOE_BLOCK) * MOE_BLOCK
    pad_end = jnp.cumsum(padded)
    pad_start = pad_end - padded
    start = jnp.cumsum(counts) - counts
    dest = pad_start[sorted_e] + (jnp.arange(n_assign) - start[sorted_e])
    n_blocks = -(-n_assign // MOE_BLOCK) + N_EXPERTS
    n_rows = n_blocks * MOE_BLOCK
    row_tok = jnp.zeros((n_rows,), jnp.int32).at[dest].set(tok.astype(jnp.int32))
    row_valid = jnp.zeros((n_rows,), jnp.bool_).at[dest].set(True)
    xs = jnp.where(row_valid[:, None], xt[row_tok], jnp.zeros((), xt.dtype)).reshape(n_blocks, MOE_BLOCK, d)
    blk_e = jnp.minimum(jnp.searchsorted(pad_end, jnp.arange(n_blocks) * MOE_BLOCK, side='right'), N_EXPERTS - 1)

    def expert_block(args):
        xb, e = args
        gu = xb @ w_gate_up[e] + b_gate_up[e]
        g = jnp.minimum(gu[:, 0::2], SWIGLU_LIMIT)
        up = jnp.clip(gu[:, 1::2], -SWIGLU_LIMIT, SWIGLU_LIMIT)
        glu = g * jax.nn.sigmoid(g * SWIGLU_ALPHA)
        return ((up + 1.0) * glu) @ w_down[e] + b_down[e]

    ys = lax.map(expert_block, (xs, blk_e)).reshape(n_rows, d)
    y_assign = ys[dest] * gate_sorted.astype(ys.dtype)[:, None]
    out = jnp.zeros((t, d), h.dtype).at[tok].add(y_assign.astype(h.dtype))
    return out.reshape(b, s, d)


def setup_inputs(seed: int = 0) -> dict:
    key = jax.random.key(seed)
    ks = iter(jax.random.split(key, 40))
    f32 = jnp.float32

    def nrm(shape, scale):
        return jax.random.normal(next(ks), shape, f32) * scale

    L = DEPTH
    x = nrm((BATCH, SEQ, D_MODEL), 1.0)
    norm1_g = 1.0 + nrm((L, D_MODEL), 0.02)
    w_in = nrm((L, D_MODEL, D_IN), D_MODEL ** -0.5)
    pool_w = nrm((L, len(POOL_WINDOWS), POOL_GROUP, POOL_GROUP), POOL_GROUP ** -0.5)
    pool_scale = 1.0 + nrm((L, D_POOL), 0.02)
    s5_lam_re = -0.5 + nrm((L, 2, S5_NGROUPS, S5_STATE), 0.01)
    s5_lam_im = math.pi * jnp.arange(S5_STATE, dtype=f32) + nrm((L, 2, S5_NGROUPS, S5_STATE), 0.01)
    s5_log_step = jax.random.uniform(next(ks), (L, 2, S5_NGROUPS), f32,
                                     minval=math.log(S5_DT_MIN), maxval=math.log(S5_DT_MAX))
    s5_b_re = nrm((L, 2, S5_NGROUPS, S5_STATE, S5_GROUP), (2.0 * S5_GROUP) ** -0.5)
    s5_b_im = nrm((L, 2, S5_NGROUPS, S5_STATE, S5_GROUP), (2.0 * S5_GROUP) ** -0.5)
    s5_c_re = nrm((L, 2, S5_NGROUPS, S5_GROUP, S5_STATE), (2.0 * S5_STATE) ** -0.5)
    s5_c_im = nrm((L, 2, S5_NGROUPS, S5_GROUP, S5_STATE), (2.0 * S5_STATE) ** -0.5)
    s5_d = nrm((L, D_S5), 1.0)
    s5_w_glu = nrm((L, D_S5, 2 * D_S5), D_S5 ** -0.5)
    s5_b_glu = nrm((L, 2 * D_S5), 0.01)
    lru_conv_w = nrm((L, CONV_WIDTH, D_LRU), CONV_WIDTH ** -0.5)
    lru_conv_b = nrm((L, D_LRU), 0.01)
    lru_w_a = nrm((L, 2, LRU_HEADS, LRU_HEAD_DIM, LRU_HEAD_DIM), LRU_HEAD_DIM ** -0.5)
    lru_b_a = nrm((L, 2, D_LRU), 0.01)
    lru_w_x = nrm((L, 2, LRU_HEADS, LRU_HEAD_DIM, LRU_HEAD_DIM), LRU_HEAD_DIM ** -0.5)
    lru_b_x = nrm((L, 2, D_LRU), 0.01)
    a0 = jax.random.uniform(next(ks), (L, 2, D_LRU), f32, minval=0.9, maxval=0.999)
    p = a0 ** (1.0 / RG_C)
    lru_lam = jnp.log(p) - jnp.log1p(-p)
    mix_gain = 1.0 + nrm((L, D_MIX), 0.02)
    w_out = nrm((L, D_MIX, D_MODEL), D_MIX ** -0.5)
    norm2_g = 1.0 + nrm((L, D_MODEL), 0.02)
    router_w = nrm((L, D_MODEL, N_EXPERTS), D_MODEL ** -0.5)
    router_b = nrm((L, N_EXPERTS), 0.01)
    w_gate_up = nrm((L, N_EXPERTS, D_MODEL, 2 * D_FF), D_MODEL ** -0.5)
    b_gate_up = nrm((L, N_EXPERTS, 2 * D_FF), 0.01)
    w_down = nrm((L, N_EXPERTS, D_FF, D_MODEL), D_FF ** -0.5)
    b_down = nrm((L, N_EXPERTS, D_MODEL), 0.01)
    final_g = 1.0 + nrm((D_MODEL,), 0.02)
    return {"x": x, "norm1_g": norm1_g, "w_in": w_in, "pool_w": pool_w, "pool_scale": pool_scale,
            "s5_lam_re": s5_lam_re, "s5_lam_im": s5_lam_im, "s5_log_step": s5_log_step,
            "s5_b_re": s5_b_re, "s5_b_im": s5_b_im, "s5_c_re": s5_c_re, "s5_c_im": s5_c_im,
            "s5_d": s5_d, "s5_w_glu": s5_w_glu, "s5_b_glu": s5_b_glu,
            "lru_conv_w": lru_conv_w, "lru_conv_b": lru_conv_b, "lru_w_a": lru_w_a, "lru_b_a": lru_b_a,
            "lru_w_x": lru_w_x, "lru_b_x": lru_b_x, "lru_lam": lru_lam,
            "mix_gain": mix_gain, "w_out": w_out, "norm2_g": norm2_g,
            "router_w": router_w, "router_b": router_b, "w_gate_up": w_gate_up, "b_gate_up": b_gate_up,
            "w_down": w_down, "b_down": b_down, "final_g": final_g}


def reference(x, norm1_g, w_in, pool_w, pool_scale, s5_lam_re, s5_lam_im, s5_log_step, s5_b_re, s5_b_im,
              s5_c_re, s5_c_im, s5_d, s5_w_glu, s5_b_glu, lru_conv_w, lru_conv_b, lru_w_a, lru_b_a,
              lru_w_x, lru_b_x, lru_lam, mix_gain, w_out, norm2_g, router_w, router_b, w_gate_up,
              b_gate_up, w_down, b_down, final_g):
    for l in range(DEPTH):
        h = rmsnorm(x, norm1_g[l])
        x = x + token_mixer(h, w_in[l], pool_w[l], pool_scale[l], s5_lam_re[l], s5_lam_im[l], s5_log_step[l],
                            s5_b_re[l], s5_b_im[l], s5_c_re[l], s5_c_im[l], s5_d[l], s5_w_glu[l], s5_b_glu[l],
                            lru_conv_w[l], lru_conv_b[l], lru_w_a[l], lru_b_a[l], lru_w_x[l], lru_b_x[l],
                            lru_lam[l], mix_gain[l], w_out[l])
        h = rmsnorm(x, norm2_g[l])
        x = x + moe_ffn(h, router_w[l], router_b[l], w_gate_up[l], b_gate_up[l], w_down[l], b_down[l])
    return rmsnorm(x, final_g)
```

```python
import functools
import math

import jax
import jax.numpy as jnp
from jax import lax
from jax.experimental import pallas as pl
from jax.experimental.pallas import tpu as pltpu

F32 = jnp.float32
BF16 = jnp.bfloat16

D_MODEL = 1024
D_POOL = 256
D_S5 = 384
D_LRU = 384
D_IN = D_POOL + D_S5 + 2 * D_LRU
POOL_WINDOWS = (2, 4, 8, 16)
POOL_GROUP = 64
S5_GROUP = 16
S5_NGROUPS = 24
S5_STATE = 64
LRU_HEADS = 6
LRU_HEAD_DIM = 64
RG_C = 8.0
N_EXPERTS = 32
TOP_K = 4
D_FF = 1024
SWIGLU_LIMIT = 7.0
SWIGLU_ALPHA = 1.702
EPS = 1e-5

ROW_TILE = 512
S5_CHUNK = 64
POOL_TILE = 512
LRU_TILE = 256
HALO = 8
MOE_ROWS = 256
VMEM_LIMIT = 48 * 1024 * 1024


def _cparams(sem):
    return pltpu.CompilerParams(dimension_semantics=sem, vmem_limit_bytes=VMEM_LIMIT)


def _rms(x, g):
    return x * lax.rsqrt(jnp.mean(x * x, axis=-1, keepdims=True) + EPS) * g


def _gelu(x):
    return 0.5 * x * (1.0 + jnp.tanh(0.7978845608028654 * (x + 0.044715 * (x * x * x))))


def _sigmoid(x):
    return 1.0 / (1.0 + jnp.exp(-x))


def _combine(x_ref, y_refs, gt_ref):
    x = x_ref[...]
    gt = gt_ref[...]
    for k in range(TOP_K):
        x = x + gt[:, k:k + 1] * y_refs[k][...].astype(F32)
    return x


def _inproj_body(has_add, *refs):
    if has_add:
        x_ref, y0, y1, y2, y3, gt_ref, g_ref, w_ref, xo_ref, za, zb, zc, zg = refs
        x = _combine(x_ref, (y0, y1, y2, y3), gt_ref)
        xo_ref[...] = x
    else:
        x_ref, g_ref, w_ref, za, zb, zc, zg = refs
        x = x_ref[...]
    h = _rms(x, g_ref[...]).astype(BF16)
    z = jnp.dot(h, w_ref[...], preferred_element_type=F32)
    za[...] = z[:, :D_POOL]
    zb[...] = z[:, D_POOL:D_POOL + D_S5]
    zc[...] = z[:, D_POOL + D_S5:D_POOL + D_S5 + D_LRU]
    zg[...] = z[:, D_POOL + D_S5 + D_LRU:]


def _inproj(x, add, g, w_bf16):
    t = x.shape[0]
    tm = ROW_TILE
    row = lambda n: pl.BlockSpec((tm, n), lambda i: (i, 0))
    full = lambda a: pl.BlockSpec(a.shape, lambda i: (0,) * a.ndim)
    z_shapes = [jax.ShapeDtypeStruct((t, n), F32) for n in (D_POOL, D_S5, D_LRU, D_LRU)]
    z_specs = [row(n) for n in (D_POOL, D_S5, D_LRU, D_LRU)]
    if add is None:
        ins, in_specs = [x, g, w_bf16], [row(D_MODEL), full(g), full(w_bf16)]
        out_shape, out_specs = z_shapes, z_specs
    else:
        ys, gates = add
        ins = [x, *ys, gates, g, w_bf16]
        in_specs = [row(D_MODEL)] + [row(D_MODEL)] * TOP_K + [row(TOP_K), full(g), full(w_bf16)]
        out_shape = [jax.ShapeDtypeStruct((t, D_MODEL), F32)] + z_shapes
        out_specs = [row(D_MODEL)] + z_specs
    return pl.pallas_call(
        functools.partial(_inproj_body, add is not None),
        grid=(t // tm,), in_specs=in_specs, out_specs=out_specs, out_shape=out_shape,
        compiler_params=_cparams(("parallel",)), name="inproj")(*ins)


def _pool_body(seq, prev_ref, cur_ref, next_ref, w_ref, sc_ref, gain_ref, o_ref, u_s, s2_s, s4_s, s8_s):
    tl = cur_ref.shape[0]
    i = pl.program_id(1)
    nt = pl.num_programs(1)
    zero8 = jnp.zeros((HALO, D_POOL), F32)
    for buf in (u_s, s2_s, s4_s, s8_s):
        buf[0:HALO, :] = zero8
        buf[tl + 3 * HALO:tl + 4 * HALO, :] = zero8
    u_s[HALO:2 * HALO, :] = jnp.where(i > 0, prev_ref[...], 0.0)
    u_s[2 * HALO:2 * HALO + tl, :] = cur_ref[...]
    u_s[2 * HALO + tl:3 * HALO + tl, :] = jnp.where(i < nt - 1, next_ref[...], 0.0)
    r = tl + 2 * HALO
    s2_s[HALO:HALO + r, :] = u_s[HALO - 1:HALO - 1 + r, :] + u_s[HALO:HALO + r, :]
    s4_s[HALO:HALO + r, :] = s2_s[HALO - 1:HALO - 1 + r, :] + s2_s[HALO + 1:HALO + 1 + r, :]
    s8_s[HALO:HALO + r, :] = s4_s[HALO - 2:HALO - 2 + r, :] + s4_s[HALO + 2:HALO + 2 + r, :]
    o = 2 * HALO
    s16 = s8_s[o - 4:o - 4 + tl, :] + s8_s[o + 4:o + 4 + tl, :]
    s8 = s8_s[o:o + tl, :]
    s4 = s4_s[o:o + tl, :]
    s2 = s2_s[o:o + tl, :]
    u = u_s[o:o + tl, :]
    lane = lax.broadcasted_iota(jnp.int32, (tl, D_POOL), 1)
    tpos = lax.broadcasted_iota(jnp.int32, (tl, D_POOL), 0) + i * tl
    g0, g1, g2 = lane < POOL_GROUP, lane < 2 * POOL_GROUP, lane < 3 * POOL_GROUP
    half = jnp.where(g0, 1, jnp.where(g1, 2, jnp.where(g2, 4, 8)))
    wsum = jnp.where(g0, s2, jnp.where(g1, s4, jnp.where(g2, s8, s16)))
    cnt = (jnp.minimum(tpos + half, seq) - jnp.maximum(tpos - half, 0)).astype(F32)
    d = wsum / cnt - u
    y = jnp.dot(d.astype(BF16), w_ref[...], preferred_element_type=F32) * sc_ref[...]
    o_ref[...] = _rms(y, gain_ref[...])


def _pool_mixer(za, batch, seq, w_bd, scale, gain):
    t = za.shape[0]
    tl = POOL_TILE
    nt = seq // tl
    hb = tl // HALO
    nhb = t // HALO
    cur = pl.BlockSpec((tl, D_POOL), lambda b, i: (b * nt + i, 0))
    prev = pl.BlockSpec((HALO, D_POOL), lambda b, i: (jnp.maximum((b * nt + i) * hb - 1, 0), 0))
    nxt = pl.BlockSpec((HALO, D_POOL), lambda b, i: (jnp.minimum((b * nt + i + 1) * hb, nhb - 1), 0))
    full = lambda a: pl.BlockSpec(a.shape, lambda b, i: (0,) * a.ndim)
    return pl.pallas_call(
        functools.partial(_pool_body, seq),
        grid=(batch, nt),
        in_specs=[prev, cur, nxt, full(w_bd), full(scale), full(gain)],
        out_specs=cur,
        out_shape=jax.ShapeDtypeStruct((t, D_POOL), F32),
        scratch_shapes=[pltpu.VMEM((tl + 4 * HALO, D_POOL), F32)] * 4,
        compiler_params=_cparams(("parallel", "parallel")), name="pool_mixer")(za, za, za, w_bd, scale, gain)


def _s5_tables(lam_re, lam_im, log_step, b_re, b_im, c_re, c_im, n_chunks):
    L = S5_CHUNK
    hp = lax.Precision.HIGHEST
    lr = lam_re.astype(F32)
    li = lam_im.astype(F32)
    dt = jnp.exp(log_step.astype(F32))[..., None]
    jj = jnp.arange(L + 1, dtype=F32)[None, None, :, None]
    ar = (lr * dt)[:, :, None, :]
    ai = (li * dt)[:, :, None, :]
    mag = jnp.exp(jj * ar)
    e_r = mag * jnp.cos(jj * ai)
    e_i = mag * jnp.sin(jj * ai)
    den = lr * lr + li * li
    nr = e_r[:, :, 1, :] - 1.0
    ni = e_i[:, :, 1, :]
    q_r = (nr * lr + ni * li) / den
    q_i = (ni * lr - nr * li) / den
    br = b_re.astype(F32)
    bi = b_im.astype(F32)
    bb_r = q_r[..., None] * br - q_i[..., None] * bi
    bb_i = q_r[..., None] * bi + q_i[..., None] * br
    cr = c_re.astype(F32)
    ci = c_im.astype(F32)
    g_r = e_r[..., None] * bb_r[:, :, None] - e_i[..., None] * bb_i[:, :, None]
    g_i = e_r[..., None] * bb_i[:, :, None] + e_i[..., None] * bb_r[:, :, None]
    kk = (jnp.einsum('dgcp,dgjpk->dgjck', cr, g_r[:, :, :L], precision=hp)
          - jnp.einsum('dgcp,dgjpk->dgjck', ci, g_i[:, :, :L], precision=hp))
    kf, kb = kk[0], kk[1]
    kall = jnp.concatenate([kb[:, :0:-1], (kf[:, :1] + kb[:, :1]), kf[:, 1:]], axis=1)
    s_idx = jnp.arange(L)[:, None]
    t_idx = jnp.arange(L)[None, :]
    tm = jnp.take(kall, (t_idx - s_idx) + (L - 1), axis=1)
    tmat = tm.transpose(0, 1, 4, 2, 3).reshape(S5_NGROUPS, L * S5_GROUP, L * S5_GROUP)
    wf_r = g_r[0, :, L - 1::-1][:, :L]
    wf_i = g_i[0, :, L - 1::-1][:, :L]
    wb_r = g_r[1, :, :L]
    wb_i = g_i[1, :, :L]
    to_w = lambda a: a.transpose(0, 1, 3, 2).reshape(S5_NGROUPS, L * S5_GROUP, S5_STATE)
    wst = jnp.concatenate([to_w(wf_r), to_w(wb_r), to_w(wf_i), to_w(wb_i)], axis=-1)
    ef_r, ef_i = e_r[0, :, 1:L + 1], e_i[0, :, 1:L + 1]
    eb_r, eb_i = e_r[1, :, L:0:-1], e_i[1, :, L:0:-1]
    def carry_ops(er, ei, c_r, c_i):
        vr = c_r[:, None] * er[:, :, None, :] - c_i[:, None] * ei[:, :, None, :]
        vi = -(c_r[:, None] * ei[:, :, None, :] + c_i[:, None] * er[:, :, None, :])
        to_v = lambda a: a.transpose(0, 3, 1, 2).reshape(S5_NGROUPS, S5_STATE, L * S5_GROUP)
        return to_v(vr), to_v(vi)
    vf_r, vf_i = carry_ops(ef_r, ef_i, cr[0], ci[0])
    vb_r, vb_i = carry_ops(eb_r, eb_i, cr[1], ci[1])
    vmat = jnp.concatenate([vf_r, vb_r, vf_i, vb_i], axis=1)
    n_steps = max(1, int(math.ceil(math.log2(n_chunks))))
    kpow = (L * (2 ** jnp.arange(n_steps, dtype=F32)))[None, None, :, None]
    pmag = jnp.exp(kpow * ar)
    p_r = pmag * jnp.cos(kpow * ai)
    p_i = pmag * jnp.sin(kpow * ai)
    pad = jnp.zeros((S5_NGROUPS, 8 - n_steps % 8 if n_steps % 8 else 0, 2 * S5_STATE), F32)
    p_r = jnp.concatenate([jnp.concatenate([p_r[0], p_r[1]], axis=-1), pad], axis=1)
    p_i = jnp.concatenate([jnp.concatenate([p_i[0], p_i[1]], axis=-1), pad], axis=1)
    ptab = jnp.concatenate([p_r, p_i], axis=1)
    return tmat.astype(BF16), wst.astype(BF16), vmat.astype(BF16), ptab, n_steps


def _s5_body(n_chunks, n_steps, u_ref, t_ref, w_ref, v_ref, p_ref, y_ref):
    u = u_ref[0]
    n = u.shape[0]
    y = jnp.dot(u, t_ref[0], preferred_element_type=F32)
    s = jnp.dot(u, w_ref[0], preferred_element_type=F32)
    two_p = 2 * S5_STATE
    xr = s[:, :two_p]
    xi = s[:, two_p:]
    row = lax.broadcasted_iota(jnp.int32, (n, two_p), 0) % n_chunks
    is_fwd = lax.broadcasted_iota(jnp.int32, (n, two_p), 1) < S5_STATE
    pim0 = p_ref.shape[1] // 2

    def shifted(a, k):
        down = jnp.where(row >= k, pltpu.roll(a, k, 0), 0.0)
        up = jnp.where(row < n_chunks - k, pltpu.roll(a, n - k, 0), 0.0)
        return jnp.where(is_fwd, down, up)

    for j in range(n_steps):
        k = 1 << j
        pr = p_ref[0, j:j + 1, :]
        pi = p_ref[0, pim0 + j:pim0 + j + 1, :]
        sr = shifted(xr, k)
        si = shifted(xi, k)
        xr, xi = xr + pr * sr - pi * si, xi + pr * si + pi * sr
    carry = jnp.concatenate([shifted(xr, 1), shifted(xi, 1)], axis=1).astype(BF16)
    y_ref[0] = y + jnp.dot(carry, v_ref[0], preferred_element_type=F32)


def _s5_core(zb, batch, seq, tables):
    tmat, wst, vmat, ptab, n_steps = tables
    L = S5_CHUNK
    n_chunks = seq // L
    n = batch * n_chunks
    ug = zb.astype(BF16).reshape(n, L, S5_NGROUPS, S5_GROUP).transpose(2, 0, 1, 3).reshape(S5_NGROUPS, n, L * S5_GROUP)
    per_g = lambda a: pl.BlockSpec((1,) + a.shape[1:], lambda g: (g, 0, 0))
    yg = pl.pallas_call(
        functools.partial(_s5_body, n_chunks, n_steps),
        grid=(S5_NGROUPS,),
        in_specs=[per_g(ug), per_g(tmat), per_g(wst), per_g(vmat), per_g(ptab)],
        out_specs=pl.BlockSpec((1, n, L * S5_GROUP), lambda g: (g, 0, 0)),
        out_shape=jax.ShapeDtypeStruct((S5_NGROUPS, n, L * S5_GROUP), F32),
        compiler_params=_cparams(("parallel",)), name="s5_core")(ug, tmat, wst, vmat, ptab)
    return yg.reshape(S5_NGROUPS, n, L, S5_GROUP).transpose(1, 2, 0, 3).reshape(batch * seq, D_S5)


def _lru_body(reverse, prev_ref, cur_ref, next_ref, cw_ref, cb_ref, wg_ref, bg_ref, sp_ref, *rest):
    if reverse:
        hf_ref, gate_ref, gain_ref, o_ref, x_s, a_s, b_s, h_s, carry_s = rest
    else:
        o_ref, x_s, a_s, b_s, h_s, carry_s = rest
    nb, tl, _ = cur_ref.shape
    step = pl.program_id(0)
    nt = pl.num_programs(0)
    ti = nt - 1 - step if reverse else step

    @pl.when(step == 0)
    def _():
        carry_s[...] = jnp.zeros_like(carry_s)

    cw = cw_ref[...]
    for b in range(nb):
        x_s[0:HALO, :] = jnp.where(ti > 0, prev_ref[b], 0.0)
        x_s[HALO:HALO + tl, :] = cur_ref[b]
        x_s[HALO + tl:2 * HALO + tl, :] = jnp.where(ti < nt - 1, next_ref[b], 0.0)
        xc = cb_ref[...]
        for k in range(4):
            xc = xc + cw[k:k + 1, :] * x_s[HALO - 1 + k:HALO - 1 + k + tl, :]
        gates = jnp.dot(xc.astype(BF16), wg_ref[...], preferred_element_type=F32) + bg_ref[...]
        r = _sigmoid(gates[:, :D_LRU])
        ig = _sigmoid(gates[:, D_LRU:])
        log_a = (-RG_C) * r * sp_ref[...]
        a = jnp.exp(log_a)
        a_s[b] = a
        b_s[b] = jnp.sqrt(-jnp.tanh(log_a) * (1.0 + a * a)) * (ig * xc)

    def scan_step(s, hs):
        t = tl - 1 - s if reverse else s
        out = []
        for b in range(nb):
            h = a_s[b, pl.ds(t, 1), :] * hs[b] + b_s[b, pl.ds(t, 1), :]
            h_s[b, pl.ds(t, 1), :] = h
            out.append(h)
        return tuple(out)

    hs = lax.fori_loop(0, tl, scan_step, tuple(carry_s[b:b + 1, :] for b in range(nb)), unroll=8)
    for b in range(nb):
        carry_s[b:b + 1, :] = hs[b]

    if reverse:
        for b in range(nb):
            y = (hf_ref[b] + h_s[b]) * _gelu(gate_ref[b])
            o_ref[b] = _rms(y, gain_ref[...])
    else:
        o_ref[...] = h_s[...]


def _lru_pass(reverse, zc3, params, extra):
    nb, seq, _ = zc3.shape
    tl = LRU_TILE
    nt = seq // tl
    hb = tl // HALO
    nhb = seq // HALO
    tix = (lambda i: nt - 1 - i) if reverse else (lambda i: i)
    cur = pl.BlockSpec((nb, tl, D_LRU), lambda i: (0, tix(i), 0))
    prev = pl.BlockSpec((nb, HALO, D_LRU), lambda i: (0, jnp.maximum(tix(i) * hb - 1, 0), 0))
    nxt = pl.BlockSpec((nb, HALO, D_LRU), lambda i: (0, jnp.minimum((tix(i) + 1) * hb, nhb - 1), 0))
    full = lambda a: pl.BlockSpec(a.shape, lambda i: (0,) * a.ndim)
    ins = [zc3, zc3, zc3, *params]
    in_specs = [prev, cur, nxt] + [full(a) for a in params]
    if reverse:
        hf, gate, gain = extra
        ins += [hf, gate, gain]
        in_specs += [cur, cur, full(gain)]
    return pl.pallas_call(
        functools.partial(_lru_body, reverse),
        grid=(nt,), in_specs=in_specs, out_specs=cur,
        out_shape=jax.ShapeDtypeStruct((nb, seq, D_LRU), F32),
        scratch_shapes=[pltpu.VMEM((tl + 2 * HALO, D_LRU), F32),
                        pltpu.VMEM((nb, tl, D_LRU), F32),
                        pltpu.VMEM((nb, tl, D_LRU), F32),
                        pltpu.VMEM((nb, tl, D_LRU), F32),
                        pltpu.VMEM((8, D_LRU), F32)],
        compiler_params=_cparams(("arbitrary",)),
        name="lru_bwd" if reverse else "lru_fwd")(*ins)


def _block_diag(w):
    h, d, _ = w.shape
    eye = jnp.eye(h, dtype=w.dtype)
    return (eye[:, None, :, None] * w[:, :, None, :]).reshape(h * d, h * d)


def _lru_mixer(zc, zg, batch, seq, conv_w, conv_b, w_a, b_a, w_x, b_x, lam, gain):
    zc3 = zc.reshape(batch, seq, D_LRU)
    zg3 = zg.reshape(batch, seq, D_LRU)
    outs = None
    for d in (0, 1):
        wg = jnp.concatenate([_block_diag(w_a[d]), _block_diag(w_x[d])], axis=1).astype(BF16)
        bg = jnp.concatenate([b_a[d], b_x[d]])[None, :]
        sp = jax.nn.softplus(-lam[d].astype(F32))[None, :]
        params = [conv_w, conv_b[None, :], wg, bg, sp]
        extra = None if d == 0 else (outs, zg3, gain[None, :])
        outs = _lru_pass(d == 1, zc3, params, extra)
    return outs.reshape(batch * seq, D_LRU)


def _mixout_body(x_ref, oa_ref, ys_ref, ub_ref, oc_ref, sd_ref, wglu_ref, bglu_ref, gb_ref,
                 wout_ref, g2_ref, rw_ref, rb_ref, xo_ref, h_ref, ti_ref, gt_ref):
    tm = x_ref.shape[0]
    yb = ys_ref[...] + sd_ref[...] * ub_ref[...]
    gl = jnp.dot(_gelu(yb).astype(BF16), wglu_ref[...], preferred_element_type=F32) + bglu_ref[...]
    ob = _rms(gl[:, :D_S5] * _sigmoid(gl[:, D_S5:]), gb_ref[...])
    o = jnp.concatenate([oa_ref[...], ob, oc_ref[...]], axis=1).astype(BF16)
    x = x_ref[...] + jnp.dot(o, wout_ref[...], preferred_element_type=F32)
    xo_ref[...] = x
    h = _rms(x, g2_ref[...])
    h_ref[...] = h.astype(BF16)
    logits = jnp.dot(h, rw_ref[...], preferred_element_type=F32, precision=lax.Precision.HIGHEST) + rb_ref[...]
    lane = lax.broadcasted_iota(jnp.int32, (tm, N_EXPERTS), 1).astype(F32)
    lane_k = lax.broadcasted_iota(jnp.int32, (tm, TOP_K), 1)
    idx_out = jnp.zeros((tm, TOP_K), F32)
    val_out = jnp.zeros((tm, TOP_K), F32)
    top = None
    for k in range(TOP_K):
        m = jnp.max(logits, axis=-1, keepdims=True)
        idx = jnp.min(jnp.where(logits == m, lane, float(N_EXPERTS)), axis=-1, keepdims=True)
        top = m if top is None else top
        idx_out = jnp.where(lane_k == k, idx, idx_out)
        val_out = jnp.where(lane_k == k, jnp.exp(m - top), val_out)
        logits = jnp.where(lane == idx, -jnp.inf, logits)
    ti_ref[...] = idx_out.astype(jnp.int32)
    gt_ref[...] = val_out / jnp.sum(val_out, axis=-1, keepdims=True)


def _mixout(x, oa, ys5, zb, oc, s5_d, wglu, bglu, gain_b, wout, g2, rw, rb):
    t = x.shape[0]
    tm = ROW_TILE
    row = lambda n: pl.BlockSpec((tm, n), lambda i: (i, 0))
    full = lambda a: pl.BlockSpec(a.shape, lambda i: (0,) * a.ndim)
    params = [s5_d, wglu, bglu, gain_b, wout, g2, rw, rb]
    return pl.pallas_call(
        _mixout_body,
        grid=(t // tm,),
        in_specs=[row(D_MODEL), row(D_POOL), row(D_S5), row(D_S5), row(D_LRU)] + [full(a) for a in params],
        out_specs=[row(D_MODEL), row(D_MODEL), row(TOP_K), row(TOP_K)],
        out_shape=[jax.ShapeDtypeStruct((t, D_MODEL), F32), jax.ShapeDtypeStruct((t, D_MODEL), BF16),
                   jax.ShapeDtypeStruct((t, TOP_K), jnp.int32), jax.ShapeDtypeStruct((t, TOP_K), F32)],
        compiler_params=_cparams(("parallel",)), name="mix_out")(x, oa, ys5, zb, oc, *params)


def _expert_body(be_ref, nu_ref, x_ref, wg_ref, wu_ref, bg_ref, bu_ref, wd_ref, bd_ref, y_ref):
    @pl.when(pl.program_id(0) < nu_ref[0])
    def _():
        x = x_ref[...]
        g = jnp.dot(x, wg_ref[0], preferred_element_type=F32) + bg_ref[0]
        u = jnp.dot(x, wu_ref[0], preferred_element_type=F32) + bu_ref[0]
        g = jnp.minimum(g, SWIGLU_LIMIT)
        u = jnp.clip(u, -SWIGLU_LIMIT, SWIGLU_LIMIT)
        act = ((u + 1.0) * (g * _sigmoid(g * SWIGLU_ALPHA))).astype(BF16)
        y = jnp.dot(act, wd_ref[0], preferred_element_type=F32) + bd_ref[0]
        y_ref[...] = y.astype(y_ref.dtype)

    @pl.when(pl.program_id(0) >= nu_ref[0])
    def _():
        y_ref[...] = jnp.zeros_like(y_ref)


def _expert_mlp(xs, blk_e, n_used, wg, wu, bg, bu, wd, bd):
    n_rows = xs.shape[0]
    bm = MOE_ROWS
    rows = pl.BlockSpec((bm, D_MODEL), lambda i, be, nu: (jnp.minimum(i, nu[0] - 1), 0))
    per_e = lambda a: pl.BlockSpec((1,) + a.shape[1:], lambda i, be, nu: (be[i], 0, 0))
    grid_spec = pltpu.PrefetchScalarGridSpec(
        num_scalar_prefetch=2, grid=(n_rows // bm,),
        in_specs=[rows, per_e(wg), per_e(wu), per_e(bg), per_e(bu), per_e(wd), per_e(bd)],
        out_specs=pl.BlockSpec((bm, D_MODEL), lambda i, be, nu: (i, 0)))
    return pl.pallas_call(
        _expert_body, grid_spec=grid_spec,
        out_shape=jax.ShapeDtypeStruct((n_rows, D_MODEL), BF16),
        compiler_params=_cparams(("arbitrary",)), name="expert_mlp")(blk_e, n_used, xs, wg, wu, bg, bu, wd, bd)


def _moe_dispatch(top_i, t):
    bm = MOE_ROWS
    n_blocks = (t * TOP_K) // bm + N_EXPERTS
    sel = jnp.sum(jax.nn.one_hot(top_i, N_EXPERTS, dtype=jnp.int32), axis=1)
    csum = jnp.cumsum(sel, axis=0)
    counts = csum[-1]
    padded = ((counts + bm - 1) // bm) * bm
    pad_end = jnp.cumsum(padded)
    pad_start = pad_end - padded
    rank = jnp.take_along_axis(csum - sel, top_i, axis=1)
    dest = pad_start[top_i] + rank
    tok = jnp.broadcast_to(jnp.arange(t, dtype=jnp.int32)[:, None], (t, TOP_K))
    row_tok = jnp.zeros((n_blocks * bm,), jnp.int32).at[dest.reshape(-1)].set(tok.reshape(-1))
    blk_e = jnp.minimum(jnp.searchsorted(pad_end, jnp.arange(n_blocks) * bm, side='right'),
                        N_EXPERTS - 1).astype(jnp.int32)
    n_used = (pad_end[-1] // bm).astype(jnp.int32).reshape(1)
    return dest, row_tok, blk_e, n_used


def _final_body(x_ref, y0, y1, y2, y3, gt_ref, g_ref, o_ref):
    o_ref[...] = _rms(_combine(x_ref, (y0, y1, y2, y3), gt_ref), g_ref[...])


def _final(x, ys, gates, g):
    t = x.shape[0]
    tm = ROW_TILE
    row = lambda n: pl.BlockSpec((tm, n), lambda i: (i, 0))
    return pl.pallas_call(
        _final_body, grid=(t // tm,),
        in_specs=[row(D_MODEL)] * (1 + TOP_K) + [row(TOP_K), pl.BlockSpec(g.shape, lambda i: (0, 0))],
        out_specs=row(D_MODEL), out_shape=jax.ShapeDtypeStruct((t, D_MODEL), F32),
        compiler_params=_cparams(("parallel",)), name="final_norm")(x, *ys, gates, g)


def kernel(x, norm1_g, w_in, pool_w, pool_scale, s5_lam_re, s5_lam_im, s5_log_step, s5_b_re, s5_b_im, s5_c_re, s5_c_im, s5_d, s5_w_glu, s5_b_glu, lru_conv_w, lru_conv_b, lru_w_a, lru_b_a, lru_w_x, lru_b_x, lru_lam, mix_gain, w_out, norm2_g, router_w, router_b, w_gate_up, b_gate_up, w_down, b_down, final_g):
    batch, seq, d = x.shape
    t = batch * seq
    depth = norm1_g.shape[0]
    xt = x.reshape(t, d)
    add = None
    o1, o2 = D_POOL, D_POOL + D_S5
    for l in range(depth):
        if add is None:
            za, zb, zc, zg = _inproj(xt, None, norm1_g[l][None, :], w_in[l].astype(BF16))
        else:
            xt, za, zb, zc, zg = _inproj(xt, add, norm1_g[l][None, :], w_in[l].astype(BF16))
        oa = _pool_mixer(za, batch, seq, _block_diag(pool_w[l]).astype(BF16), pool_scale[l][None, :],
                         mix_gain[l][None, :o1])
        tables = _s5_tables(s5_lam_re[l], s5_lam_im[l], s5_log_step[l], s5_b_re[l], s5_b_im[l],
                            s5_c_re[l], s5_c_im[l], seq // S5_CHUNK)
        ys5 = _s5_core(zb, batch, seq, tables)
        oc = _lru_mixer(zc, zg, batch, seq, lru_conv_w[l], lru_conv_b[l], lru_w_a[l], lru_b_a[l],
                        lru_w_x[l], lru_b_x[l], lru_lam[l], mix_gain[l][o2:])
        xt, h2, top_i, gates = _mixout(
            xt, oa, ys5, zb, oc, s5_d[l][None, :], s5_w_glu[l].astype(BF16), s5_b_glu[l][None, :],
            mix_gain[l][None, o1:o2], w_out[l].astype(BF16), norm2_g[l][None, :], router_w[l], router_b[l][None, :])
        dest, row_tok, blk_e, n_used = _moe_dispatch(top_i, t)
        xs = jnp.take(h2, row_tok, axis=0)
        wgu = w_gate_up[l]
        ys = _expert_mlp(xs, blk_e, n_used,
                         wgu[:, :, 0::2].astype(BF16), wgu[:, :, 1::2].astype(BF16),
                         b_gate_up[l][:, None, 0::2], b_gate_up[l][:, None, 1::2],
                         w_down[l].astype(BF16), b_down[l][:, None, :])
        add = ([jnp.take(ys, dest[:, k], axis=0) for k in range(TOP_K)], gates)
    out = _final(xt, add[0], add[1], final_g[None, :])
    return out.reshape(batch, seq, d)
```

```python
import functools
import math

import jax
import jax.numpy as jnp
from jax import lax
from jax.experimental import pallas as pl
from jax.experimental.pallas import tpu as pltpu

F32 = jnp.float32
BF16 = jnp.bfloat16

D_MODEL = 1024
D_POOL = 256
D_S5 = 384
D_LRU = 384
D_IN = D_POOL + D_S5 + 2 * D_LRU
POOL_WINDOWS = (2, 4, 8, 16)
POOL_GROUP = 64
S5_GROUP = 16
S5_NGROUPS = 24
S5_STATE = 64
LRU_HEADS = 6
LRU_HEAD_DIM = 64
RG_C = 8.0
N_EXPERTS = 32
TOP_K = 4
D_FF = 1024
SWIGLU_LIMIT = 7.0
SWIGLU_ALPHA = 1.702
EPS = 1e-5

ROW_TILE = 512
S5_CHUNK = 64
POOL_TILE = 512
LRU_TILE = 256
HALO = 8
MOE_ROWS = 256
MXU_TILE = 256
VMEM_LIMIT = 48 * 1024 * 1024


def _cparams(sem):
    return pltpu.CompilerParams(dimension_semantics=sem, vmem_limit_bytes=VMEM_LIMIT)


def _rms(x, g):
    return x * lax.rsqrt(jnp.mean(x * x, axis=-1, keepdims=True) + EPS) * g


def _gelu(x):
    return 0.5 * x * (1.0 + jnp.tanh(0.7978845608028654 * (x + 0.044715 * (x * x * x))))


def _sigmoid(x):
    return 1.0 / (1.0 + jnp.exp(-x))


def _combine(x_ref, y_refs, gt_ref):
    x = x_ref[...]
    gt = gt_ref[...]
    for k in range(TOP_K):
        x = x + gt[:, k:k + 1] * y_refs[k][...].astype(F32)
    return x


def _inproj_body(has_add, *refs):
    if has_add:
        x_ref, y0, y1, y2, y3, gt_ref, g_ref, w_ref, xo_ref, za, zb, zc, zg = refs
        x = _combine(x_ref, (y0, y1, y2, y3), gt_ref)
        xo_ref[...] = x
    else:
        x_ref, g_ref, w_ref, za, zb, zc, zg = refs
        x = x_ref[...]
    h = _rms(x, g_ref[...]).astype(BF16)
    z = jnp.dot(h, w_ref[...], preferred_element_type=F32)
    za[...] = z[:, :D_POOL]
    zb[...] = z[:, D_POOL:D_POOL + D_S5]
    zc[...] = z[:, D_POOL + D_S5:D_POOL + D_S5 + D_LRU]
    zg[...] = z[:, D_POOL + D_S5 + D_LRU:]


def _inproj(x, add, g, w_bf16):
    t = x.shape[0]
    tm = ROW_TILE
    row = lambda n: pl.BlockSpec((tm, n), lambda i: (i, 0))
    full = lambda a: pl.BlockSpec(a.shape, lambda i: (0,) * a.ndim)
    z_shapes = [jax.ShapeDtypeStruct((t, n), F32) for n in (D_POOL, D_S5, D_LRU, D_LRU)]
    z_specs = [row(n) for n in (D_POOL, D_S5, D_LRU, D_LRU)]
    if add is None:
        ins, in_specs = [x, g, w_bf16], [row(D_MODEL), full(g), full(w_bf16)]
        out_shape, out_specs = z_shapes, z_specs
    else:
        ys, gates = add
        ins = [x, *ys, gates, g, w_bf16]
        in_specs = [row(D_MODEL)] + [row(D_MODEL)] * TOP_K + [row(TOP_K), full(g), full(w_bf16)]
        out_shape = [jax.ShapeDtypeStruct((t, D_MODEL), F32)] + z_shapes
        out_specs = [row(D_MODEL)] + z_specs
    return pl.pallas_call(
        functools.partial(_inproj_body, add is not None),
        grid=(t // tm,), in_specs=in_specs, out_specs=out_specs, out_shape=out_shape,
        compiler_params=_cparams(("parallel",)), name="inproj")(*ins)


def _pool_body(seq, prev_ref, cur_ref, next_ref, w_ref, sc_ref, gain_ref, o_ref, u_s, s2_s, s4_s, s8_s):
    tl = cur_ref.shape[0]
    i = pl.program_id(1)
    nt = pl.num_programs(1)
    zero8 = jnp.zeros((HALO, D_POOL), F32)
    for buf in (u_s, s2_s, s4_s, s8_s):
        buf[0:HALO, :] = zero8
        buf[tl + 3 * HALO:tl + 4 * HALO, :] = zero8
    u_s[HALO:2 * HALO, :] = jnp.where(i > 0, prev_ref[...], 0.0)
    u_s[2 * HALO:2 * HALO + tl, :] = cur_ref[...]
    u_s[2 * HALO + tl:3 * HALO + tl, :] = jnp.where(i < nt - 1, next_ref[...], 0.0)
    r = tl + 2 * HALO
    s2_s[HALO:HALO + r, :] = u_s[HALO - 1:HALO - 1 + r, :] + u_s[HALO:HALO + r, :]
    s4_s[HALO:HALO + r, :] = s2_s[HALO - 1:HALO - 1 + r, :] + s2_s[HALO + 1:HALO + 1 + r, :]
    s8_s[HALO:HALO + r, :] = s4_s[HALO - 2:HALO - 2 + r, :] + s4_s[HALO + 2:HALO + 2 + r, :]
    o = 2 * HALO
    s16 = s8_s[o - 4:o - 4 + tl, :] + s8_s[o + 4:o + 4 + tl, :]
    s8 = s8_s[o:o + tl, :]
    s4 = s4_s[o:o + tl, :]
    s2 = s2_s[o:o + tl, :]
    u = u_s[o:o + tl, :]
    lane = lax.broadcasted_iota(jnp.int32, (tl, D_POOL), 1)
    tpos = lax.broadcasted_iota(jnp.int32, (tl, D_POOL), 0) + i * tl
    g0, g1, g2 = lane < POOL_GROUP, lane < 2 * POOL_GROUP, lane < 3 * POOL_GROUP
    half = jnp.where(g0, 1, jnp.where(g1, 2, jnp.where(g2, 4, 8)))
    wsum = jnp.where(g0, s2, jnp.where(g1, s4, jnp.where(g2, s8, s16)))
    cnt = (jnp.minimum(tpos + half, seq) - jnp.maximum(tpos - half, 0)).astype(F32)
    d = wsum / cnt - u
    y = jnp.dot(d.astype(BF16), w_ref[...], preferred_element_type=F32) * sc_ref[...]
    o_ref[...] = _rms(y, gain_ref[...])


def _pool_mixer(za, batch, seq, w_bd, scale, gain):
    t = za.shape[0]
    tl = POOL_TILE
    nt = seq // tl
    hb = tl // HALO
    nhb = t // HALO
    cur = pl.BlockSpec((tl, D_POOL), lambda b, i: (b * nt + i, 0))
    prev = pl.BlockSpec((HALO, D_POOL), lambda b, i: (jnp.maximum((b * nt + i) * hb - 1, 0), 0))
    nxt = pl.BlockSpec((HALO, D_POOL), lambda b, i: (jnp.minimum((b * nt + i + 1) * hb, nhb - 1), 0))
    full = lambda a: pl.BlockSpec(a.shape, lambda b, i: (0,) * a.ndim)
    return pl.pallas_call(
        functools.partial(_pool_body, seq),
        grid=(batch, nt),
        in_specs=[prev, cur, nxt, full(w_bd), full(scale), full(gain)],
        out_specs=cur,
        out_shape=jax.ShapeDtypeStruct((t, D_POOL), F32),
        scratch_shapes=[pltpu.VMEM((tl + 4 * HALO, D_POOL), F32)] * 4,
        compiler_params=_cparams(("parallel", "parallel")), name="pool_mixer")(za, za, za, w_bd, scale, gain)


def _s5_tables(lam_re, lam_im, log_step, b_re, b_im, c_re, c_im, n_chunks):
    L = S5_CHUNK
    hp = lax.Precision.HIGHEST
    lr = lam_re.astype(F32)
    li = lam_im.astype(F32)
    dt = jnp.exp(log_step.astype(F32))[..., None]
    jj = jnp.arange(L + 1, dtype=F32)[None, None, :, None]
    ar = (lr * dt)[:, :, None, :]
    ai = (li * dt)[:, :, None, :]
    mag = jnp.exp(jj * ar)
    e_r = mag * jnp.cos(jj * ai)
    e_i = mag * jnp.sin(jj * ai)
    den = lr * lr + li * li
    nr = e_r[:, :, 1, :] - 1.0
    ni = e_i[:, :, 1, :]
    q_r = (nr * lr + ni * li) / den
    q_i = (ni * lr - nr * li) / den
    br = b_re.astype(F32)
    bi = b_im.astype(F32)
    bb_r = q_r[..., None] * br - q_i[..., None] * bi
    bb_i = q_r[..., None] * bi + q_i[..., None] * br
    cr = c_re.astype(F32)
    ci = c_im.astype(F32)
    g_r = e_r[..., None] * bb_r[:, :, None] - e_i[..., None] * bb_i[:, :, None]
    g_i = e_r[..., None] * bb_i[:, :, None] + e_i[..., None] * bb_r[:, :, None]
    kk = (jnp.einsum('dgcp,dgjpk->dgjck', cr, g_r[:, :, :L], precision=hp)
          - jnp.einsum('dgcp,dgjpk->dgjck', ci, g_i[:, :, :L], precision=hp))
    kf, kb = kk[0], kk[1]
    kall = jnp.concatenate([kb[:, :0:-1], (kf[:, :1] + kb[:, :1]), kf[:, 1:],
                            jnp.zeros_like(kf[:, :1])], axis=1)
    tmat = kall.transpose(0, 3, 1, 2).reshape(S5_NGROUPS, S5_GROUP, 2 * L * S5_GROUP)
    wf_r = g_r[0, :, L - 1::-1][:, :L]
    wf_i = g_i[0, :, L - 1::-1][:, :L]
    wb_r = g_r[1, :, :L]
    wb_i = g_i[1, :, :L]
    to_w = lambda a: a.transpose(0, 1, 3, 2).reshape(S5_NGROUPS, L * S5_GROUP, S5_STATE)
    wst = jnp.concatenate([to_w(wf_r), to_w(wb_r), to_w(wf_i), to_w(wb_i)], axis=-1)
    ef_r, ef_i = e_r[0, :, 1:L + 1], e_i[0, :, 1:L + 1]
    eb_r, eb_i = e_r[1, :, L:0:-1], e_i[1, :, L:0:-1]
    def carry_ops(er, ei, c_r, c_i):
        vr = c_r[:, None] * er[:, :, None, :] - c_i[:, None] * ei[:, :, None, :]
        vi = -(c_r[:, None] * ei[:, :, None, :] + c_i[:, None] * er[:, :, None, :])
        to_v = lambda a: a.transpose(0, 3, 1, 2).reshape(S5_NGROUPS, S5_STATE, L * S5_GROUP)
        return to_v(vr), to_v(vi)
    vf_r, vf_i = carry_ops(ef_r, ef_i, cr[0], ci[0])
    vb_r, vb_i = carry_ops(eb_r, eb_i, cr[1], ci[1])
    vmat = jnp.concatenate([vf_r, vb_r, vf_i, vb_i], axis=1)
    n_steps = max(1, int(math.ceil(math.log2(n_chunks))))
    kpow = (L * (2 ** jnp.arange(n_steps, dtype=F32)))[None, None, :, None]
    pmag = jnp.exp(kpow * ar)
    p_r = pmag * jnp.cos(kpow * ai)
    p_i = pmag * jnp.sin(kpow * ai)
    pad = jnp.zeros((S5_NGROUPS, 8 - n_steps % 8 if n_steps % 8 else 0, 2 * S5_STATE), F32)
    p_r = jnp.concatenate([jnp.concatenate([p_r[0], p_r[1]], axis=-1), pad], axis=1)
    p_i = jnp.concatenate([jnp.concatenate([p_i[0], p_i[1]], axis=-1), pad], axis=1)
    ptab = jnp.concatenate([p_r, p_i], axis=1)
    return tmat, wst.astype(BF16), vmat.astype(BF16), ptab, n_steps


def _s5_body(n_chunks, n_steps, u_ref, t_ref, w_ref, v_ref, p_ref, y_ref, toep_s):
    u = u_ref[0]
    n = u.shape[0]
    lags = t_ref[0]
    width = lags.shape[1]
    blocks_per_tile = 128 // S5_GROUP
    for res in range(blocks_per_tile):
        rot = lags if res == 0 else pltpu.roll(lags, width - res * S5_GROUP, 1)
        for s in range(S5_CHUNK):
            lag0 = S5_CHUNK - 1 - s
            if lag0 % blocks_per_tile == res:
                q = (lag0 // blocks_per_tile) * 128
                toep_s[s * S5_GROUP:(s + 1) * S5_GROUP, :] = rot[:, q:q + S5_CHUNK * S5_GROUP].astype(BF16)
    y = jnp.dot(u, toep_s[...], preferred_element_type=F32)
    s = jnp.dot(u, w_ref[0], preferred_element_type=F32)
    two_p = 2 * S5_STATE
    xr = s[:, :two_p]
    xi = s[:, two_p:]
    row = lax.broadcasted_iota(jnp.int32, (n, two_p), 0) % n_chunks
    is_fwd = lax.broadcasted_iota(jnp.int32, (n, two_p), 1) < S5_STATE
    pim0 = p_ref.shape[1] // 2

    def shifted(a, k):
        down = jnp.where(row >= k, pltpu.roll(a, k, 0), 0.0)
        up = jnp.where(row < n_chunks - k, pltpu.roll(a, n - k, 0), 0.0)
        return jnp.where(is_fwd, down, up)

    for j in range(n_steps):
        k = 1 << j
        pr = p_ref[0, j:j + 1, :]
        pi = p_ref[0, pim0 + j:pim0 + j + 1, :]
        sr = shifted(xr, k)
        si = shifted(xi, k)
        xr, xi = xr + pr * sr - pi * si, xi + pr * si + pi * sr
    carry = jnp.concatenate([shifted(xr, 1), shifted(xi, 1)], axis=1).astype(BF16)
    y_ref[0] = y + jnp.dot(carry, v_ref[0], preferred_element_type=F32)


def _s5_core(zb, batch, seq, tables):
    tmat, wst, vmat, ptab, n_steps = tables
    L = S5_CHUNK
    n_chunks = seq // L
    n = batch * n_chunks
    ug = zb.astype(BF16).reshape(n, L, S5_NGROUPS, S5_GROUP).transpose(2, 0, 1, 3).reshape(S5_NGROUPS, n, L * S5_GROUP)
    per_g = lambda a: pl.BlockSpec((1,) + a.shape[1:], lambda g: (g, 0, 0))
    yg = pl.pallas_call(
        functools.partial(_s5_body, n_chunks, n_steps),
        grid=(S5_NGROUPS,),
        in_specs=[per_g(ug), per_g(tmat), per_g(wst), per_g(vmat), per_g(ptab)],
        out_specs=pl.BlockSpec((1, n, L * S5_GROUP), lambda g: (g, 0, 0)),
        out_shape=jax.ShapeDtypeStruct((S5_NGROUPS, n, L * S5_GROUP), F32),
        scratch_shapes=[pltpu.VMEM((L * S5_GROUP, L * S5_GROUP), BF16)],
        compiler_params=_cparams(("parallel",)), name="s5_core")(ug, tmat, wst, vmat, ptab)
    return yg.reshape(S5_NGROUPS, n, L, S5_GROUP).transpose(1, 2, 0, 3).reshape(batch * seq, D_S5)


def _lru_body(reverse, prev_ref, cur_ref, next_ref, cw_ref, cb_ref, wg_ref, bg_ref, sp_ref, *rest):
    if reverse:
        hf_ref, gate_ref, gain_ref, o_ref, x_s, a_s, b_s, h_s, carry_s = rest
    else:
        o_ref, x_s, a_s, b_s, h_s, carry_s = rest
    nb, tl, _ = cur_ref.shape
    step = pl.program_id(0)
    nt = pl.num_programs(0)
    ti = nt - 1 - step if reverse else step

    @pl.when(step == 0)
    def _():
        carry_s[...] = jnp.zeros_like(carry_s)

    cw = cw_ref[...]
    for b in range(nb):
        x_s[0:HALO, :] = jnp.where(ti > 0, prev_ref[b], 0.0)
        x_s[HALO:HALO + tl, :] = cur_ref[b]
        x_s[HALO + tl:2 * HALO + tl, :] = jnp.where(ti < nt - 1, next_ref[b], 0.0)
        xc = cb_ref[...]
        for k in range(4):
            xc = xc + cw[k:k + 1, :] * x_s[HALO - 1 + k:HALO - 1 + k + tl, :]
        gates = jnp.dot(xc.astype(BF16), wg_ref[...], preferred_element_type=F32) + bg_ref[...]
        r = _sigmoid(gates[:, :D_LRU])
        ig = _sigmoid(gates[:, D_LRU:])
        log_a = (-RG_C) * r * sp_ref[...]
        a = jnp.exp(log_a)
        a_s[b] = a
        b_s[b] = jnp.sqrt(-jnp.tanh(log_a) * (1.0 + a * a)) * (ig * xc)

    def scan_step(s, hs):
        t = tl - 1 - s if reverse else s
        out = []
        for b in range(nb):
            h = a_s[b, pl.ds(t, 1), :] * hs[b] + b_s[b, pl.ds(t, 1), :]
            h_s[b, pl.ds(t, 1), :] = h
            out.append(h)
        return tuple(out)

    hs = lax.fori_loop(0, tl, scan_step, tuple(carry_s[b:b + 1, :] for b in range(nb)), unroll=8)
    for b in range(nb):
        carry_s[b:b + 1, :] = hs[b]

    if reverse:
        for b in range(nb):
            y = (hf_ref[b] + h_s[b]) * _gelu(gate_ref[b])
            o_ref[b] = _rms(y, gain_ref[...])
    else:
        o_ref[...] = h_s[...]


def _lru_pass(reverse, zc3, params, extra):
    nb, seq, _ = zc3.shape
    tl = LRU_TILE
    nt = seq // tl
    hb = tl // HALO
    nhb = seq // HALO
    tix = (lambda i: nt - 1 - i) if reverse else (lambda i: i)
    cur = pl.BlockSpec((nb, tl, D_LRU), lambda i: (0, tix(i), 0))
    prev = pl.BlockSpec((nb, HALO, D_LRU), lambda i: (0, jnp.maximum(tix(i) * hb - 1, 0), 0))
    nxt = pl.BlockSpec((nb, HALO, D_LRU), lambda i: (0, jnp.minimum((tix(i) + 1) * hb, nhb - 1), 0))
    full = lambda a: pl.BlockSpec(a.shape, lambda i: (0,) * a.ndim)
    ins = [zc3, zc3, zc3, *params]
    in_specs = [prev, cur, nxt] + [full(a) for a in params]
    if reverse:
        hf, gate, gain = extra
        ins += [hf, gate, gain]
        in_specs += [cur, cur, full(gain)]
    return pl.pallas_call(
        functools.partial(_lru_body, reverse),
        grid=(nt,), in_specs=in_specs, out_specs=cur,
        out_shape=jax.ShapeDtypeStruct((nb, seq, D_LRU), F32),
        scratch_shapes=[pltpu.VMEM((tl + 2 * HALO, D_LRU), F32),
                        pltpu.VMEM((nb, tl, D_LRU), F32),
                        pltpu.VMEM((nb, tl, D_LRU), F32),
                        pltpu.VMEM((nb, tl, D_LRU), F32),
                        pltpu.VMEM((8, D_LRU), F32)],
        compiler_params=_cparams(("arbitrary",)),
        name="lru_bwd" if reverse else "lru_fwd")(*ins)


def _block_diag(w):
    h, d, _ = w.shape
    eye = jnp.eye(h, dtype=w.dtype)
    return (eye[:, None, :, None] * w[:, :, None, :]).reshape(h * d, h * d)


def _lru_mixer(zc, zg, batch, seq, conv_w, conv_b, w_a, b_a, w_x, b_x, lam, gain):
    zc3 = zc.reshape(batch, seq, D_LRU)
    zg3 = zg.reshape(batch, seq, D_LRU)
    outs = None
    for d in (0, 1):
        wg = jnp.concatenate([_block_diag(w_a[d]), _block_diag(w_x[d])], axis=1).astype(BF16)
        bg = jnp.concatenate([b_a[d], b_x[d]])[None, :]
        sp = jax.nn.softplus(-lam[d].astype(F32))[None, :]
        params = [conv_w, conv_b[None, :], wg, bg, sp]
        extra = None if d == 0 else (outs, zg3, gain[None, :])
        outs = _lru_pass(d == 1, zc3, params, extra)
    return outs.reshape(batch * seq, D_LRU)


def _mixout_body(x_ref, oa_ref, ys_ref, ub_ref, oc_ref, sd_ref, wglu_ref, bglu_ref, gb_ref,
                 wout_ref, g2_ref, rw_ref, rb_ref, xo_ref, h_ref, ti_ref, gt_ref):
    tm = x_ref.shape[0]
    yb = ys_ref[...] + sd_ref[...] * ub_ref[...]
    gl = jnp.dot(_gelu(yb).astype(BF16), wglu_ref[...], preferred_element_type=F32) + bglu_ref[...]
    ob = _rms(gl[:, :D_S5] * _sigmoid(gl[:, D_S5:]), gb_ref[...])
    o = jnp.concatenate([oa_ref[...], ob, oc_ref[...]], axis=1).astype(BF16)
    x = x_ref[...] + jnp.dot(o, wout_ref[...], preferred_element_type=F32)
    xo_ref[...] = x
    h = _rms(x, g2_ref[...])
    h_ref[...] = h.astype(BF16)
    logits = jnp.dot(h, rw_ref[...], preferred_element_type=F32, precision=lax.Precision.HIGHEST) + rb_ref[...]
    lane = lax.broadcasted_iota(jnp.int32, (tm, N_EXPERTS), 1).astype(F32)
    lane_k = lax.broadcasted_iota(jnp.int32, (tm, TOP_K), 1)
    idx_out = jnp.zeros((tm, TOP_K), F32)
    val_out = jnp.zeros((tm, TOP_K), F32)
    top = None
    for k in range(TOP_K):
        m = jnp.max(logits, axis=-1, keepdims=True)
        idx = jnp.min(jnp.where(logits == m, lane, float(N_EXPERTS)), axis=-1, keepdims=True)
        top = m if top is None else top
        idx_out = jnp.where(lane_k == k, idx, idx_out)
        val_out = jnp.where(lane_k == k, jnp.exp(m - top), val_out)
        logits = jnp.where(lane == idx, -jnp.inf, logits)
    ti_ref[...] = idx_out.astype(jnp.int32)
    gt_ref[...] = val_out / jnp.sum(val_out, axis=-1, keepdims=True)


def _mixout(x, oa, ys5, zb, oc, s5_d, wglu, bglu, gain_b, wout, g2, rw, rb):
    t = x.shape[0]
    tm = ROW_TILE
    row = lambda n: pl.BlockSpec((tm, n), lambda i: (i, 0))
    full = lambda a: pl.BlockSpec(a.shape, lambda i: (0,) * a.ndim)
    params = [s5_d, wglu, bglu, gain_b, wout, g2, rw, rb]
    return pl.pallas_call(
        _mixout_body,
        grid=(t // tm,),
        in_specs=[row(D_MODEL), row(D_POOL), row(D_S5), row(D_S5), row(D_LRU)] + [full(a) for a in params],
        out_specs=[row(D_MODEL), row(D_MODEL), row(TOP_K), row(TOP_K)],
        out_shape=[jax.ShapeDtypeStruct((t, D_MODEL), F32), jax.ShapeDtypeStruct((t, D_MODEL), BF16),
                   jax.ShapeDtypeStruct((t, TOP_K), jnp.int32), jax.ShapeDtypeStruct((t, TOP_K), F32)],
        compiler_params=_cparams(("parallel",)), name="mix_out")(x, oa, ys5, zb, oc, *params)


def _expert_body(be_ref, nu_ref, x_ref, wg_ref, wu_ref, bg_ref, bu_ref, wd_ref, bd_ref, y_ref):
    @pl.when(pl.program_id(0) < nu_ref[0])
    def _():
        x = x_ref[...]
        g = jnp.dot(x, wg_ref[0], preferred_element_type=F32) + bg_ref[0]
        u = jnp.dot(x, wu_ref[0], preferred_element_type=F32) + bu_ref[0]
        g = jnp.minimum(g, SWIGLU_LIMIT)
        u = jnp.clip(u, -SWIGLU_LIMIT, SWIGLU_LIMIT)
        act = ((u + 1.0) * (g * _sigmoid(g * SWIGLU_ALPHA))).astype(BF16)
        y = jnp.dot(act, wd_ref[0], preferred_element_type=F32) + bd_ref[0]
        y_ref[...] = y.astype(y_ref.dtype)

    @pl.when(pl.program_id(0) >= nu_ref[0])
    def _():
        y_ref[...] = jnp.zeros_like(y_ref)


def _expert_mlp(xs, blk_e, n_used, wg, wu, bg, bu, wd, bd):
    n_rows = xs.shape[0]
    bm = MOE_ROWS
    rows = pl.BlockSpec((bm, D_MODEL), lambda i, be, nu: (jnp.minimum(i, nu[0] - 1), 0))
    per_e = lambda a: pl.BlockSpec((1,) + a.shape[1:], lambda i, be, nu: (be[i], 0, 0))
    grid_spec = pltpu.PrefetchScalarGridSpec(
        num_scalar_prefetch=2, grid=(n_rows // bm,),
        in_specs=[rows, per_e(wg), per_e(wu), per_e(bg), per_e(bu), per_e(wd), per_e(bd)],
        out_specs=pl.BlockSpec((bm, D_MODEL), lambda i, be, nu: (i, 0)))
    return pl.pallas_call(
        _expert_body, grid_spec=grid_spec,
        out_shape=jax.ShapeDtypeStruct((n_rows, D_MODEL), BF16),
        compiler_params=_cparams(("arbitrary",)), name="expert_mlp")(blk_e, n_used, xs, wg, wu, bg, bu, wd, bd)


def _deint_body(w_ref, p_ref, wg_ref, wu_ref):
    p = p_ref[...]
    half = MXU_TILE // 2
    for j in range(w_ref.shape[2] // MXU_TILE):
        w = w_ref[0, :, MXU_TILE * j:MXU_TILE * (j + 1)].astype(BF16)
        r = jnp.dot(w, p, preferred_element_type=F32)
        wg_ref[0, :, half * j:half * (j + 1)] = r[:, :half].astype(BF16)
        wu_ref[0, :, half * j:half * (j + 1)] = r[:, half:].astype(BF16)


def _deinterleave_gate_up(wgu):
    e, d, f2 = wgu.shape
    tk = ROW_TILE
    idx = jnp.arange(MXU_TILE)
    perm = jax.nn.one_hot(jnp.where(idx % 2 == 0, idx // 2, MXU_TILE // 2 + idx // 2), MXU_TILE, dtype=BF16)
    out = pl.BlockSpec((1, tk, f2 // 2), lambda i, k: (i, k, 0))
    return pl.pallas_call(
        _deint_body, grid=(e, d // tk),
        in_specs=[pl.BlockSpec((1, tk, f2), lambda i, k: (i, k, 0)), pl.BlockSpec(perm.shape, lambda i, k: (0, 0))],
        out_specs=[out, out],
        out_shape=[jax.ShapeDtypeStruct((e, d, f2 // 2), BF16)] * 2,
        compiler_params=_cparams(("parallel", "parallel")), name="deinterleave_gate_up")(wgu, perm)


def _moe_dispatch(top_i, t):
    bm = MOE_ROWS
    n_blocks = (t * TOP_K) // bm + N_EXPERTS
    onehot = jax.nn.one_hot(top_i, N_EXPERTS, dtype=jnp.int32)
    sel = jnp.sum(onehot, axis=1)
    csum = jnp.cumsum(sel, axis=0)
    counts = csum[-1]
    padded = ((counts + bm - 1) // bm) * bm
    pad_end = jnp.cumsum(padded)
    pad_start = pad_end - padded
    base = (csum - sel) + pad_start[None, :]
    dest = jnp.sum(onehot * base[:, None, :], axis=-1)
    tok = jnp.broadcast_to(jnp.arange(t, dtype=jnp.int32)[:, None], (t, TOP_K))
    row_tok = jnp.zeros((n_blocks * bm,), jnp.int32).at[dest.reshape(-1)].set(
        tok.reshape(-1), mode="promise_in_bounds", unique_indices=True)
    blk_start = jnp.arange(n_blocks, dtype=jnp.int32) * bm
    blk_e = jnp.minimum(jnp.sum((pad_end[None, :] <= blk_start[:, None]).astype(jnp.int32), axis=1),
                        N_EXPERTS - 1)
    n_used = (pad_end[-1] // bm).astype(jnp.int32).reshape(1)
    return dest, row_tok, blk_e, n_used


def _final_body(x_ref, y0, y1, y2, y3, gt_ref, g_ref, o_ref):
    o_ref[...] = _rms(_combine(x_ref, (y0, y1, y2, y3), gt_ref), g_ref[...])


def _final(x, ys, gates, g):
    t = x.shape[0]
    tm = ROW_TILE
    row = lambda n: pl.BlockSpec((tm, n), lambda i: (i, 0))
    return pl.pallas_call(
        _final_body, grid=(t // tm,),
        in_specs=[row(D_MODEL)] * (1 + TOP_K) + [row(TOP_K), pl.BlockSpec(g.shape, lambda i: (0, 0))],
        out_specs=row(D_MODEL), out_shape=jax.ShapeDtypeStruct((t, D_MODEL), F32),
        compiler_params=_cparams(("parallel",)), name="final_norm")(x, *ys, gates, g)


def kernel(x, norm1_g, w_in, pool_w, pool_scale, s5_lam_re, s5_lam_im, s5_log_step, s5_b_re, s5_b_im, s5_c_re, s5_c_im, s5_d, s5_w_glu, s5_b_glu, lru_conv_w, lru_conv_b, lru_w_a, lru_b_a, lru_w_x, lru_b_x, lru_lam, mix_gain, w_out, norm2_g, router_w, router_b, w_gate_up, b_gate_up, w_down, b_down, final_g):
    batch, seq, d = x.shape
    t = batch * seq
    depth = norm1_g.shape[0]
    xt = x.reshape(t, d)
    add = None
    o1, o2 = D_POOL, D_POOL + D_S5
    for l in range(depth):
        if add is None:
            za, zb, zc, zg = _inproj(xt, None, norm1_g[l][None, :], w_in[l].astype(BF16))
        else:
            xt, za, zb, zc, zg = _inproj(xt, add, norm1_g[l][None, :], w_in[l].astype(BF16))
        oa = _pool_mixer(za, batch, seq, _block_diag(pool_w[l]).astype(BF16), pool_scale[l][None, :],
                         mix_gain[l][None, :o1])
        tables = _s5_tables(s5_lam_re[l], s5_lam_im[l], s5_log_step[l], s5_b_re[l], s5_b_im[l],
                            s5_c_re[l], s5_c_im[l], seq // S5_CHUNK)
        ys5 = _s5_core(zb, batch, seq, tables)
        oc = _lru_mixer(zc, zg, batch, seq, lru_conv_w[l], lru_conv_b[l], lru_w_a[l], lru_b_a[l],
                        lru_w_x[l], lru_b_x[l], lru_lam[l], mix_gain[l][o2:])
        xt, h2, top_i, gates = _mixout(
            xt, oa, ys5, zb, oc, s5_d[l][None, :], s5_w_glu[l].astype(BF16), s5_b_glu[l][None, :],
            mix_gain[l][None, o1:o2], w_out[l].astype(BF16), norm2_g[l][None, :], router_w[l], router_b[l][None, :])
        dest, row_tok, blk_e, n_used = _moe_dispatch(top_i, t)
        xs = h2.at[row_tok].get(mode="promise_in_bounds")
        wg, wu = _deinterleave_gate_up(w_gate_up[l])
        ys = _expert_mlp(xs, blk_e, n_used, wg, wu,
                         b_gate_up[l][:, None, 0::2], b_gate_up[l][:, None, 1::2],
                         w_down[l].astype(BF16), b_down[l][:, None, :])
        add = ([ys.at[dest[:, k]].get(mode="promise_in_bounds") for k in range(TOP_K)], gates)
    out = _final(xt, add[0], add[1], final_g[None, :])
    return out.reshape(batch, seq, d)
```

```python
import functools
import math

import jax
import jax.numpy as jnp
from jax import lax
from jax.experimental import pallas as pl
from jax.experimental.pallas import tpu as pltpu

F32 = jnp.float32
BF16 = jnp.bfloat16

D_MODEL = 1024
D_POOL = 256
D_S5 = 384
D_LRU = 384
D_IN = D_POOL + D_S5 + 2 * D_LRU
POOL_WINDOWS = (2, 4, 8, 16)
POOL_GROUP = 64
S5_GROUP = 16
S5_NGROUPS = 24
S5_STATE = 64
LRU_HEADS = 6
LRU_HEAD_DIM = 64
RG_C = 8.0
N_EXPERTS = 32
TOP_K = 4
D_FF = 1024
SWIGLU_LIMIT = 7.0
SWIGLU_ALPHA = 1.702
EPS = 1e-5

ROW_TILE = 512
S5_CHUNK = 64
POOL_TILE = 512
LRU_TILE = 256
HALO = 8
MOE_ROWS = 256
MXU_TILE = 256
EXPERT_VMEM_LIMIT = 56 * 1024 * 1024
VMEM_LIMIT = 48 * 1024 * 1024


def _cparams(sem):
    return pltpu.CompilerParams(dimension_semantics=sem, vmem_limit_bytes=VMEM_LIMIT)


def _rms(x, g):
    return x * lax.rsqrt(jnp.mean(x * x, axis=-1, keepdims=True) + EPS) * g


def _gelu(x):
    return 0.5 * x * (1.0 + jnp.tanh(0.7978845608028654 * (x + 0.044715 * (x * x * x))))


def _sigmoid(x):
    return 1.0 / (1.0 + jnp.exp(-x))


def _combine(x_ref, y_refs, gt_ref):
    x = x_ref[...]
    gt = gt_ref[...]
    for k in range(TOP_K):
        x = x + gt[:, k:k + 1] * y_refs[k][...].astype(F32)
    return x


def _inproj_body(has_add, *refs):
    if has_add:
        x_ref, y0, y1, y2, y3, gt_ref, g_ref, w_ref, xo_ref, za, zb, zc, zg = refs
        x = _combine(x_ref, (y0, y1, y2, y3), gt_ref)
        xo_ref[...] = x
    else:
        x_ref, g_ref, w_ref, za, zb, zc, zg = refs
        x = x_ref[...]
    h = _rms(x, g_ref[...]).astype(BF16)
    z = jnp.dot(h, w_ref[...], preferred_element_type=F32)
    za[...] = z[:, :D_POOL]
    zb[...] = z[:, D_POOL:D_POOL + D_S5]
    zc[...] = z[:, D_POOL + D_S5:D_POOL + D_S5 + D_LRU]
    zg[...] = z[:, D_POOL + D_S5 + D_LRU:]


def _inproj(x, add, g, w_bf16):
    t = x.shape[0]
    tm = ROW_TILE
    row = lambda n: pl.BlockSpec((tm, n), lambda i: (i, 0))
    full = lambda a: pl.BlockSpec(a.shape, lambda i: (0,) * a.ndim)
    z_shapes = [jax.ShapeDtypeStruct((t, n), F32) for n in (D_POOL, D_S5, D_LRU, D_LRU)]
    z_specs = [row(n) for n in (D_POOL, D_S5, D_LRU, D_LRU)]
    if add is None:
        ins, in_specs = [x, g, w_bf16], [row(D_MODEL), full(g), full(w_bf16)]
        out_shape, out_specs = z_shapes, z_specs
    else:
        ys, gates = add
        ins = [x, *ys, gates, g, w_bf16]
        in_specs = [row(D_MODEL)] + [row(D_MODEL)] * TOP_K + [row(TOP_K), full(g), full(w_bf16)]
        out_shape = [jax.ShapeDtypeStruct((t, D_MODEL), F32)] + z_shapes
        out_specs = [row(D_MODEL)] + z_specs
    return pl.pallas_call(
        functools.partial(_inproj_body, add is not None),
        grid=(t // tm,), in_specs=in_specs, out_specs=out_specs, out_shape=out_shape,
        compiler_params=_cparams(("parallel",)), name="inproj")(*ins)


def _pool_body(seq, prev_ref, cur_ref, next_ref, w_ref, sc_ref, gain_ref, o_ref, u_s, s2_s, s4_s, s8_s):
    tl = cur_ref.shape[0]
    i = pl.program_id(1)
    nt = pl.num_programs(1)
    zero8 = jnp.zeros((HALO, D_POOL), F32)
    for buf in (u_s, s2_s, s4_s, s8_s):
        buf[0:HALO, :] = zero8
        buf[tl + 3 * HALO:tl + 4 * HALO, :] = zero8
    u_s[HALO:2 * HALO, :] = jnp.where(i > 0, prev_ref[...], 0.0)
    u_s[2 * HALO:2 * HALO + tl, :] = cur_ref[...]
    u_s[2 * HALO + tl:3 * HALO + tl, :] = jnp.where(i < nt - 1, next_ref[...], 0.0)
    r = tl + 2 * HALO
    s2_s[HALO:HALO + r, :] = u_s[HALO - 1:HALO - 1 + r, :] + u_s[HALO:HALO + r, :]
    s4_s[HALO:HALO + r, :] = s2_s[HALO - 1:HALO - 1 + r, :] + s2_s[HALO + 1:HALO + 1 + r, :]
    s8_s[HALO:HALO + r, :] = s4_s[HALO - 2:HALO - 2 + r, :] + s4_s[HALO + 2:HALO + 2 + r, :]
    o = 2 * HALO
    s16 = s8_s[o - 4:o - 4 + tl, :] + s8_s[o + 4:o + 4 + tl, :]
    s8 = s8_s[o:o + tl, :]
    s4 = s4_s[o:o + tl, :]
    s2 = s2_s[o:o + tl, :]
    u = u_s[o:o + tl, :]
    lane = lax.broadcasted_iota(jnp.int32, (tl, D_POOL), 1)
    tpos = lax.broadcasted_iota(jnp.int32, (tl, D_POOL), 0) + i * tl
    g0, g1, g2 = lane < POOL_GROUP, lane < 2 * POOL_GROUP, lane < 3 * POOL_GROUP
    half = jnp.where(g0, 1, jnp.where(g1, 2, jnp.where(g2, 4, 8)))
    wsum = jnp.where(g0, s2, jnp.where(g1, s4, jnp.where(g2, s8, s16)))
    cnt = (jnp.minimum(tpos + half, seq) - jnp.maximum(tpos - half, 0)).astype(F32)
    d = wsum / cnt - u
    y = jnp.dot(d.astype(BF16), w_ref[...], preferred_element_type=F32) * sc_ref[...]
    o_ref[...] = _rms(y, gain_ref[...])


def _pool_mixer(za, batch, seq, w_bd, scale, gain):
    t = za.shape[0]
    tl = POOL_TILE
    nt = seq // tl
    hb = tl // HALO
    nhb = t // HALO
    cur = pl.BlockSpec((tl, D_POOL), lambda b, i: (b * nt + i, 0))
    prev = pl.BlockSpec((HALO, D_POOL), lambda b, i: (jnp.maximum((b * nt + i) * hb - 1, 0), 0))
    nxt = pl.BlockSpec((HALO, D_POOL), lambda b, i: (jnp.minimum((b * nt + i + 1) * hb, nhb - 1), 0))
    full = lambda a: pl.BlockSpec(a.shape, lambda b, i: (0,) * a.ndim)
    return pl.pallas_call(
        functools.partial(_pool_body, seq),
        grid=(batch, nt),
        in_specs=[prev, cur, nxt, full(w_bd), full(scale), full(gain)],
        out_specs=cur,
        out_shape=jax.ShapeDtypeStruct((t, D_POOL), F32),
        scratch_shapes=[pltpu.VMEM((tl + 4 * HALO, D_POOL), F32)] * 4,
        compiler_params=_cparams(("parallel", "parallel")), name="pool_mixer")(za, za, za, w_bd, scale, gain)


def _s5_tables(lam_re, lam_im, log_step, b_re, b_im, c_re, c_im, n_chunks):
    L = S5_CHUNK
    hp = lax.Precision.HIGHEST
    lr = lam_re.astype(F32)
    li = lam_im.astype(F32)
    dt = jnp.exp(log_step.astype(F32))[..., None]
    jj = jnp.arange(L + 1, dtype=F32)[None, None, :, None]
    ar = (lr * dt)[:, :, None, :]
    ai = (li * dt)[:, :, None, :]
    mag = jnp.exp(jj * ar)
    e_r = mag * jnp.cos(jj * ai)
    e_i = mag * jnp.sin(jj * ai)
    den = lr * lr + li * li
    nr = e_r[:, :, 1, :] - 1.0
    ni = e_i[:, :, 1, :]
    q_r = (nr * lr + ni * li) / den
    q_i = (ni * lr - nr * li) / den
    br = b_re.astype(F32)
    bi = b_im.astype(F32)
    bb_r = q_r[..., None] * br - q_i[..., None] * bi
    bb_i = q_r[..., None] * bi + q_i[..., None] * br
    cr = c_re.astype(F32)
    ci = c_im.astype(F32)
    g_r = e_r[..., None] * bb_r[:, :, None] - e_i[..., None] * bb_i[:, :, None]
    g_i = e_r[..., None] * bb_i[:, :, None] + e_i[..., None] * bb_r[:, :, None]
    kk = (jnp.einsum('dgcp,dgjpk->dgjck', cr, g_r[:, :, :L], precision=hp)
          - jnp.einsum('dgcp,dgjpk->dgjck', ci, g_i[:, :, :L], precision=hp))
    kf, kb = kk[0], kk[1]
    kall = jnp.concatenate([kb[:, :0:-1], (kf[:, :1] + kb[:, :1]), kf[:, 1:],
                            jnp.zeros_like(kf[:, :1])], axis=1)
    tmat = kall.transpose(0, 3, 1, 2).reshape(S5_NGROUPS, S5_GROUP, 2 * L * S5_GROUP)
    wf_r = g_r[0, :, L - 1::-1][:, :L]
    wf_i = g_i[0, :, L - 1::-1][:, :L]
    wb_r = g_r[1, :, :L]
    wb_i = g_i[1, :, :L]
    to_w = lambda a: a.transpose(0, 1, 3, 2).reshape(S5_NGROUPS, L * S5_GROUP, S5_STATE)
    wst = jnp.concatenate([to_w(wf_r), to_w(wb_r), to_w(wf_i), to_w(wb_i)], axis=-1)
    ef_r, ef_i = e_r[0, :, 1:L + 1], e_i[0, :, 1:L + 1]
    eb_r, eb_i = e_r[1, :, L:0:-1], e_i[1, :, L:0:-1]
    def carry_ops(er, ei, c_r, c_i):
        vr = c_r[:, None] * er[:, :, None, :] - c_i[:, None] * ei[:, :, None, :]
        vi = -(c_r[:, None] * ei[:, :, None, :] + c_i[:, None] * er[:, :, None, :])
        to_v = lambda a: a.transpose(0, 3, 1, 2).reshape(S5_NGROUPS, S5_STATE, L * S5_GROUP)
        return to_v(vr), to_v(vi)
    vf_r, vf_i = carry_ops(ef_r, ef_i, cr[0], ci[0])
    vb_r, vb_i = carry_ops(eb_r, eb_i, cr[1], ci[1])
    vmat = jnp.concatenate([vf_r, vb_r, vf_i, vb_i], axis=1)
    n_steps = max(1, int(math.ceil(math.log2(n_chunks))))
    kpow = (L * (2 ** jnp.arange(n_steps, dtype=F32)))[None, None, :, None]
    pmag = jnp.exp(kpow * ar)
    p_r = pmag * jnp.cos(kpow * ai)
    p_i = pmag * jnp.sin(kpow * ai)
    pad = jnp.zeros((S5_NGROUPS, 8 - n_steps % 8 if n_steps % 8 else 0, 2 * S5_STATE), F32)
    p_r = jnp.concatenate([jnp.concatenate([p_r[0], p_r[1]], axis=-1), pad], axis=1)
    p_i = jnp.concatenate([jnp.concatenate([p_i[0], p_i[1]], axis=-1), pad], axis=1)
    ptab = jnp.concatenate([p_r, p_i], axis=1)
    return tmat, wst.astype(BF16), vmat.astype(BF16), ptab, n_steps


def _s5_body(n_chunks, n_steps, u_ref, t_ref, w_ref, v_ref, p_ref, y_ref, toep_s):
    u = u_ref[0]
    n = u.shape[0]
    lags = t_ref[0]
    width = lags.shape[1]
    blocks_per_tile = 128 // S5_GROUP
    for res in range(blocks_per_tile):
        rot = lags if res == 0 else pltpu.roll(lags, width - res * S5_GROUP, 1)
        for s in range(S5_CHUNK):
            lag0 = S5_CHUNK - 1 - s
            if lag0 % blocks_per_tile == res:
                q = (lag0 // blocks_per_tile) * 128
                toep_s[s * S5_GROUP:(s + 1) * S5_GROUP, :] = rot[:, q:q + S5_CHUNK * S5_GROUP].astype(BF16)
    y = jnp.dot(u, toep_s[...], preferred_element_type=F32)
    s = jnp.dot(u, w_ref[0], preferred_element_type=F32)
    two_p = 2 * S5_STATE
    xr = s[:, :two_p]
    xi = s[:, two_p:]
    row = lax.broadcasted_iota(jnp.int32, (n, two_p), 0) % n_chunks
    is_fwd = lax.broadcasted_iota(jnp.int32, (n, two_p), 1) < S5_STATE
    pim0 = p_ref.shape[1] // 2

    def shifted(a, k):
        down = jnp.where(row >= k, pltpu.roll(a, k, 0), 0.0)
        up = jnp.where(row < n_chunks - k, pltpu.roll(a, n - k, 0), 0.0)
        return jnp.where(is_fwd, down, up)

    for j in range(n_steps):
        k = 1 << j
        pr = p_ref[0, j:j + 1, :]
        pi = p_ref[0, pim0 + j:pim0 + j + 1, :]
        sr = shifted(xr, k)
        si = shifted(xi, k)
        xr, xi = xr + pr * sr - pi * si, xi + pr * si + pi * sr
    carry = jnp.concatenate([shifted(xr, 1), shifted(xi, 1)], axis=1).astype(BF16)
    y_ref[0] = y + jnp.dot(carry, v_ref[0], preferred_element_type=F32)


def _s5_core(zb, batch, seq, tables):
    tmat, wst, vmat, ptab, n_steps = tables
    L = S5_CHUNK
    n_chunks = seq // L
    n = batch * n_chunks
    ug = zb.astype(BF16).reshape(n, L, S5_NGROUPS, S5_GROUP).transpose(2, 0, 1, 3).reshape(S5_NGROUPS, n, L * S5_GROUP)
    per_g = lambda a: pl.BlockSpec((1,) + a.shape[1:], lambda g: (g, 0, 0))
    yg = pl.pallas_call(
        functools.partial(_s5_body, n_chunks, n_steps),
        grid=(S5_NGROUPS,),
        in_specs=[per_g(ug), per_g(tmat), per_g(wst), per_g(vmat), per_g(ptab)],
        out_specs=pl.BlockSpec((1, n, L * S5_GROUP), lambda g: (g, 0, 0)),
        out_shape=jax.ShapeDtypeStruct((S5_NGROUPS, n, L * S5_GROUP), F32),
        scratch_shapes=[pltpu.VMEM((L * S5_GROUP, L * S5_GROUP), BF16)],
        compiler_params=_cparams(("parallel",)), name="s5_core")(ug, tmat, wst, vmat, ptab)
    return yg.reshape(S5_NGROUPS, n, L, S5_GROUP).transpose(1, 2, 0, 3).reshape(batch * seq, D_S5)


def _lru_body(reverse, prev_ref, cur_ref, next_ref, cw_ref, cb_ref, wg_ref, bg_ref, sp_ref, *rest):
    if reverse:
        hf_ref, gate_ref, gain_ref, o_ref, x_s, a_s, b_s, h_s, carry_s = rest
    else:
        o_ref, x_s, a_s, b_s, h_s, carry_s = rest
    nb, tl, _ = cur_ref.shape
    step = pl.program_id(0)
    nt = pl.num_programs(0)
    ti = nt - 1 - step if reverse else step

    @pl.when(step == 0)
    def _():
        carry_s[...] = jnp.zeros_like(carry_s)

    cw = cw_ref[...]
    for b in range(nb):
        x_s[0:HALO, :] = jnp.where(ti > 0, prev_ref[b], 0.0)
        x_s[HALO:HALO + tl, :] = cur_ref[b]
        x_s[HALO + tl:2 * HALO + tl, :] = jnp.where(ti < nt - 1, next_ref[b], 0.0)
        xc = cb_ref[...]
        for k in range(4):
            xc = xc + cw[k:k + 1, :] * x_s[HALO - 1 + k:HALO - 1 + k + tl, :]
        gates = jnp.dot(xc.astype(BF16), wg_ref[...], preferred_element_type=F32) + bg_ref[...]
        r = _sigmoid(gates[:, :D_LRU])
        ig = _sigmoid(gates[:, D_LRU:])
        log_a = (-RG_C) * r * sp_ref[...]
        a = jnp.exp(log_a)
        a_s[b] = a
        b_s[b] = jnp.sqrt(-jnp.tanh(log_a) * (1.0 + a * a)) * (ig * xc)

    def scan_step(s, hs):
        t = tl - 1 - s if reverse else s
        out = []
        for b in range(nb):
            h = a_s[b, pl.ds(t, 1), :] * hs[b] + b_s[b, pl.ds(t, 1), :]
            h_s[b, pl.ds(t, 1), :] = h
            out.append(h)
        return tuple(out)

    hs = lax.fori_loop(0, tl, scan_step, tuple(carry_s[b:b + 1, :] for b in range(nb)), unroll=8)
    for b in range(nb):
        carry_s[b:b + 1, :] = hs[b]

    if reverse:
        for b in range(nb):
            y = (hf_ref[b] + h_s[b]) * _gelu(gate_ref[b])
            o_ref[b] = _rms(y, gain_ref[...])
    else:
        o_ref[...] = h_s[...]


def _lru_pass(reverse, zc3, params, extra):
    nb, seq, _ = zc3.shape
    tl = LRU_TILE
    nt = seq // tl
    hb = tl // HALO
    nhb = seq // HALO
    tix = (lambda i: nt - 1 - i) if reverse else (lambda i: i)
    cur = pl.BlockSpec((nb, tl, D_LRU), lambda i: (0, tix(i), 0))
    prev = pl.BlockSpec((nb, HALO, D_LRU), lambda i: (0, jnp.maximum(tix(i) * hb - 1, 0), 0))
    nxt = pl.BlockSpec((nb, HALO, D_LRU), lambda i: (0, jnp.minimum((tix(i) + 1) * hb, nhb - 1), 0))
    full = lambda a: pl.BlockSpec(a.shape, lambda i: (0,) * a.ndim)
    ins = [zc3, zc3, zc3, *params]
    in_specs = [prev, cur, nxt] + [full(a) for a in params]
    if reverse:
        hf, gate, gain = extra
        ins += [hf, gate, gain]
        in_specs += [cur, cur, full(gain)]
    return pl.pallas_call(
        functools.partial(_lru_body, reverse),
        grid=(nt,), in_specs=in_specs, out_specs=cur,
        out_shape=jax.ShapeDtypeStruct((nb, seq, D_LRU), F32),
        scratch_shapes=[pltpu.VMEM((tl + 2 * HALO, D_LRU), F32),
                        pltpu.VMEM((nb, tl, D_LRU), F32),
                        pltpu.VMEM((nb, tl, D_LRU), F32),
                        pltpu.VMEM((nb, tl, D_LRU), F32),
                        pltpu.VMEM((8, D_LRU), F32)],
        compiler_params=_cparams(("arbitrary",)),
        name="lru_bwd" if reverse else "lru_fwd")(*ins)


def _block_diag(w):
    h, d, _ = w.shape
    eye = jnp.eye(h, dtype=w.dtype)
    return (eye[:, None, :, None] * w[:, :, None, :]).reshape(h * d, h * d)


def _lru_mixer(zc, zg, batch, seq, conv_w, conv_b, w_a, b_a, w_x, b_x, lam, gain):
    zc3 = zc.reshape(batch, seq, D_LRU)
    zg3 = zg.reshape(batch, seq, D_LRU)
    outs = None
    for d in (0, 1):
        wg = jnp.concatenate([_block_diag(w_a[d]), _block_diag(w_x[d])], axis=1).astype(BF16)
        bg = jnp.concatenate([b_a[d], b_x[d]])[None, :]
        sp = jax.nn.softplus(-lam[d].astype(F32))[None, :]
        params = [conv_w, conv_b[None, :], wg, bg, sp]
        extra = None if d == 0 else (outs, zg3, gain[None, :])
        outs = _lru_pass(d == 1, zc3, params, extra)
    return outs.reshape(batch * seq, D_LRU)


def _mixout_body(x_ref, oa_ref, ys_ref, ub_ref, oc_ref, sd_ref, wglu_ref, bglu_ref, gb_ref,
                 wout_ref, g2_ref, rw_ref, rb_ref, xo_ref, h_ref, ti_ref, gt_ref):
    tm = x_ref.shape[0]
    yb = ys_ref[...] + sd_ref[...] * ub_ref[...]
    gl = jnp.dot(_gelu(yb).astype(BF16), wglu_ref[...], preferred_element_type=F32) + bglu_ref[...]
    ob = _rms(gl[:, :D_S5] * _sigmoid(gl[:, D_S5:]), gb_ref[...])
    o = jnp.concatenate([oa_ref[...], ob, oc_ref[...]], axis=1).astype(BF16)
    x = x_ref[...] + jnp.dot(o, wout_ref[...], preferred_element_type=F32)
    xo_ref[...] = x
    h = _rms(x, g2_ref[...])
    h_hi = h.astype(BF16)
    h_ref[...] = h_hi
    h_lo = (h - h_hi.astype(F32)).astype(BF16)
    nt_dims = (((1,), (1,)), ((), ()))
    la = lax.dot_general(rw_ref[...], h_hi, nt_dims, preferred_element_type=F32)
    lb = lax.dot_general(rw_ref[0:N_EXPERTS, :], h_lo, nt_dims, preferred_element_type=F32)
    logits = la[:N_EXPERTS] + la[N_EXPERTS:] + lb + rb_ref[...]
    eidx = lax.broadcasted_iota(jnp.int32, (N_EXPERTS, tm), 0).astype(F32)
    idx_rows, val_rows = [], []
    for k in range(TOP_K):
        m = jnp.max(logits, axis=0, keepdims=True)
        idx = jnp.min(jnp.where(logits == m, eidx, float(N_EXPERTS)), axis=0, keepdims=True)
        idx_rows.append(idx)
        val_rows.append(jnp.exp(m - val_rows[0]) if k else m)
        logits = jnp.where(eidx == idx, -jnp.inf, logits)
    top = val_rows[0]
    vals = jnp.concatenate([jnp.ones_like(top)] + [v for v in val_rows[1:]], axis=0)
    ti_ref[...] = jnp.concatenate(idx_rows, axis=0).astype(jnp.int32)
    gt_ref[...] = vals / jnp.sum(vals, axis=0, keepdims=True)


def _mixout(x, oa, ys5, zb, oc, s5_d, wglu, bglu, gain_b, wout, g2, rw, rb):
    t = x.shape[0]
    tm = ROW_TILE
    row = lambda n: pl.BlockSpec((tm, n), lambda i: (i, 0))
    col = pl.BlockSpec((TOP_K, tm), lambda i: (0, i))
    full = lambda a: pl.BlockSpec(a.shape, lambda i: (0,) * a.ndim)
    rw_hi = rw.astype(BF16)
    rw_lo = (rw - rw_hi.astype(F32)).astype(BF16)
    rw2 = jnp.concatenate([rw_hi.T, rw_lo.T], axis=0)
    params = [s5_d, wglu, bglu, gain_b, wout, g2, rw2, rb.reshape(N_EXPERTS, 1)]
    return pl.pallas_call(
        _mixout_body,
        grid=(t // tm,),
        in_specs=[row(D_MODEL), row(D_POOL), row(D_S5), row(D_S5), row(D_LRU)] + [full(a) for a in params],
        out_specs=[row(D_MODEL), row(D_MODEL), col, col],
        out_shape=[jax.ShapeDtypeStruct((t, D_MODEL), F32), jax.ShapeDtypeStruct((t, D_MODEL), BF16),
                   jax.ShapeDtypeStruct((TOP_K, t), jnp.int32), jax.ShapeDtypeStruct((TOP_K, t), F32)],
        compiler_params=_cparams(("parallel",)), name="mix_out")(x, oa, ys5, zb, oc, *params)


def _expert_body(be_ref, nu_ref, first_ref, x_ref, wgu_ref, bg_ref, bu_ref, wd_ref, bd_ref, p_ref, y_ref,
                 wg_s, wu_s, wd_s):
    i = pl.program_id(0)
    live = i < nu_ref[0]

    @pl.when(jnp.logical_and(live, first_ref[i] == 1))
    def _():
        p = p_ref[...]
        half = MXU_TILE // 2
        for j in range(wgu_ref.shape[3] // MXU_TILE):
            w = wgu_ref[0, 0, :, MXU_TILE * j:MXU_TILE * (j + 1)].astype(BF16)
            r = jnp.dot(w, p, preferred_element_type=F32)
            wg_s[:, half * j:half * (j + 1)] = r[:, :half].astype(BF16)
            wu_s[:, half * j:half * (j + 1)] = r[:, half:].astype(BF16)
        wd_s[...] = wd_ref[0, 0].astype(BF16)

    @pl.when(live)
    def _():
        x = x_ref[...]
        g = jnp.dot(x, wg_s[...], preferred_element_type=F32) + bg_ref[0, 0]
        u = jnp.dot(x, wu_s[...], preferred_element_type=F32) + bu_ref[0, 0]
        g = jnp.minimum(g, SWIGLU_LIMIT)
        u = jnp.clip(u, -SWIGLU_LIMIT, SWIGLU_LIMIT)
        act = ((u + 1.0) * (g * _sigmoid(g * SWIGLU_ALPHA))).astype(BF16)
        y = jnp.dot(act, wd_s[...], preferred_element_type=F32) + bd_ref[0, 0]
        y_ref[...] = y.astype(y_ref.dtype)

    @pl.when(jnp.logical_not(live))
    def _():
        y_ref[...] = jnp.zeros_like(y_ref)


def _expert_mlp(layer, xs, blk_e, n_used, w_gate_up, b_gate, b_up, w_down, b_down):
    n_rows = xs.shape[0]
    bm = MOE_ROWS
    first = jnp.concatenate([jnp.ones((1,), jnp.int32), (blk_e[1:] != blk_e[:-1]).astype(jnp.int32)])
    idx = jnp.arange(MXU_TILE)
    perm = jax.nn.one_hot(jnp.where(idx % 2 == 0, idx // 2, MXU_TILE // 2 + idx // 2), MXU_TILE, dtype=BF16)
    rows = pl.BlockSpec((bm, D_MODEL), lambda i, be, nu, fi: (jnp.minimum(i, nu[0] - 1), 0))
    per_e = lambda a: pl.BlockSpec((1, 1) + a.shape[2:], lambda i, be, nu, fi: (layer, be[i], 0, 0))
    grid_spec = pltpu.PrefetchScalarGridSpec(
        num_scalar_prefetch=3, grid=(n_rows // bm,),
        in_specs=[rows, per_e(w_gate_up), per_e(b_gate), per_e(b_up), per_e(w_down), per_e(b_down),
                  pl.BlockSpec(perm.shape, lambda i, be, nu, fi: (0, 0))],
        out_specs=pl.BlockSpec((bm, D_MODEL), lambda i, be, nu, fi: (i, 0)),
        scratch_shapes=[pltpu.VMEM((D_MODEL, D_FF), BF16), pltpu.VMEM((D_MODEL, D_FF), BF16),
                        pltpu.VMEM((D_FF, D_MODEL), BF16)])
    return pl.pallas_call(
        _expert_body, grid_spec=grid_spec,
        out_shape=jax.ShapeDtypeStruct((n_rows, D_MODEL), BF16),
        compiler_params=pltpu.CompilerParams(dimension_semantics=("arbitrary",),
                                             vmem_limit_bytes=EXPERT_VMEM_LIMIT),
        name="expert_mlp")(blk_e, n_used, first, xs, w_gate_up, b_gate, b_up, w_down, b_down, perm)


def _moe_dispatch(top_i, t):
    bm = MOE_ROWS
    n_blocks = (t * TOP_K) // bm + N_EXPERTS
    onehot = jax.nn.one_hot(top_i, N_EXPERTS, dtype=jnp.int32)
    sel = jnp.sum(onehot, axis=1)
    csum = jnp.cumsum(sel, axis=0)
    counts = csum[-1]
    padded = ((counts + bm - 1) // bm) * bm
    pad_end = jnp.cumsum(padded)
    pad_start = pad_end - padded
    base = (csum - sel) + pad_start[None, :]
    dest = jnp.sum(onehot * base[:, None, :], axis=-1)
    tok = jnp.broadcast_to(jnp.arange(t, dtype=jnp.int32)[:, None], (t, TOP_K))
    row_tok = jnp.zeros((n_blocks * bm,), jnp.int32).at[dest.reshape(-1)].set(
        tok.reshape(-1), mode="promise_in_bounds", unique_indices=True)
    blk_start = jnp.arange(n_blocks, dtype=jnp.int32) * bm
    blk_e = jnp.minimum(jnp.sum((pad_end[None, :] <= blk_start[:, None]).astype(jnp.int32), axis=1),
                        N_EXPERTS - 1)
    n_used = (pad_end[-1] // bm).astype(jnp.int32).reshape(1)
    return dest, row_tok, blk_e, n_used


def _final_body(x_ref, y0, y1, y2, y3, gt_ref, g_ref, o_ref):
    o_ref[...] = _rms(_combine(x_ref, (y0, y1, y2, y3), gt_ref), g_ref[...])


def _final(x, ys, gates, g):
    t = x.shape[0]
    tm = ROW_TILE
    row = lambda n: pl.BlockSpec((tm, n), lambda i: (i, 0))
    return pl.pallas_call(
        _final_body, grid=(t // tm,),
        in_specs=[row(D_MODEL)] * (1 + TOP_K) + [row(TOP_K), pl.BlockSpec(g.shape, lambda i: (0, 0))],
        out_specs=row(D_MODEL), out_shape=jax.ShapeDtypeStruct((t, D_MODEL), F32),
        compiler_params=_cparams(("parallel",)), name="final_norm")(x, *ys, gates, g)


def kernel(x, norm1_g, w_in, pool_w, pool_scale, s5_lam_re, s5_lam_im, s5_log_step, s5_b_re, s5_b_im, s5_c_re, s5_c_im, s5_d, s5_w_glu, s5_b_glu, lru_conv_w, lru_conv_b, lru_w_a, lru_b_a, lru_w_x, lru_b_x, lru_lam, mix_gain, w_out, norm2_g, router_w, router_b, w_gate_up, b_gate_up, w_down, b_down, final_g):
    batch, seq, d = x.shape
    t = batch * seq
    depth = norm1_g.shape[0]
    xt = x.reshape(t, d)
    add = None
    o1, o2 = D_POOL, D_POOL + D_S5
    b_gate = b_gate_up[:, :, None, 0::2]
    b_up = b_gate_up[:, :, None, 1::2]
    b_dn = b_down[:, :, None, :]
    for l in range(depth):
        if add is None:
            za, zb, zc, zg = _inproj(xt, None, norm1_g[l][None, :], w_in[l].astype(BF16))
        else:
            xt, za, zb, zc, zg = _inproj(xt, add, norm1_g[l][None, :], w_in[l].astype(BF16))
        oa = _pool_mixer(za, batch, seq, _block_diag(pool_w[l]).astype(BF16), pool_scale[l][None, :],
                         mix_gain[l][None, :o1])
        tables = _s5_tables(s5_lam_re[l], s5_lam_im[l], s5_log_step[l], s5_b_re[l], s5_b_im[l],
                            s5_c_re[l], s5_c_im[l], seq // S5_CHUNK)
        ys5 = _s5_core(zb, batch, seq, tables)
        oc = _lru_mixer(zc, zg, batch, seq, lru_conv_w[l], lru_conv_b[l], lru_w_a[l], lru_b_a[l],
                        lru_w_x[l], lru_b_x[l], lru_lam[l], mix_gain[l][o2:])
        xt, h2, top_i, gates = _mixout(
            xt, oa, ys5, zb, oc, s5_d[l][None, :], s5_w_glu[l].astype(BF16), s5_b_glu[l][None, :],
            mix_gain[l][None, o1:o2], w_out[l].astype(BF16), norm2_g[l][None, :], router_w[l], router_b[l])
        top_i, gates = top_i.T, gates.T
        dest, row_tok, blk_e, n_used = _moe_dispatch(top_i, t)
        xs = h2.at[row_tok].get(mode="promise_in_bounds")
        ys = _expert_mlp(l, xs, blk_e, n_used, w_gate_up, b_gate, b_up, w_down, b_dn)
        add = ([ys.at[dest[:, k]].get(mode="promise_in_bounds") for k in range(TOP_K)], gates)
    out = _final(xt, add[0], add[1], final_g[None, :])
    return out.reshape(batch, seq, d)
```

```python
import functools
import math

import jax
import jax.numpy as jnp
from jax import lax
from jax.experimental import pallas as pl
from jax.experimental.pallas import tpu as pltpu
from jax.experimental.pallas import tpu_sc as plsc

F32 = jnp.float32
BF16 = jnp.bfloat16

D_MODEL = 1024
D_POOL = 256
D_S5 = 384
D_LRU = 384
D_IN = D_POOL + D_S5 + 2 * D_LRU
POOL_WINDOWS = (2, 4, 8, 16)
POOL_GROUP = 64
S5_GROUP = 16
S5_NGROUPS = 24
S5_STATE = 64
LRU_HEADS = 6
LRU_HEAD_DIM = 64
RG_C = 8.0
N_EXPERTS = 32
TOP_K = 4
D_FF = 1024
SWIGLU_LIMIT = 7.0
SWIGLU_ALPHA = 1.702
EPS = 1e-5

ROW_TILE = 512
S5_CHUNK = 64
POOL_TILE = 512
LRU_TILE = 256
HALO = 8
MOE_ROWS = 256
SC_WINDOW = 128
SC_COL_SPLIT = 2
MXU_TILE = 256
EXPERT_VMEM_LIMIT = 56 * 1024 * 1024
VMEM_LIMIT = 48 * 1024 * 1024


def _cparams(sem):
    return pltpu.CompilerParams(dimension_semantics=sem, vmem_limit_bytes=VMEM_LIMIT)


def _rms(x, g):
    return x * lax.rsqrt(jnp.mean(x * x, axis=-1, keepdims=True) + EPS) * g


def _gelu(x):
    return 0.5 * x * (1.0 + jnp.tanh(0.7978845608028654 * (x + 0.044715 * (x * x * x))))


def _sigmoid(x):
    return 1.0 / (1.0 + jnp.exp(-x))


def _combine(x_ref, y_refs, gt_ref):
    x = x_ref[...]
    gt = gt_ref[...]
    for k in range(TOP_K):
        x = x + gt[:, k:k + 1] * y_refs[k][...].astype(F32)
    return x


def _inproj_body(has_add, *refs):
    if has_add:
        x_ref, y0, y1, y2, y3, gt_ref, g_ref, w_ref, xo_ref, za, zb, zc, zg = refs
        x = _combine(x_ref, (y0, y1, y2, y3), gt_ref)
        xo_ref[...] = x
    else:
        x_ref, g_ref, w_ref, za, zb, zc, zg = refs
        x = x_ref[...]
    h = _rms(x, g_ref[...]).astype(BF16)
    z = jnp.dot(h, w_ref[...], preferred_element_type=F32)
    za[...] = z[:, :D_POOL]
    zb[...] = z[:, D_POOL:D_POOL + D_S5]
    zc[...] = z[:, D_POOL + D_S5:D_POOL + D_S5 + D_LRU]
    zg[...] = z[:, D_POOL + D_S5 + D_LRU:]


def _inproj(x, add, g, w_bf16):
    t = x.shape[0]
    tm = ROW_TILE
    row = lambda n: pl.BlockSpec((tm, n), lambda i: (i, 0))
    full = lambda a: pl.BlockSpec(a.shape, lambda i: (0,) * a.ndim)
    z_shapes = [jax.ShapeDtypeStruct((t, n), F32) for n in (D_POOL, D_S5, D_LRU, D_LRU)]
    z_specs = [row(n) for n in (D_POOL, D_S5, D_LRU, D_LRU)]
    if add is None:
        ins, in_specs = [x, g, w_bf16], [row(D_MODEL), full(g), full(w_bf16)]
        out_shape, out_specs = z_shapes, z_specs
    else:
        ys, gates = add
        ins = [x, *ys, gates, g, w_bf16]
        in_specs = [row(D_MODEL)] + [row(D_MODEL)] * TOP_K + [row(TOP_K), full(g), full(w_bf16)]
        out_shape = [jax.ShapeDtypeStruct((t, D_MODEL), F32)] + z_shapes
        out_specs = [row(D_MODEL)] + z_specs
    return pl.pallas_call(
        functools.partial(_inproj_body, add is not None),
        grid=(t // tm,), in_specs=in_specs, out_specs=out_specs, out_shape=out_shape,
        compiler_params=_cparams(("parallel",)), name="inproj")(*ins)


def _pool_body(seq, prev_ref, cur_ref, next_ref, w_ref, sc_ref, gain_ref, o_ref, u_s, s2_s, s4_s, s8_s):
    tl = cur_ref.shape[0]
    i = pl.program_id(1)
    nt = pl.num_programs(1)
    zero8 = jnp.zeros((HALO, D_POOL), F32)
    for buf in (u_s, s2_s, s4_s, s8_s):
        buf[0:HALO, :] = zero8
        buf[tl + 3 * HALO:tl + 4 * HALO, :] = zero8
    u_s[HALO:2 * HALO, :] = jnp.where(i > 0, prev_ref[...], 0.0)
    u_s[2 * HALO:2 * HALO + tl, :] = cur_ref[...]
    u_s[2 * HALO + tl:3 * HALO + tl, :] = jnp.where(i < nt - 1, next_ref[...], 0.0)
    r = tl + 2 * HALO
    s2_s[HALO:HALO + r, :] = u_s[HALO - 1:HALO - 1 + r, :] + u_s[HALO:HALO + r, :]
    s4_s[HALO:HALO + r, :] = s2_s[HALO - 1:HALO - 1 + r, :] + s2_s[HALO + 1:HALO + 1 + r, :]
    s8_s[HALO:HALO + r, :] = s4_s[HALO - 2:HALO - 2 + r, :] + s4_s[HALO + 2:HALO + 2 + r, :]
    o = 2 * HALO
    s16 = s8_s[o - 4:o - 4 + tl, :] + s8_s[o + 4:o + 4 + tl, :]
    s8 = s8_s[o:o + tl, :]
    s4 = s4_s[o:o + tl, :]
    s2 = s2_s[o:o + tl, :]
    u = u_s[o:o + tl, :]
    lane = lax.broadcasted_iota(jnp.int32, (tl, D_POOL), 1)
    tpos = lax.broadcasted_iota(jnp.int32, (tl, D_POOL), 0) + i * tl
    g0, g1, g2 = lane < POOL_GROUP, lane < 2 * POOL_GROUP, lane < 3 * POOL_GROUP
    half = jnp.where(g0, 1, jnp.where(g1, 2, jnp.where(g2, 4, 8)))
    wsum = jnp.where(g0, s2, jnp.where(g1, s4, jnp.where(g2, s8, s16)))
    cnt = (jnp.minimum(tpos + half, seq) - jnp.maximum(tpos - half, 0)).astype(F32)
    d = wsum / cnt - u
    y = jnp.dot(d.astype(BF16), w_ref[...], preferred_element_type=F32) * sc_ref[...]
    o_ref[...] = _rms(y, gain_ref[...])


def _pool_mixer(za, batch, seq, w_bd, scale, gain):
    t = za.shape[0]
    tl = POOL_TILE
    nt = seq // tl
    hb = tl // HALO
    nhb = t // HALO
    cur = pl.BlockSpec((tl, D_POOL), lambda b, i: (b * nt + i, 0))
    prev = pl.BlockSpec((HALO, D_POOL), lambda b, i: (jnp.maximum((b * nt + i) * hb - 1, 0), 0))
    nxt = pl.BlockSpec((HALO, D_POOL), lambda b, i: (jnp.minimum((b * nt + i + 1) * hb, nhb - 1), 0))
    full = lambda a: pl.BlockSpec(a.shape, lambda b, i: (0,) * a.ndim)
    return pl.pallas_call(
        functools.partial(_pool_body, seq),
        grid=(batch, nt),
        in_specs=[prev, cur, nxt, full(w_bd), full(scale), full(gain)],
        out_specs=cur,
        out_shape=jax.ShapeDtypeStruct((t, D_POOL), F32),
        scratch_shapes=[pltpu.VMEM((tl + 4 * HALO, D_POOL), F32)] * 4,
        compiler_params=_cparams(("parallel", "parallel")), name="pool_mixer")(za, za, za, w_bd, scale, gain)


def _s5_tables(lam_re, lam_im, log_step, b_re, b_im, c_re, c_im, n_chunks):
    L = S5_CHUNK
    hp = lax.Precision.HIGHEST
    lr = lam_re.astype(F32)
    li = lam_im.astype(F32)
    dt = jnp.exp(log_step.astype(F32))[..., None]
    jj = jnp.arange(L + 1, dtype=F32)[None, None, :, None]
    ar = (lr * dt)[:, :, None, :]
    ai = (li * dt)[:, :, None, :]
    mag = jnp.exp(jj * ar)
    e_r = mag * jnp.cos(jj * ai)
    e_i = mag * jnp.sin(jj * ai)
    den = lr * lr + li * li
    nr = e_r[:, :, 1, :] - 1.0
    ni = e_i[:, :, 1, :]
    q_r = (nr * lr + ni * li) / den
    q_i = (ni * lr - nr * li) / den
    br = b_re.astype(F32)
    bi = b_im.astype(F32)
    bb_r = q_r[..., None] * br - q_i[..., None] * bi
    bb_i = q_r[..., None] * bi + q_i[..., None] * br
    cr = c_re.astype(F32)
    ci = c_im.astype(F32)
    g_r = e_r[..., None] * bb_r[:, :, None] - e_i[..., None] * bb_i[:, :, None]
    g_i = e_r[..., None] * bb_i[:, :, None] + e_i[..., None] * bb_r[:, :, None]
    kk = (jnp.einsum('dgcp,dgjpk->dgjck', cr, g_r[:, :, :L], precision=hp)
          - jnp.einsum('dgcp,dgjpk->dgjck', ci, g_i[:, :, :L], precision=hp))
    kf, kb = kk[0], kk[1]
    kall = jnp.concatenate([kb[:, :0:-1], (kf[:, :1] + kb[:, :1]), kf[:, 1:],
                            jnp.zeros_like(kf[:, :1])], axis=1)
    tmat = kall.transpose(0, 3, 1, 2).reshape(S5_NGROUPS, S5_GROUP, 2 * L * S5_GROUP)
    wf_r = g_r[0, :, L - 1::-1][:, :L]
    wf_i = g_i[0, :, L - 1::-1][:, :L]
    wb_r = g_r[1, :, :L]
    wb_i = g_i[1, :, :L]
    to_w = lambda a: a.transpose(0, 1, 3, 2).reshape(S5_NGROUPS, L * S5_GROUP, S5_STATE)
    wst = jnp.concatenate([to_w(wf_r), to_w(wb_r), to_w(wf_i), to_w(wb_i)], axis=-1)
    ef_r, ef_i = e_r[0, :, 1:L + 1], e_i[0, :, 1:L + 1]
    eb_r, eb_i = e_r[1, :, L:0:-1], e_i[1, :, L:0:-1]
    def carry_ops(er, ei, c_r, c_i):
        vr = c_r[:, None] * er[:, :, None, :] - c_i[:, None] * ei[:, :, None, :]
        vi = -(c_r[:, None] * ei[:, :, None, :] + c_i[:, None] * er[:, :, None, :])
        to_v = lambda a: a.transpose(0, 3, 1, 2).reshape(S5_NGROUPS, S5_STATE, L * S5_GROUP)
        return to_v(vr), to_v(vi)
    vf_r, vf_i = carry_ops(ef_r, ef_i, cr[0], ci[0])
    vb_r, vb_i = carry_ops(eb_r, eb_i, cr[1], ci[1])
    vmat = jnp.concatenate([vf_r, vb_r, vf_i, vb_i], axis=1)
    n_steps = max(1, int(math.ceil(math.log2(n_chunks))))
    kpow = (L * (2 ** jnp.arange(n_steps, dtype=F32)))[None, None, :, None]
    pmag = jnp.exp(kpow * ar)
    p_r = pmag * jnp.cos(kpow * ai)
    p_i = pmag * jnp.sin(kpow * ai)
    pad = jnp.zeros((S5_NGROUPS, 8 - n_steps % 8 if n_steps % 8 else 0, 2 * S5_STATE), F32)
    p_r = jnp.concatenate([jnp.concatenate([p_r[0], p_r[1]], axis=-1), pad], axis=1)
    p_i = jnp.concatenate([jnp.concatenate([p_i[0], p_i[1]], axis=-1), pad], axis=1)
    ptab = jnp.concatenate([p_r, p_i], axis=1)
    return tmat, wst.astype(BF16), vmat.astype(BF16), ptab, n_steps


def _s5_body(n_chunks, n_steps, u_ref, t_ref, w_ref, v_ref, p_ref, y_ref, toep_s):
    u = u_ref[0]
    n = u.shape[0]
    lags = t_ref[0]
    width = lags.shape[1]
    blocks_per_tile = 128 // S5_GROUP
    for res in range(blocks_per_tile):
        rot = lags if res == 0 else pltpu.roll(lags, width - res * S5_GROUP, 1)
        for s in range(S5_CHUNK):
            lag0 = S5_CHUNK - 1 - s
            if lag0 % blocks_per_tile == res:
                q = (lag0 // blocks_per_tile) * 128
                toep_s[s * S5_GROUP:(s + 1) * S5_GROUP, :] = rot[:, q:q + S5_CHUNK * S5_GROUP].astype(BF16)
    y = jnp.dot(u, toep_s[...], preferred_element_type=F32)
    s = jnp.dot(u, w_ref[0], preferred_element_type=F32)
    two_p = 2 * S5_STATE
    xr = s[:, :two_p]
    xi = s[:, two_p:]
    row = lax.broadcasted_iota(jnp.int32, (n, two_p), 0) % n_chunks
    is_fwd = lax.broadcasted_iota(jnp.int32, (n, two_p), 1) < S5_STATE
    pim0 = p_ref.shape[1] // 2

    def shifted(a, k):
        down = jnp.where(row >= k, pltpu.roll(a, k, 0), 0.0)
        up = jnp.where(row < n_chunks - k, pltpu.roll(a, n - k, 0), 0.0)
        return jnp.where(is_fwd, down, up)

    for j in range(n_steps):
        k = 1 << j
        pr = p_ref[0, j:j + 1, :]
        pi = p_ref[0, pim0 + j:pim0 + j + 1, :]
        sr = shifted(xr, k)
        si = shifted(xi, k)
        xr, xi = xr + pr * sr - pi * si, xi + pr * si + pi * sr
    carry = jnp.concatenate([shifted(xr, 1), shifted(xi, 1)], axis=1).astype(BF16)
    y_ref[0] = y + jnp.dot(carry, v_ref[0], preferred_element_type=F32)


def _s5_core(zb, batch, seq, tables):
    tmat, wst, vmat, ptab, n_steps = tables
    L = S5_CHUNK
    n_chunks = seq // L
    n = batch * n_chunks
    ug = zb.astype(BF16).reshape(n, L, S5_NGROUPS, S5_GROUP).transpose(2, 0, 1, 3).reshape(S5_NGROUPS, n, L * S5_GROUP)
    per_g = lambda a: pl.BlockSpec((1,) + a.shape[1:], lambda g: (g, 0, 0))
    yg = pl.pallas_call(
        functools.partial(_s5_body, n_chunks, n_steps),
        grid=(S5_NGROUPS,),
        in_specs=[per_g(ug), per_g(tmat), per_g(wst), per_g(vmat), per_g(ptab)],
        out_specs=pl.BlockSpec((1, n, L * S5_GROUP), lambda g: (g, 0, 0)),
        out_shape=jax.ShapeDtypeStruct((S5_NGROUPS, n, L * S5_GROUP), F32),
        scratch_shapes=[pltpu.VMEM((L * S5_GROUP, L * S5_GROUP), BF16)],
        compiler_params=_cparams(("parallel",)), name="s5_core")(ug, tmat, wst, vmat, ptab)
    return yg.reshape(S5_NGROUPS, n, L, S5_GROUP).transpose(1, 2, 0, 3).reshape(batch * seq, D_S5)


def _lru_body(reverse, prev_ref, cur_ref, next_ref, cw_ref, cb_ref, wg_ref, bg_ref, sp_ref, *rest):
    if reverse:
        hf_ref, gate_ref, gain_ref, o_ref, x_s, a_s, b_s, h_s, carry_s = rest
    else:
        o_ref, x_s, a_s, b_s, h_s, carry_s = rest
    nb, tl, _ = cur_ref.shape
    step = pl.program_id(0)
    nt = pl.num_programs(0)
    ti = nt - 1 - step if reverse else step

    @pl.when(step == 0)
    def _():
        carry_s[...] = jnp.zeros_like(carry_s)

    cw = cw_ref[...]
    for b in range(nb):
        x_s[0:HALO, :] = jnp.where(ti > 0, prev_ref[b], 0.0)
        x_s[HALO:HALO + tl, :] = cur_ref[b]
        x_s[HALO + tl:2 * HALO + tl, :] = jnp.where(ti < nt - 1, next_ref[b], 0.0)
        xc = cb_ref[...]
        for k in range(4):
            xc = xc + cw[k:k + 1, :] * x_s[HALO - 1 + k:HALO - 1 + k + tl, :]
        gates = jnp.dot(xc.astype(BF16), wg_ref[...], preferred_element_type=F32) + bg_ref[...]
        r = _sigmoid(gates[:, :D_LRU])
        ig = _sigmoid(gates[:, D_LRU:])
        log_a = (-RG_C) * r * sp_ref[...]
        a = jnp.exp(log_a)
        a_s[b] = a
        b_s[b] = jnp.sqrt(-jnp.tanh(log_a) * (1.0 + a * a)) * (ig * xc)

    def scan_step(s, hs):
        t = tl - 1 - s if reverse else s
        out = []
        for b in range(nb):
            h = a_s[b, pl.ds(t, 1), :] * hs[b] + b_s[b, pl.ds(t, 1), :]
            h_s[b, pl.ds(t, 1), :] = h
            out.append(h)
        return tuple(out)

    hs = lax.fori_loop(0, tl, scan_step, tuple(carry_s[b:b + 1, :] for b in range(nb)), unroll=8)
    for b in range(nb):
        carry_s[b:b + 1, :] = hs[b]

    if reverse:
        for b in range(nb):
            y = (hf_ref[b] + h_s[b]) * _gelu(gate_ref[b])
            o_ref[b] = _rms(y, gain_ref[...])
    else:
        o_ref[...] = h_s[...]


def _lru_pass(reverse, zc3, params, extra):
    nb, seq, _ = zc3.shape
    tl = LRU_TILE
    nt = seq // tl
    hb = tl // HALO
    nhb = seq // HALO
    tix = (lambda i: nt - 1 - i) if reverse else (lambda i: i)
    cur = pl.BlockSpec((nb, tl, D_LRU), lambda i: (0, tix(i), 0))
    prev = pl.BlockSpec((nb, HALO, D_LRU), lambda i: (0, jnp.maximum(tix(i) * hb - 1, 0), 0))
    nxt = pl.BlockSpec((nb, HALO, D_LRU), lambda i: (0, jnp.minimum((tix(i) + 1) * hb, nhb - 1), 0))
    full = lambda a: pl.BlockSpec(a.shape, lambda i: (0,) * a.ndim)
    ins = [zc3, zc3, zc3, *params]
    in_specs = [prev, cur, nxt] + [full(a) for a in params]
    if reverse:
        hf, gate, gain = extra
        ins += [hf, gate, gain]
        in_specs += [cur, cur, full(gain)]
    return pl.pallas_call(
        functools.partial(_lru_body, reverse),
        grid=(nt,), in_specs=in_specs, out_specs=cur,
        out_shape=jax.ShapeDtypeStruct((nb, seq, D_LRU), F32),
        scratch_shapes=[pltpu.VMEM((tl + 2 * HALO, D_LRU), F32),
                        pltpu.VMEM((nb, tl, D_LRU), F32),
                        pltpu.VMEM((nb, tl, D_LRU), F32),
                        pltpu.VMEM((nb, tl, D_LRU), F32),
                        pltpu.VMEM((8, D_LRU), F32)],
        compiler_params=_cparams(("arbitrary",)),
        name="lru_bwd" if reverse else "lru_fwd")(*ins)


def _block_diag(w):
    h, d, _ = w.shape
    eye = jnp.eye(h, dtype=w.dtype)
    return (eye[:, None, :, None] * w[:, :, None, :]).reshape(h * d, h * d)


def _lru_mixer(zc, zg, batch, seq, conv_w, conv_b, w_a, b_a, w_x, b_x, lam, gain):
    zc3 = zc.reshape(batch, seq, D_LRU)
    zg3 = zg.reshape(batch, seq, D_LRU)
    outs = None
    for d in (0, 1):
        wg = jnp.concatenate([_block_diag(w_a[d]), _block_diag(w_x[d])], axis=1).astype(BF16)
        bg = jnp.concatenate([b_a[d], b_x[d]])[None, :]
        sp = jax.nn.softplus(-lam[d].astype(F32))[None, :]
        params = [conv_w, conv_b[None, :], wg, bg, sp]
        extra = None if d == 0 else (outs, zg3, gain[None, :])
        outs = _lru_pass(d == 1, zc3, params, extra)
    return outs.reshape(batch * seq, D_LRU)


def _mixout_body(x_ref, oa_ref, ys_ref, ub_ref, oc_ref, sd_ref, wglu_ref, bglu_ref, gb_ref,
                 wout_ref, g2_ref, rw_ref, rb_ref, xo_ref, h_ref, ti_ref, gt_ref):
    tm = x_ref.shape[0]
    yb = ys_ref[...] + sd_ref[...] * ub_ref[...]
    gl = jnp.dot(_gelu(yb).astype(BF16), wglu_ref[...], preferred_element_type=F32) + bglu_ref[...]
    ob = _rms(gl[:, :D_S5] * _sigmoid(gl[:, D_S5:]), gb_ref[...])
    o = jnp.concatenate([oa_ref[...], ob, oc_ref[...]], axis=1).astype(BF16)
    x = x_ref[...] + jnp.dot(o, wout_ref[...], preferred_element_type=F32)
    xo_ref[...] = x
    h = _rms(x, g2_ref[...])
    h_hi = h.astype(BF16)
    bits = pltpu.bitcast(h_hi.astype(F32), jnp.uint32)
    h_ref[...] = lax.shift_right_logical(bits[:, :D_MODEL // 2], jnp.uint32(16)) | (
        bits[:, D_MODEL // 2:] & jnp.uint32(0xFFFF0000))
    h_lo = (h - h_hi.astype(F32)).astype(BF16)
    nt_dims = (((1,), (1,)), ((), ()))
    la = lax.dot_general(rw_ref[...], h_hi, nt_dims, preferred_element_type=F32)
    lb = lax.dot_general(rw_ref[0:N_EXPERTS, :], h_lo, nt_dims, preferred_element_type=F32)
    logits = la[:N_EXPERTS] + la[N_EXPERTS:] + lb + rb_ref[...]
    eidx = lax.broadcasted_iota(jnp.int32, (N_EXPERTS, tm), 0).astype(F32)
    idx_rows, val_rows = [], []
    for k in range(TOP_K):
        m = jnp.max(logits, axis=0, keepdims=True)
        idx = jnp.min(jnp.where(logits == m, eidx, float(N_EXPERTS)), axis=0, keepdims=True)
        idx_rows.append(idx)
        val_rows.append(jnp.exp(m - val_rows[0]) if k else m)
        logits = jnp.where(eidx == idx, -jnp.inf, logits)
    top = val_rows[0]
    vals = jnp.concatenate([jnp.ones_like(top)] + [v for v in val_rows[1:]], axis=0)
    ti_ref[...] = jnp.concatenate(idx_rows, axis=0).astype(jnp.int32)
    gt_ref[...] = vals / jnp.sum(vals, axis=0, keepdims=True)


def _mixout(x, oa, ys5, zb, oc, s5_d, wglu, bglu, gain_b, wout, g2, rw, rb):
    t = x.shape[0]
    tm = ROW_TILE
    row = lambda n: pl.BlockSpec((tm, n), lambda i: (i, 0))
    col = pl.BlockSpec((TOP_K, tm), lambda i: (0, i))
    full = lambda a: pl.BlockSpec(a.shape, lambda i: (0,) * a.ndim)
    rw_hi = rw.astype(BF16)
    rw_lo = (rw - rw_hi.astype(F32)).astype(BF16)
    rw2 = jnp.concatenate([rw_hi.T, rw_lo.T], axis=0)
    params = [s5_d, wglu, bglu, gain_b, wout, g2, rw2, rb.reshape(N_EXPERTS, 1)]
    return pl.pallas_call(
        _mixout_body,
        grid=(t // tm,),
        in_specs=[row(D_MODEL), row(D_POOL), row(D_S5), row(D_S5), row(D_LRU)] + [full(a) for a in params],
        out_specs=[row(D_MODEL), row(D_MODEL // 2), col, col],
        out_shape=[jax.ShapeDtypeStruct((t, D_MODEL), F32), jax.ShapeDtypeStruct((t, D_MODEL // 2), jnp.uint32),
                   jax.ShapeDtypeStruct((TOP_K, t), jnp.int32), jax.ShapeDtypeStruct((TOP_K, t), F32)],
        compiler_params=_cparams(("parallel",)), name="mix_out")(x, oa, ys5, zb, oc, *params)


def _expert_body(be_ref, nu_ref, first_ref, xl_ref, xh_ref, wgu_ref, bg_ref, bu_ref, wd_ref, bd_ref, p_ref, y_ref,
                 wg_s, wu_s, wd_s):
    i = pl.program_id(0)
    live = i < nu_ref[0]

    @pl.when(jnp.logical_and(live, first_ref[i] == 1))
    def _():
        p = p_ref[...]
        half = MXU_TILE // 2
        for j in range(wgu_ref.shape[3] // MXU_TILE):
            w = wgu_ref[0, 0, :, MXU_TILE * j:MXU_TILE * (j + 1)].astype(BF16)
            r = jnp.dot(w, p, preferred_element_type=F32)
            wg_s[:, half * j:half * (j + 1)] = r[:, :half].astype(BF16)
            wu_s[:, half * j:half * (j + 1)] = r[:, half:].astype(BF16)
        wd_s[...] = wd_ref[0, 0].astype(BF16)

    @pl.when(live)
    def _():
        words = jnp.concatenate([xl_ref[...], xh_ref[...]], axis=1)
        lo = pltpu.bitcast(lax.shift_left(words, jnp.uint32(16)), F32)
        hi = pltpu.bitcast(words & jnp.uint32(0xFFFF0000), F32)
        x = jnp.concatenate([lo, hi], axis=1).astype(BF16)
        g = jnp.dot(x, wg_s[...], preferred_element_type=F32) + bg_ref[0, 0]
        u = jnp.dot(x, wu_s[...], preferred_element_type=F32) + bu_ref[0, 0]
        g = jnp.minimum(g, SWIGLU_LIMIT)
        u = jnp.clip(u, -SWIGLU_LIMIT, SWIGLU_LIMIT)
        act = ((u + 1.0) * (g * _sigmoid(g * SWIGLU_ALPHA))).astype(BF16)
        y = jnp.dot(act, wd_s[...], preferred_element_type=F32) + bd_ref[0, 0]
        y_ref[...] = y.astype(y_ref.dtype)

    @pl.when(jnp.logical_not(live))
    def _():
        y_ref[...] = jnp.zeros_like(y_ref)


def _expert_mlp(layer, xs, blk_e, n_used, w_gate_up, b_gate, b_up, w_down, b_down):
    n_rows, dh = xs[0].shape
    bm = MOE_ROWS
    first = jnp.concatenate([jnp.ones((1,), jnp.int32), (blk_e[1:] != blk_e[:-1]).astype(jnp.int32)])
    idx = jnp.arange(MXU_TILE)
    perm = jax.nn.one_hot(jnp.where(idx % 2 == 0, idx // 2, MXU_TILE // 2 + idx // 2), MXU_TILE, dtype=BF16)
    rows = pl.BlockSpec((bm, dh), lambda i, be, nu, fi: (jnp.minimum(i, nu[0] - 1), 0))
    per_e = lambda a: pl.BlockSpec((1, 1) + a.shape[2:], lambda i, be, nu, fi: (layer, be[i], 0, 0))
    grid_spec = pltpu.PrefetchScalarGridSpec(
        num_scalar_prefetch=3, grid=(n_rows // bm,),
        in_specs=[rows, rows, per_e(w_gate_up), per_e(b_gate), per_e(b_up), per_e(w_down), per_e(b_down),
                  pl.BlockSpec(perm.shape, lambda i, be, nu, fi: (0, 0))],
        out_specs=pl.BlockSpec((bm, D_MODEL), lambda i, be, nu, fi: (i, 0)),
        scratch_shapes=[pltpu.VMEM((D_MODEL, D_FF), BF16), pltpu.VMEM((D_MODEL, D_FF), BF16),
                        pltpu.VMEM((D_FF, D_MODEL), BF16)])
    return pl.pallas_call(
        _expert_body, grid_spec=grid_spec,
        out_shape=jax.ShapeDtypeStruct((n_rows, D_MODEL), BF16),
        compiler_params=pltpu.CompilerParams(dimension_semantics=("arbitrary",),
                                             vmem_limit_bytes=EXPERT_VMEM_LIMIT),
        name="expert_mlp")(blk_e, n_used, first, *xs, w_gate_up, b_gate, b_up, w_down, b_down, perm)


def _sc_dispatch(h, dest, n_rows):
    t, d = h.shape
    win = SC_WINDOW
    dh = d // SC_COL_SPLIT
    idx = [dest[:, k].reshape(1, t) for k in range(TOP_K)]
    mesh = plsc.VectorSubcoreMesh(core_axis_name="core", subcore_axis_name="subcore")

    def scatter_cols(j):
        @functools.partial(pl.kernel, out_type=jax.ShapeDtypeStruct((n_rows, dh), h.dtype), mesh=mesh,
                           scratch_types=[], name="sc_dispatch")
        def scatter_rows(x_hbm, i0_hbm, i1_hbm, i2_hbm, i3_hbm, o_hbm):
            def body(x_vmem, *idx_vmem):
                for iv in idx_vmem:
                    pltpu.sync_copy(x_vmem, o_hbm.at[iv.at[0]])

            pltpu.emit_pipeline(
                body, grid=(t // win,),
                in_specs=[pl.BlockSpec((win, dh), lambda i: (i, j))]
                         + [pl.BlockSpec((1, win), lambda i: (0, i))] * TOP_K,
                out_specs=[],
                core_axis_name=("core", "subcore"),
                dimension_semantics=(pltpu.PARALLEL,),
            )(x_hbm, i0_hbm, i1_hbm, i2_hbm, i3_hbm)

        return scatter_rows(h, *idx)

    return [scatter_cols(j) for j in range(SC_COL_SPLIT)]


def _moe_dispatch(top_i, t):
    bm = MOE_ROWS
    n_blocks = (t * TOP_K) // bm + N_EXPERTS
    onehot = jax.nn.one_hot(top_i, N_EXPERTS, dtype=jnp.int32)
    sel = jnp.sum(onehot, axis=1)
    csum = jnp.cumsum(sel, axis=0)
    counts = csum[-1]
    padded = ((counts + bm - 1) // bm) * bm
    pad_end = jnp.cumsum(padded)
    pad_start = pad_end - padded
    base = (csum - sel) + pad_start[None, :]
    dest = jnp.sum(onehot * base[:, None, :], axis=-1)
    blk_start = jnp.arange(n_blocks, dtype=jnp.int32) * bm
    blk_e = jnp.minimum(jnp.sum((pad_end[None, :] <= blk_start[:, None]).astype(jnp.int32), axis=1),
                        N_EXPERTS - 1)
    n_used = (pad_end[-1] // bm).astype(jnp.int32).reshape(1)
    return dest, n_blocks * bm, blk_e, n_used


def _final_body(x_ref, y0, y1, y2, y3, gt_ref, g_ref, o_ref):
    o_ref[...] = _rms(_combine(x_ref, (y0, y1, y2, y3), gt_ref), g_ref[...])


def _final(x, ys, gates, g):
    t = x.shape[0]
    tm = ROW_TILE
    row = lambda n: pl.BlockSpec((tm, n), lambda i: (i, 0))
    return pl.pallas_call(
        _final_body, grid=(t // tm,),
        in_specs=[row(D_MODEL)] * (1 + TOP_K) + [row(TOP_K), pl.BlockSpec(g.shape, lambda i: (0, 0))],
        out_specs=row(D_MODEL), out_shape=jax.ShapeDtypeStruct((t, D_MODEL), F32),
        compiler_params=_cparams(("parallel",)), name="final_norm")(x, *ys, gates, g)


def kernel(x, norm1_g, w_in, pool_w, pool_scale, s5_lam_re, s5_lam_im, s5_log_step, s5_b_re, s5_b_im, s5_c_re, s5_c_im, s5_d, s5_w_glu, s5_b_glu, lru_conv_w, lru_conv_b, lru_w_a, lru_b_a, lru_w_x, lru_b_x, lru_lam, mix_gain, w_out, norm2_g, router_w, router_b, w_gate_up, b_gate_up, w_down, b_down, final_g):
    batch, seq, d = x.shape
    t = batch * seq
    depth = norm1_g.shape[0]
    xt = x.reshape(t, d)
    add = None
    o1, o2 = D_POOL, D_POOL + D_S5
    b_gate = b_gate_up[:, :, None, 0::2]
    b_up = b_gate_up[:, :, None, 1::2]
    b_dn = b_down[:, :, None, :]
    for l in range(depth):
        if add is None:
            za, zb, zc, zg = _inproj(xt, None, norm1_g[l][None, :], w_in[l].astype(BF16))
        else:
            xt, za, zb, zc, zg = _inproj(xt, add, norm1_g[l][None, :], w_in[l].astype(BF16))
        oa = _pool_mixer(za, batch, seq, _block_diag(pool_w[l]).astype(BF16), pool_scale[l][None, :],
                         mix_gain[l][None, :o1])
        tables = _s5_tables(s5_lam_re[l], s5_lam_im[l], s5_log_step[l], s5_b_re[l], s5_b_im[l],
                            s5_c_re[l], s5_c_im[l], seq // S5_CHUNK)
        ys5 = _s5_core(zb, batch, seq, tables)
        oc = _lru_mixer(zc, zg, batch, seq, lru_conv_w[l], lru_conv_b[l], lru_w_a[l], lru_b_a[l],
                        lru_w_x[l], lru_b_x[l], lru_lam[l], mix_gain[l][o2:])
        xt, h2, top_i, gates = _mixout(
            xt, oa, ys5, zb, oc, s5_d[l][None, :], s5_w_glu[l].astype(BF16), s5_b_glu[l][None, :],
            mix_gain[l][None, o1:o2], w_out[l].astype(BF16), norm2_g[l][None, :], router_w[l], router_b[l])
        top_i, gates = top_i.T, gates.T
        dest, n_rows, blk_e, n_used = _moe_dispatch(top_i, t)
        xs = _sc_dispatch(h2, dest, n_rows)
        ys = _expert_mlp(l, xs, blk_e, n_used, w_gate_up, b_gate, b_up, w_down, b_dn)
        add = ([ys.at[dest[:, k]].get(mode="promise_in_bounds") for k in range(TOP_K)], gates)
    out = _final(xt, add[0], add[1], final_g[None, :])
    return out.reshape(batch, seq, d)
```

```python
import functools
import math

import jax
import jax.numpy as jnp
from jax import lax
from jax.experimental import pallas as pl
from jax.experimental.pallas import tpu as pltpu
from jax.experimental.pallas import tpu_sc as plsc

F32 = jnp.float32
BF16 = jnp.bfloat16

D_MODEL = 1024
D_POOL = 256
D_S5 = 384
D_LRU = 384
D_IN = D_POOL + D_S5 + 2 * D_LRU
POOL_WINDOWS = (2, 4, 8, 16)
POOL_GROUP = 64
S5_GROUP = 16
S5_NGROUPS = 24
S5_STATE = 64
LRU_HEADS = 6
LRU_HEAD_DIM = 64
RG_C = 8.0
N_EXPERTS = 32
TOP_K = 4
D_FF = 1024
SWIGLU_LIMIT = 7.0
SWIGLU_ALPHA = 1.702
EPS = 1e-5

ROW_TILE = 512
S5_CHUNK = 64
POOL_TILE = 512
LRU_TILE = 256
HALO = 8
MOE_ROWS = 512
SC_WINDOW = 128
SC_COL_SPLIT = 2
MXU_TILE = 256
EXPERT_VMEM_LIMIT = 56 * 1024 * 1024
VMEM_LIMIT = 48 * 1024 * 1024


def _cparams(sem):
    return pltpu.CompilerParams(dimension_semantics=sem, vmem_limit_bytes=VMEM_LIMIT)


def _rms(x, g):
    return x * lax.rsqrt(jnp.mean(x * x, axis=-1, keepdims=True) + EPS) * g


def _gelu(x):
    return 0.5 * x * (1.0 + jnp.tanh(0.7978845608028654 * (x + 0.044715 * (x * x * x))))


def _sigmoid(x):
    return 1.0 / (1.0 + jnp.exp(-x))


def _combine(x_ref, y_refs, gt_ref):
    x = x_ref[...]
    gt = gt_ref[...]
    for k in range(TOP_K):
        x = x + gt[:, k:k + 1] * y_refs[k][...].astype(F32)
    return x


def _inproj_body(has_add, *refs):
    if has_add:
        x_ref, y0, y1, y2, y3, gt_ref, g_ref, w_ref, xo_ref, za, zb, zc, zg = refs
        x = _combine(x_ref, (y0, y1, y2, y3), gt_ref)
        xo_ref[...] = x
    else:
        x_ref, g_ref, w_ref, za, zb, zc, zg = refs
        x = x_ref[...]
    h = _rms(x, g_ref[...]).astype(BF16)
    z = jnp.dot(h, w_ref[...], preferred_element_type=F32)
    za[...] = z[:, :D_POOL]
    zb[...] = z[:, D_POOL:D_POOL + D_S5]
    zc[...] = z[:, D_POOL + D_S5:D_POOL + D_S5 + D_LRU]
    zg[...] = z[:, D_POOL + D_S5 + D_LRU:]


def _inproj(x, add, g, w_bf16):
    t = x.shape[0]
    tm = ROW_TILE
    row = lambda n: pl.BlockSpec((tm, n), lambda i: (i, 0))
    full = lambda a: pl.BlockSpec(a.shape, lambda i: (0,) * a.ndim)
    z_shapes = [jax.ShapeDtypeStruct((t, n), F32) for n in (D_POOL, D_S5, D_LRU, D_LRU)]
    z_specs = [row(n) for n in (D_POOL, D_S5, D_LRU, D_LRU)]
    if add is None:
        ins, in_specs = [x, g, w_bf16], [row(D_MODEL), full(g), full(w_bf16)]
        out_shape, out_specs = z_shapes, z_specs
    else:
        ys, gates = add
        ins = [x, *ys, gates, g, w_bf16]
        in_specs = [row(D_MODEL)] + [row(D_MODEL)] * TOP_K + [row(TOP_K), full(g), full(w_bf16)]
        out_shape = [jax.ShapeDtypeStruct((t, D_MODEL), F32)] + z_shapes
        out_specs = [row(D_MODEL)] + z_specs
    return pl.pallas_call(
        functools.partial(_inproj_body, add is not None),
        grid=(t // tm,), in_specs=in_specs, out_specs=out_specs, out_shape=out_shape,
        compiler_params=_cparams(("parallel",)), name="inproj")(*ins)


def _pool_body(seq, prev_ref, cur_ref, next_ref, w_ref, sc_ref, gain_ref, o_ref, u_s, s2_s, s4_s, s8_s):
    tl = cur_ref.shape[0]
    i = pl.program_id(1)
    nt = pl.num_programs(1)
    zero8 = jnp.zeros((HALO, D_POOL), F32)
    for buf in (u_s, s2_s, s4_s, s8_s):
        buf[0:HALO, :] = zero8
        buf[tl + 3 * HALO:tl + 4 * HALO, :] = zero8
    u_s[HALO:2 * HALO, :] = jnp.where(i > 0, prev_ref[...], 0.0)
    u_s[2 * HALO:2 * HALO + tl, :] = cur_ref[...]
    u_s[2 * HALO + tl:3 * HALO + tl, :] = jnp.where(i < nt - 1, next_ref[...], 0.0)
    r = tl + 2 * HALO
    s2_s[HALO:HALO + r, :] = u_s[HALO - 1:HALO - 1 + r, :] + u_s[HALO:HALO + r, :]
    s4_s[HALO:HALO + r, :] = s2_s[HALO - 1:HALO - 1 + r, :] + s2_s[HALO + 1:HALO + 1 + r, :]
    s8_s[HALO:HALO + r, :] = s4_s[HALO - 2:HALO - 2 + r, :] + s4_s[HALO + 2:HALO + 2 + r, :]
    o = 2 * HALO
    s16 = s8_s[o - 4:o - 4 + tl, :] + s8_s[o + 4:o + 4 + tl, :]
    s8 = s8_s[o:o + tl, :]
    s4 = s4_s[o:o + tl, :]
    s2 = s2_s[o:o + tl, :]
    u = u_s[o:o + tl, :]
    lane = lax.broadcasted_iota(jnp.int32, (tl, D_POOL), 1)
    tpos = lax.broadcasted_iota(jnp.int32, (tl, D_POOL), 0) + i * tl
    g0, g1, g2 = lane < POOL_GROUP, lane < 2 * POOL_GROUP, lane < 3 * POOL_GROUP
    half = jnp.where(g0, 1, jnp.where(g1, 2, jnp.where(g2, 4, 8)))
    wsum = jnp.where(g0, s2, jnp.where(g1, s4, jnp.where(g2, s8, s16)))
    cnt = (jnp.minimum(tpos + half, seq) - jnp.maximum(tpos - half, 0)).astype(F32)
    d = wsum / cnt - u
    y = jnp.dot(d.astype(BF16), w_ref[...], preferred_element_type=F32) * sc_ref[...]
    o_ref[...] = _rms(y, gain_ref[...])


def _pool_mixer(za, batch, seq, w_bd, scale, gain):
    t = za.shape[0]
    tl = POOL_TILE
    nt = seq // tl
    hb = tl // HALO
    nhb = t // HALO
    cur = pl.BlockSpec((tl, D_POOL), lambda b, i: (b * nt + i, 0))
    prev = pl.BlockSpec((HALO, D_POOL), lambda b, i: (jnp.maximum((b * nt + i) * hb - 1, 0), 0))
    nxt = pl.BlockSpec((HALO, D_POOL), lambda b, i: (jnp.minimum((b * nt + i + 1) * hb, nhb - 1), 0))
    full = lambda a: pl.BlockSpec(a.shape, lambda b, i: (0,) * a.ndim)
    return pl.pallas_call(
        functools.partial(_pool_body, seq),
        grid=(batch, nt),
        in_specs=[prev, cur, nxt, full(w_bd), full(scale), full(gain)],
        out_specs=cur,
        out_shape=jax.ShapeDtypeStruct((t, D_POOL), F32),
        scratch_shapes=[pltpu.VMEM((tl + 4 * HALO, D_POOL), F32)] * 4,
        compiler_params=_cparams(("parallel", "parallel")), name="pool_mixer")(za, za, za, w_bd, scale, gain)


def _s5_tables(lam_re, lam_im, log_step, b_re, b_im, c_re, c_im, n_chunks):
    L = S5_CHUNK
    hp = lax.Precision.HIGHEST
    lr = lam_re.astype(F32)
    li = lam_im.astype(F32)
    dt = jnp.exp(log_step.astype(F32))[..., None]
    jj = jnp.arange(L + 1, dtype=F32)[None, None, :, None]
    ar = (lr * dt)[:, :, None, :]
    ai = (li * dt)[:, :, None, :]
    mag = jnp.exp(jj * ar)
    e_r = mag * jnp.cos(jj * ai)
    e_i = mag * jnp.sin(jj * ai)
    den = lr * lr + li * li
    nr = e_r[:, :, 1, :] - 1.0
    ni = e_i[:, :, 1, :]
    q_r = (nr * lr + ni * li) / den
    q_i = (ni * lr - nr * li) / den
    br = b_re.astype(F32)
    bi = b_im.astype(F32)
    bb_r = q_r[..., None] * br - q_i[..., None] * bi
    bb_i = q_r[..., None] * bi + q_i[..., None] * br
    cr = c_re.astype(F32)
    ci = c_im.astype(F32)
    g_r = e_r[..., None] * bb_r[:, :, None] - e_i[..., None] * bb_i[:, :, None]
    g_i = e_r[..., None] * bb_i[:, :, None] + e_i[..., None] * bb_r[:, :, None]
    kk = (jnp.einsum('dgcp,dgjpk->dgjck', cr, g_r[:, :, :L], precision=hp)
          - jnp.einsum('dgcp,dgjpk->dgjck', ci, g_i[:, :, :L], precision=hp))
    kf, kb = kk[0], kk[1]
    kall = jnp.concatenate([kb[:, :0:-1], (kf[:, :1] + kb[:, :1]), kf[:, 1:],
                            jnp.zeros_like(kf[:, :1])], axis=1)
    tmat = kall.transpose(0, 3, 1, 2).reshape(S5_NGROUPS, S5_GROUP, 2 * L * S5_GROUP)
    wf_r = g_r[0, :, L - 1::-1][:, :L]
    wf_i = g_i[0, :, L - 1::-1][:, :L]
    wb_r = g_r[1, :, :L]
    wb_i = g_i[1, :, :L]
    to_w = lambda a: a.transpose(0, 1, 3, 2).reshape(S5_NGROUPS, L * S5_GROUP, S5_STATE)
    wst = jnp.concatenate([to_w(wf_r), to_w(wb_r), to_w(wf_i), to_w(wb_i)], axis=-1)
    ef_r, ef_i = e_r[0, :, 1:L + 1], e_i[0, :, 1:L + 1]
    eb_r, eb_i = e_r[1, :, L:0:-1], e_i[1, :, L:0:-1]
    def carry_ops(er, ei, c_r, c_i):
        vr = c_r[:, None] * er[:, :, None, :] - c_i[:, None] * ei[:, :, None, :]
        vi = -(c_r[:, None] * ei[:, :, None, :] + c_i[:, None] * er[:, :, None, :])
        to_v = lambda a: a.transpose(0, 3, 1, 2).reshape(S5_NGROUPS, S5_STATE, L * S5_GROUP)
        return to_v(vr), to_v(vi)
    vf_r, vf_i = carry_ops(ef_r, ef_i, cr[0], ci[0])
    vb_r, vb_i = carry_ops(eb_r, eb_i, cr[1], ci[1])
    vmat = jnp.concatenate([vf_r, vb_r, vf_i, vb_i], axis=1)
    n_steps = max(1, int(math.ceil(math.log2(n_chunks))))
    kpow = (L * (2 ** jnp.arange(n_steps, dtype=F32)))[None, None, :, None]
    pmag = jnp.exp(kpow * ar)
    p_r = pmag * jnp.cos(kpow * ai)
    p_i = pmag * jnp.sin(kpow * ai)
    pad = jnp.zeros((S5_NGROUPS, 8 - n_steps % 8 if n_steps % 8 else 0, 2 * S5_STATE), F32)
    p_r = jnp.concatenate([jnp.concatenate([p_r[0], p_r[1]], axis=-1), pad], axis=1)
    p_i = jnp.concatenate([jnp.concatenate([p_i[0], p_i[1]], axis=-1), pad], axis=1)
    ptab = jnp.concatenate([p_r, p_i], axis=1)
    return tmat, wst.astype(BF16), vmat.astype(BF16), ptab, n_steps


def _s5_body(n_chunks, n_steps, u_ref, t_ref, w_ref, v_ref, p_ref, y_ref, toep_s):
    u = u_ref[0]
    n = u.shape[0]
    lags = t_ref[0]
    width = lags.shape[1]
    blocks_per_tile = 128 // S5_GROUP
    for res in range(blocks_per_tile):
        rot = lags if res == 0 else pltpu.roll(lags, width - res * S5_GROUP, 1)
        for s in range(S5_CHUNK):
            lag0 = S5_CHUNK - 1 - s
            if lag0 % blocks_per_tile == res:
                q = (lag0 // blocks_per_tile) * 128
                toep_s[s * S5_GROUP:(s + 1) * S5_GROUP, :] = rot[:, q:q + S5_CHUNK * S5_GROUP].astype(BF16)
    y = jnp.dot(u, toep_s[...], preferred_element_type=F32)
    s = jnp.dot(u, w_ref[0], preferred_element_type=F32)
    two_p = 2 * S5_STATE
    xr = s[:, :two_p]
    xi = s[:, two_p:]
    row = lax.broadcasted_iota(jnp.int32, (n, two_p), 0) % n_chunks
    is_fwd = lax.broadcasted_iota(jnp.int32, (n, two_p), 1) < S5_STATE
    pim0 = p_ref.shape[1] // 2

    def shifted(a, k):
        down = jnp.where(row >= k, pltpu.roll(a, k, 0), 0.0)
        up = jnp.where(row < n_chunks - k, pltpu.roll(a, n - k, 0), 0.0)
        return jnp.where(is_fwd, down, up)

    for j in range(n_steps):
        k = 1 << j
        pr = p_ref[0, j:j + 1, :]
        pi = p_ref[0, pim0 + j:pim0 + j + 1, :]
        sr = shifted(xr, k)
        si = shifted(xi, k)
        xr, xi = xr + pr * sr - pi * si, xi + pr * si + pi * sr
    carry = jnp.concatenate([shifted(xr, 1), shifted(xi, 1)], axis=1).astype(BF16)
    y_ref[0] = y + jnp.dot(carry, v_ref[0], preferred_element_type=F32)


def _s5_core(zb, batch, seq, tables):
    tmat, wst, vmat, ptab, n_steps = tables
    L = S5_CHUNK
    n_chunks = seq // L
    n = batch * n_chunks
    ug = zb.astype(BF16).reshape(n, L, S5_NGROUPS, S5_GROUP).transpose(2, 0, 1, 3).reshape(S5_NGROUPS, n, L * S5_GROUP)
    per_g = lambda a: pl.BlockSpec((1,) + a.shape[1:], lambda g: (g, 0, 0))
    yg = pl.pallas_call(
        functools.partial(_s5_body, n_chunks, n_steps),
        grid=(S5_NGROUPS,),
        in_specs=[per_g(ug), per_g(tmat), per_g(wst), per_g(vmat), per_g(ptab)],
        out_specs=pl.BlockSpec((1, n, L * S5_GROUP), lambda g: (g, 0, 0)),
        out_shape=jax.ShapeDtypeStruct((S5_NGROUPS, n, L * S5_GROUP), F32),
        scratch_shapes=[pltpu.VMEM((L * S5_GROUP, L * S5_GROUP), BF16)],
        compiler_params=_cparams(("parallel",)), name="s5_core")(ug, tmat, wst, vmat, ptab)
    return yg.reshape(S5_NGROUPS, n, L, S5_GROUP).transpose(1, 2, 0, 3).reshape(batch * seq, D_S5)


def _lru_body(reverse, prev_ref, cur_ref, next_ref, cw_ref, cb_ref, wg_ref, bg_ref, sp_ref, *rest):
    if reverse:
        hf_ref, gate_ref, gain_ref, o_ref, x_s, a_s, b_s, h_s, carry_s = rest
    else:
        o_ref, x_s, a_s, b_s, h_s, carry_s = rest
    nb, tl, _ = cur_ref.shape
    step = pl.program_id(0)
    nt = pl.num_programs(0)
    ti = nt - 1 - step if reverse else step

    @pl.when(step == 0)
    def _():
        carry_s[...] = jnp.zeros_like(carry_s)

    cw = cw_ref[...]
    for b in range(nb):
        x_s[0:HALO, :] = jnp.where(ti > 0, prev_ref[b], 0.0)
        x_s[HALO:HALO + tl, :] = cur_ref[b]
        x_s[HALO + tl:2 * HALO + tl, :] = jnp.where(ti < nt - 1, next_ref[b], 0.0)
        xc = cb_ref[...]
        for k in range(4):
            xc = xc + cw[k:k + 1, :] * x_s[HALO - 1 + k:HALO - 1 + k + tl, :]
        gates = jnp.dot(xc.astype(BF16), wg_ref[...], preferred_element_type=F32) + bg_ref[...]
        r = _sigmoid(gates[:, :D_LRU])
        ig = _sigmoid(gates[:, D_LRU:])
        log_a = (-RG_C) * r * sp_ref[...]
        a = jnp.exp(log_a)
        a_s[b] = a
        om = 1.0 - a * a
        b_s[b] = (om * lax.rsqrt(jnp.maximum(om, 1e-30))) * (ig * xc)

    def scan_step(s, hs):
        t = tl - 1 - s if reverse else s
        out = []
        for b in range(nb):
            h = a_s[b, pl.ds(t, 1), :] * hs[b] + b_s[b, pl.ds(t, 1), :]
            h_s[b, pl.ds(t, 1), :] = h
            out.append(h)
        return tuple(out)

    hs = lax.fori_loop(0, tl, scan_step, tuple(carry_s[b:b + 1, :] for b in range(nb)), unroll=8)
    for b in range(nb):
        carry_s[b:b + 1, :] = hs[b]

    if reverse:
        for b in range(nb):
            y = (hf_ref[b] + h_s[b]) * _gelu(gate_ref[b])
            o_ref[b] = _rms(y, gain_ref[...])
    else:
        o_ref[...] = h_s[...]


def _lru_pass(reverse, zc3, params, extra):
    nb, seq, _ = zc3.shape
    tl = LRU_TILE
    nt = seq // tl
    hb = tl // HALO
    nhb = seq // HALO
    tix = (lambda i: nt - 1 - i) if reverse else (lambda i: i)
    cur = pl.BlockSpec((nb, tl, D_LRU), lambda i: (0, tix(i), 0))
    prev = pl.BlockSpec((nb, HALO, D_LRU), lambda i: (0, jnp.maximum(tix(i) * hb - 1, 0), 0))
    nxt = pl.BlockSpec((nb, HALO, D_LRU), lambda i: (0, jnp.minimum((tix(i) + 1) * hb, nhb - 1), 0))
    full = lambda a: pl.BlockSpec(a.shape, lambda i: (0,) * a.ndim)
    ins = [zc3, zc3, zc3, *params]
    in_specs = [prev, cur, nxt] + [full(a) for a in params]
    if reverse:
        hf, gate, gain = extra
        ins += [hf, gate, gain]
        in_specs += [cur, cur, full(gain)]
    return pl.pallas_call(
        functools.partial(_lru_body, reverse),
        grid=(nt,), in_specs=in_specs, out_specs=cur,
        out_shape=jax.ShapeDtypeStruct((nb, seq, D_LRU), F32),
        scratch_shapes=[pltpu.VMEM((tl + 2 * HALO, D_LRU), F32),
                        pltpu.VMEM((nb, tl, D_LRU), F32),
                        pltpu.VMEM((nb, tl, D_LRU), F32),
                        pltpu.VMEM((nb, tl, D_LRU), F32),
                        pltpu.VMEM((8, D_LRU), F32)],
        compiler_params=_cparams(("arbitrary",)),
        name="lru_bwd" if reverse else "lru_fwd")(*ins)


def _block_diag(w):
    h, d, _ = w.shape
    eye = jnp.eye(h, dtype=w.dtype)
    return (eye[:, None, :, None] * w[:, :, None, :]).reshape(h * d, h * d)


def _lru_mixer(zc, zg, batch, seq, conv_w, conv_b, w_a, b_a, w_x, b_x, lam, gain):
    zc3 = zc.reshape(batch, seq, D_LRU)
    zg3 = zg.reshape(batch, seq, D_LRU)
    outs = None
    for d in (0, 1):
        wg = jnp.concatenate([_block_diag(w_a[d]), _block_diag(w_x[d])], axis=1).astype(BF16)
        bg = jnp.concatenate([b_a[d], b_x[d]])[None, :]
        sp = jax.nn.softplus(-lam[d].astype(F32))[None, :]
        params = [conv_w, conv_b[None, :], wg, bg, sp]
        extra = None if d == 0 else (outs, zg3, gain[None, :])
        outs = _lru_pass(d == 1, zc3, params, extra)
    return outs.reshape(batch * seq, D_LRU)


def _mixout_body(x_ref, oa_ref, ys_ref, ub_ref, oc_ref, sd_ref, wglu_ref, bglu_ref, gb_ref,
                 wout_ref, g2_ref, rw_ref, rb_ref, xo_ref, h_ref, ti_ref, gt_ref):
    tm = x_ref.shape[0]
    yb = ys_ref[...] + sd_ref[...] * ub_ref[...]
    gl = jnp.dot(_gelu(yb).astype(BF16), wglu_ref[...], preferred_element_type=F32) + bglu_ref[...]
    ob = _rms(gl[:, :D_S5] * _sigmoid(gl[:, D_S5:]), gb_ref[...])
    o = jnp.concatenate([oa_ref[...], ob, oc_ref[...]], axis=1).astype(BF16)
    x = x_ref[...] + jnp.dot(o, wout_ref[...], preferred_element_type=F32)
    xo_ref[...] = x
    h = _rms(x, g2_ref[...])
    h_hi = h.astype(BF16)
    bits = pltpu.bitcast(h_hi.astype(F32), jnp.uint32)
    h_ref[...] = lax.shift_right_logical(bits[:, :D_MODEL // 2], jnp.uint32(16)) | (
        bits[:, D_MODEL // 2:] & jnp.uint32(0xFFFF0000))
    h_lo = (h - h_hi.astype(F32)).astype(BF16)
    nt_dims = (((1,), (1,)), ((), ()))
    la = lax.dot_general(rw_ref[...], h_hi, nt_dims, preferred_element_type=F32)
    lb = lax.dot_general(rw_ref[0:N_EXPERTS, :], h_lo, nt_dims, preferred_element_type=F32)
    logits = la[:N_EXPERTS] + la[N_EXPERTS:] + lb + rb_ref[...]
    eidx = lax.broadcasted_iota(jnp.int32, (N_EXPERTS, tm), 0).astype(F32)
    idx_rows, val_rows = [], []
    for k in range(TOP_K):
        m = jnp.max(logits, axis=0, keepdims=True)
        idx = jnp.min(jnp.where(logits == m, eidx, float(N_EXPERTS)), axis=0, keepdims=True)
        idx_rows.append(idx)
        val_rows.append(jnp.exp(m - val_rows[0]) if k else m)
        logits = jnp.where(eidx == idx, -jnp.inf, logits)
    top = val_rows[0]
    vals = jnp.concatenate([jnp.ones_like(top)] + [v for v in val_rows[1:]], axis=0)
    ti_ref[...] = jnp.concatenate(idx_rows, axis=0).astype(jnp.int32)
    gt_ref[...] = vals / jnp.sum(vals, axis=0, keepdims=True)


def _mixout(x, oa, ys5, zb, oc, s5_d, wglu, bglu, gain_b, wout, g2, rw, rb):
    t = x.shape[0]
    tm = ROW_TILE
    row = lambda n: pl.BlockSpec((tm, n), lambda i: (i, 0))
    col = pl.BlockSpec((TOP_K, tm), lambda i: (0, i))
    full = lambda a: pl.BlockSpec(a.shape, lambda i: (0,) * a.ndim)
    rw_hi = rw.astype(BF16)
    rw_lo = (rw - rw_hi.astype(F32)).astype(BF16)
    rw2 = jnp.concatenate([rw_hi.T, rw_lo.T], axis=0)
    params = [s5_d, wglu, bglu, gain_b, wout, g2, rw2, rb.reshape(N_EXPERTS, 1)]
    return pl.pallas_call(
        _mixout_body,
        grid=(t // tm,),
        in_specs=[row(D_MODEL), row(D_POOL), row(D_S5), row(D_S5), row(D_LRU)] + [full(a) for a in params],
        out_specs=[row(D_MODEL), row(D_MODEL // 2), col, col],
        out_shape=[jax.ShapeDtypeStruct((t, D_MODEL), F32), jax.ShapeDtypeStruct((t, D_MODEL // 2), jnp.uint32),
                   jax.ShapeDtypeStruct((TOP_K, t), jnp.int32), jax.ShapeDtypeStruct((TOP_K, t), F32)],
        compiler_params=_cparams(("parallel",)), name="mix_out")(x, oa, ys5, zb, oc, *params)


def _expert_body(be_ref, nu_ref, first_ref, xl_ref, xh_ref, wgu_ref, bg_ref, bu_ref, wd_ref, bd_ref, p_ref, y_ref,
                 wg_s, wu_s, wd_s):
    i = pl.program_id(0)
    live = i < nu_ref[0]

    @pl.when(jnp.logical_and(live, first_ref[i] == 1))
    def _():
        p = p_ref[...]
        half = MXU_TILE // 2
        for j in range(wgu_ref.shape[3] // MXU_TILE):
            w = wgu_ref[0, 0, :, MXU_TILE * j:MXU_TILE * (j + 1)].astype(BF16)
            r = jnp.dot(w, p, preferred_element_type=F32)
            wg_s[:, half * j:half * (j + 1)] = r[:, :half].astype(BF16)
            wu_s[:, half * j:half * (j + 1)] = r[:, half:].astype(BF16)
        wd_s[...] = wd_ref[0, 0].astype(BF16)

    @pl.when(live)
    def _():
        words = jnp.concatenate([xl_ref[...], xh_ref[...]], axis=1)
        lo = pltpu.bitcast(lax.shift_left(words, jnp.uint32(16)), F32)
        hi = pltpu.bitcast(words & jnp.uint32(0xFFFF0000), F32)
        x = jnp.concatenate([lo, hi], axis=1).astype(BF16)
        g = jnp.dot(x, wg_s[...], preferred_element_type=F32) + bg_ref[0, 0]
        u = jnp.dot(x, wu_s[...], preferred_element_type=F32) + bu_ref[0, 0]
        g = jnp.minimum(g, SWIGLU_LIMIT)
        u = jnp.clip(u, -SWIGLU_LIMIT, SWIGLU_LIMIT)
        act = ((u + 1.0) * (g * _sigmoid(g * SWIGLU_ALPHA))).astype(BF16)
        y = jnp.dot(act, wd_s[...], preferred_element_type=F32) + bd_ref[0, 0]
        y_ref[...] = y.astype(y_ref.dtype)

    @pl.when(jnp.logical_not(live))
    def _():
        y_ref[...] = jnp.zeros_like(y_ref)


def _expert_mlp(layer, xs, blk_e, n_used, w_gate_up, b_gate, b_up, w_down, b_down):
    n_rows, dh = xs[0].shape
    bm = MOE_ROWS
    first = jnp.concatenate([jnp.ones((1,), jnp.int32), (blk_e[1:] != blk_e[:-1]).astype(jnp.int32)])
    idx = jnp.arange(MXU_TILE)
    perm = jax.nn.one_hot(jnp.where(idx % 2 == 0, idx // 2, MXU_TILE // 2 + idx // 2), MXU_TILE, dtype=BF16)
    rows = pl.BlockSpec((bm, dh), lambda i, be, nu, fi: (jnp.minimum(i, nu[0] - 1), 0))
    per_e = lambda a: pl.BlockSpec((1, 1) + a.shape[2:], lambda i, be, nu, fi: (layer, be[i], 0, 0))
    grid_spec = pltpu.PrefetchScalarGridSpec(
        num_scalar_prefetch=3, grid=(n_rows // bm,),
        in_specs=[rows, rows, per_e(w_gate_up), per_e(b_gate), per_e(b_up), per_e(w_down), per_e(b_down),
                  pl.BlockSpec(perm.shape, lambda i, be, nu, fi: (0, 0))],
        out_specs=pl.BlockSpec((bm, D_MODEL), lambda i, be, nu, fi: (i, 0)),
        scratch_shapes=[pltpu.VMEM((D_MODEL, D_FF), BF16), pltpu.VMEM((D_MODEL, D_FF), BF16),
                        pltpu.VMEM((D_FF, D_MODEL), BF16)])
    return pl.pallas_call(
        _expert_body, grid_spec=grid_spec,
        out_shape=jax.ShapeDtypeStruct((n_rows, D_MODEL), BF16),
        compiler_params=pltpu.CompilerParams(dimension_semantics=("arbitrary",),
                                             vmem_limit_bytes=EXPERT_VMEM_LIMIT),
        name="expert_mlp")(blk_e, n_used, first, *xs, w_gate_up, b_gate, b_up, w_down, b_down, perm)


def _sc_dispatch(h, dest, n_rows):
    t, d = h.shape
    win = SC_WINDOW
    dh = d // SC_COL_SPLIT
    idx = [dest[:, k].reshape(1, t) for k in range(TOP_K)]
    mesh = plsc.VectorSubcoreMesh(core_axis_name="core", subcore_axis_name="subcore")

    def scatter_cols(j):
        @functools.partial(pl.kernel, out_type=jax.ShapeDtypeStruct((n_rows, dh), h.dtype), mesh=mesh,
                           scratch_types=[], name="sc_dispatch")
        def scatter_rows(x_hbm, i0_hbm, i1_hbm, i2_hbm, i3_hbm, o_hbm):
            def body(x_vmem, *idx_vmem):
                for iv in idx_vmem:
                    pltpu.sync_copy(x_vmem, o_hbm.at[iv.at[0]])

            pltpu.emit_pipeline(
                body, grid=(t // win,),
                in_specs=[pl.BlockSpec((win, dh), lambda i: (i, j))]
                         + [pl.BlockSpec((1, win), lambda i: (0, i))] * TOP_K,
                out_specs=[],
                core_axis_name=("core", "subcore"),
                dimension_semantics=(pltpu.PARALLEL,),
            )(x_hbm, i0_hbm, i1_hbm, i2_hbm, i3_hbm)

        return scatter_rows(h, *idx)

    return [scatter_cols(j) for j in range(SC_COL_SPLIT)]


def _moe_dispatch(top_i, t):
    bm = MOE_ROWS
    n_blocks = (t * TOP_K) // bm + N_EXPERTS
    onehot = jax.nn.one_hot(top_i, N_EXPERTS, dtype=jnp.int32)
    sel = jnp.sum(onehot, axis=1)
    csum = jnp.cumsum(sel, axis=0)
    counts = csum[-1]
    padded = ((counts + bm - 1) // bm) * bm
    pad_end = jnp.cumsum(padded)
    pad_start = pad_end - padded
    base = (csum - sel) + pad_start[None, :]
    dest = jnp.sum(onehot * base[:, None, :], axis=-1)
    blk_start = jnp.arange(n_blocks, dtype=jnp.int32) * bm
    blk_e = jnp.minimum(jnp.sum((pad_end[None, :] <= blk_start[:, None]).astype(jnp.int32), axis=1),
                        N_EXPERTS - 1)
    n_used = (pad_end[-1] // bm).astype(jnp.int32).reshape(1)
    return dest, n_blocks * bm, blk_e, n_used


def _final_body(x_ref, y0, y1, y2, y3, gt_ref, g_ref, o_ref):
    o_ref[...] = _rms(_combine(x_ref, (y0, y1, y2, y3), gt_ref), g_ref[...])


def _final(x, ys, gates, g):
    t = x.shape[0]
    tm = ROW_TILE
    row = lambda n: pl.BlockSpec((tm, n), lambda i: (i, 0))
    return pl.pallas_call(
        _final_body, grid=(t // tm,),
        in_specs=[row(D_MODEL)] * (1 + TOP_K) + [row(TOP_K), pl.BlockSpec(g.shape, lambda i: (0, 0))],
        out_specs=row(D_MODEL), out_shape=jax.ShapeDtypeStruct((t, D_MODEL), F32),
        compiler_params=_cparams(("parallel",)), name="final_norm")(x, *ys, gates, g)


def kernel(x, norm1_g, w_in, pool_w, pool_scale, s5_lam_re, s5_lam_im, s5_log_step, s5_b_re, s5_b_im, s5_c_re, s5_c_im, s5_d, s5_w_glu, s5_b_glu, lru_conv_w, lru_conv_b, lru_w_a, lru_b_a, lru_w_x, lru_b_x, lru_lam, mix_gain, w_out, norm2_g, router_w, router_b, w_gate_up, b_gate_up, w_down, b_down, final_g):
    batch, seq, d = x.shape
    t = batch * seq
    depth = norm1_g.shape[0]
    xt = x.reshape(t, d)
    add = None
    o1, o2 = D_POOL, D_POOL + D_S5
    b_gate = b_gate_up[:, :, None, 0::2]
    b_up = b_gate_up[:, :, None, 1::2]
    b_dn = b_down[:, :, None, :]
    for l in range(depth):
        if add is None:
            za, zb, zc, zg = _inproj(xt, None, norm1_g[l][None, :], w_in[l].astype(BF16))
        else:
            xt, za, zb, zc, zg = _inproj(xt, add, norm1_g[l][None, :], w_in[l].astype(BF16))
        oa = _pool_mixer(za, batch, seq, _block_diag(pool_w[l]).astype(BF16), pool_scale[l][None, :],
                         mix_gain[l][None, :o1])
        tables = _s5_tables(s5_lam_re[l], s5_lam_im[l], s5_log_step[l], s5_b_re[l], s5_b_im[l],
                            s5_c_re[l], s5_c_im[l], seq // S5_CHUNK)
        ys5 = _s5_core(zb, batch, seq, tables)
        oc = _lru_mixer(zc, zg, batch, seq, lru_conv_w[l], lru_conv_b[l], lru_w_a[l], lru_b_a[l],
                        lru_w_x[l], lru_b_x[l], lru_lam[l], mix_gain[l][o2:])
        xt, h2, top_i, gates = _mixout(
            xt, oa, ys5, zb, oc, s5_d[l][None, :], s5_w_glu[l].astype(BF16), s5_b_glu[l][None, :],
            mix_gain[l][None, o1:o2], w_out[l].astype(BF16), norm2_g[l][None, :], router_w[l], router_b[l])
        top_i, gates = top_i.T, gates.T
        dest, n_rows, blk_e, n_used = _moe_dispatch(top_i, t)
        xs = _sc_dispatch(h2, dest, n_rows)
        ys = _expert_mlp(l, xs, blk_e, n_used, w_gate_up, b_gate, b_up, w_down, b_dn)
        add = ([ys.at[dest[:, k]].get(mode="promise_in_bounds") for k in range(TOP_K)], gates)
    out = _final(xt, add[0], add[1], final_g[None, :])
    return out.reshape(batch, seq, d)
```

```python
import functools
import math

import jax
import jax.numpy as jnp
from jax import lax
from jax.experimental import pallas as pl
from jax.experimental.pallas import tpu as pltpu
from jax.experimental.pallas import tpu_sc as plsc

F32 = jnp.float32
BF16 = jnp.bfloat16

D_MODEL = 1024
D_POOL = 256
D_S5 = 384
D_LRU = 384
POOL_WINDOWS = (2, 4, 8, 16)
POOL_GROUP = 64
S5_GROUP = 16
S5_NGROUPS = 24
S5_STATE = 64
LRU_HEADS = 6
LRU_HEAD_DIM = 64
RG_C = 8.0
N_EXPERTS = 32
TOP_K = 4
D_FF = 1024
SWIGLU_LIMIT = 7.0
SWIGLU_ALPHA = 1.702
EPS = 1e-5

ROW_TILE = 512
ROW_SPLIT = 2
S5_CHUNK = 128
POOL_TILE = 512
LRU_TILE = 256
HALO = 8
MOE_ROWS = 512
SC_WINDOW = 128
SC_COL_SPLIT = 2
MXU_TILE = 256
EXPERT_VMEM_LIMIT = 56 * 1024 * 1024
VMEM_LIMIT = 48 * 1024 * 1024


def _cparams(sem):
    return pltpu.CompilerParams(dimension_semantics=sem, vmem_limit_bytes=VMEM_LIMIT)


def _rms(x, g):
    return x * lax.rsqrt(jnp.mean(x * x, axis=-1, keepdims=True) + EPS) * g


def _gelu(x):
    return 0.5 * x * (1.0 + jnp.tanh(0.7978845608028654 * (x + 0.044715 * (x * x * x))))


def _sigmoid(x):
    return 1.0 / (1.0 + jnp.exp(-x))


def _combine(x_ref, y_refs, gt_ref):
    x = x_ref[...]
    gt = gt_ref[...]
    for k in range(TOP_K):
        x = x + gt[:, k:k + 1] * y_refs[k][...].astype(F32)
    return x


def _inproj_body(has_add, *refs):
    if has_add:
        x_ref, y0, y1, y2, y3, gt_ref, g_ref, w_ref, wbt_ref, xo_ref, za, zbt, zc, zg = refs
        x = _combine(x_ref, (y0, y1, y2, y3), gt_ref)
        xo_ref[...] = x
    else:
        x_ref, g_ref, w_ref, wbt_ref, za, zbt, zc, zg = refs
        x = x_ref[...]
    h = _rms(x, g_ref[...]).astype(BF16)
    z = jnp.dot(h, w_ref[...], preferred_element_type=F32)
    za[...] = z[:, :D_POOL]
    zc[...] = z[:, D_POOL:D_POOL + D_LRU]
    zg[...] = z[:, D_POOL + D_LRU:]
    zt = lax.dot_general(wbt_ref[...], h, (((1,), (1,)), ((), ())), preferred_element_type=F32)
    per_step = zt.shape[1] // S5_CHUNK
    for half in range(ROW_SPLIT):
        @pl.when(pl.program_id(1) == half)
        def _():
            for n in range(per_step):
                zbt[:, half * per_step + n, :] = zt[:, n * S5_CHUNK:(n + 1) * S5_CHUNK]


def _inproj(x, add, g, w_rest, w_s5t):
    t = x.shape[0]
    tm = ROW_TILE
    nc = tm * ROW_SPLIT // S5_CHUNK
    row = lambda n: pl.BlockSpec((tm, n), lambda i, j: (i * ROW_SPLIT + j, 0))
    full = lambda a: pl.BlockSpec(a.shape, lambda i, j: (0,) * a.ndim)
    z_shapes = [jax.ShapeDtypeStruct((t, D_POOL), F32), jax.ShapeDtypeStruct((D_S5, t // S5_CHUNK, S5_CHUNK), F32),
                jax.ShapeDtypeStruct((t, D_LRU), F32), jax.ShapeDtypeStruct((t, D_LRU), F32)]
    z_specs = [row(D_POOL), pl.BlockSpec((D_S5, nc, S5_CHUNK), lambda i, j: (0, i, 0)), row(D_LRU), row(D_LRU)]
    if add is None:
        ins, in_specs = [x, g, w_rest, w_s5t], [row(D_MODEL), full(g), full(w_rest), full(w_s5t)]
        out_shape, out_specs = z_shapes, z_specs
    else:
        ys, gates = add
        ins = [x, *ys, gates, g, w_rest, w_s5t]
        in_specs = [row(D_MODEL)] + [row(D_MODEL)] * TOP_K + [row(TOP_K), full(g), full(w_rest), full(w_s5t)]
        out_shape = [jax.ShapeDtypeStruct((t, D_MODEL), F32)] + z_shapes
        out_specs = [row(D_MODEL)] + z_specs
    return pl.pallas_call(
        functools.partial(_inproj_body, add is not None),
        grid=(t // (tm * ROW_SPLIT), ROW_SPLIT), in_specs=in_specs, out_specs=out_specs, out_shape=out_shape,
        compiler_params=_cparams(("parallel", "arbitrary")), name="inproj")(*ins)


def _pool_body(seq, prev_ref, cur_ref, next_ref, w_ref, sc_ref, gain_ref, o_ref, u_s, s2_s, s4_s, s8_s):
    tl = cur_ref.shape[0]
    i = pl.program_id(1)
    nt = pl.num_programs(1)
    zero8 = jnp.zeros((HALO, D_POOL), F32)
    for buf in (u_s, s2_s, s4_s, s8_s):
        buf[0:HALO, :] = zero8
        buf[tl + 3 * HALO:tl + 4 * HALO, :] = zero8
    u_s[HALO:2 * HALO, :] = jnp.where(i > 0, prev_ref[...], 0.0)
    u_s[2 * HALO:2 * HALO + tl, :] = cur_ref[...]
    u_s[2 * HALO + tl:3 * HALO + tl, :] = jnp.where(i < nt - 1, next_ref[...], 0.0)
    r = tl + 2 * HALO
    s2_s[HALO:HALO + r, :] = u_s[HALO - 1:HALO - 1 + r, :] + u_s[HALO:HALO + r, :]
    s4_s[HALO:HALO + r, :] = s2_s[HALO - 1:HALO - 1 + r, :] + s2_s[HALO + 1:HALO + 1 + r, :]
    s8_s[HALO:HALO + r, :] = s4_s[HALO - 2:HALO - 2 + r, :] + s4_s[HALO + 2:HALO + 2 + r, :]
    o = 2 * HALO
    s16 = s8_s[o - 4:o - 4 + tl, :] + s8_s[o + 4:o + 4 + tl, :]
    s8 = s8_s[o:o + tl, :]
    s4 = s4_s[o:o + tl, :]
    s2 = s2_s[o:o + tl, :]
    u = u_s[o:o + tl, :]
    lane = lax.broadcasted_iota(jnp.int32, (tl, D_POOL), 1)
    tpos = lax.broadcasted_iota(jnp.int32, (tl, D_POOL), 0) + i * tl
    g0, g1, g2 = lane < POOL_GROUP, lane < 2 * POOL_GROUP, lane < 3 * POOL_GROUP
    half = jnp.where(g0, 1, jnp.where(g1, 2, jnp.where(g2, 4, 8)))
    wsum = jnp.where(g0, s2, jnp.where(g1, s4, jnp.where(g2, s8, s16)))
    cnt = (jnp.minimum(tpos + half, seq) - jnp.maximum(tpos - half, 0)).astype(F32)
    d = wsum / cnt - u
    y = jnp.dot(d.astype(BF16), w_ref[...], preferred_element_type=F32) * sc_ref[...]
    o_ref[...] = _rms(y, gain_ref[...])


def _pool_mixer(za, batch, seq, w_bd, scale, gain):
    t = za.shape[0]
    tl = POOL_TILE
    nt = seq // tl
    hb = tl // HALO
    nhb = t // HALO
    cur = pl.BlockSpec((tl, D_POOL), lambda b, i: (b * nt + i, 0))
    prev = pl.BlockSpec((HALO, D_POOL), lambda b, i: (jnp.maximum((b * nt + i) * hb - 1, 0), 0))
    nxt = pl.BlockSpec((HALO, D_POOL), lambda b, i: (jnp.minimum((b * nt + i + 1) * hb, nhb - 1), 0))
    full = lambda a: pl.BlockSpec(a.shape, lambda b, i: (0,) * a.ndim)
    return pl.pallas_call(
        functools.partial(_pool_body, seq),
        grid=(batch, nt),
        in_specs=[prev, cur, nxt, full(w_bd), full(scale), full(gain)],
        out_specs=cur,
        out_shape=jax.ShapeDtypeStruct((t, D_POOL), F32),
        scratch_shapes=[pltpu.VMEM((tl + 4 * HALO, D_POOL), F32)] * 4,
        compiler_params=_cparams(("parallel", "parallel")), name="pool_mixer")(za, za, za, w_bd, scale, gain)


def _s5_tables(lam_re, lam_im, log_step, b_re, b_im, c_re, c_im, n_chunks):
    L, C, P, G = S5_CHUNK, S5_GROUP, S5_STATE, S5_NGROUPS
    hp = lax.Precision.HIGHEST
    lr = lam_re.astype(F32)
    li = lam_im.astype(F32)
    dt = jnp.exp(log_step.astype(F32))[..., None]
    ar = lr * dt
    ai = li * dt
    jj = jnp.arange(L + 1, dtype=F32)

    def powers(j, a_r, a_i):
        mag = jnp.exp(j * a_r)
        return mag * jnp.cos(j * a_i), mag * jnp.sin(j * a_i)

    e_r, e_i = powers(jj[None, None, :, None], ar[:, :, None, :], ai[:, :, None, :])
    et_r, et_i = powers(jj[None, None, None, :], ar[..., None], ai[..., None])
    den = lr * lr + li * li
    nr = e_r[:, :, 1, :] - 1.0
    ni = e_i[:, :, 1, :]
    q_r = (nr * lr + ni * li) / den
    q_i = (ni * lr - nr * li) / den
    br = b_re.astype(F32)
    bi = b_im.astype(F32)
    bbt_r = (q_r[..., None] * br - q_i[..., None] * bi).transpose(0, 1, 3, 2)
    bbt_i = (q_r[..., None] * bi + q_i[..., None] * br).transpose(0, 1, 3, 2)
    cr = c_re.astype(F32)
    ci = c_im.astype(F32)
    m_r = cr[:, :, None] * bbt_r[:, :, :, None] - ci[:, :, None] * bbt_i[:, :, :, None]
    m_i = cr[:, :, None] * bbt_i[:, :, :, None] + ci[:, :, None] * bbt_r[:, :, :, None]
    kk = (jnp.einsum('dgxcp,dgpj->dgxcj', m_r, et_r[..., :L], precision=hp)
          - jnp.einsum('dgxcp,dgpj->dgxcj', m_i, et_i[..., :L], precision=hp))
    kf, kb = kk[0], kk[1]
    lagtab = jnp.concatenate([kb[..., :0:-1], kf[..., :1] + kb[..., :1], kf[..., 1:],
                              jnp.zeros_like(kf[..., :1])], axis=-1)
    lagtab = lagtab.reshape(G, C * C, 2 * L)
    def summary(er, ei, b_r, b_i):
        w_r = er[:, None] * b_r[:, :, None] - ei[:, None] * b_i[:, :, None]
        w_i = er[:, None] * b_i[:, :, None] + ei[:, None] * b_r[:, :, None]
        return w_r.reshape(G, C * L, P), w_i.reshape(G, C * L, P)
    wf_r, wf_i = summary(e_r[0, :, L - 1::-1], e_i[0, :, L - 1::-1], bbt_r[0], bbt_i[0])
    wb_r, wb_i = summary(e_r[1, :, :L], e_i[1, :, :L], bbt_r[1], bbt_i[1])
    wst = jnp.concatenate([wf_r, wb_r, wf_i, wb_i], axis=-1)
    def carry_ops(er, ei, c_r, c_i):
        ct_r = c_r.transpose(0, 2, 1)[..., None]
        ct_i = c_i.transpose(0, 2, 1)[..., None]
        v_r = ct_r * er[:, :, None] - ct_i * ei[:, :, None]
        v_i = -(ct_r * ei[:, :, None] + ct_i * er[:, :, None])
        return v_r.reshape(G, P, C * L), v_i.reshape(G, P, C * L)
    vf_r, vf_i = carry_ops(et_r[0, :, :, 1:L + 1], et_i[0, :, :, 1:L + 1], cr[0], ci[0])
    vb_r, vb_i = carry_ops(et_r[1, :, :, L:0:-1], et_i[1, :, :, L:0:-1], cr[1], ci[1])
    vmat = jnp.concatenate([vf_r, vb_r, vf_i, vb_i], axis=1)
    n_steps = max(1, int(math.ceil(math.log2(n_chunks))))
    kpow = (L * (2 ** jnp.arange(n_steps, dtype=F32)))[None, None, :, None]
    p_r, p_i = powers(kpow, ar[:, :, None, :], ai[:, :, None, :])
    pad = jnp.zeros((G, (-n_steps) % 8, 2 * P), F32)
    p_r = jnp.concatenate([jnp.concatenate([p_r[0], p_r[1]], axis=-1), pad], axis=1)
    p_i = jnp.concatenate([jnp.concatenate([p_i[0], p_i[1]], axis=-1), pad], axis=1)
    ptab = jnp.concatenate([p_r, p_i], axis=1)
    return lagtab, wst.astype(BF16), vmat.astype(BF16), ptab, n_steps


def _s5_body(n_chunks, n_steps, u_ref, lag_ref, w_ref, v_ref, p_ref, y_ref, toep_s):
    L, C = S5_CHUNK, S5_GROUP
    n = u_ref.shape[1]
    s_i = lax.broadcasted_iota(jnp.int32, (L, L), 0)
    t_i = lax.broadcasted_iota(jnp.int32, (L, L), 1)
    non_positive_lag = t_i <= s_i

    def build_block_row(cp, carry):
        r0 = pl.multiple_of(cp * L, L)
        for c in range(C):
            k = lag_ref[0, pl.ds(cp * C + c, 1), :]
            neg = pltpu.roll(jnp.broadcast_to(k[:, :L], (L, L)), 1, 1, stride=1, stride_axis=0)
            pos = pltpu.roll(jnp.broadcast_to(k[:, L:], (L, L)), 1, 1, stride=1, stride_axis=0)
            toep_s[pl.ds(r0, L), c * L:(c + 1) * L] = jnp.where(non_positive_lag, neg, pos).astype(BF16)
        return carry

    lax.fori_loop(0, C, build_block_row, 0)
    u = jnp.concatenate([u_ref[c] for c in range(C)], axis=1).astype(BF16)
    y = jnp.dot(u, toep_s[...], preferred_element_type=F32)
    s = jnp.dot(u, w_ref[0], preferred_element_type=F32)
    two_p = 2 * S5_STATE
    xr = s[:, :two_p]
    xi = s[:, two_p:]
    row = lax.broadcasted_iota(jnp.int32, (n, two_p), 0) % n_chunks
    is_fwd = lax.broadcasted_iota(jnp.int32, (n, two_p), 1) < S5_STATE
    pim0 = p_ref.shape[1] // 2

    def shifted(a, k):
        down = jnp.where(row >= k, pltpu.roll(a, k, 0), 0.0)
        up = jnp.where(row < n_chunks - k, pltpu.roll(a, n - k, 0), 0.0)
        return jnp.where(is_fwd, down, up)

    for j in range(n_steps):
        k = 1 << j
        pr = p_ref[0, j:j + 1, :]
        pi = p_ref[0, pim0 + j:pim0 + j + 1, :]
        sr = shifted(xr, k)
        si = shifted(xi, k)
        xr, xi = xr + pr * sr - pi * si, xi + pr * si + pi * sr
    carry = jnp.concatenate([shifted(xr, 1), shifted(xi, 1)], axis=1).astype(BF16)
    y = y + jnp.dot(carry, v_ref[0], preferred_element_type=F32)
    for c in range(C):
        y_ref[c] = y[:, c * L:(c + 1) * L]


def _s5_core(zbt, batch, seq, tables):
    lagtab, wst, vmat, ptab, n_steps = tables
    L = S5_CHUNK
    n_chunks = seq // L
    n = batch * n_chunks
    per_g = lambda a: pl.BlockSpec((1,) + a.shape[1:], lambda g: (g, 0, 0))
    grp = pl.BlockSpec((S5_GROUP, n, L), lambda g: (g, 0, 0))
    return pl.pallas_call(
        functools.partial(_s5_body, n_chunks, n_steps),
        grid=(S5_NGROUPS,),
        in_specs=[grp, per_g(lagtab), per_g(wst), per_g(vmat), per_g(ptab)],
        out_specs=grp,
        out_shape=jax.ShapeDtypeStruct((D_S5, n, L), F32),
        scratch_shapes=[pltpu.VMEM((S5_GROUP * L, S5_GROUP * L), BF16)],
        compiler_params=_cparams(("parallel",)), name="s5_core")(zbt, lagtab, wst, vmat, ptab)


def _lru_body(reverse, prev_ref, cur_ref, next_ref, cw_ref, cb_ref, wg_ref, bg_ref, sp_ref, *rest):
    if reverse:
        hf_ref, gate_ref, gain_ref, o_ref, x_s, a_s, b_s, h_s, carry_s = rest
    else:
        o_ref, x_s, a_s, b_s, h_s, carry_s = rest
    nb, tl, _ = cur_ref.shape
    step = pl.program_id(0)
    nt = pl.num_programs(0)
    ti = nt - 1 - step if reverse else step

    @pl.when(step == 0)
    def _():
        carry_s[...] = jnp.zeros_like(carry_s)

    cw = cw_ref[...]
    for b in range(nb):
        x_s[0:HALO, :] = jnp.where(ti > 0, prev_ref[b], 0.0)
        x_s[HALO:HALO + tl, :] = cur_ref[b]
        x_s[HALO + tl:2 * HALO + tl, :] = jnp.where(ti < nt - 1, next_ref[b], 0.0)
        xc = cb_ref[...]
        for k in range(4):
            xc = xc + cw[k:k + 1, :] * x_s[HALO - 1 + k:HALO - 1 + k + tl, :]
        gates = jnp.dot(xc.astype(BF16), wg_ref[...], preferred_element_type=F32) + bg_ref[...]
        r = _sigmoid(gates[:, :D_LRU])
        ig = _sigmoid(gates[:, D_LRU:])
        log_a = (-RG_C) * r * sp_ref[...]
        a = jnp.exp(log_a)
        a_s[b] = a
        om = 1.0 - a * a
        b_s[b] = (om * lax.rsqrt(jnp.maximum(om, 1e-30))) * (ig * xc)

    def scan_step(s, hs):
        t = tl - 1 - s if reverse else s
        out = []
        for b in range(nb):
            h = a_s[b, pl.ds(t, 1), :] * hs[b] + b_s[b, pl.ds(t, 1), :]
            h_s[b, pl.ds(t, 1), :] = h
            out.append(h)
        return tuple(out)

    hs = lax.fori_loop(0, tl, scan_step, tuple(carry_s[b:b + 1, :] for b in range(nb)), unroll=8)
    for b in range(nb):
        carry_s[b:b + 1, :] = hs[b]

    if reverse:
        for b in range(nb):
            y = (hf_ref[b] + h_s[b]) * _gelu(gate_ref[b])
            o_ref[b] = _rms(y, gain_ref[...])
    else:
        o_ref[...] = h_s[...]


def _lru_pass(reverse, zc3, params, extra):
    nb, seq, _ = zc3.shape
    tl = LRU_TILE
    nt = seq // tl
    hb = tl // HALO
    nhb = seq // HALO
    tix = (lambda i: nt - 1 - i) if reverse else (lambda i: i)
    cur = pl.BlockSpec((nb, tl, D_LRU), lambda i: (0, tix(i), 0))
    prev = pl.BlockSpec((nb, HALO, D_LRU), lambda i: (0, jnp.maximum(tix(i) * hb - 1, 0), 0))
    nxt = pl.BlockSpec((nb, HALO, D_LRU), lambda i: (0, jnp.minimum((tix(i) + 1) * hb, nhb - 1), 0))
    full = lambda a: pl.BlockSpec(a.shape, lambda i: (0,) * a.ndim)
    ins = [zc3, zc3, zc3, *params]
    in_specs = [prev, cur, nxt] + [full(a) for a in params]
    if reverse:
        hf, gate, gain = extra
        ins += [hf, gate, gain]
        in_specs += [cur, cur, full(gain)]
    return pl.pallas_call(
        functools.partial(_lru_body, reverse),
        grid=(nt,), in_specs=in_specs, out_specs=cur,
        out_shape=jax.ShapeDtypeStruct((nb, seq, D_LRU), F32),
        scratch_shapes=[pltpu.VMEM((tl + 2 * HALO, D_LRU), F32),
                        pltpu.VMEM((nb, tl, D_LRU), F32),
                        pltpu.VMEM((nb, tl, D_LRU), F32),
                        pltpu.VMEM((nb, tl, D_LRU), F32),
                        pltpu.VMEM((8, D_LRU), F32)],
        compiler_params=_cparams(("arbitrary",)),
        name="lru_bwd" if reverse else "lru_fwd")(*ins)


def _block_diag(w):
    h, d, _ = w.shape
    eye = jnp.eye(h, dtype=w.dtype)
    return (eye[:, None, :, None] * w[:, :, None, :]).reshape(h * d, h * d)


def _lru_mixer(zc, zg, batch, seq, conv_w, conv_b, w_a, b_a, w_x, b_x, lam, gain):
    zc3 = zc.reshape(batch, seq, D_LRU)
    zg3 = zg.reshape(batch, seq, D_LRU)
    outs = None
    for d in (0, 1):
        wg = jnp.concatenate([_block_diag(w_a[d]), _block_diag(w_x[d])], axis=1).astype(BF16)
        bg = jnp.concatenate([b_a[d], b_x[d]])[None, :]
        sp = jax.nn.softplus(-lam[d].astype(F32))[None, :]
        params = [conv_w, conv_b[None, :], wg, bg, sp]
        extra = None if d == 0 else (outs, zg3, gain[None, :])
        outs = _lru_pass(d == 1, zc3, params, extra)
    return outs.reshape(batch * seq, D_LRU)


def _mixout_body(x_ref, oa_ref, y3_ref, u3_ref, oc_ref, sd_ref, wglu_ref, bglu_ref, gb_ref,
                 wout_ref, g2_ref, rw_ref, rb_ref, xo_ref, h_ref, ti_ref, gt_ref, yt_s, ut_s):
    tm = x_ref.shape[0]
    per_step = tm // S5_CHUNK
    for half in range(ROW_SPLIT):
        @pl.when(pl.program_id(1) == half)
        def _():
            for n in range(per_step):
                cols = slice(n * S5_CHUNK, (n + 1) * S5_CHUNK)
                yt_s[:, cols] = y3_ref[:, half * per_step + n, :]
                ut_s[:, cols] = u3_ref[:, half * per_step + n, :]
    vt = _gelu(yt_s[...] + sd_ref[...] * ut_s[...]).astype(BF16)
    gl = lax.dot_general(vt, wglu_ref[...], (((0,), (0,)), ((), ())), preferred_element_type=F32) + bglu_ref[...]
    ob = _rms(gl[:, :D_S5] * _sigmoid(gl[:, D_S5:]), gb_ref[...])
    o = jnp.concatenate([oa_ref[...], ob, oc_ref[...]], axis=1).astype(BF16)
    x = x_ref[...] + jnp.dot(o, wout_ref[...], preferred_element_type=F32)
    xo_ref[...] = x
    h = _rms(x, g2_ref[...])
    h_hi = h.astype(BF16)
    bits = pltpu.bitcast(h_hi.astype(F32), jnp.uint32)
    h_ref[...] = lax.shift_right_logical(bits[:, :D_MODEL // 2], jnp.uint32(16)) | (
        bits[:, D_MODEL // 2:] & jnp.uint32(0xFFFF0000))
    h_lo = (h - h_hi.astype(F32)).astype(BF16)
    nt_dims = (((1,), (1,)), ((), ()))
    la = lax.dot_general(rw_ref[...], h_hi, nt_dims, preferred_element_type=F32)
    lb = lax.dot_general(rw_ref[0:N_EXPERTS, :], h_lo, nt_dims, preferred_element_type=F32)
    logits = la[:N_EXPERTS] + la[N_EXPERTS:] + lb + rb_ref[...]
    eidx = lax.broadcasted_iota(jnp.int32, (N_EXPERTS, tm), 0).astype(F32)
    idx_rows, val_rows = [], []
    for k in range(TOP_K):
        m = jnp.max(logits, axis=0, keepdims=True)
        idx = jnp.min(jnp.where(logits == m, eidx, float(N_EXPERTS)), axis=0, keepdims=True)
        idx_rows.append(idx)
        val_rows.append(jnp.exp(m - val_rows[0]) if k else m)
        logits = jnp.where(eidx == idx, -jnp.inf, logits)
    top = val_rows[0]
    vals = jnp.concatenate([jnp.ones_like(top)] + [v for v in val_rows[1:]], axis=0)
    ti_ref[...] = jnp.concatenate(idx_rows, axis=0).astype(jnp.int32)
    gt_ref[...] = vals / jnp.sum(vals, axis=0, keepdims=True)


def _mixout(x, oa, yt3, zbt3, oc, s5_d, wglu, bglu, gain_b, wout, g2, rw, rb):
    t = x.shape[0]
    tm = ROW_TILE
    nc = tm * ROW_SPLIT // S5_CHUNK
    row = lambda n: pl.BlockSpec((tm, n), lambda i, j: (i * ROW_SPLIT + j, 0))
    col = pl.BlockSpec((TOP_K, tm), lambda i, j: (0, i * ROW_SPLIT + j))
    chunks = pl.BlockSpec((D_S5, nc, S5_CHUNK), lambda i, j: (0, i, 0))
    full = lambda a: pl.BlockSpec(a.shape, lambda i, j: (0,) * a.ndim)
    rw_hi = rw.astype(BF16)
    rw_lo = (rw - rw_hi.astype(F32)).astype(BF16)
    rw2 = jnp.concatenate([rw_hi.T, rw_lo.T], axis=0)
    params = [s5_d.reshape(D_S5, 1), wglu, bglu, gain_b, wout, g2, rw2, rb.reshape(N_EXPERTS, 1)]
    return pl.pallas_call(
        _mixout_body,
        grid=(t // (tm * ROW_SPLIT), ROW_SPLIT),
        in_specs=[row(D_MODEL), row(D_POOL), chunks, chunks, row(D_LRU)] + [full(a) for a in params],
        out_specs=[row(D_MODEL), row(D_MODEL // 2), col, col],
        out_shape=[jax.ShapeDtypeStruct((t, D_MODEL), F32), jax.ShapeDtypeStruct((t, D_MODEL // 2), jnp.uint32),
                   jax.ShapeDtypeStruct((TOP_K, t), jnp.int32), jax.ShapeDtypeStruct((TOP_K, t), F32)],
        scratch_shapes=[pltpu.VMEM((D_S5, tm), F32), pltpu.VMEM((D_S5, tm), F32)],
        compiler_params=_cparams(("parallel", "arbitrary")), name="mix_out")(x, oa, yt3, zbt3, oc, *params)


def _expert_body(be_ref, nu_ref, first_ref, xl_ref, xh_ref, wgu_ref, bg_ref, bu_ref, wd_ref, bd_ref, p_ref, y_ref,
                 wg_s, wu_s, wd_s):
    i = pl.program_id(0)
    live = i < nu_ref[0]

    @pl.when(jnp.logical_and(live, first_ref[i] == 1))
    def _():
        p = p_ref[...]
        half = MXU_TILE // 2
        for j in range(wgu_ref.shape[3] // MXU_TILE):
            w = wgu_ref[0, 0, :, MXU_TILE * j:MXU_TILE * (j + 1)].astype(BF16)
            r = jnp.dot(w, p, preferred_element_type=F32)
            wg_s[:, half * j:half * (j + 1)] = r[:, :half].astype(BF16)
            wu_s[:, half * j:half * (j + 1)] = r[:, half:].astype(BF16)
        wd_s[...] = wd_ref[0, 0].astype(BF16)

    @pl.when(live)
    def _():
        words = jnp.concatenate([xl_ref[...], xh_ref[...]], axis=1)
        lo = pltpu.bitcast(lax.shift_left(words, jnp.uint32(16)), F32)
        hi = pltpu.bitcast(words & jnp.uint32(0xFFFF0000), F32)
        x = jnp.concatenate([lo, hi], axis=1).astype(BF16)
        g = jnp.dot(x, wg_s[...], preferred_element_type=F32) + bg_ref[0, 0]
        u = jnp.dot(x, wu_s[...], preferred_element_type=F32) + bu_ref[0, 0]
        g = jnp.minimum(g, SWIGLU_LIMIT)
        u = jnp.clip(u, -SWIGLU_LIMIT, SWIGLU_LIMIT)
        act = ((u + 1.0) * (g * _sigmoid(g * SWIGLU_ALPHA))).astype(BF16)
        y = jnp.dot(act, wd_s[...], preferred_element_type=F32) + bd_ref[0, 0]
        y_ref[...] = y.astype(y_ref.dtype)

    @pl.when(jnp.logical_not(live))
    def _():
        y_ref[...] = jnp.zeros_like(y_ref)


def _expert_mlp(layer, xs, blk_e, n_used, w_gate_up, b_gate, b_up, w_down, b_down):
    n_rows, dh = xs[0].shape
    bm = MOE_ROWS
    first = jnp.concatenate([jnp.ones((1,), jnp.int32), (blk_e[1:] != blk_e[:-1]).astype(jnp.int32)])
    idx = jnp.arange(MXU_TILE)
    perm = jax.nn.one_hot(jnp.where(idx % 2 == 0, idx // 2, MXU_TILE // 2 + idx // 2), MXU_TILE, dtype=BF16)
    rows = pl.BlockSpec((bm, dh), lambda i, be, nu, fi: (jnp.minimum(i, nu[0] - 1), 0))
    per_e = lambda a: pl.BlockSpec((1, 1) + a.shape[2:], lambda i, be, nu, fi: (layer, be[i], 0, 0))
    grid_spec = pltpu.PrefetchScalarGridSpec(
        num_scalar_prefetch=3, grid=(n_rows // bm,),
        in_specs=[rows, rows, per_e(w_gate_up), per_e(b_gate), per_e(b_up), per_e(w_down), per_e(b_down),
                  pl.BlockSpec(perm.shape, lambda i, be, nu, fi: (0, 0))],
        out_specs=pl.BlockSpec((bm, D_MODEL), lambda i, be, nu, fi: (i, 0)),
        scratch_shapes=[pltpu.VMEM((D_MODEL, D_FF), BF16), pltpu.VMEM((D_MODEL, D_FF), BF16),
                        pltpu.VMEM((D_FF, D_MODEL), BF16)])
    return pl.pallas_call(
        _expert_body, grid_spec=grid_spec,
        out_shape=jax.ShapeDtypeStruct((n_rows, D_MODEL), BF16),
        compiler_params=pltpu.CompilerParams(dimension_semantics=("arbitrary",),
                                             vmem_limit_bytes=EXPERT_VMEM_LIMIT),
        name="expert_mlp")(blk_e, n_used, first, *xs, w_gate_up, b_gate, b_up, w_down, b_down, perm)


def _sc_dispatch(h, dest, n_rows):
    t, d = h.shape
    win = SC_WINDOW
    dh = d // SC_COL_SPLIT
    idx = [dest[:, k].reshape(1, t) for k in range(TOP_K)]
    mesh = plsc.VectorSubcoreMesh(core_axis_name="core", subcore_axis_name="subcore")

    def scatter_cols(j):
        @functools.partial(pl.kernel, out_type=jax.ShapeDtypeStruct((n_rows, dh), h.dtype), mesh=mesh,
                           scratch_types=[], name="sc_dispatch")
        def scatter_rows(x_hbm, i0_hbm, i1_hbm, i2_hbm, i3_hbm, o_hbm):
            def body(x_vmem, *idx_vmem):
                for iv in idx_vmem:
                    pltpu.sync_copy(x_vmem, o_hbm.at[iv.at[0]])

            pltpu.emit_pipeline(
                body, grid=(t // win,),
                in_specs=[pl.BlockSpec((win, dh), lambda i: (i, j))]
                         + [pl.BlockSpec((1, win), lambda i: (0, i))] * TOP_K,
                out_specs=[],
                core_axis_name=("core", "subcore"),
                dimension_semantics=(pltpu.PARALLEL,),
            )(x_hbm, i0_hbm, i1_hbm, i2_hbm, i3_hbm)

        return scatter_rows(h, *idx)

    return [scatter_cols(j) for j in range(SC_COL_SPLIT)]


def _moe_dispatch(top_i, t):
    bm = MOE_ROWS
    n_blocks = (t * TOP_K) // bm + N_EXPERTS
    onehot = jax.nn.one_hot(top_i, N_EXPERTS, dtype=jnp.int32)
    sel = jnp.sum(onehot, axis=1)
    csum = jnp.cumsum(sel, axis=0)
    counts = csum[-1]
    padded = ((counts + bm - 1) // bm) * bm
    pad_end = jnp.cumsum(padded)
    pad_start = pad_end - padded
    base = (csum - sel) + pad_start[None, :]
    dest = jnp.sum(onehot * base[:, None, :], axis=-1)
    blk_start = jnp.arange(n_blocks, dtype=jnp.int32) * bm
    blk_e = jnp.minimum(jnp.sum((pad_end[None, :] <= blk_start[:, None]).astype(jnp.int32), axis=1),
                        N_EXPERTS - 1)
    n_used = (pad_end[-1] // bm).astype(jnp.int32).reshape(1)
    return dest, n_blocks * bm, blk_e, n_used


def _final_body(x_ref, y0, y1, y2, y3, gt_ref, g_ref, o_ref):
    o_ref[...] = _rms(_combine(x_ref, (y0, y1, y2, y3), gt_ref), g_ref[...])


def _final(x, ys, gates, g):
    t = x.shape[0]
    tm = ROW_TILE
    row = lambda n: pl.BlockSpec((tm, n), lambda i: (i, 0))
    return pl.pallas_call(
        _final_body, grid=(t // tm,),
        in_specs=[row(D_MODEL)] * (1 + TOP_K) + [row(TOP_K), pl.BlockSpec(g.shape, lambda i: (0, 0))],
        out_specs=row(D_MODEL), out_shape=jax.ShapeDtypeStruct((t, D_MODEL), F32),
        compiler_params=_cparams(("parallel",)), name="final_norm")(x, *ys, gates, g)


def kernel(x, norm1_g, w_in, pool_w, pool_scale, s5_lam_re, s5_lam_im, s5_log_step, s5_b_re, s5_b_im, s5_c_re, s5_c_im, s5_d, s5_w_glu, s5_b_glu, lru_conv_w, lru_conv_b, lru_w_a, lru_b_a, lru_w_x, lru_b_x, lru_lam, mix_gain, w_out, norm2_g, router_w, router_b, w_gate_up, b_gate_up, w_down, b_down, final_g):
    batch, seq, d = x.shape
    t = batch * seq
    depth = norm1_g.shape[0]
    xt = x.reshape(t, d)
    add = None
    o1, o2 = D_POOL, D_POOL + D_S5
    b_gate = b_gate_up[:, :, None, 0::2]
    b_up = b_gate_up[:, :, None, 1::2]
    b_dn = b_down[:, :, None, :]
    for l in range(depth):
        w_rest = jnp.concatenate([w_in[l][:, :o1], w_in[l][:, o2:]], axis=1).astype(BF16)
        w_s5t = w_in[l][:, o1:o2].T.astype(BF16)
        if add is None:
            za, zbt, zc, zg = _inproj(xt, None, norm1_g[l][None, :], w_rest, w_s5t)
        else:
            xt, za, zbt, zc, zg = _inproj(xt, add, norm1_g[l][None, :], w_rest, w_s5t)
        oa = _pool_mixer(za, batch, seq, _block_diag(pool_w[l]).astype(BF16), pool_scale[l][None, :],
                         mix_gain[l][None, :o1])
        tables = _s5_tables(s5_lam_re[l], s5_lam_im[l], s5_log_step[l], s5_b_re[l], s5_b_im[l],
                            s5_c_re[l], s5_c_im[l], seq // S5_CHUNK)
        yt = _s5_core(zbt, batch, seq, tables)
        oc = _lru_mixer(zc, zg, batch, seq, lru_conv_w[l], lru_conv_b[l], lru_w_a[l], lru_b_a[l],
                        lru_w_x[l], lru_b_x[l], lru_lam[l], mix_gain[l][o2:])
        xt, h2, top_i, gates = _mixout(
            xt, oa, yt, zbt, oc, s5_d[l], s5_w_glu[l].astype(BF16), s5_b_glu[l][None, :],
            mix_gain[l][None, o1:o2], w_out[l].astype(BF16), norm2_g[l][None, :], router_w[l], router_b[l])
        top_i, gates = top_i.T, gates.T
        dest, n_rows, blk_e, n_used = _moe_dispatch(top_i, t)
        xs = _sc_dispatch(h2, dest, n_rows)
        ys = _expert_mlp(l, xs, blk_e, n_used, w_gate_up, b_gate, b_up, w_down, b_dn)
        add = ([ys.at[dest[:, k]].get(mode="promise_in_bounds") for k in range(TOP_K)], gates)
    out = _final(xt, add[0], add[1], final_g[None, :])
    return out.reshape(batch, seq, d)
```

```python
import functools
import math

import jax
import jax.numpy as jnp
from jax import lax
from jax.experimental import pallas as pl
from jax.experimental.pallas import tpu as pltpu
from jax.experimental.pallas import tpu_sc as plsc

F32 = jnp.float32
BF16 = jnp.bfloat16

D_MODEL = 1024
D_POOL = 256
D_S5 = 384
D_LRU = 384
POOL_WINDOWS = (2, 4, 8, 16)
POOL_GROUP = 64
S5_GROUP = 16
S5_NGROUPS = 24
S5_STATE = 64
LRU_HEADS = 6
LRU_HEAD_DIM = 64
RG_C = 8.0
N_EXPERTS = 32
TOP_K = 4
D_FF = 1024
SWIGLU_LIMIT = 7.0
SWIGLU_ALPHA = 1.702
EPS = 1e-5

ROW_TILE = 512
ROW_SPLIT = 2
S5_CHUNK = 128
POOL_TILE = 512
LRU_TILE = 256
HALO = 8
MOE_ROWS = 512
SC_WINDOW = 128
SC_COL_SPLIT = 2
MXU_TILE = 256
EXPERT_VMEM_LIMIT = 56 * 1024 * 1024
VMEM_LIMIT = 48 * 1024 * 1024


def _cparams(sem):
    return pltpu.CompilerParams(dimension_semantics=sem, vmem_limit_bytes=VMEM_LIMIT)


def _rms(x, g):
    return x * lax.rsqrt(jnp.mean(x * x, axis=-1, keepdims=True) + EPS) * g


def _gelu(x):
    return 0.5 * x * (1.0 + jnp.tanh(0.7978845608028654 * (x + 0.044715 * (x * x * x))))


def _sigmoid(x):
    return 1.0 / (1.0 + jnp.exp(-x))


def _combine(x_ref, y_refs, gt_ref):
    x = x_ref[...]
    gt = gt_ref[...]
    for k in range(TOP_K):
        x = x + gt[:, k:k + 1] * y_refs[k][...].astype(F32)
    return x


def _inproj_body(has_add, *refs):
    if has_add:
        x_ref, y0, y1, y2, y3, gt_ref, g_ref, w_ref, wbt_ref, xo_ref, za, zbt, zc, zg = refs
        x = _combine(x_ref, (y0, y1, y2, y3), gt_ref)
        xo_ref[...] = x
    else:
        x_ref, g_ref, w_ref, wbt_ref, za, zbt, zc, zg = refs
        x = x_ref[...]
    h = _rms(x, g_ref[...]).astype(BF16)
    z = jnp.dot(h, w_ref[...], preferred_element_type=F32)
    za[...] = z[:, :D_POOL]
    zc[...] = z[:, D_POOL:D_POOL + D_LRU]
    zg[...] = z[:, D_POOL + D_LRU:]
    zt = lax.dot_general(wbt_ref[...], h, (((1,), (1,)), ((), ())), preferred_element_type=F32)
    per_step = zt.shape[1] // S5_CHUNK
    for n in range(per_step):
        zbt[:, pl.program_id(1) * per_step + n, :] = zt[:, n * S5_CHUNK:(n + 1) * S5_CHUNK]


def _inproj(x, add, g, w_rest, w_s5t):
    t = x.shape[0]
    tm = ROW_TILE
    nc = tm * ROW_SPLIT // S5_CHUNK
    row = lambda n: pl.BlockSpec((tm, n), lambda i, j: (i * ROW_SPLIT + j, 0))
    full = lambda a: pl.BlockSpec(a.shape, lambda i, j: (0,) * a.ndim)
    z_shapes = [jax.ShapeDtypeStruct((t, D_POOL), F32), jax.ShapeDtypeStruct((D_S5, t // S5_CHUNK, S5_CHUNK), F32),
                jax.ShapeDtypeStruct((t, D_LRU), F32), jax.ShapeDtypeStruct((t, D_LRU), F32)]
    z_specs = [row(D_POOL), pl.BlockSpec((D_S5, nc, S5_CHUNK), lambda i, j: (0, i, 0)), row(D_LRU), row(D_LRU)]
    if add is None:
        ins, in_specs = [x, g, w_rest, w_s5t], [row(D_MODEL), full(g), full(w_rest), full(w_s5t)]
        out_shape, out_specs = z_shapes, z_specs
    else:
        ys, gates = add
        ins = [x, *ys, gates, g, w_rest, w_s5t]
        in_specs = [row(D_MODEL)] + [row(D_MODEL)] * TOP_K + [row(TOP_K), full(g), full(w_rest), full(w_s5t)]
        out_shape = [jax.ShapeDtypeStruct((t, D_MODEL), F32)] + z_shapes
        out_specs = [row(D_MODEL)] + z_specs
    return pl.pallas_call(
        functools.partial(_inproj_body, add is not None),
        grid=(t // (tm * ROW_SPLIT), ROW_SPLIT), in_specs=in_specs, out_specs=out_specs, out_shape=out_shape,
        compiler_params=_cparams(("parallel", "arbitrary")), name="inproj")(*ins)


def _pool_body(seq, prev_ref, cur_ref, next_ref, w_ref, sc_ref, gain_ref, o_ref, u_s, s2_s, s4_s, s8_s):
    tl = cur_ref.shape[0]
    i = pl.program_id(1)
    nt = pl.num_programs(1)
    zero8 = jnp.zeros((HALO, D_POOL), F32)
    for buf in (u_s, s2_s, s4_s, s8_s):
        buf[0:HALO, :] = zero8
        buf[tl + 3 * HALO:tl + 4 * HALO, :] = zero8
    u_s[HALO:2 * HALO, :] = jnp.where(i > 0, prev_ref[...], 0.0)
    u_s[2 * HALO:2 * HALO + tl, :] = cur_ref[...]
    u_s[2 * HALO + tl:3 * HALO + tl, :] = jnp.where(i < nt - 1, next_ref[...], 0.0)
    r = tl + 2 * HALO
    s2_s[HALO:HALO + r, :] = u_s[HALO - 1:HALO - 1 + r, :] + u_s[HALO:HALO + r, :]
    s4_s[HALO:HALO + r, :] = s2_s[HALO - 1:HALO - 1 + r, :] + s2_s[HALO + 1:HALO + 1 + r, :]
    s8_s[HALO:HALO + r, :] = s4_s[HALO - 2:HALO - 2 + r, :] + s4_s[HALO + 2:HALO + 2 + r, :]
    o = 2 * HALO
    s16 = s8_s[o - 4:o - 4 + tl, :] + s8_s[o + 4:o + 4 + tl, :]
    s8 = s8_s[o:o + tl, :]
    s4 = s4_s[o:o + tl, :]
    s2 = s2_s[o:o + tl, :]
    u = u_s[o:o + tl, :]
    lane = lax.broadcasted_iota(jnp.int32, (tl, D_POOL), 1)
    tpos = lax.broadcasted_iota(jnp.int32, (tl, D_POOL), 0) + i * tl
    g0, g1, g2 = lane < POOL_GROUP, lane < 2 * POOL_GROUP, lane < 3 * POOL_GROUP
    half = jnp.where(g0, 1, jnp.where(g1, 2, jnp.where(g2, 4, 8)))
    wsum = jnp.where(g0, s2, jnp.where(g1, s4, jnp.where(g2, s8, s16)))
    cnt = (jnp.minimum(tpos + half, seq) - jnp.maximum(tpos - half, 0)).astype(F32)
    d = wsum / cnt - u
    y = jnp.dot(d.astype(BF16), w_ref[...], preferred_element_type=F32) * sc_ref[...]
    o_ref[...] = _rms(y, gain_ref[...])


def _pool_mixer(za, batch, seq, w_bd, scale, gain):
    t = za.shape[0]
    tl = POOL_TILE
    nt = seq // tl
    hb = tl // HALO
    nhb = t // HALO
    cur = pl.BlockSpec((tl, D_POOL), lambda b, i: (b * nt + i, 0))
    prev = pl.BlockSpec((HALO, D_POOL), lambda b, i: (jnp.maximum((b * nt + i) * hb - 1, 0), 0))
    nxt = pl.BlockSpec((HALO, D_POOL), lambda b, i: (jnp.minimum((b * nt + i + 1) * hb, nhb - 1), 0))
    full = lambda a: pl.BlockSpec(a.shape, lambda b, i: (0,) * a.ndim)
    return pl.pallas_call(
        functools.partial(_pool_body, seq),
        grid=(batch, nt),
        in_specs=[prev, cur, nxt, full(w_bd), full(scale), full(gain)],
        out_specs=cur,
        out_shape=jax.ShapeDtypeStruct((t, D_POOL), F32),
        scratch_shapes=[pltpu.VMEM((tl + 4 * HALO, D_POOL), F32)] * 4,
        compiler_params=_cparams(("parallel", "parallel")), name="pool_mixer")(za, za, za, w_bd, scale, gain)


def _s5_scan_steps(n_chunks):
    return max(1, int(math.ceil(math.log2(n_chunks))))


def _s5_tables(lam_re, lam_im, log_step, b_re, b_im, c_re, c_im, n_chunks):
    L, C, P, G = S5_CHUNK, S5_GROUP, S5_STATE, S5_NGROUPS
    hp = lax.Precision.HIGHEST
    lr = lam_re.astype(F32)
    li = lam_im.astype(F32)
    dt = jnp.exp(log_step.astype(F32))[..., None]
    ar = lr * dt
    ai = li * dt
    jj = jnp.arange(L + 1, dtype=F32)

    def powers(j, a_r, a_i):
        mag = jnp.exp(j * a_r)
        return mag * jnp.cos(j * a_i), mag * jnp.sin(j * a_i)

    e_r, e_i = powers(jj[None, None, :, None], ar[:, :, None, :], ai[:, :, None, :])
    et_r, et_i = powers(jj[None, None, None, :], ar[..., None], ai[..., None])
    den = lr * lr + li * li
    nr = e_r[:, :, 1, :] - 1.0
    ni = e_i[:, :, 1, :]
    q_r = (nr * lr + ni * li) / den
    q_i = (ni * lr - nr * li) / den
    br = b_re.astype(F32)
    bi = b_im.astype(F32)
    bbt_r = (q_r[..., None] * br - q_i[..., None] * bi).transpose(0, 1, 3, 2)
    bbt_i = (q_r[..., None] * bi + q_i[..., None] * br).transpose(0, 1, 3, 2)
    cr = c_re.astype(F32)
    ci = c_im.astype(F32)
    m_r = cr[:, :, None] * bbt_r[:, :, :, None] - ci[:, :, None] * bbt_i[:, :, :, None]
    m_i = cr[:, :, None] * bbt_i[:, :, :, None] + ci[:, :, None] * bbt_r[:, :, :, None]
    kk = (jnp.einsum('dgxcp,dgpj->dgxcj', m_r, et_r[..., :L], precision=hp)
          - jnp.einsum('dgxcp,dgpj->dgxcj', m_i, et_i[..., :L], precision=hp))
    kf, kb = kk[0], kk[1]
    lagtab = jnp.concatenate([kb[..., :0:-1], kf[..., :1] + kb[..., :1], kf[..., 1:],
                              jnp.zeros_like(kf[..., :1])], axis=-1)
    bits = lax.bitcast_convert_type(lagtab.astype(BF16).astype(F32), jnp.uint32).reshape(G, C * C, 2 * L)
    lagtab = lax.shift_right_logical(bits[..., :L], jnp.uint32(16)) | (bits[..., L:] & jnp.uint32(0xFFFF0000))
    def summary(er, ei, b_r, b_i):
        w_r = er[:, None] * b_r[:, :, None] - ei[:, None] * b_i[:, :, None]
        w_i = er[:, None] * b_i[:, :, None] + ei[:, None] * b_r[:, :, None]
        return w_r.reshape(G, C * L, P), w_i.reshape(G, C * L, P)
    wf_r, wf_i = summary(e_r[0, :, L - 1::-1], e_i[0, :, L - 1::-1], bbt_r[0], bbt_i[0])
    wb_r, wb_i = summary(e_r[1, :, :L], e_i[1, :, :L], bbt_r[1], bbt_i[1])
    wst = jnp.concatenate([wf_r, wb_r, wf_i, wb_i], axis=-1)
    def carry_ops(er, ei, c_r, c_i):
        ct_r = c_r.transpose(0, 2, 1)[..., None]
        ct_i = c_i.transpose(0, 2, 1)[..., None]
        v_r = ct_r * er[:, :, None] - ct_i * ei[:, :, None]
        v_i = -(ct_r * ei[:, :, None] + ct_i * er[:, :, None])
        return v_r.reshape(G, P, C * L), v_i.reshape(G, P, C * L)
    vf_r, vf_i = carry_ops(et_r[0, :, :, 1:L + 1], et_i[0, :, :, 1:L + 1], cr[0], ci[0])
    vb_r, vb_i = carry_ops(et_r[1, :, :, L:0:-1], et_i[1, :, :, L:0:-1], cr[1], ci[1])
    vmat = jnp.concatenate([vf_r, vb_r, vf_i, vb_i], axis=1)
    n_steps = _s5_scan_steps(n_chunks)
    kpow =(L * (2 ** jnp.arange(n_steps, dtype=F32)))[None, None, :, None]
    p_r, p_i = powers(kpow, ar[:, :, None, :], ai[:, :, None, :])
    pad = jnp.zeros((G, (-n_steps) % 8, 2 * P), F32)
    p_r = jnp.concatenate([jnp.concatenate([p_r[0], p_r[1]], axis=-1), pad], axis=1)
    p_i = jnp.concatenate([jnp.concatenate([p_i[0], p_i[1]], axis=-1), pad], axis=1)
    ptab = jnp.concatenate([p_r, p_i], axis=1)
    return lagtab, wst.astype(BF16), vmat.astype(BF16), ptab


def _s5_body(n_chunks, n_steps, u_ref, lag_ref, w_ref, v_ref, p_ref, y_ref, toep_s):
    L, C = S5_CHUNK, S5_GROUP
    n = u_ref.shape[1]
    s_i = lax.broadcasted_iota(jnp.int32, (L, L), 0)
    t_i = lax.broadcasted_iota(jnp.int32, (L, L), 1)
    non_positive_lag = t_i <= s_i

    def build_block_row(cp, carry):
        r0 = pl.multiple_of(cp * L, L)
        for c in range(C):
            k = lag_ref[0, pl.ds(cp * C + c, 1), :]
            r = pltpu.roll(jnp.broadcast_to(k, (L, L)), 1, 1, stride=1, stride_axis=0)
            bits = jnp.where(non_positive_lag, lax.shift_left(r, jnp.uint32(16)), r & jnp.uint32(0xFFFF0000))
            toep_s[pl.ds(r0, L), c * L:(c + 1) * L] = pltpu.bitcast(bits, F32).astype(BF16)
        return carry

    lax.fori_loop(0, C, build_block_row, 0)
    u = jnp.concatenate([u_ref[c] for c in range(C)], axis=1).astype(BF16)
    y = jnp.dot(u, toep_s[...], preferred_element_type=F32)
    s = jnp.dot(u, w_ref[0], preferred_element_type=F32)
    two_p = 2 * S5_STATE
    xr = s[:, :two_p]
    xi = s[:, two_p:]
    row = lax.broadcasted_iota(jnp.int32, (n, two_p), 0) % n_chunks
    is_fwd = lax.broadcasted_iota(jnp.int32, (n, two_p), 1) < S5_STATE
    pim0 = p_ref.shape[1] // 2

    def shifted(a, k):
        down = jnp.where(row >= k, pltpu.roll(a, k, 0), 0.0)
        up = jnp.where(row < n_chunks - k, pltpu.roll(a, n - k, 0), 0.0)
        return jnp.where(is_fwd, down, up)

    for j in range(n_steps):
        k = 1 << j
        pr = p_ref[0, j:j + 1, :]
        pi = p_ref[0, pim0 + j:pim0 + j + 1, :]
        sr = shifted(xr, k)
        si = shifted(xi, k)
        xr, xi = xr + pr * sr - pi * si, xi + pr * si + pi * sr
    carry = jnp.concatenate([shifted(xr, 1), shifted(xi, 1)], axis=1).astype(BF16)
    y = y + jnp.dot(carry, v_ref[0], preferred_element_type=F32)
    for c in range(C):
        y_ref[c] = y[:, c * L:(c + 1) * L]


def _s5_core(zbt, batch, seq, tables):
    lagtab, wst, vmat, ptab = tables
    L = S5_CHUNK
    n_chunks = seq // L
    n_steps = _s5_scan_steps(n_chunks)
    n = batch * n_chunks
    per_g = lambda a: pl.BlockSpec((1,) + a.shape[1:], lambda g: (g, 0, 0))
    grp = pl.BlockSpec((S5_GROUP, n, L), lambda g: (g, 0, 0))
    return pl.pallas_call(
        functools.partial(_s5_body, n_chunks, n_steps),
        grid=(S5_NGROUPS,),
        in_specs=[grp, per_g(lagtab), per_g(wst), per_g(vmat), per_g(ptab)],
        out_specs=grp,
        out_shape=jax.ShapeDtypeStruct((D_S5, n, L), F32),
        scratch_shapes=[pltpu.VMEM((S5_GROUP * L, S5_GROUP * L), BF16)],
        compiler_params=_cparams(("parallel",)), name="s5_core")(zbt, lagtab, wst, vmat, ptab)


def _lru_body(reverse, prev_ref, cur_ref, next_ref, cw_ref, cb_ref, wg_ref, bg_ref, sp_ref, *rest):
    if reverse:
        hf_ref, gate_ref, gain_ref, o_ref, x_s, a_s, b_s, h_s, carry_s = rest
    else:
        o_ref, x_s, a_s, b_s, h_s, carry_s = rest
    nb, tl, _ = cur_ref.shape
    step = pl.program_id(0)
    nt = pl.num_programs(0)
    ti = nt - 1 - step if reverse else step

    @pl.when(step == 0)
    def _():
        carry_s[...] = jnp.zeros_like(carry_s)

    cw = cw_ref[...]
    for b in range(nb):
        x_s[0:HALO, :] = jnp.where(ti > 0, prev_ref[b], 0.0)
        x_s[HALO:HALO + tl, :] = cur_ref[b]
        x_s[HALO + tl:2 * HALO + tl, :] = jnp.where(ti < nt - 1, next_ref[b], 0.0)
        xc = cb_ref[...]
        for k in range(4):
            xc = xc + cw[k:k + 1, :] * x_s[HALO - 1 + k:HALO - 1 + k + tl, :]
        gates = jnp.dot(xc.astype(BF16), wg_ref[...], preferred_element_type=F32) + bg_ref[...]
        r = _sigmoid(gates[:, :D_LRU])
        ig = _sigmoid(gates[:, D_LRU:])
        log_a = (-RG_C) * r * sp_ref[...]
        a = jnp.exp(log_a)
        a_s[b] = a
        om = 1.0 - a * a
        b_s[b] = (om * lax.rsqrt(jnp.maximum(om, 1e-30))) * (ig * xc)

    def scan_step(s, hs):
        t = tl - 1 - s if reverse else s
        out = []
        for b in range(nb):
            h = a_s[b, pl.ds(t, 1), :] * hs[b] + b_s[b, pl.ds(t, 1), :]
            h_s[b, pl.ds(t, 1), :] = h
            out.append(h)
        return tuple(out)

    hs = lax.fori_loop(0, tl, scan_step, tuple(carry_s[b:b + 1, :] for b in range(nb)), unroll=8)
    for b in range(nb):
        carry_s[b:b + 1, :] = hs[b]

    if reverse:
        for b in range(nb):
            y = (hf_ref[b] + h_s[b]) * _gelu(gate_ref[b])
            o_ref[b] = _rms(y, gain_ref[...])
    else:
        o_ref[...] = h_s[...]


def _lru_pass(reverse, zc3, params, extra):
    nb, seq, _ = zc3.shape
    tl = LRU_TILE
    nt = seq // tl
    hb = tl // HALO
    nhb = seq // HALO
    tix = (lambda i: nt - 1 - i) if reverse else (lambda i: i)
    cur = pl.BlockSpec((nb, tl, D_LRU), lambda i: (0, tix(i), 0))
    prev = pl.BlockSpec((nb, HALO, D_LRU), lambda i: (0, jnp.maximum(tix(i) * hb - 1, 0), 0))
    nxt = pl.BlockSpec((nb, HALO, D_LRU), lambda i: (0, jnp.minimum((tix(i) + 1) * hb, nhb - 1), 0))
    full = lambda a: pl.BlockSpec(a.shape, lambda i: (0,) * a.ndim)
    ins = [zc3, zc3, zc3, *params]
    in_specs = [prev, cur, nxt] + [full(a) for a in params]
    if reverse:
        hf, gate, gain = extra
        ins += [hf, gate, gain]
        in_specs += [cur, cur, full(gain)]
    return pl.pallas_call(
        functools.partial(_lru_body, reverse),
        grid=(nt,), in_specs=in_specs, out_specs=cur,
        out_shape=jax.ShapeDtypeStruct((nb, seq, D_LRU), F32),
        scratch_shapes=[pltpu.VMEM((tl + 2 * HALO, D_LRU), F32),
                        pltpu.VMEM((nb, tl, D_LRU), F32),
                        pltpu.VMEM((nb, tl, D_LRU), F32),
                        pltpu.VMEM((nb, tl, D_LRU), F32),
                        pltpu.VMEM((8, D_LRU), F32)],
        compiler_params=_cparams(("arbitrary",)),
        name="lru_bwd" if reverse else "lru_fwd")(*ins)


def _block_diag(w):
    h, d, _ = w.shape
    eye = jnp.eye(h, dtype=w.dtype)
    return (eye[:, None, :, None] * w[:, :, None, :]).reshape(h * d, h * d)


def _lru_params(w_a, b_a, w_x, b_x, lam):
    bd = jax.vmap(jax.vmap(_block_diag))
    wg = jnp.concatenate([bd(w_a), bd(w_x)], axis=-1).astype(BF16)
    bg = jnp.concatenate([b_a, b_x], axis=-1)[:, :, None, :]
    sp = jax.nn.softplus(-lam.astype(F32))[:, :, None, :]
    return wg, bg, sp


def _lru_mixer(zc, zg, batch, seq, conv_w, conv_b, wg, bg, sp, gain):
    zc3 = zc.reshape(batch, seq, D_LRU)
    zg3 = zg.reshape(batch, seq, D_LRU)
    outs = None
    for d in (0, 1):
        params = [conv_w, conv_b[None, :], wg[d], bg[d], sp[d]]
        extra = None if d == 0 else (outs, zg3, gain[None, :])
        outs = _lru_pass(d == 1, zc3, params, extra)
    return outs.reshape(batch * seq, D_LRU)


def _mixout_body(x_ref, oa_ref, y3_ref, u3_ref, oc_ref, sd_ref, wglu_ref, bglu_ref, gb_ref,
                 wout_ref, g2_ref, rw_ref, rb_ref, xo_ref, h_ref, ti_ref, gt_ref):
    tm = x_ref.shape[0]
    per_step = tm // S5_CHUNK
    c0 = pl.program_id(1) * per_step
    yt = jnp.concatenate([y3_ref[:, c0 + n, :] for n in range(per_step)], axis=1)
    ut = jnp.concatenate([u3_ref[:, c0 + n, :] for n in range(per_step)], axis=1)
    vt = _gelu(yt + sd_ref[...] * ut).astype(BF16)
    gl = lax.dot_general(vt, wglu_ref[...], (((0,), (0,)), ((), ())), preferred_element_type=F32) + bglu_ref[...]
    ob = _rms(gl[:, :D_S5] * _sigmoid(gl[:, D_S5:]), gb_ref[...])
    o = jnp.concatenate([oa_ref[...], ob, oc_ref[...]], axis=1).astype(BF16)
    x = x_ref[...] + jnp.dot(o, wout_ref[...], preferred_element_type=F32)
    xo_ref[...] = x
    h = _rms(x, g2_ref[...])
    h_hi = h.astype(BF16)
    bits = pltpu.bitcast(h_hi.astype(F32), jnp.uint32)
    h_ref[...] = lax.shift_right_logical(bits[:, :D_MODEL // 2], jnp.uint32(16)) | (
        bits[:, D_MODEL // 2:] & jnp.uint32(0xFFFF0000))
    h_lo = (h - h_hi.astype(F32)).astype(BF16)
    nt_dims = (((1,), (1,)), ((), ()))
    la = lax.dot_general(rw_ref[...], h_hi, nt_dims, preferred_element_type=F32)
    lb = lax.dot_general(rw_ref[0:N_EXPERTS, :], h_lo, nt_dims, preferred_element_type=F32)
    logits = la[:N_EXPERTS] + la[N_EXPERTS:] + lb + rb_ref[...]
    eidx = lax.broadcasted_iota(jnp.int32, (N_EXPERTS, tm), 0).astype(F32)
    idx_rows, val_rows = [], []
    for k in range(TOP_K):
        m = jnp.max(logits, axis=0, keepdims=True)
        idx = jnp.min(jnp.where(logits == m, eidx, float(N_EXPERTS)), axis=0, keepdims=True)
        idx_rows.append(idx)
        val_rows.append(jnp.exp(m - val_rows[0]) if k else m)
        logits = jnp.where(eidx == idx, -jnp.inf, logits)
    top = val_rows[0]
    vals = jnp.concatenate([jnp.ones_like(top)] + [v for v in val_rows[1:]], axis=0)
    ti_ref[...] = jnp.concatenate(idx_rows, axis=0).astype(jnp.int32)
    gt_ref[...] = vals / jnp.sum(vals, axis=0, keepdims=True)


def _router_split(rw):
    rw_hi = rw.astype(BF16)
    rw_lo = (rw - rw_hi.astype(F32)).astype(BF16)
    return jnp.concatenate([jnp.swapaxes(rw_hi, -1, -2), jnp.swapaxes(rw_lo, -1, -2)], axis=-2)


def _mixout(x, oa, yt3, zbt3, oc, s5_d, wglu, bglu, gain_b, wout, g2, rw2, rb):
    t = x.shape[0]
    tm = ROW_TILE
    nc = tm * ROW_SPLIT // S5_CHUNK
    row = lambda n: pl.BlockSpec((tm, n), lambda i, j: (i * ROW_SPLIT + j, 0))
    col = pl.BlockSpec((TOP_K, tm), lambda i, j: (0, i * ROW_SPLIT + j))
    chunks = pl.BlockSpec((D_S5, nc, S5_CHUNK), lambda i, j: (0, i, 0))
    full = lambda a: pl.BlockSpec(a.shape, lambda i, j: (0,) * a.ndim)
    params = [s5_d.reshape(D_S5, 1), wglu, bglu, gain_b, wout, g2, rw2, rb.reshape(N_EXPERTS, 1)]
    return pl.pallas_call(
        _mixout_body,
        grid=(t // (tm * ROW_SPLIT), ROW_SPLIT),
        in_specs=[row(D_MODEL), row(D_POOL), chunks, chunks, row(D_LRU)] + [full(a) for a in params],
        out_specs=[row(D_MODEL), row(D_MODEL // 2), col, col],
        out_shape=[jax.ShapeDtypeStruct((t, D_MODEL), F32), jax.ShapeDtypeStruct((t, D_MODEL // 2), jnp.uint32),
                   jax.ShapeDtypeStruct((TOP_K, t), jnp.int32), jax.ShapeDtypeStruct((TOP_K, t), F32)],
        compiler_params=_cparams(("parallel", "arbitrary")), name="mix_out")(x, oa, yt3, zbt3, oc, *params)


def _expert_body(be_ref, nu_ref, first_ref, xl_ref, xh_ref, wgu_ref, bg_ref, bu_ref, wd_ref, bd_ref, p_ref, y_ref,
                 wg_s, wu_s, wd_s):
    i = pl.program_id(0)
    live = i < nu_ref[0]

    @pl.when(jnp.logical_and(live, first_ref[i] == 1))
    def _():
        p = p_ref[...]
        half = MXU_TILE // 2
        for j in range(wgu_ref.shape[3] // MXU_TILE):
            w = wgu_ref[0, 0, :, MXU_TILE * j:MXU_TILE * (j + 1)].astype(BF16)
            r = jnp.dot(w, p, preferred_element_type=F32)
            wg_s[:, half * j:half * (j + 1)] = r[:, :half].astype(BF16)
            wu_s[:, half * j:half * (j + 1)] = r[:, half:].astype(BF16)
        wd_s[...] = wd_ref[0, 0].astype(BF16)

    @pl.when(live)
    def _():
        words = jnp.concatenate([xl_ref[...], xh_ref[...]], axis=1)
        lo = pltpu.bitcast(lax.shift_left(words, jnp.uint32(16)), F32)
        hi = pltpu.bitcast(words & jnp.uint32(0xFFFF0000), F32)
        x = jnp.concatenate([lo, hi], axis=1).astype(BF16)
        g = jnp.dot(x, wg_s[...], preferred_element_type=F32) + bg_ref[0, 0]
        u = jnp.dot(x, wu_s[...], preferred_element_type=F32) + bu_ref[0, 0]
        g = jnp.minimum(g, SWIGLU_LIMIT)
        u = jnp.clip(u, -SWIGLU_LIMIT, SWIGLU_LIMIT)
        act = ((u + 1.0) * (g * _sigmoid(g * SWIGLU_ALPHA))).astype(BF16)
        y = jnp.dot(act, wd_s[...], preferred_element_type=F32) + bd_ref[0, 0]
        y_ref[...] = y.astype(y_ref.dtype)

    @pl.when(jnp.logical_not(live))
    def _():
        y_ref[...] = jnp.zeros_like(y_ref)


def _expert_mlp(layer, xs, blk_e, n_used, w_gate_up, b_gate, b_up, w_down, b_down):
    n_rows, dh = xs[0].shape
    bm = MOE_ROWS
    first = jnp.concatenate([jnp.ones((1,), jnp.int32), (blk_e[1:] != blk_e[:-1]).astype(jnp.int32)])
    idx = jnp.arange(MXU_TILE)
    perm = jax.nn.one_hot(jnp.where(idx % 2 == 0, idx // 2, MXU_TILE // 2 + idx // 2), MXU_TILE, dtype=BF16)
    rows = pl.BlockSpec((bm, dh), lambda i, be, nu, fi: (jnp.minimum(i, nu[0] - 1), 0))
    per_e = lambda a: pl.BlockSpec((1, 1) + a.shape[2:], lambda i, be, nu, fi: (layer, be[i], 0, 0))
    grid_spec = pltpu.PrefetchScalarGridSpec(
        num_scalar_prefetch=3, grid=(n_rows // bm,),
        in_specs=[rows, rows, per_e(w_gate_up), per_e(b_gate), per_e(b_up), per_e(w_down), per_e(b_down),
                  pl.BlockSpec(perm.shape, lambda i, be, nu, fi: (0, 0))],
        out_specs=pl.BlockSpec((bm, D_MODEL), lambda i, be, nu, fi: (i, 0)),
        scratch_shapes=[pltpu.VMEM((D_MODEL, D_FF), BF16), pltpu.VMEM((D_MODEL, D_FF), BF16),
                        pltpu.VMEM((D_FF, D_MODEL), BF16)])
    return pl.pallas_call(
        _expert_body, grid_spec=grid_spec,
        out_shape=jax.ShapeDtypeStruct((n_rows, D_MODEL), BF16),
        compiler_params=pltpu.CompilerParams(dimension_semantics=("arbitrary",),
                                             vmem_limit_bytes=EXPERT_VMEM_LIMIT),
        name="expert_mlp")(blk_e, n_used, first, *xs, w_gate_up, b_gate, b_up, w_down, b_down, perm)


def _sc_dispatch(h, dest, n_rows):
    t, d = h.shape
    win = SC_WINDOW
    dh = d // SC_COL_SPLIT
    idx = [dest[:, k].reshape(1, t) for k in range(TOP_K)]
    mesh = plsc.VectorSubcoreMesh(core_axis_name="core", subcore_axis_name="subcore")

    def scatter_cols(j):
        @functools.partial(pl.kernel, out_type=jax.ShapeDtypeStruct((n_rows, dh), h.dtype), mesh=mesh,
                           scratch_types=[], name="sc_dispatch")
        def scatter_rows(x_hbm, i0_hbm, i1_hbm, i2_hbm, i3_hbm, o_hbm):
            def body(x_vmem, *idx_vmem):
                for iv in idx_vmem:
                    pltpu.sync_copy(x_vmem, o_hbm.at[iv.at[0]])

            pltpu.emit_pipeline(
                body, grid=(t // win,),
                in_specs=[pl.BlockSpec((win, dh), lambda i: (i, j))]
                         + [pl.BlockSpec((1, win), lambda i: (0, i))] * TOP_K,
                out_specs=[],
                core_axis_name=("core", "subcore"),
                dimension_semantics=(pltpu.PARALLEL,),
            )(x_hbm, i0_hbm, i1_hbm, i2_hbm, i3_hbm)

        return scatter_rows(h, *idx)

    return [scatter_cols(j) for j in range(SC_COL_SPLIT)]


def _moe_dispatch(top_i, t):
    bm = MOE_ROWS
    n_blocks = (t * TOP_K) // bm + N_EXPERTS
    onehot = jax.nn.one_hot(top_i, N_EXPERTS, dtype=jnp.int32)
    sel = jnp.sum(onehot, axis=1)
    csum = jnp.cumsum(sel, axis=0)
    counts = csum[-1]
    padded = ((counts + bm - 1) // bm) * bm
    pad_end = jnp.cumsum(padded)
    pad_start = pad_end - padded
    base = (csum - sel) + pad_start[None, :]
    dest = jnp.sum(onehot * base[:, None, :], axis=-1)
    blk_start = jnp.arange(n_blocks, dtype=jnp.int32) * bm
    blk_e = jnp.minimum(jnp.sum((pad_end[None, :] <= blk_start[:, None]).astype(jnp.int32), axis=1),
                        N_EXPERTS - 1)
    n_used = (pad_end[-1] // bm).astype(jnp.int32).reshape(1)
    return dest, n_blocks * bm, blk_e, n_used


def _final_body(x_ref, y0, y1, y2, y3, gt_ref, g_ref, o_ref):
    o_ref[...] = _rms(_combine(x_ref, (y0, y1, y2, y3), gt_ref), g_ref[...])


def _final(x, ys, gates, g):
    t = x.shape[0]
    tm = ROW_TILE
    row = lambda n: pl.BlockSpec((tm, n), lambda i: (i, 0))
    return pl.pallas_call(
        _final_body, grid=(t // tm,),
        in_specs=[row(D_MODEL)] * (1 + TOP_K) + [row(TOP_K), pl.BlockSpec(g.shape, lambda i: (0, 0))],
        out_specs=row(D_MODEL), out_shape=jax.ShapeDtypeStruct((t, D_MODEL), F32),
        compiler_params=_cparams(("parallel",)), name="final_norm")(x, *ys, gates, g)


def kernel(x, norm1_g, w_in, pool_w, pool_scale, s5_lam_re, s5_lam_im, s5_log_step, s5_b_re, s5_b_im, s5_c_re, s5_c_im, s5_d, s5_w_glu, s5_b_glu, lru_conv_w, lru_conv_b, lru_w_a, lru_b_a, lru_w_x, lru_b_x, lru_lam, mix_gain, w_out, norm2_g, router_w, router_b, w_gate_up, b_gate_up, w_down, b_down, final_g):
    batch, seq, d = x.shape
    t = batch * seq
    depth = norm1_g.shape[0]
    xt = x.reshape(t, d)
    add = None
    o1, o2 = D_POOL, D_POOL + D_S5
    b_gate = b_gate_up[:, :, None, 0::2]
    b_up = b_gate_up[:, :, None, 1::2]
    b_dn = b_down[:, :, None, :]
    w_rest = jnp.concatenate([w_in[:, :, :o1], w_in[:, :, o2:]], axis=2).astype(BF16)
    w_s5t = jnp.swapaxes(w_in[:, :, o1:o2], 1, 2).astype(BF16)
    pool_bd = jax.vmap(_block_diag)(pool_w).astype(BF16)
    s5_tabs = jax.vmap(functools.partial(_s5_tables, n_chunks=seq // S5_CHUNK))(
        s5_lam_re, s5_lam_im, s5_log_step, s5_b_re, s5_b_im, s5_c_re, s5_c_im)
    lru_wg, lru_bg, lru_sp = _lru_params(lru_w_a, lru_b_a, lru_w_x, lru_b_x, lru_lam)
    wglu = s5_w_glu.astype(BF16)
    wout = w_out.astype(BF16)
    rw2 = _router_split(router_w)
    for l in range(depth):
        if add is None:
            za, zbt, zc, zg = _inproj(xt, None, norm1_g[l][None, :], w_rest[l], w_s5t[l])
        else:
            xt, za, zbt, zc, zg = _inproj(xt, add, norm1_g[l][None, :], w_rest[l], w_s5t[l])
        oa = _pool_mixer(za, batch, seq, pool_bd[l], pool_scale[l][None, :], mix_gain[l][None, :o1])
        yt = _s5_core(zbt, batch, seq, [tab[l] for tab in s5_tabs])
        oc = _lru_mixer(zc, zg, batch, seq, lru_conv_w[l], lru_conv_b[l], lru_wg[l], lru_bg[l], lru_sp[l],
                        mix_gain[l][o2:])
        xt, h2, top_i, gates = _mixout(
            xt, oa, yt, zbt, oc, s5_d[l], wglu[l], s5_b_glu[l][None, :],
            mix_gain[l][None, o1:o2], wout[l], norm2_g[l][None, :], rw2[l], router_b[l])
        top_i, gates = top_i.T, gates.T
        dest, n_rows, blk_e, n_used = _moe_dispatch(top_i, t)
        xs = _sc_dispatch(h2, dest, n_rows)
        ys = _expert_mlp(l, xs, blk_e, n_used, w_gate_up, b_gate, b_up, w_down, b_dn)
        add = ([ys.at[dest[:, k]].get(mode="promise_in_bounds") for k in range(TOP_K)], gates)
    out = _final(xt, add[0], add[1], final_g[None, :])
    return out.reshape(batch, seq, d)
```

```python
import functools
import math

import jax
import jax.numpy as jnp
from jax import lax
from jax.experimental import pallas as pl
from jax.experimental.pallas import tpu as pltpu
from jax.experimental.pallas import tpu_sc as plsc

F32 = jnp.float32
BF16 = jnp.bfloat16
FP8 = jnp.float8_e4m3fn
FP8_TARGET = 224.0
FP8_TINY = 1e-30

D_MODEL = 1024
D_POOL = 256
D_S5 = 384
D_LRU = 384
POOL_WINDOWS = (2, 4, 8, 16)
POOL_GROUP = 64
S5_GROUP = 16
S5_NGROUPS = 24
S5_STATE = 64
LRU_HEADS = 6
LRU_HEAD_DIM = 64
RG_C = 8.0
N_EXPERTS = 32
TOP_K = 4
D_FF = 1024
SWIGLU_LIMIT = 7.0
SWIGLU_ALPHA = 1.702
EPS = 1e-5

ROW_TILE = 512
ROW_SPLIT = 2
S5_CHUNK = 128
POOL_TILE = 512
LRU_TILE = 256
HALO = 8
MOE_ROWS = 512
SC_WINDOW = 128
SC_COL_SPLIT = 2
MXU_TILE = 256
EXPERT_VMEM_LIMIT = 56 * 1024 * 1024
VMEM_LIMIT = 48 * 1024 * 1024


def _cparams(sem):
    return pltpu.CompilerParams(dimension_semantics=sem, vmem_limit_bytes=VMEM_LIMIT)


def _rms(x, g):
    return x * lax.rsqrt(jnp.mean(x * x, axis=-1, keepdims=True) + EPS) * g


def _gelu(x):
    return 0.5 * x * (1.0 + jnp.tanh(0.7978845608028654 * (x + 0.044715 * (x * x * x))))


def _sigmoid(x):
    return 1.0 / (1.0 + jnp.exp(-x))


def _combine(x_ref, y_refs, gt_ref):
    x = x_ref[...]
    gt = gt_ref[...]
    for k in range(TOP_K):
        x = x + gt[:, k:k + 1] * y_refs[k][...].astype(F32)
    return x


def _inproj_body(has_add, *refs):
    if has_add:
        x_ref, y0, y1, y2, y3, gt_ref, g_ref, w_ref, wbt_ref, xo_ref, za, zbt, zc, zg = refs
        x = _combine(x_ref, (y0, y1, y2, y3), gt_ref)
        xo_ref[...] = x
    else:
        x_ref, g_ref, w_ref, wbt_ref, za, zbt, zc, zg = refs
        x = x_ref[...]
    h = _rms(x, g_ref[...]).astype(BF16)
    z = jnp.dot(h, w_ref[...], preferred_element_type=F32)
    za[...] = z[:, :D_POOL]
    zc[...] = z[:, D_POOL:D_POOL + D_LRU]
    zg[...] = z[:, D_POOL + D_LRU:]
    zt = lax.dot_general(wbt_ref[...], h, (((1,), (1,)), ((), ())), preferred_element_type=F32)
    per_step = zt.shape[1] // S5_CHUNK
    for n in range(per_step):
        zbt[:, pl.program_id(1) * per_step + n, :] = zt[:, n * S5_CHUNK:(n + 1) * S5_CHUNK]


def _inproj(x, add, g, w_rest, w_s5t):
    t = x.shape[0]
    tm = ROW_TILE
    nc = tm * ROW_SPLIT // S5_CHUNK
    row = lambda n: pl.BlockSpec((tm, n), lambda i, j: (i * ROW_SPLIT + j, 0))
    full = lambda a: pl.BlockSpec(a.shape, lambda i, j: (0,) * a.ndim)
    z_shapes = [jax.ShapeDtypeStruct((t, D_POOL), F32), jax.ShapeDtypeStruct((D_S5, t // S5_CHUNK, S5_CHUNK), F32),
                jax.ShapeDtypeStruct((t, D_LRU), F32), jax.ShapeDtypeStruct((t, D_LRU), F32)]
    z_specs = [row(D_POOL), pl.BlockSpec((D_S5, nc, S5_CHUNK), lambda i, j: (0, i, 0)), row(D_LRU), row(D_LRU)]
    if add is None:
        ins, in_specs = [x, g, w_rest, w_s5t], [row(D_MODEL), full(g), full(w_rest), full(w_s5t)]
        out_shape, out_specs = z_shapes, z_specs
    else:
        ys, gates = add
        ins = [x, *ys, gates, g, w_rest, w_s5t]
        in_specs = [row(D_MODEL)] + [row(D_MODEL)] * TOP_K + [row(TOP_K), full(g), full(w_rest), full(w_s5t)]
        out_shape = [jax.ShapeDtypeStruct((t, D_MODEL), F32)] + z_shapes
        out_specs = [row(D_MODEL)] + z_specs
    return pl.pallas_call(
        functools.partial(_inproj_body, add is not None),
        grid=(t // (tm * ROW_SPLIT), ROW_SPLIT), in_specs=in_specs, out_specs=out_specs, out_shape=out_shape,
        compiler_params=_cparams(("parallel", "arbitrary")), name="inproj")(*ins)


def _pool_body(seq, prev_ref, cur_ref, next_ref, w_ref, sc_ref, gain_ref, o_ref, u_s, s2_s, s4_s, s8_s):
    tl = cur_ref.shape[0]
    i = pl.program_id(1)
    nt = pl.num_programs(1)
    zero8 = jnp.zeros((HALO, D_POOL), F32)
    for buf in (u_s, s2_s, s4_s, s8_s):
        buf[0:HALO, :] = zero8
        buf[tl + 3 * HALO:tl + 4 * HALO, :] = zero8
    u_s[HALO:2 * HALO, :] = jnp.where(i > 0, prev_ref[...], 0.0)
    u_s[2 * HALO:2 * HALO + tl, :] = cur_ref[...]
    u_s[2 * HALO + tl:3 * HALO + tl, :] = jnp.where(i < nt - 1, next_ref[...], 0.0)
    r = tl + 2 * HALO
    s2_s[HALO:HALO + r, :] = u_s[HALO - 1:HALO - 1 + r, :] + u_s[HALO:HALO + r, :]
    s4_s[HALO:HALO + r, :] = s2_s[HALO - 1:HALO - 1 + r, :] + s2_s[HALO + 1:HALO + 1 + r, :]
    s8_s[HALO:HALO + r, :] = s4_s[HALO - 2:HALO - 2 + r, :] + s4_s[HALO + 2:HALO + 2 + r, :]
    o = 2 * HALO
    s16 = s8_s[o - 4:o - 4 + tl, :] + s8_s[o + 4:o + 4 + tl, :]
    s8 = s8_s[o:o + tl, :]
    s4 = s4_s[o:o + tl, :]
    s2 = s2_s[o:o + tl, :]
    u = u_s[o:o + tl, :]
    lane = lax.broadcasted_iota(jnp.int32, (tl, D_POOL), 1)
    tpos = lax.broadcasted_iota(jnp.int32, (tl, D_POOL), 0) + i * tl
    g0, g1, g2 = lane < POOL_GROUP, lane < 2 * POOL_GROUP, lane < 3 * POOL_GROUP
    half = jnp.where(g0, 1, jnp.where(g1, 2, jnp.where(g2, 4, 8)))
    wsum = jnp.where(g0, s2, jnp.where(g1, s4, jnp.where(g2, s8, s16)))
    cnt = (jnp.minimum(tpos + half, seq) - jnp.maximum(tpos - half, 0)).astype(F32)
    d = wsum / cnt - u
    y = jnp.dot(d.astype(BF16), w_ref[...], preferred_element_type=F32) * sc_ref[...]
    o_ref[...] = _rms(y, gain_ref[...])


def _pool_mixer(za, batch, seq, w_bd, scale, gain):
    t = za.shape[0]
    tl = POOL_TILE
    nt = seq // tl
    hb = tl // HALO
    nhb = t // HALO
    cur = pl.BlockSpec((tl, D_POOL), lambda b, i: (b * nt + i, 0))
    prev = pl.BlockSpec((HALO, D_POOL), lambda b, i: (jnp.maximum((b * nt + i) * hb - 1, 0), 0))
    nxt = pl.BlockSpec((HALO, D_POOL), lambda b, i: (jnp.minimum((b * nt + i + 1) * hb, nhb - 1), 0))
    full = lambda a: pl.BlockSpec(a.shape, lambda b, i: (0,) * a.ndim)
    return pl.pallas_call(
        functools.partial(_pool_body, seq),
        grid=(batch, nt),
        in_specs=[prev, cur, nxt, full(w_bd), full(scale), full(gain)],
        out_specs=cur,
        out_shape=jax.ShapeDtypeStruct((t, D_POOL), F32),
        scratch_shapes=[pltpu.VMEM((tl + 4 * HALO, D_POOL), F32)] * 4,
        compiler_params=_cparams(("parallel", "parallel")), name="pool_mixer")(za, za, za, w_bd, scale, gain)


def _s5_scan_steps(n_chunks):
    return max(1, int(math.ceil(math.log2(n_chunks))))


def _s5_tables(lam_re, lam_im, log_step, b_re, b_im, c_re, c_im, n_chunks):
    L, C, P, G = S5_CHUNK, S5_GROUP, S5_STATE, S5_NGROUPS
    hp = lax.Precision.HIGHEST
    lr = lam_re.astype(F32)
    li = lam_im.astype(F32)
    dt = jnp.exp(log_step.astype(F32))[..., None]
    ar = lr * dt
    ai = li * dt
    jj = jnp.arange(L + 1, dtype=F32)

    def powers(j, a_r, a_i):
        mag = jnp.exp(j * a_r)
        return mag * jnp.cos(j * a_i), mag * jnp.sin(j * a_i)

    e_r, e_i = powers(jj[None, None, :, None], ar[:, :, None, :], ai[:, :, None, :])
    et_r, et_i = powers(jj[None, None, None, :], ar[..., None], ai[..., None])
    den = lr * lr + li * li
    nr = e_r[:, :, 1, :] - 1.0
    ni = e_i[:, :, 1, :]
    q_r = (nr * lr + ni * li) / den
    q_i = (ni * lr - nr * li) / den
    br = b_re.astype(F32)
    bi = b_im.astype(F32)
    bbt_r = (q_r[..., None] * br - q_i[..., None] * bi).transpose(0, 1, 3, 2)
    bbt_i = (q_r[..., None] * bi + q_i[..., None] * br).transpose(0, 1, 3, 2)
    cr = c_re.astype(F32)
    ci = c_im.astype(F32)
    m_r = cr[:, :, None] * bbt_r[:, :, :, None] - ci[:, :, None] * bbt_i[:, :, :, None]
    m_i = cr[:, :, None] * bbt_i[:, :, :, None] + ci[:, :, None] * bbt_r[:, :, :, None]
    kk = (jnp.einsum('dgxcp,dgpj->dgxcj', m_r, et_r[..., :L], precision=hp)
          - jnp.einsum('dgxcp,dgpj->dgxcj', m_i, et_i[..., :L], precision=hp))
    kf, kb = kk[0], kk[1]
    lagtab = jnp.concatenate([kb[..., :0:-1], kf[..., :1] + kb[..., :1], kf[..., 1:],
                              jnp.zeros_like(kf[..., :1])], axis=-1)
    bits = lax.bitcast_convert_type(lagtab.astype(BF16).astype(F32), jnp.uint32).reshape(G, C * C, 2 * L)
    lagtab = lax.shift_right_logical(bits[..., :L], jnp.uint32(16)) | (bits[..., L:] & jnp.uint32(0xFFFF0000))
    def summary(er, ei, b_r, b_i):
        w_r = er[:, None] * b_r[:, :, None] - ei[:, None] * b_i[:, :, None]
        w_i = er[:, None] * b_i[:, :, None] + ei[:, None] * b_r[:, :, None]
        return w_r.reshape(G, C * L, P), w_i.reshape(G, C * L, P)
    wf_r, wf_i = summary(e_r[0, :, L - 1::-1], e_i[0, :, L - 1::-1], bbt_r[0], bbt_i[0])
    wb_r, wb_i = summary(e_r[1, :, :L], e_i[1, :, :L], bbt_r[1], bbt_i[1])
    wst = jnp.concatenate([wf_r, wb_r, wf_i, wb_i], axis=-1)
    def carry_ops(er, ei, c_r, c_i):
        ct_r = c_r.transpose(0, 2, 1)[..., None]
        ct_i = c_i.transpose(0, 2, 1)[..., None]
        v_r = ct_r * er[:, :, None] - ct_i * ei[:, :, None]
        v_i = -(ct_r * ei[:, :, None] + ct_i * er[:, :, None])
        return v_r.reshape(G, P, C * L), v_i.reshape(G, P, C * L)
    vf_r, vf_i = carry_ops(et_r[0, :, :, 1:L + 1], et_i[0, :, :, 1:L + 1], cr[0], ci[0])
    vb_r, vb_i = carry_ops(et_r[1, :, :, L:0:-1], et_i[1, :, :, L:0:-1], cr[1], ci[1])
    vmat = jnp.concatenate([vf_r, vb_r, vf_i, vb_i], axis=1)
    n_steps = _s5_scan_steps(n_chunks)
    kpow =(L * (2 ** jnp.arange(n_steps, dtype=F32)))[None, None, :, None]
    p_r, p_i = powers(kpow, ar[:, :, None, :], ai[:, :, None, :])
    pad = jnp.zeros((G, (-n_steps) % 8, 2 * P), F32)
    p_r = jnp.concatenate([jnp.concatenate([p_r[0], p_r[1]], axis=-1), pad], axis=1)
    p_i = jnp.concatenate([jnp.concatenate([p_i[0], p_i[1]], axis=-1), pad], axis=1)
    ptab = jnp.concatenate([p_r, p_i], axis=1)
    return lagtab, wst.astype(BF16), vmat.astype(BF16), ptab


def _s5_body(n_chunks, n_steps, u_ref, lag_ref, w_ref, v_ref, p_ref, y_ref, toep_s):
    L, C = S5_CHUNK, S5_GROUP
    n = u_ref.shape[1]
    s_i = lax.broadcasted_iota(jnp.int32, (L, L), 0)
    t_i = lax.broadcasted_iota(jnp.int32, (L, L), 1)
    non_positive_lag = t_i <= s_i

    def build_block_row(cp, carry):
        r0 = pl.multiple_of(cp * L, L)
        for c in range(C):
            k = lag_ref[0, pl.ds(cp * C + c, 1), :]
            r = pltpu.roll(jnp.broadcast_to(k, (L, L)), 1, 1, stride=1, stride_axis=0)
            bits = jnp.where(non_positive_lag, lax.shift_left(r, jnp.uint32(16)), r & jnp.uint32(0xFFFF0000))
            toep_s[pl.ds(r0, L), c * L:(c + 1) * L] = pltpu.bitcast(bits, F32).astype(BF16)
        return carry

    lax.fori_loop(0, C, build_block_row, 0)
    u = jnp.concatenate([u_ref[c] for c in range(C)], axis=1).astype(BF16)
    y = jnp.dot(u, toep_s[...], preferred_element_type=F32)
    s = jnp.dot(u, w_ref[0], preferred_element_type=F32)
    two_p = 2 * S5_STATE
    xr = s[:, :two_p]
    xi = s[:, two_p:]
    row = lax.broadcasted_iota(jnp.int32, (n, two_p), 0) % n_chunks
    is_fwd = lax.broadcasted_iota(jnp.int32, (n, two_p), 1) < S5_STATE
    pim0 = p_ref.shape[1] // 2

    def shifted(a, k):
        down = jnp.where(row >= k, pltpu.roll(a, k, 0), 0.0)
        up = jnp.where(row < n_chunks - k, pltpu.roll(a, n - k, 0), 0.0)
        return jnp.where(is_fwd, down, up)

    for j in range(n_steps):
        k = 1 << j
        pr = p_ref[0, j:j + 1, :]
        pi = p_ref[0, pim0 + j:pim0 + j + 1, :]
        sr = shifted(xr, k)
        si = shifted(xi, k)
        xr, xi = xr + pr * sr - pi * si, xi + pr * si + pi * sr
    carry = jnp.concatenate([shifted(xr, 1), shifted(xi, 1)], axis=1).astype(BF16)
    y = y + jnp.dot(carry, v_ref[0], preferred_element_type=F32)
    for c in range(C):
        y_ref[c] = y[:, c * L:(c + 1) * L]


def _s5_core(zbt, batch, seq, tables):
    lagtab, wst, vmat, ptab = tables
    L = S5_CHUNK
    n_chunks = seq // L
    n_steps = _s5_scan_steps(n_chunks)
    n = batch * n_chunks
    per_g = lambda a: pl.BlockSpec((1,) + a.shape[1:], lambda g: (g, 0, 0))
    grp = pl.BlockSpec((S5_GROUP, n, L), lambda g: (g, 0, 0))
    return pl.pallas_call(
        functools.partial(_s5_body, n_chunks, n_steps),
        grid=(S5_NGROUPS,),
        in_specs=[grp, per_g(lagtab), per_g(wst), per_g(vmat), per_g(ptab)],
        out_specs=grp,
        out_shape=jax.ShapeDtypeStruct((D_S5, n, L), F32),
        scratch_shapes=[pltpu.VMEM((S5_GROUP * L, S5_GROUP * L), BF16)],
        compiler_params=_cparams(("parallel",)), name="s5_core")(zbt, lagtab, wst, vmat, ptab)


def _lru_body(reverse, prev_ref, cur_ref, next_ref, cw_ref, cb_ref, wg_ref, bg_ref, sp_ref, *rest):
    if reverse:
        hf_ref, gate_ref, gain_ref, o_ref, x_s, a_s, b_s, h_s, carry_s = rest
    else:
        o_ref, x_s, a_s, b_s, h_s, carry_s = rest
    nb, tl, _ = cur_ref.shape
    step = pl.program_id(0)
    nt = pl.num_programs(0)
    ti = nt - 1 - step if reverse else step

    @pl.when(step == 0)
    def _():
        carry_s[...] = jnp.zeros_like(carry_s)

    cw = cw_ref[...]
    for b in range(nb):
        x_s[0:HALO, :] = jnp.where(ti > 0, prev_ref[b], 0.0)
        x_s[HALO:HALO + tl, :] = cur_ref[b]
        x_s[HALO + tl:2 * HALO + tl, :] = jnp.where(ti < nt - 1, next_ref[b], 0.0)
        xc = cb_ref[...]
        for k in range(4):
            xc = xc + cw[k:k + 1, :] * x_s[HALO - 1 + k:HALO - 1 + k + tl, :]
        gates = jnp.dot(xc.astype(BF16), wg_ref[...], preferred_element_type=F32) + bg_ref[...]
        r = _sigmoid(gates[:, :D_LRU])
        ig = _sigmoid(gates[:, D_LRU:])
        log_a = (-RG_C) * r * sp_ref[...]
        a = jnp.exp(log_a)
        a_s[b] = a
        om = 1.0 - a * a
        b_s[b] = (om * lax.rsqrt(jnp.maximum(om, 1e-30))) * (ig * xc)

    def scan_step(s, hs):
        t = tl - 1 - s if reverse else s
        out = []
        for b in range(nb):
            h = a_s[b, pl.ds(t, 1), :] * hs[b] + b_s[b, pl.ds(t, 1), :]
            h_s[b, pl.ds(t, 1), :] = h
            out.append(h)
        return tuple(out)

    hs = lax.fori_loop(0, tl, scan_step, tuple(carry_s[b:b + 1, :] for b in range(nb)), unroll=8)
    for b in range(nb):
        carry_s[b:b + 1, :] = hs[b]

    if reverse:
        for b in range(nb):
            y = (hf_ref[b] + h_s[b]) * _gelu(gate_ref[b])
            o_ref[b] = _rms(y, gain_ref[...])
    else:
        o_ref[...] = h_s[...]


def _lru_pass(reverse, zc3, params, extra):
    nb, seq, _ = zc3.shape
    tl = LRU_TILE
    nt = seq // tl
    hb = tl // HALO
    nhb = seq // HALO
    tix = (lambda i: nt - 1 - i) if reverse else (lambda i: i)
    cur = pl.BlockSpec((nb, tl, D_LRU), lambda i: (0, tix(i), 0))
    prev = pl.BlockSpec((nb, HALO, D_LRU), lambda i: (0, jnp.maximum(tix(i) * hb - 1, 0), 0))
    nxt = pl.BlockSpec((nb, HALO, D_LRU), lambda i: (0, jnp.minimum((tix(i) + 1) * hb, nhb - 1), 0))
    full = lambda a: pl.BlockSpec(a.shape, lambda i: (0,) * a.ndim)
    ins = [zc3, zc3, zc3, *params]
    in_specs = [prev, cur, nxt] + [full(a) for a in params]
    if reverse:
        hf, gate, gain = extra
        ins += [hf, gate, gain]
        in_specs += [cur, cur, full(gain)]
    return pl.pallas_call(
        functools.partial(_lru_body, reverse),
        grid=(nt,), in_specs=in_specs, out_specs=cur,
        out_shape=jax.ShapeDtypeStruct((nb, seq, D_LRU), F32),
        scratch_shapes=[pltpu.VMEM((tl + 2 * HALO, D_LRU), F32),
                        pltpu.VMEM((nb, tl, D_LRU), F32),
                        pltpu.VMEM((nb, tl, D_LRU), F32),
                        pltpu.VMEM((nb, tl, D_LRU), F32),
                        pltpu.VMEM((8, D_LRU), F32)],
        compiler_params=_cparams(("arbitrary",)),
        name="lru_bwd" if reverse else "lru_fwd")(*ins)


def _block_diag(w):
    h, d, _ = w.shape
    eye = jnp.eye(h, dtype=w.dtype)
    return (eye[:, None, :, None] * w[:, :, None, :]).reshape(h * d, h * d)


def _lru_params(w_a, b_a, w_x, b_x, lam):
    bd = jax.vmap(_block_diag)
    wg = jnp.concatenate([bd(w_a), bd(w_x)], axis=-1).astype(BF16)
    bg = jnp.concatenate([b_a, b_x], axis=-1)[:, None, :]
    sp = jax.nn.softplus(-lam.astype(F32))[:, None, :]
    return wg, bg, sp


def _lru_mixer(zc, zg, batch, seq, conv_w, conv_b, wg, bg, sp, gain):
    zc3 = zc.reshape(batch, seq, D_LRU)
    zg3 = zg.reshape(batch, seq, D_LRU)
    outs = None
    for d in (0, 1):
        params = [conv_w, conv_b[None, :], wg[d], bg[d], sp[d]]
        extra = None if d == 0 else (outs, zg3, gain[None, :])
        outs = _lru_pass(d == 1, zc3, params, extra)
    return outs.reshape(batch * seq, D_LRU)


def _mixout_body(x_ref, oa_ref, y3_ref, u3_ref, oc_ref, sd_ref, wglu_ref, bglu_ref, gb_ref,
                 wout_ref, g2_ref, rw_ref, rb_ref, xo_ref, h_ref, ti_ref, gt_ref):
    tm = x_ref.shape[0]
    per_step = tm // S5_CHUNK
    c0 = pl.program_id(1) * per_step
    yt = jnp.concatenate([y3_ref[:, c0 + n, :] for n in range(per_step)], axis=1)
    ut = jnp.concatenate([u3_ref[:, c0 + n, :] for n in range(per_step)], axis=1)
    vt = _gelu(yt + sd_ref[...] * ut).astype(BF16)
    gl = lax.dot_general(vt, wglu_ref[...], (((0,), (0,)), ((), ())), preferred_element_type=F32) + bglu_ref[...]
    ob = _rms(gl[:, :D_S5] * _sigmoid(gl[:, D_S5:]), gb_ref[...])
    o = jnp.concatenate([oa_ref[...], ob, oc_ref[...]], axis=1).astype(BF16)
    x = x_ref[...] + jnp.dot(o, wout_ref[...], preferred_element_type=F32)
    xo_ref[...] = x
    h = _rms(x, g2_ref[...])
    h_hi = h.astype(BF16)
    bits = pltpu.bitcast(h_hi.astype(F32), jnp.uint32)
    h_ref[...] = lax.shift_right_logical(bits[:, :D_MODEL // 2], jnp.uint32(16)) | (
        bits[:, D_MODEL // 2:] & jnp.uint32(0xFFFF0000))
    h_lo = (h - h_hi.astype(F32)).astype(BF16)
    nt_dims = (((1,), (1,)), ((), ()))
    la = lax.dot_general(rw_ref[...], h_hi, nt_dims, preferred_element_type=F32)
    lb = lax.dot_general(rw_ref[0:N_EXPERTS, :], h_lo, nt_dims, preferred_element_type=F32)
    logits = la[:N_EXPERTS] + la[N_EXPERTS:] + lb + rb_ref[...]
    eidx = lax.broadcasted_iota(jnp.int32, (N_EXPERTS, tm), 0).astype(F32)
    idx_rows, val_rows = [], []
    for k in range(TOP_K):
        m = jnp.max(logits, axis=0, keepdims=True)
        idx = jnp.min(jnp.where(logits == m, eidx, float(N_EXPERTS)), axis=0, keepdims=True)
        idx_rows.append(idx)
        val_rows.append(jnp.exp(m - val_rows[0]) if k else m)
        logits = jnp.where(eidx == idx, -jnp.inf, logits)
    top = val_rows[0]
    vals = jnp.concatenate([jnp.ones_like(top)] + [v for v in val_rows[1:]], axis=0)
    ti_ref[...] = jnp.concatenate(idx_rows, axis=0).astype(jnp.int32)
    gt_ref[...] = vals / jnp.sum(vals, axis=0, keepdims=True)


def _router_split(rw):
    rw_hi = rw.astype(BF16)
    rw_lo = (rw - rw_hi.astype(F32)).astype(BF16)
    return jnp.concatenate([jnp.swapaxes(rw_hi, -1, -2), jnp.swapaxes(rw_lo, -1, -2)], axis=-2)


def _mixout(x, oa, yt3, zbt3, oc, s5_d, wglu, bglu, gain_b, wout, g2, rw2, rb):
    t = x.shape[0]
    tm = ROW_TILE
    nc = tm * ROW_SPLIT // S5_CHUNK
    row = lambda n: pl.BlockSpec((tm, n), lambda i, j: (i * ROW_SPLIT + j, 0))
    col = pl.BlockSpec((TOP_K, tm), lambda i, j: (0, i * ROW_SPLIT + j))
    chunks = pl.BlockSpec((D_S5, nc, S5_CHUNK), lambda i, j: (0, i, 0))
    full = lambda a: pl.BlockSpec(a.shape, lambda i, j: (0,) * a.ndim)
    params = [s5_d.reshape(D_S5, 1), wglu, bglu, gain_b, wout, g2, rw2, rb.reshape(N_EXPERTS, 1)]
    return pl.pallas_call(
        _mixout_body,
        grid=(t // (tm * ROW_SPLIT), ROW_SPLIT),
        in_specs=[row(D_MODEL), row(D_POOL), chunks, chunks, row(D_LRU)] + [full(a) for a in params],
        out_specs=[row(D_MODEL), row(D_MODEL // 2), col, col],
        out_shape=[jax.ShapeDtypeStruct((t, D_MODEL), F32), jax.ShapeDtypeStruct((t, D_MODEL // 2), jnp.uint32),
                   jax.ShapeDtypeStruct((TOP_K, t), jnp.int32), jax.ShapeDtypeStruct((TOP_K, t), F32)],
        compiler_params=_cparams(("parallel", "arbitrary")), name="mix_out")(x, oa, yt3, zbt3, oc, *params)


def _row_quant(parts):
    amax = functools.reduce(jnp.maximum, [jnp.max(jnp.abs(p), axis=1, keepdims=True) for p in parts])
    scale = FP8_TARGET / jnp.maximum(amax, FP8_TINY)
    q = jnp.concatenate([p * scale for p in parts], axis=1).astype(FP8)
    return q, jnp.maximum(amax, FP8_TINY) * (1.0 / FP8_TARGET)


def _expert_body(be_ref, nu_ref, first_ref, xl_ref, xh_ref, wgu_ref, bg_ref, bu_ref, wd_ref, bd_ref, p_ref, y_ref,
                 wg_s, wu_s, wd_s, ws_s):
    i = pl.program_id(0)
    live = i < nu_ref[0]

    @pl.when(jnp.logical_and(live, first_ref[i] == 1))
    def _():
        p = p_ref[...]
        half = MXU_TILE // 2
        n_groups = wgu_ref.shape[3] // MXU_TILE
        amax = functools.reduce(jnp.maximum, [
            jnp.max(jnp.abs(wgu_ref[0, 0, :, MXU_TILE * j:MXU_TILE * (j + 1)]), axis=(0, 1), keepdims=True)
            for j in range(n_groups)])
        s_gu = FP8_TARGET / jnp.maximum(amax, FP8_TINY)
        for j in range(n_groups):
            w = wgu_ref[0, 0, :, MXU_TILE * j:MXU_TILE * (j + 1)].astype(BF16)
            r = jnp.dot(w, p, preferred_element_type=F32) * s_gu
            wg_s[:, half * j:half * (j + 1)] = r[:, :half].astype(FP8)
            wu_s[:, half * j:half * (j + 1)] = r[:, half:].astype(FP8)
        wd = wd_ref[0, 0]
        dmax = jnp.max(jnp.abs(wd), axis=(0, 1), keepdims=True)
        s_d = FP8_TARGET / jnp.maximum(dmax, FP8_TINY)
        wd_s[...] = (wd * s_d).astype(FP8)
        ws_s[0:1, :] = jnp.broadcast_to(1.0 / s_gu, (1, ws_s.shape[1]))
        ws_s[1:2, :] = jnp.broadcast_to(1.0 / s_d, (1, ws_s.shape[1]))

    @pl.when(live)
    def _():
        words = jnp.concatenate([xl_ref[...], xh_ref[...]], axis=1)
        lo = pltpu.bitcast(lax.shift_left(words, jnp.uint32(16)), F32)
        hi = pltpu.bitcast(words & jnp.uint32(0xFFFF0000), F32)
        x, x_inv = _row_quant([lo, hi])
        gu_inv = x_inv * ws_s[0:1, 0:1]
        g = jnp.dot(x, wg_s[...], preferred_element_type=F32) * gu_inv + bg_ref[0, 0]
        u = jnp.dot(x, wu_s[...], preferred_element_type=F32) * gu_inv + bu_ref[0, 0]
        g = jnp.minimum(g, SWIGLU_LIMIT)
        u = jnp.clip(u, -SWIGLU_LIMIT, SWIGLU_LIMIT)
        act, a_inv = _row_quant([(u + 1.0) * (g * _sigmoid(g * SWIGLU_ALPHA))])
        y = jnp.dot(act, wd_s[...], preferred_element_type=F32) * (a_inv * ws_s[1:2, 0:1]) + bd_ref[0, 0]
        y_ref[...] = y.astype(y_ref.dtype)

    @pl.when(jnp.logical_not(live))
    def _():
        y_ref[...] = jnp.zeros_like(y_ref)


def _expert_mlp(layer, xs, blk_e, n_used, w_gate_up, b_gate, b_up, w_down, b_down):
    n_rows, dh = xs[0].shape
    bm = MOE_ROWS
    first = jnp.concatenate([jnp.ones((1,), jnp.int32), (blk_e[1:] != blk_e[:-1]).astype(jnp.int32)])
    idx = jnp.arange(MXU_TILE)
    perm = jax.nn.one_hot(jnp.where(idx % 2 == 0, idx // 2, MXU_TILE // 2 + idx // 2), MXU_TILE, dtype=BF16)
    rows = pl.BlockSpec((bm, dh), lambda i, be, nu, fi: (jnp.minimum(i, nu[0] - 1), 0))
    per_e = lambda a: pl.BlockSpec((1, 1) + a.shape[2:], lambda i, be, nu, fi: (layer, be[i], 0, 0))
    grid_spec = pltpu.PrefetchScalarGridSpec(
        num_scalar_prefetch=3, grid=(n_rows // bm,),
        in_specs=[rows, rows, per_e(w_gate_up), per_e(b_gate), per_e(b_up), per_e(w_down), per_e(b_down),
                  pl.BlockSpec(perm.shape, lambda i, be, nu, fi: (0, 0))],
        out_specs=pl.BlockSpec((bm, D_MODEL), lambda i, be, nu, fi: (i, 0)),
        scratch_shapes=[pltpu.VMEM((D_MODEL, D_FF), FP8), pltpu.VMEM((D_MODEL, D_FF), FP8),
                        pltpu.VMEM((D_FF, D_MODEL), FP8), pltpu.VMEM((8, 128), F32)])
    return pl.pallas_call(
        _expert_body, grid_spec=grid_spec,
        out_shape=jax.ShapeDtypeStruct((n_rows, D_MODEL), BF16),
        compiler_params=pltpu.CompilerParams(dimension_semantics=("arbitrary",),
                                             vmem_limit_bytes=EXPERT_VMEM_LIMIT),
        name="expert_mlp")(blk_e, n_used, first, *xs, w_gate_up, b_gate, b_up, w_down, b_down, perm)


def _sc_dispatch(h, dest, n_rows):
    t, d = h.shape
    win = SC_WINDOW
    dh = d // SC_COL_SPLIT
    idx = [dest[:, k].reshape(1, t) for k in range(TOP_K)]
    mesh = plsc.VectorSubcoreMesh(core_axis_name="core", subcore_axis_name="subcore")

    def scatter_cols(j):
        @functools.partial(pl.kernel, out_type=jax.ShapeDtypeStruct((n_rows, dh), h.dtype), mesh=mesh,
                           scratch_types=[], name="sc_dispatch")
        def scatter_rows(x_hbm, i0_hbm, i1_hbm, i2_hbm, i3_hbm, o_hbm):
            def body(x_vmem, *idx_vmem):
                for iv in idx_vmem:
                    pltpu.sync_copy(x_vmem, o_hbm.at[iv.at[0]])

            pltpu.emit_pipeline(
                body, grid=(t // win,),
                in_specs=[pl.BlockSpec((win, dh), lambda i: (i, j))]
                         + [pl.BlockSpec((1, win), lambda i: (0, i))] * TOP_K,
                out_specs=[],
                core_axis_name=("core", "subcore"),
                dimension_semantics=(pltpu.PARALLEL,),
            )(x_hbm, i0_hbm, i1_hbm, i2_hbm, i3_hbm)

        return scatter_rows(h, *idx)

    return [scatter_cols(j) for j in range(SC_COL_SPLIT)]


def _moe_dispatch(top_i, t):
    bm = MOE_ROWS
    n_blocks = (t * TOP_K) // bm + N_EXPERTS
    onehot = jax.nn.one_hot(top_i, N_EXPERTS, dtype=jnp.int32)
    sel = jnp.sum(onehot, axis=1)
    csum = jnp.cumsum(sel, axis=0)
    counts = csum[-1]
    padded = ((counts + bm - 1) // bm) * bm
    pad_end = jnp.cumsum(padded)
    pad_start = pad_end - padded
    base = (csum - sel) + pad_start[None, :]
    dest = jnp.sum(onehot * base[:, None, :], axis=-1)
    blk_start = jnp.arange(n_blocks, dtype=jnp.int32) * bm
    blk_e = jnp.minimum(jnp.sum((pad_end[None, :] <= blk_start[:, None]).astype(jnp.int32), axis=1),
                        N_EXPERTS - 1)
    n_used = (pad_end[-1] // bm).astype(jnp.int32).reshape(1)
    return dest, n_blocks * bm, blk_e, n_used


def _final_body(x_ref, y0, y1, y2, y3, gt_ref, g_ref, o_ref):
    o_ref[...] = _rms(_combine(x_ref, (y0, y1, y2, y3), gt_ref), g_ref[...])


def _final(x, ys, gates, g):
    t = x.shape[0]
    tm = ROW_TILE
    row = lambda n: pl.BlockSpec((tm, n), lambda i: (i, 0))
    return pl.pallas_call(
        _final_body, grid=(t // tm,),
        in_specs=[row(D_MODEL)] * (1 + TOP_K) + [row(TOP_K), pl.BlockSpec(g.shape, lambda i: (0, 0))],
        out_specs=row(D_MODEL), out_shape=jax.ShapeDtypeStruct((t, D_MODEL), F32),
        compiler_params=_cparams(("parallel",)), name="final_norm")(x, *ys, gates, g)


def kernel(x, norm1_g, w_in, pool_w, pool_scale, s5_lam_re, s5_lam_im, s5_log_step, s5_b_re, s5_b_im, s5_c_re, s5_c_im, s5_d, s5_w_glu, s5_b_glu, lru_conv_w, lru_conv_b, lru_w_a, lru_b_a, lru_w_x, lru_b_x, lru_lam, mix_gain, w_out, norm2_g, router_w, router_b, w_gate_up, b_gate_up, w_down, b_down, final_g):
    batch, seq, d = x.shape
    t = batch * seq
    depth = norm1_g.shape[0]
    xt = x.reshape(t, d)
    add = None
    o1, o2 = D_POOL, D_POOL + D_S5
    b_gate = b_gate_up[:, :, None, 0::2]
    b_up = b_gate_up[:, :, None, 1::2]
    b_dn = b_down[:, :, None, :]
    for l in range(depth):
        w_rest = jnp.concatenate([w_in[l][:, :o1], w_in[l][:, o2:]], axis=1).astype(BF16)
        w_s5t = w_in[l][:, o1:o2].T.astype(BF16)
        s5_tabs = _s5_tables(s5_lam_re[l], s5_lam_im[l], s5_log_step[l], s5_b_re[l], s5_b_im[l],
                             s5_c_re[l], s5_c_im[l], seq // S5_CHUNK)
        lru_wg, lru_bg, lru_sp = _lru_params(lru_w_a[l], lru_b_a[l], lru_w_x[l], lru_b_x[l], lru_lam[l])
        if add is None:
            za, zbt, zc, zg = _inproj(xt, None, norm1_g[l][None, :], w_rest, w_s5t)
        else:
            xt, za, zbt, zc, zg = _inproj(xt, add, norm1_g[l][None, :], w_rest, w_s5t)
        oa = _pool_mixer(za, batch, seq, _block_diag(pool_w[l]).astype(BF16), pool_scale[l][None, :],
                         mix_gain[l][None, :o1])
        yt = _s5_core(zbt, batch, seq, s5_tabs)
        oc = _lru_mixer(zc, zg, batch, seq, lru_conv_w[l], lru_conv_b[l], lru_wg, lru_bg, lru_sp,
                        mix_gain[l][o2:])
        xt, h2, top_i, gates = _mixout(
            xt, oa, yt, zbt, oc, s5_d[l], s5_w_glu[l].astype(BF16), s5_b_glu[l][None, :],
            mix_gain[l][None, o1:o2], w_out[l].astype(BF16), norm2_g[l][None, :], _router_split(router_w[l]),
            router_b[l])
        top_i, gates = top_i.T, gates.T
        dest, n_rows, blk_e, n_used = _moe_dispatch(top_i, t)
        xs = _sc_dispatch(h2, dest, n_rows)
        ys = _expert_mlp(l, xs, blk_e, n_used, w_gate_up, b_gate, b_up, w_down, b_dn)
        add = ([ys.at[dest[:, k]].get(mode="promise_in_bounds") for k in range(TOP_K)], gates)
    out = _final(xt, add[0], add[1], final_g[None, :])
    return out.reshape(batch, seq, d)
```

```python
import functools
import math

import jax
import jax.numpy as jnp
from jax import lax
from jax.experimental import pallas as pl
from jax.experimental.pallas import tpu as pltpu
from jax.experimental.pallas import tpu_sc as plsc

F32 = jnp.float32
BF16 = jnp.bfloat16
FP8 = jnp.float8_e4m3fn
FP8_TARGET = 416.0
FP8_TINY = 1e-30

D_MODEL = 1024
D_POOL = 256
D_S5 = 384
D_LRU = 384
POOL_WINDOWS = (2, 4, 8, 16)
POOL_GROUP = 64
S5_GROUP = 16
S5_NGROUPS = 24
S5_STATE = 64
LRU_HEADS = 6
LRU_HEAD_DIM = 64
RG_C = 8.0
N_EXPERTS = 32
TOP_K = 4
D_FF = 1024
SWIGLU_LIMIT = 7.0
SWIGLU_ALPHA = 1.702
EPS = 1e-5

ROW_TILE = 512
ROW_SPLIT = 2
S5_CHUNK = 128
POOL_TILE = 512
LRU_TILE = 256
HALO = 8
MOE_ROWS = 512
TOKEN_PARTS = 2
SC_WINDOW = 128
SC_COL_SPLIT = 2
MXU_TILE = 256
EXPERT_VMEM_LIMIT = 56 * 1024 * 1024
VMEM_LIMIT = 48 * 1024 * 1024


def _cparams(sem):
    return pltpu.CompilerParams(dimension_semantics=sem, vmem_limit_bytes=VMEM_LIMIT)


def _rms(x, g):
    return x * lax.rsqrt(jnp.mean(x * x, axis=-1, keepdims=True) + EPS) * g


def _gelu(x):
    return 0.5 * x * (1.0 + jnp.tanh(0.7978845608028654 * (x + 0.044715 * (x * x * x))))


def _sigmoid(x):
    return 1.0 / (1.0 + jnp.exp(-x))


def _combine(x_ref, y_refs, gt_ref):
    x = x_ref[...]
    gt = gt_ref[...]
    for k in range(TOP_K):
        x = x + gt[:, k:k + 1] * y_refs[k][...].astype(F32)
    return x


def _inproj_body(has_add, *refs):
    if has_add:
        x_ref, y0, y1, y2, y3, gt_ref, g_ref, w_ref, wbt_ref = refs[:9]
        xo_ref, za, zbt, zc, zg = refs[-5:]
        x = _combine(x_ref, (y0, y1, y2, y3), gt_ref)
        xo_ref[...] = x
    else:
        x_ref, g_ref, w_ref, wbt_ref, za, zbt, zc, zg = refs
        x = x_ref[...]
    h = _rms(x, g_ref[...]).astype(BF16)
    z = jnp.dot(h, w_ref[...], preferred_element_type=F32)
    za[...] = z[:, :D_POOL]
    zc[...] = z[:, D_POOL:D_POOL + D_LRU]
    zg[...] = z[:, D_POOL + D_LRU:]
    zt = lax.dot_general(wbt_ref[...], h, (((1,), (1,)), ((), ())), preferred_element_type=F32)
    per_step = zt.shape[1] // S5_CHUNK
    for n in range(per_step):
        zbt[:, pl.program_id(1) * per_step + n, :] = zt[:, n * S5_CHUNK:(n + 1) * S5_CHUNK]


def _inproj(x, add, g, w_rest, w_s5t):
    t = x.shape[0]
    tm = ROW_TILE
    nc = tm * ROW_SPLIT // S5_CHUNK
    row = lambda n: pl.BlockSpec((tm, n), lambda i, j: (i * ROW_SPLIT + j, 0))
    full = lambda a: pl.BlockSpec(a.shape, lambda i, j: (0,) * a.ndim)
    z_shapes = [jax.ShapeDtypeStruct((t, D_POOL), F32), jax.ShapeDtypeStruct((D_S5, t // S5_CHUNK, S5_CHUNK), F32),
                jax.ShapeDtypeStruct((t, D_LRU), F32), jax.ShapeDtypeStruct((t, D_LRU), F32)]
    z_specs = [row(D_POOL), pl.BlockSpec((D_S5, nc, S5_CHUNK), lambda i, j: (0, i, 0)), row(D_LRU), row(D_LRU)]
    n_outer = t // (tm * ROW_SPLIT)
    if add is None:
        return pl.pallas_call(
            functools.partial(_inproj_body, False),
            grid=(n_outer, ROW_SPLIT), in_specs=[row(D_MODEL), full(g), full(w_rest), full(w_s5t)],
            out_specs=z_specs, out_shape=z_shapes,
            compiler_params=_cparams(("parallel", "arbitrary")), name="inproj")(x, g, w_rest, w_s5t)
    parts, gates = add
    per_part = n_outer // len(parts)
    out_shape = [jax.ShapeDtypeStruct((t, D_MODEL), F32)] + z_shapes
    outs = []
    for p, ys in enumerate(parts):
        off = p * per_part
        row_p = lambda n, off=off: pl.BlockSpec((tm, n), lambda i, j: ((i + off) * ROW_SPLIT + j, 0))
        row_y = pl.BlockSpec((tm, D_MODEL), lambda i, j: (i * ROW_SPLIT + j, 0))
        zbt_p = pl.BlockSpec((D_S5, nc, S5_CHUNK), lambda i, j, off=off: (0, i + off, 0))
        n_in = 1 + TOP_K + 4
        outs = pl.pallas_call(
            functools.partial(_inproj_body, True),
            grid=(per_part, ROW_SPLIT),
            in_specs=[row_p(D_MODEL)] + [row_y] * TOP_K + [row_p(TOP_K), full(g), full(w_rest), full(w_s5t)]
                     + [pl.BlockSpec(memory_space=pl.ANY)] * len(outs),
            out_specs=[row_p(D_MODEL), row_p(D_POOL), zbt_p, row_p(D_LRU), row_p(D_LRU)],
            out_shape=out_shape,
            input_output_aliases={n_in + k: k for k in range(len(outs))},
            compiler_params=_cparams(("parallel", "arbitrary")), name="inproj")(
                x, *ys, gates, g, w_rest, w_s5t, *outs)
    return outs


def _pool_body(seq, prev_ref, cur_ref, next_ref, w_ref, sc_ref, gain_ref, o_ref, u_s, s2_s, s4_s, s8_s):
    tl = cur_ref.shape[0]
    i = pl.program_id(1)
    nt = pl.num_programs(1)
    zero8 = jnp.zeros((HALO, D_POOL), F32)
    for buf in (u_s, s2_s, s4_s, s8_s):
        buf[0:HALO, :] = zero8
        buf[tl + 3 * HALO:tl + 4 * HALO, :] = zero8
    u_s[HALO:2 * HALO, :] = jnp.where(i > 0, prev_ref[...], 0.0)
    u_s[2 * HALO:2 * HALO + tl, :] = cur_ref[...]
    u_s[2 * HALO + tl:3 * HALO + tl, :] = jnp.where(i < nt - 1, next_ref[...], 0.0)
    r = tl + 2 * HALO
    s2_s[HALO:HALO + r, :] = u_s[HALO - 1:HALO - 1 + r, :] + u_s[HALO:HALO + r, :]
    s4_s[HALO:HALO + r, :] = s2_s[HALO - 1:HALO - 1 + r, :] + s2_s[HALO + 1:HALO + 1 + r, :]
    s8_s[HALO:HALO + r, :] = s4_s[HALO - 2:HALO - 2 + r, :] + s4_s[HALO + 2:HALO + 2 + r, :]
    o = 2 * HALO
    s16 = s8_s[o - 4:o - 4 + tl, :] + s8_s[o + 4:o + 4 + tl, :]
    s8 = s8_s[o:o + tl, :]
    s4 = s4_s[o:o + tl, :]
    s2 = s2_s[o:o + tl, :]
    u = u_s[o:o + tl, :]
    lane = lax.broadcasted_iota(jnp.int32, (tl, D_POOL), 1)
    tpos = lax.broadcasted_iota(jnp.int32, (tl, D_POOL), 0) + i * tl
    g0, g1, g2 = lane < POOL_GROUP, lane < 2 * POOL_GROUP, lane < 3 * POOL_GROUP
    half = jnp.where(g0, 1, jnp.where(g1, 2, jnp.where(g2, 4, 8)))
    wsum = jnp.where(g0, s2, jnp.where(g1, s4, jnp.where(g2, s8, s16)))
    cnt = (jnp.minimum(tpos + half, seq) - jnp.maximum(tpos - half, 0)).astype(F32)
    d = wsum / cnt - u
    y = jnp.dot(d.astype(BF16), w_ref[...], preferred_element_type=F32) * sc_ref[...]
    o_ref[...] = _rms(y, gain_ref[...])


def _pool_mixer(za, batch, seq, w_bd, scale, gain):
    t = za.shape[0]
    tl = POOL_TILE
    nt = seq // tl
    hb = tl // HALO
    nhb = t // HALO
    cur = pl.BlockSpec((tl, D_POOL), lambda b, i: (b * nt + i, 0))
    prev = pl.BlockSpec((HALO, D_POOL), lambda b, i: (jnp.maximum((b * nt + i) * hb - 1, 0), 0))
    nxt = pl.BlockSpec((HALO, D_POOL), lambda b, i: (jnp.minimum((b * nt + i + 1) * hb, nhb - 1), 0))
    full = lambda a: pl.BlockSpec(a.shape, lambda b, i: (0,) * a.ndim)
    return pl.pallas_call(
        functools.partial(_pool_body, seq),
        grid=(batch, nt),
        in_specs=[prev, cur, nxt, full(w_bd), full(scale), full(gain)],
        out_specs=cur,
        out_shape=jax.ShapeDtypeStruct((t, D_POOL), F32),
        scratch_shapes=[pltpu.VMEM((tl + 4 * HALO, D_POOL), F32)] * 4,
        compiler_params=_cparams(("parallel", "parallel")), name="pool_mixer")(za, za, za, w_bd, scale, gain)


def _s5_scan_steps(n_chunks):
    return max(1, int(math.ceil(math.log2(n_chunks))))


def _s5_tables(lam_re, lam_im, log_step, b_re, b_im, c_re, c_im, n_chunks):
    L, C, P, G = S5_CHUNK, S5_GROUP, S5_STATE, S5_NGROUPS
    hp = lax.Precision.HIGHEST
    lr = lam_re.astype(F32)
    li = lam_im.astype(F32)
    dt = jnp.exp(log_step.astype(F32))[..., None]
    ar = lr * dt
    ai = li * dt
    jj = jnp.arange(L + 1, dtype=F32)

    def powers(j, a_r, a_i):
        mag = jnp.exp(j * a_r)
        return mag * jnp.cos(j * a_i), mag * jnp.sin(j * a_i)

    e_r, e_i = powers(jj[None, None, :, None], ar[:, :, None, :], ai[:, :, None, :])
    et_r, et_i = powers(jj[None, None, None, :], ar[..., None], ai[..., None])
    den = lr * lr + li * li
    nr = e_r[:, :, 1, :] - 1.0
    ni = e_i[:, :, 1, :]
    q_r = (nr * lr + ni * li) / den
    q_i = (ni * lr - nr * li) / den
    br = b_re.astype(F32)
    bi = b_im.astype(F32)
    bbt_r = (q_r[..., None] * br - q_i[..., None] * bi).transpose(0, 1, 3, 2)
    bbt_i = (q_r[..., None] * bi + q_i[..., None] * br).transpose(0, 1, 3, 2)
    cr = c_re.astype(F32)
    ci = c_im.astype(F32)
    m_r = cr[:, :, None] * bbt_r[:, :, :, None] - ci[:, :, None] * bbt_i[:, :, :, None]
    m_i = cr[:, :, None] * bbt_i[:, :, :, None] + ci[:, :, None] * bbt_r[:, :, :, None]
    kk = (jnp.einsum('dgxcp,dgpj->dgxcj', m_r, et_r[..., :L], precision=hp)
          - jnp.einsum('dgxcp,dgpj->dgxcj', m_i, et_i[..., :L], precision=hp))
    kf, kb = kk[0], kk[1]
    lagtab = jnp.concatenate([kb[..., :0:-1], kf[..., :1] + kb[..., :1], kf[..., 1:],
                              jnp.zeros_like(kf[..., :1])], axis=-1)
    bits = lax.bitcast_convert_type(lagtab.astype(BF16).astype(F32), jnp.uint32).reshape(G, C * C, 2 * L)
    lagtab = lax.shift_right_logical(bits[..., :L], jnp.uint32(16)) | (bits[..., L:] & jnp.uint32(0xFFFF0000))
    def summary(er, ei, b_r, b_i):
        w_r = er[:, None] * b_r[:, :, None] - ei[:, None] * b_i[:, :, None]
        w_i = er[:, None] * b_i[:, :, None] + ei[:, None] * b_r[:, :, None]
        return w_r.reshape(G, C * L, P), w_i.reshape(G, C * L, P)
    wf_r, wf_i = summary(e_r[0, :, L - 1::-1], e_i[0, :, L - 1::-1], bbt_r[0], bbt_i[0])
    wb_r, wb_i = summary(e_r[1, :, :L], e_i[1, :, :L], bbt_r[1], bbt_i[1])
    wst = jnp.concatenate([wf_r, wb_r, wf_i, wb_i], axis=-1)
    def carry_ops(er, ei, c_r, c_i):
        ct_r = c_r.transpose(0, 2, 1)[..., None]
        ct_i = c_i.transpose(0, 2, 1)[..., None]
        v_r = ct_r * er[:, :, None] - ct_i * ei[:, :, None]
        v_i = -(ct_r * ei[:, :, None] + ct_i * er[:, :, None])
        return v_r.reshape(G, P, C * L), v_i.reshape(G, P, C * L)
    vf_r, vf_i = carry_ops(et_r[0, :, :, 1:L + 1], et_i[0, :, :, 1:L + 1], cr[0], ci[0])
    vb_r, vb_i = carry_ops(et_r[1, :, :, L:0:-1], et_i[1, :, :, L:0:-1], cr[1], ci[1])
    vmat = jnp.concatenate([vf_r, vb_r, vf_i, vb_i], axis=1)
    n_steps = _s5_scan_steps(n_chunks)
    kpow =(L * (2 ** jnp.arange(n_steps, dtype=F32)))[None, None, :, None]
    p_r, p_i = powers(kpow, ar[:, :, None, :], ai[:, :, None, :])
    pad = jnp.zeros((G, (-n_steps) % 8, 2 * P), F32)
    p_r = jnp.concatenate([jnp.concatenate([p_r[0], p_r[1]], axis=-1), pad], axis=1)
    p_i = jnp.concatenate([jnp.concatenate([p_i[0], p_i[1]], axis=-1), pad], axis=1)
    ptab = jnp.concatenate([p_r, p_i], axis=1)
    return lagtab, wst.astype(BF16), vmat.astype(BF16), ptab


def _s5_body(n_chunks, n_steps, u_ref, lag_ref, w_ref, v_ref, p_ref, y_ref, toep_s):
    L, C = S5_CHUNK, S5_GROUP
    n = u_ref.shape[1]
    s_i = lax.broadcasted_iota(jnp.int32, (L, L), 0)
    t_i = lax.broadcasted_iota(jnp.int32, (L, L), 1)
    non_positive_lag = t_i <= s_i

    def build_block_row(cp, carry):
        r0 = pl.multiple_of(cp * L, L)
        for c in range(C):
            k = lag_ref[0, pl.ds(cp * C + c, 1), :]
            r = pltpu.roll(jnp.broadcast_to(k, (L, L)), 1, 1, stride=1, stride_axis=0)
            bits = jnp.where(non_positive_lag, lax.shift_left(r, jnp.uint32(16)), r & jnp.uint32(0xFFFF0000))
            toep_s[pl.ds(r0, L), c * L:(c + 1) * L] = pltpu.bitcast(bits, F32).astype(BF16)
        return carry

    lax.fori_loop(0, C, build_block_row, 0)
    u = jnp.concatenate([u_ref[c] for c in range(C)], axis=1).astype(BF16)
    y = jnp.dot(u, toep_s[...], preferred_element_type=F32)
    s = jnp.dot(u, w_ref[0], preferred_element_type=F32)
    two_p = 2 * S5_STATE
    xr = s[:, :two_p]
    xi = s[:, two_p:]
    row = lax.broadcasted_iota(jnp.int32, (n, two_p), 0) % n_chunks
    is_fwd = lax.broadcasted_iota(jnp.int32, (n, two_p), 1) < S5_STATE
    pim0 = p_ref.shape[1] // 2

    def shifted(a, k):
        down = jnp.where(row >= k, pltpu.roll(a, k, 0), 0.0)
        up = jnp.where(row < n_chunks - k, pltpu.roll(a, n - k, 0), 0.0)
        return jnp.where(is_fwd, down, up)

    for j in range(n_steps):
        k = 1 << j
        pr = p_ref[0, j:j + 1, :]
        pi = p_ref[0, pim0 + j:pim0 + j + 1, :]
        sr = shifted(xr, k)
        si = shifted(xi, k)
        xr, xi = xr + pr * sr - pi * si, xi + pr * si + pi * sr
    carry = jnp.concatenate([shifted(xr, 1), shifted(xi, 1)], axis=1).astype(BF16)
    y = y + jnp.dot(carry, v_ref[0], preferred_element_type=F32)
    for c in range(C):
        y_ref[c] = y[:, c * L:(c + 1) * L]


def _s5_core(zbt, batch, seq, tables):
    lagtab, wst, vmat, ptab = tables
    L = S5_CHUNK
    n_chunks = seq // L
    n_steps = _s5_scan_steps(n_chunks)
    n = batch * n_chunks
    per_g = lambda a: pl.BlockSpec((1,) + a.shape[1:], lambda g: (g, 0, 0))
    grp = pl.BlockSpec((S5_GROUP, n, L), lambda g: (g, 0, 0))
    return pl.pallas_call(
        functools.partial(_s5_body, n_chunks, n_steps),
        grid=(S5_NGROUPS,),
        in_specs=[grp, per_g(lagtab), per_g(wst), per_g(vmat), per_g(ptab)],
        out_specs=grp,
        out_shape=jax.ShapeDtypeStruct((D_S5, n, L), F32),
        scratch_shapes=[pltpu.VMEM((S5_GROUP * L, S5_GROUP * L), BF16)],
        compiler_params=_cparams(("parallel",)), name="s5_core")(zbt, lagtab, wst, vmat, ptab)


def _lru_body(reverse, prev_ref, cur_ref, next_ref, cw_ref, cb_ref, wg_ref, bg_ref, sp_ref, *rest):
    if reverse:
        hf_ref, gate_ref, gain_ref, o_ref, x_s, a_s, b_s, h_s, carry_s = rest
    else:
        o_ref, x_s, a_s, b_s, h_s, carry_s = rest
    nb, tl, _ = cur_ref.shape
    step = pl.program_id(0)
    nt = pl.num_programs(0)
    ti = nt - 1 - step if reverse else step

    @pl.when(step == 0)
    def _():
        carry_s[...] = jnp.zeros_like(carry_s)

    cw = cw_ref[...]
    for b in range(nb):
        x_s[0:HALO, :] = jnp.where(ti > 0, prev_ref[b], 0.0)
        x_s[HALO:HALO + tl, :] = cur_ref[b]
        x_s[HALO + tl:2 * HALO + tl, :] = jnp.where(ti < nt - 1, next_ref[b], 0.0)
        xc = cb_ref[...]
        for k in range(4):
            xc = xc + cw[k:k + 1, :] * x_s[HALO - 1 + k:HALO - 1 + k + tl, :]
        gates = jnp.dot(xc.astype(BF16), wg_ref[...], preferred_element_type=F32) + bg_ref[...]
        r = _sigmoid(gates[:, :D_LRU])
        ig = _sigmoid(gates[:, D_LRU:])
        log_a = (-RG_C) * r * sp_ref[...]
        a = jnp.exp(log_a)
        a_s[b] = a
        om = 1.0 - a * a
        b_s[b] = (om * lax.rsqrt(jnp.maximum(om, 1e-30))) * (ig * xc)

    def scan_step(s, hs):
        t = tl - 1 - s if reverse else s
        out = []
        for b in range(nb):
            h = a_s[b, pl.ds(t, 1), :] * hs[b] + b_s[b, pl.ds(t, 1), :]
            h_s[b, pl.ds(t, 1), :] = h
            out.append(h)
        return tuple(out)

    hs = lax.fori_loop(0, tl, scan_step, tuple(carry_s[b:b + 1, :] for b in range(nb)), unroll=8)
    for b in range(nb):
        carry_s[b:b + 1, :] = hs[b]

    if reverse:
        for b in range(nb):
            y = (hf_ref[b] + h_s[b]) * _gelu(gate_ref[b])
            o_ref[b] = _rms(y, gain_ref[...])
    else:
        o_ref[...] = h_s[...]


def _lru_pass(reverse, zc3, params, extra):
    nb, seq, _ = zc3.shape
    tl = LRU_TILE
    nt = seq // tl
    hb = tl // HALO
    nhb = seq // HALO
    tix = (lambda i: nt - 1 - i) if reverse else (lambda i: i)
    cur = pl.BlockSpec((nb, tl, D_LRU), lambda i: (0, tix(i), 0))
    prev = pl.BlockSpec((nb, HALO, D_LRU), lambda i: (0, jnp.maximum(tix(i) * hb - 1, 0), 0))
    nxt = pl.BlockSpec((nb, HALO, D_LRU), lambda i: (0, jnp.minimum((tix(i) + 1) * hb, nhb - 1), 0))
    full = lambda a: pl.BlockSpec(a.shape, lambda i: (0,) * a.ndim)
    ins = [zc3, zc3, zc3, *params]
    in_specs = [prev, cur, nxt] + [full(a) for a in params]
    if reverse:
        hf, gate, gain = extra
        ins += [hf, gate, gain]
        in_specs += [cur, cur, full(gain)]
    return pl.pallas_call(
        functools.partial(_lru_body, reverse),
        grid=(nt,), in_specs=in_specs, out_specs=cur,
        out_shape=jax.ShapeDtypeStruct((nb, seq, D_LRU), F32),
        scratch_shapes=[pltpu.VMEM((tl + 2 * HALO, D_LRU), F32),
                        pltpu.VMEM((nb, tl, D_LRU), F32),
                        pltpu.VMEM((nb, tl, D_LRU), F32),
                        pltpu.VMEM((nb, tl, D_LRU), F32),
                        pltpu.VMEM((8, D_LRU), F32)],
        compiler_params=_cparams(("arbitrary",)),
        name="lru_bwd" if reverse else "lru_fwd")(*ins)


def _block_diag(w):
    h, d, _ = w.shape
    eye = jnp.eye(h, dtype=w.dtype)
    return (eye[:, None, :, None] * w[:, :, None, :]).reshape(h * d, h * d)


def _lru_params(w_a, b_a, w_x, b_x, lam):
    bd = jax.vmap(_block_diag)
    wg = jnp.concatenate([bd(w_a), bd(w_x)], axis=-1).astype(BF16)
    bg = jnp.concatenate([b_a, b_x], axis=-1)[:, None, :]
    sp = jax.nn.softplus(-lam.astype(F32))[:, None, :]
    return wg, bg, sp


def _lru_mixer(zc, zg, batch, seq, conv_w, conv_b, wg, bg, sp, gain):
    zc3 = zc.reshape(batch, seq, D_LRU)
    zg3 = zg.reshape(batch, seq, D_LRU)
    outs = None
    for d in (0, 1):
        params = [conv_w, conv_b[None, :], wg[d], bg[d], sp[d]]
        extra = None if d == 0 else (outs, zg3, gain[None, :])
        outs = _lru_pass(d == 1, zc3, params, extra)
    return outs.reshape(batch * seq, D_LRU)


def _mixout_body(x_ref, oa_ref, y3_ref, u3_ref, oc_ref, sd_ref, wglu_ref, bglu_ref, gb_ref,
                 wout_ref, g2_ref, rw_ref, rb_ref, xo_ref, h_ref, ti_ref, gt_ref):
    tm = x_ref.shape[0]
    per_step = tm // S5_CHUNK
    c0 = pl.program_id(1) * per_step
    yt = jnp.concatenate([y3_ref[:, c0 + n, :] for n in range(per_step)], axis=1)
    ut = jnp.concatenate([u3_ref[:, c0 + n, :] for n in range(per_step)], axis=1)
    vt = _gelu(yt + sd_ref[...] * ut).astype(BF16)
    gl = lax.dot_general(vt, wglu_ref[...], (((0,), (0,)), ((), ())), preferred_element_type=F32) + bglu_ref[...]
    ob = _rms(gl[:, :D_S5] * _sigmoid(gl[:, D_S5:]), gb_ref[...])
    o = jnp.concatenate([oa_ref[...], ob, oc_ref[...]], axis=1).astype(BF16)
    x = x_ref[...] + jnp.dot(o, wout_ref[...], preferred_element_type=F32)
    xo_ref[...] = x
    h = _rms(x, g2_ref[...])
    h_hi = h.astype(BF16)
    bits = pltpu.bitcast(h_hi.astype(F32), jnp.uint32)
    h_ref[...] = lax.shift_right_logical(bits[:, :D_MODEL // 2], jnp.uint32(16)) | (
        bits[:, D_MODEL // 2:] & jnp.uint32(0xFFFF0000))
    h_lo = (h - h_hi.astype(F32)).astype(BF16)
    nt_dims = (((1,), (1,)), ((), ()))
    la = lax.dot_general(rw_ref[...], h_hi, nt_dims, preferred_element_type=F32)
    lb = lax.dot_general(rw_ref[0:N_EXPERTS, :], h_lo, nt_dims, preferred_element_type=F32)
    logits = la[:N_EXPERTS] + la[N_EXPERTS:] + lb + rb_ref[...]
    eidx = lax.broadcasted_iota(jnp.int32, (N_EXPERTS, tm), 0).astype(F32)
    idx_rows, val_rows = [], []
    for k in range(TOP_K):
        m = jnp.max(logits, axis=0, keepdims=True)
        idx = jnp.min(jnp.where(logits == m, eidx, float(N_EXPERTS)), axis=0, keepdims=True)
        idx_rows.append(idx)
        val_rows.append(jnp.exp(m - val_rows[0]) if k else m)
        logits = jnp.where(eidx == idx, -jnp.inf, logits)
    top = val_rows[0]
    vals = jnp.concatenate([jnp.ones_like(top)] + [v for v in val_rows[1:]], axis=0)
    ti_ref[...] = jnp.concatenate(idx_rows, axis=0).astype(jnp.int32)
    gt_ref[...] = vals / jnp.sum(vals, axis=0, keepdims=True)


def _router_split(rw):
    rw_hi = rw.astype(BF16)
    rw_lo = (rw - rw_hi.astype(F32)).astype(BF16)
    return jnp.concatenate([jnp.swapaxes(rw_hi, -1, -2), jnp.swapaxes(rw_lo, -1, -2)], axis=-2)


def _mixout(x, oa, yt3, zbt3, oc, s5_d, wglu, bglu, gain_b, wout, g2, rw2, rb):
    t = x.shape[0]
    tm = ROW_TILE
    nc = tm * ROW_SPLIT // S5_CHUNK
    row = lambda n: pl.BlockSpec((tm, n), lambda i, j: (i * ROW_SPLIT + j, 0))
    col = pl.BlockSpec((TOP_K, tm), lambda i, j: (0, i * ROW_SPLIT + j))
    chunks = pl.BlockSpec((D_S5, nc, S5_CHUNK), lambda i, j: (0, i, 0))
    full = lambda a: pl.BlockSpec(a.shape, lambda i, j: (0,) * a.ndim)
    params = [s5_d.reshape(D_S5, 1), wglu, bglu, gain_b, wout, g2, rw2, rb.reshape(N_EXPERTS, 1)]
    return pl.pallas_call(
        _mixout_body,
        grid=(t // (tm * ROW_SPLIT), ROW_SPLIT),
        in_specs=[row(D_MODEL), row(D_POOL), chunks, chunks, row(D_LRU)] + [full(a) for a in params],
        out_specs=[row(D_MODEL), row(D_MODEL // 2), col, col],
        out_shape=[jax.ShapeDtypeStruct((t, D_MODEL), F32), jax.ShapeDtypeStruct((t, D_MODEL // 2), jnp.uint32),
                   jax.ShapeDtypeStruct((TOP_K, t), jnp.int32), jax.ShapeDtypeStruct((TOP_K, t), F32)],
        compiler_params=_cparams(("parallel", "arbitrary")), name="mix_out")(x, oa, yt3, zbt3, oc, *params)


def _expert_body(be_ref, nu_ref, first_ref, xl_ref, xh_ref, wgu_ref, bg_ref, bu_ref, wd_ref, bd_ref, p_ref, xs_ref,
                 y_ref, wg_s, wu_s, wd_s, ws_s):
    i = pl.program_id(0)
    live = i < nu_ref[0]
    x_scale = xs_ref[...]
    act_scale = FP8_TARGET / ((SWIGLU_LIMIT + 1.0) * SWIGLU_LIMIT)

    @pl.when(jnp.logical_and(live, first_ref[i] == 1))
    def _():
        p = p_ref[...]
        half = MXU_TILE // 2
        n_groups = wgu_ref.shape[3] // MXU_TILE
        amax = functools.reduce(jnp.maximum, [
            jnp.max(jnp.abs(wgu_ref[0, 0, :, MXU_TILE * j:MXU_TILE * (j + 1)]), axis=(0, 1), keepdims=True)
            for j in range(n_groups)])
        s_gu = FP8_TARGET / jnp.maximum(amax, FP8_TINY)
        for j in range(n_groups):
            w = wgu_ref[0, 0, :, MXU_TILE * j:MXU_TILE * (j + 1)].astype(BF16)
            r = jnp.dot(w, p, preferred_element_type=F32) * s_gu
            wg_s[:, half * j:half * (j + 1)] = r[:, :half].astype(FP8)
            wu_s[:, half * j:half * (j + 1)] = r[:, half:].astype(FP8)
        wd = wd_ref[0, 0]
        dmax = jnp.max(jnp.abs(wd), axis=(0, 1), keepdims=True)
        s_d = FP8_TARGET / jnp.maximum(dmax, FP8_TINY)
        wd_s[...] = (wd * s_d).astype(FP8)
        ws_s[0:1, :] = jnp.broadcast_to(1.0 / (s_gu * x_scale), (1, ws_s.shape[1]))
        ws_s[1:2, :] = jnp.broadcast_to(1.0 / (s_d * act_scale), (1, ws_s.shape[1]))

    @pl.when(live)
    def _():
        words = jnp.concatenate([xl_ref[...], xh_ref[...]], axis=1)
        lo = pltpu.bitcast(lax.shift_left(words, jnp.uint32(16)), F32)
        hi = pltpu.bitcast(words & jnp.uint32(0xFFFF0000), F32)
        x = (jnp.concatenate([lo, hi], axis=1) * x_scale).astype(FP8)
        gu_inv = ws_s[0:1, 0:1]
        g = jnp.dot(x, wg_s[...], preferred_element_type=F32) * gu_inv + bg_ref[0, 0]
        u = jnp.dot(x, wu_s[...], preferred_element_type=F32) * gu_inv + bu_ref[0, 0]
        g = jnp.minimum(g, SWIGLU_LIMIT)
        u = jnp.clip(u, -SWIGLU_LIMIT, SWIGLU_LIMIT)
        act = ((u + 1.0) * (g * _sigmoid(g * SWIGLU_ALPHA)) * act_scale).astype(FP8)
        y = jnp.dot(act, wd_s[...], preferred_element_type=F32) * ws_s[1:2, 0:1] + bd_ref[0, 0]
        y_ref[...] = y.astype(y_ref.dtype)

    @pl.when(jnp.logical_not(live))
    def _():
        y_ref[...] = jnp.zeros_like(y_ref)


def _expert_mlp(layer, xs, x_bound, blk_e, n_used, w_gate_up, b_gate, b_up, w_down, b_down):
    n_rows, dh = xs[0].shape
    bm = MOE_ROWS
    x_scale = (FP8_TARGET / jnp.maximum(x_bound, FP8_TINY)).astype(F32).reshape(1, 1)
    first = jnp.concatenate([jnp.ones((1,), jnp.int32), (blk_e[1:] != blk_e[:-1]).astype(jnp.int32)])
    idx = jnp.arange(MXU_TILE)
    perm = jax.nn.one_hot(jnp.where(idx % 2 == 0, idx // 2, MXU_TILE // 2 + idx // 2), MXU_TILE, dtype=BF16)
    rows = pl.BlockSpec((bm, dh), lambda i, be, nu, fi: (jnp.minimum(i, nu[0] - 1), 0))
    per_e = lambda a: pl.BlockSpec((1, 1) + a.shape[2:], lambda i, be, nu, fi: (layer, be[i], 0, 0))
    grid_spec = pltpu.PrefetchScalarGridSpec(
        num_scalar_prefetch=3, grid=(n_rows // bm,),
        in_specs=[rows, rows, per_e(w_gate_up), per_e(b_gate), per_e(b_up), per_e(w_down), per_e(b_down),
                  pl.BlockSpec(perm.shape, lambda i, be, nu, fi: (0, 0)),
                  pl.BlockSpec((1, 1), lambda i, be, nu, fi: (0, 0))],
        out_specs=pl.BlockSpec((bm, D_MODEL), lambda i, be, nu, fi: (i, 0)),
        scratch_shapes=[pltpu.VMEM((D_MODEL, D_FF), FP8), pltpu.VMEM((D_MODEL, D_FF), FP8),
                        pltpu.VMEM((D_FF, D_MODEL), FP8), pltpu.VMEM((8, 128), F32)])
    return pl.pallas_call(
        _expert_body, grid_spec=grid_spec,
        out_shape=jax.ShapeDtypeStruct((n_rows, D_MODEL), BF16),
        compiler_params=pltpu.CompilerParams(dimension_semantics=("arbitrary",),
                                             vmem_limit_bytes=EXPERT_VMEM_LIMIT),
        name="expert_mlp")(blk_e, n_used, first, *xs, w_gate_up, b_gate, b_up, w_down, b_down, perm, x_scale)


def _sc_dispatch(h, dest, n_rows):
    t, d = h.shape
    win = SC_WINDOW
    dh = d // SC_COL_SPLIT
    idx = [dest[:, k].reshape(1, t) for k in range(TOP_K)]
    mesh = plsc.VectorSubcoreMesh(core_axis_name="core", subcore_axis_name="subcore")

    def scatter_cols(j):
        @functools.partial(pl.kernel, out_type=jax.ShapeDtypeStruct((n_rows, dh), h.dtype), mesh=mesh,
                           scratch_types=[], name="sc_dispatch")
        def scatter_rows(x_hbm, i0_hbm, i1_hbm, i2_hbm, i3_hbm, o_hbm):
            def body(x_vmem, *idx_vmem):
                for iv in idx_vmem:
                    pltpu.sync_copy(x_vmem, o_hbm.at[iv.at[0]])

            pltpu.emit_pipeline(
                body, grid=(t // win,),
                in_specs=[pl.BlockSpec((win, dh), lambda i: (i, j))]
                         + [pl.BlockSpec((1, win), lambda i: (0, i))] * TOP_K,
                out_specs=[],
                core_axis_name=("core", "subcore"),
                dimension_semantics=(pltpu.PARALLEL,),
            )(x_hbm, i0_hbm, i1_hbm, i2_hbm, i3_hbm)

        return scatter_rows(h, *idx)

    return [scatter_cols(j) for j in range(SC_COL_SPLIT)]


def _moe_dispatch(top_i, t):
    bm = MOE_ROWS
    n_blocks = (t * TOP_K) // bm + N_EXPERTS
    onehot = jax.nn.one_hot(top_i, N_EXPERTS, dtype=jnp.int32)
    sel = jnp.sum(onehot, axis=1)
    csum = jnp.cumsum(sel, axis=0)
    counts = csum[-1]
    padded = ((counts + bm - 1) // bm) * bm
    pad_end = jnp.cumsum(padded)
    pad_start = pad_end - padded
    base = (csum - sel) + pad_start[None, :]
    dest = jnp.sum(onehot * base[:, None, :], axis=-1)
    blk_start = jnp.arange(n_blocks, dtype=jnp.int32) * bm
    blk_e = jnp.minimum(jnp.sum((pad_end[None, :] <= blk_start[:, None]).astype(jnp.int32), axis=1),
                        N_EXPERTS - 1)
    n_used = (pad_end[-1] // bm).astype(jnp.int32).reshape(1)
    return dest, n_blocks * bm, blk_e, n_used


def _final_body(x_ref, y0, y1, y2, y3, gt_ref, g_ref, *rest):
    o_ref = rest[-1]
    o_ref[...] = _rms(_combine(x_ref, (y0, y1, y2, y3), gt_ref), g_ref[...])


def _final(x, add, g):
    t = x.shape[0]
    tm = ROW_TILE
    parts, gates = add
    per_part = t // tm // len(parts)
    out = None
    for p, ys in enumerate(parts):
        off = p * per_part
        row_p = lambda n, off=off: pl.BlockSpec((tm, n), lambda i: (i + off, 0))
        row_y = pl.BlockSpec((tm, D_MODEL), lambda i: (i, 0))
        prev = [] if out is None else [out]
        out = pl.pallas_call(
            _final_body, grid=(per_part,),
            in_specs=[row_p(D_MODEL)] + [row_y] * TOP_K + [row_p(TOP_K), pl.BlockSpec(g.shape, lambda i: (0, 0))]
                     + [pl.BlockSpec(memory_space=pl.ANY)] * len(prev),
            out_specs=row_p(D_MODEL), out_shape=jax.ShapeDtypeStruct((t, D_MODEL), F32),
            input_output_aliases={2 + TOP_K + 1: 0} if prev else {},
            compiler_params=_cparams(("parallel",)), name="final_norm")(x, *ys, gates, g, *prev)
    return out


def kernel(x, norm1_g, w_in, pool_w, pool_scale, s5_lam_re, s5_lam_im, s5_log_step, s5_b_re, s5_b_im, s5_c_re, s5_c_im, s5_d, s5_w_glu, s5_b_glu, lru_conv_w, lru_conv_b, lru_w_a, lru_b_a, lru_w_x, lru_b_x, lru_lam, mix_gain, w_out, norm2_g, router_w, router_b, w_gate_up, b_gate_up, w_down, b_down, final_g):
    batch, seq, d = x.shape
    t = batch * seq
    depth = norm1_g.shape[0]
    xt = x.reshape(t, d)
    add = None
    o1, o2 = D_POOL, D_POOL + D_S5
    b_gate = b_gate_up[:, :, None, 0::2]
    b_up = b_gate_up[:, :, None, 1::2]
    b_dn = b_down[:, :, None, :]
    for l in range(depth):
        w_rest = jnp.concatenate([w_in[l][:, :o1], w_in[l][:, o2:]], axis=1).astype(BF16)
        w_s5t = w_in[l][:, o1:o2].T.astype(BF16)
        s5_tabs = _s5_tables(s5_lam_re[l], s5_lam_im[l], s5_log_step[l], s5_b_re[l], s5_b_im[l],
                             s5_c_re[l], s5_c_im[l], seq // S5_CHUNK)
        lru_wg, lru_bg, lru_sp = _lru_params(lru_w_a[l], lru_b_a[l], lru_w_x[l], lru_b_x[l], lru_lam[l])
        if add is None:
            za, zbt, zc, zg = _inproj(xt, None, norm1_g[l][None, :], w_rest, w_s5t)
        else:
            xt, za, zbt, zc, zg = _inproj(xt, add, norm1_g[l][None, :], w_rest, w_s5t)
        oa = _pool_mixer(za, batch, seq, _block_diag(pool_w[l]).astype(BF16), pool_scale[l][None, :],
                         mix_gain[l][None, :o1])
        yt = _s5_core(zbt, batch, seq, s5_tabs)
        oc = _lru_mixer(zc, zg, batch, seq, lru_conv_w[l], lru_conv_b[l], lru_wg, lru_bg, lru_sp,
                        mix_gain[l][o2:])
        xt, h2, top_i, gates = _mixout(
            xt, oa, yt, zbt, oc, s5_d[l], s5_w_glu[l].astype(BF16), s5_b_glu[l][None, :],
            mix_gain[l][None, o1:o2], w_out[l].astype(BF16), norm2_g[l][None, :], _router_split(router_w[l]),
            router_b[l])
        top_i, gates = top_i.T, gates.T
        dest, n_rows, blk_e, n_used = _moe_dispatch(top_i, t)
        xs = _sc_dispatch(h2, dest, n_rows)
        x_bound = math.sqrt(D_MODEL) * jnp.max(jnp.abs(norm2_g[l]))
        ys = _expert_mlp(l, xs, x_bound, blk_e, n_used, w_gate_up, b_gate, b_up, w_down, b_dn)
        tp = t // TOKEN_PARTS
        add = ([[ys.at[dest[p * tp:(p + 1) * tp, k]].get(mode="promise_in_bounds") for k in range(TOP_K)]
                for p in range(TOKEN_PARTS)], gates)
    out = _final(xt, add, final_g[None, :])
    return out.reshape(batch, seq, d)
```

```python
import functools
import math

import jax
import jax.numpy as jnp
from jax import lax
from jax.experimental import pallas as pl
from jax.experimental.pallas import tpu as pltpu
from jax.experimental.pallas import tpu_sc as plsc

F32 = jnp.float32
BF16 = jnp.bfloat16
FP8 = jnp.float8_e4m3fn
FP8_TARGET = 416.0
FP8_TINY = 1e-30

D_MODEL = 1024
D_POOL = 256
D_S5 = 384
D_LRU = 384
POOL_WINDOWS = (2, 4, 8, 16)
POOL_GROUP = 64
S5_GROUP = 16
S5_NGROUPS = 24
S5_STATE = 64
LRU_HEADS = 6
LRU_HEAD_DIM = 64
RG_C = 8.0
N_EXPERTS = 32
TOP_K = 4
D_FF = 1024
SWIGLU_LIMIT = 7.0
SWIGLU_ALPHA = 1.702
EPS = 1e-5

ROW_TILE = 512
ROW_SPLIT = 2
S5_CHUNK = 128
POOL_TILE = 512
LRU_TILE = 256
HALO = 8
MOE_ROWS = 512
TOKEN_PARTS = 2
SC_WINDOW = 128
SC_COL_SPLIT = 2
MXU_TILE = 256
EXPERT_VMEM_LIMIT = 56 * 1024 * 1024
VMEM_LIMIT = 48 * 1024 * 1024


def _cparams(sem):
    return pltpu.CompilerParams(dimension_semantics=sem, vmem_limit_bytes=VMEM_LIMIT)


def _rms(x, g):
    return x * lax.rsqrt(jnp.mean(x * x, axis=-1, keepdims=True) + EPS) * g


def _gelu(x):
    return 0.5 * x * (1.0 + jnp.tanh(0.7978845608028654 * (x + 0.044715 * (x * x * x))))


def _sigmoid(x):
    return 1.0 / (1.0 + jnp.exp(-x))


def _combine(x_ref, y_refs, gt_ref):
    x = x_ref[...]
    gt = gt_ref[...]
    for k in range(TOP_K):
        x = x + gt[:, k:k + 1] * y_refs[k][...].astype(F32)
    return x


def _inproj_body(has_add, *refs):
    if has_add:
        x_ref, y0, y1, y2, y3, gt_ref, g_ref, w_ref, wbt_ref = refs[:9]
        xo_ref, za, zbt, zc, zg = refs[-5:]
        x = _combine(x_ref, (y0, y1, y2, y3), gt_ref)
        xo_ref[...] = x
    else:
        x_ref, g_ref, w_ref, wbt_ref, za, zbt, zc, zg = refs
        x = x_ref[...]
    h = _rms(x, g_ref[...]).astype(BF16)
    z = jnp.dot(h, w_ref[...], preferred_element_type=F32)
    za[...] = z[:, :D_POOL]
    zc[...] = z[:, D_POOL:D_POOL + D_LRU]
    zg[...] = z[:, D_POOL + D_LRU:]
    zt = lax.dot_general(wbt_ref[...], h, (((1,), (1,)), ((), ())), preferred_element_type=F32)
    per_step = zt.shape[1] // S5_CHUNK
    for n in range(per_step):
        zbt[:, pl.program_id(1) * per_step + n, :] = zt[:, n * S5_CHUNK:(n + 1) * S5_CHUNK]


def _inproj(x, add, g, w_rest, w_s5t):
    t = x.shape[0]
    tm = ROW_TILE
    nc = tm * ROW_SPLIT // S5_CHUNK
    row = lambda n: pl.BlockSpec((tm, n), lambda i, j: (i * ROW_SPLIT + j, 0))
    full = lambda a: pl.BlockSpec(a.shape, lambda i, j: (0,) * a.ndim)
    z_shapes = [jax.ShapeDtypeStruct((t, D_POOL), F32), jax.ShapeDtypeStruct((D_S5, t // S5_CHUNK, S5_CHUNK), F32),
                jax.ShapeDtypeStruct((t, D_LRU), F32), jax.ShapeDtypeStruct((t, D_LRU), F32)]
    z_specs = [row(D_POOL), pl.BlockSpec((D_S5, nc, S5_CHUNK), lambda i, j: (0, i, 0)), row(D_LRU), row(D_LRU)]
    n_outer = t // (tm * ROW_SPLIT)
    if add is None:
        return pl.pallas_call(
            functools.partial(_inproj_body, False),
            grid=(n_outer, ROW_SPLIT), in_specs=[row(D_MODEL), full(g), full(w_rest), full(w_s5t)],
            out_specs=z_specs, out_shape=z_shapes,
            compiler_params=_cparams(("parallel", "arbitrary")), name="inproj")(x, g, w_rest, w_s5t)
    parts, gates = add
    per_part = n_outer // len(parts)
    out_shape = [jax.ShapeDtypeStruct((t, D_MODEL), F32)] + z_shapes
    outs = []
    for p, ys in enumerate(parts):
        off = p * per_part
        row_p = lambda n, off=off: pl.BlockSpec((tm, n), lambda i, j: ((i + off) * ROW_SPLIT + j, 0))
        row_y = pl.BlockSpec((tm, D_MODEL), lambda i, j: (i * ROW_SPLIT + j, 0))
        zbt_p = pl.BlockSpec((D_S5, nc, S5_CHUNK), lambda i, j, off=off: (0, i + off, 0))
        n_in = 1 + TOP_K + 4
        outs = pl.pallas_call(
            functools.partial(_inproj_body, True),
            grid=(per_part, ROW_SPLIT),
            in_specs=[row_p(D_MODEL)] + [row_y] * TOP_K + [row_p(TOP_K), full(g), full(w_rest), full(w_s5t)]
                     + [pl.BlockSpec(memory_space=pl.ANY)] * len(outs),
            out_specs=[row_p(D_MODEL), row_p(D_POOL), zbt_p, row_p(D_LRU), row_p(D_LRU)],
            out_shape=out_shape,
            input_output_aliases={n_in + k: k for k in range(len(outs))},
            compiler_params=_cparams(("parallel", "arbitrary")), name="inproj")(
                x, *ys, gates, g, w_rest, w_s5t, *outs)
    return outs


def _pool_body(seq, prev_ref, cur_ref, next_ref, w_ref, sc_ref, gain_ref, o_ref, u_s, s2_s, s4_s, s8_s):
    tl = cur_ref.shape[0]
    i = pl.program_id(1)
    nt = pl.num_programs(1)
    zero8 = jnp.zeros((HALO, D_POOL), F32)
    for buf in (u_s, s2_s, s4_s, s8_s):
        buf[0:HALO, :] = zero8
        buf[tl + 3 * HALO:tl + 4 * HALO, :] = zero8
    u_s[HALO:2 * HALO, :] = jnp.where(i > 0, prev_ref[...], 0.0)
    u_s[2 * HALO:2 * HALO + tl, :] = cur_ref[...]
    u_s[2 * HALO + tl:3 * HALO + tl, :] = jnp.where(i < nt - 1, next_ref[...], 0.0)
    r = tl + 2 * HALO
    s2_s[HALO:HALO + r, :] = u_s[HALO - 1:HALO - 1 + r, :] + u_s[HALO:HALO + r, :]
    s4_s[HALO:HALO + r, :] = s2_s[HALO - 1:HALO - 1 + r, :] + s2_s[HALO + 1:HALO + 1 + r, :]
    s8_s[HALO:HALO + r, :] = s4_s[HALO - 2:HALO - 2 + r, :] + s4_s[HALO + 2:HALO + 2 + r, :]
    o = 2 * HALO
    s16 = s8_s[o - 4:o - 4 + tl, :] + s8_s[o + 4:o + 4 + tl, :]
    s8 = s8_s[o:o + tl, :]
    s4 = s4_s[o:o + tl, :]
    s2 = s2_s[o:o + tl, :]
    u = u_s[o:o + tl, :]
    lane = lax.broadcasted_iota(jnp.int32, (tl, D_POOL), 1)
    tpos = lax.broadcasted_iota(jnp.int32, (tl, D_POOL), 0) + i * tl
    g0, g1, g2 = lane < POOL_GROUP, lane < 2 * POOL_GROUP, lane < 3 * POOL_GROUP
    half = jnp.where(g0, 1, jnp.where(g1, 2, jnp.where(g2, 4, 8)))
    wsum = jnp.where(g0, s2, jnp.where(g1, s4, jnp.where(g2, s8, s16)))
    cnt = (jnp.minimum(tpos + half, seq) - jnp.maximum(tpos - half, 0)).astype(F32)
    d = wsum / cnt - u
    y = jnp.dot(d.astype(BF16), w_ref[...], preferred_element_type=F32) * sc_ref[...]
    o_ref[...] = _rms(y, gain_ref[...])


def _pool_mixer(za, batch, seq, w_bd, scale, gain):
    t = za.shape[0]
    tl = POOL_TILE
    nt = seq // tl
    hb = tl // HALO
    nhb = t // HALO
    cur = pl.BlockSpec((tl, D_POOL), lambda b, i: (b * nt + i, 0))
    prev = pl.BlockSpec((HALO, D_POOL), lambda b, i: (jnp.maximum((b * nt + i) * hb - 1, 0), 0))
    nxt = pl.BlockSpec((HALO, D_POOL), lambda b, i: (jnp.minimum((b * nt + i + 1) * hb, nhb - 1), 0))
    full = lambda a: pl.BlockSpec(a.shape, lambda b, i: (0,) * a.ndim)
    return pl.pallas_call(
        functools.partial(_pool_body, seq),
        grid=(batch, nt),
        in_specs=[prev, cur, nxt, full(w_bd), full(scale), full(gain)],
        out_specs=cur,
        out_shape=jax.ShapeDtypeStruct((t, D_POOL), F32),
        scratch_shapes=[pltpu.VMEM((tl + 4 * HALO, D_POOL), F32)] * 4,
        compiler_params=_cparams(("parallel", "parallel")), name="pool_mixer")(za, za, za, w_bd, scale, gain)


def _s5_scan_steps(n_chunks):
    return max(1, int(math.ceil(math.log2(n_chunks))))


def _s5_tables(lam_re, lam_im, log_step, b_re, b_im, c_re, c_im, n_chunks):
    L, C, P, G = S5_CHUNK, S5_GROUP, S5_STATE, S5_NGROUPS
    hp = lax.Precision.HIGHEST
    lr = lam_re.astype(F32)
    li = lam_im.astype(F32)
    dt = jnp.exp(log_step.astype(F32))[..., None]
    ar = lr * dt
    ai = li * dt
    jj = jnp.arange(L + 1, dtype=F32)

    def powers(j, a_r, a_i):
        mag = jnp.exp(j * a_r)
        return mag * jnp.cos(j * a_i), mag * jnp.sin(j * a_i)

    e_r, e_i = powers(jj[None, None, :, None], ar[:, :, None, :], ai[:, :, None, :])
    et_r, et_i = powers(jj[None, None, None, :], ar[..., None], ai[..., None])
    den = lr * lr + li * li
    nr = e_r[:, :, 1, :] - 1.0
    ni = e_i[:, :, 1, :]
    q_r = (nr * lr + ni * li) / den
    q_i = (ni * lr - nr * li) / den
    br = b_re.astype(F32)
    bi = b_im.astype(F32)
    bbt_r = (q_r[..., None] * br - q_i[..., None] * bi).transpose(0, 1, 3, 2)
    bbt_i = (q_r[..., None] * bi + q_i[..., None] * br).transpose(0, 1, 3, 2)
    cr = c_re.astype(F32)
    ci = c_im.astype(F32)
    m_r = cr[:, :, None] * bbt_r[:, :, :, None] - ci[:, :, None] * bbt_i[:, :, :, None]
    m_i = cr[:, :, None] * bbt_i[:, :, :, None] + ci[:, :, None] * bbt_r[:, :, :, None]
    kk = (jnp.einsum('dgxcp,dgpj->dgxcj', m_r, et_r[..., :L], precision=hp)
          - jnp.einsum('dgxcp,dgpj->dgxcj', m_i, et_i[..., :L], precision=hp))
    kf, kb = kk[0], kk[1]
    lagtab = jnp.concatenate([kb[..., :0:-1], kf[..., :1] + kb[..., :1], kf[..., 1:],
                              jnp.zeros_like(kf[..., :1])], axis=-1)
    bits = lax.bitcast_convert_type(lagtab.astype(BF16).astype(F32), jnp.uint32).reshape(G, C * C, 2 * L)
    lagtab = lax.shift_right_logical(bits[..., :L], jnp.uint32(16)) | (bits[..., L:] & jnp.uint32(0xFFFF0000))
    fb = lambda f, b, axis: jnp.concatenate([f, b], axis=axis)
    pw_s = jnp.stack([fb(e_r[0, :, L - 1::-1], e_r[1, :, :L], -1),
                      fb(e_i[0, :, L - 1::-1], e_i[1, :, :L], -1)], axis=1)
    bb = jnp.stack([fb(bbt_r[0], bbt_r[1], -1), fb(bbt_i[0], bbt_i[1], -1)], axis=1)
    pw_t = jnp.stack([fb(et_r[0, :, :, 1:L + 1], et_r[1, :, :, L:0:-1], 1),
                      fb(et_i[0, :, :, 1:L + 1], et_i[1, :, :, L:0:-1], 1)], axis=1)
    crt = cr.transpose(0, 1, 3, 2)
    cit = ci.transpose(0, 1, 3, 2)
    cc = jnp.stack([fb(crt[0], crt[1], 1), fb(cit[0], cit[1], 1)], axis=1)
    n_steps = _s5_scan_steps(n_chunks)
    kpow =(L * (2 ** jnp.arange(n_steps, dtype=F32)))[None, None, :, None]
    p_r, p_i = powers(kpow, ar[:, :, None, :], ai[:, :, None, :])
    pad = jnp.zeros((G, (-n_steps) % 8, 2 * P), F32)
    p_r = jnp.concatenate([jnp.concatenate([p_r[0], p_r[1]], axis=-1), pad], axis=1)
    p_i = jnp.concatenate([jnp.concatenate([p_i[0], p_i[1]], axis=-1), pad], axis=1)
    ptab = jnp.concatenate([p_r, p_i], axis=1)
    return lagtab, pw_s, bb, pw_t, cc, ptab


def _s5_body(n_chunks, n_steps, u_ref, lag_ref, pws_ref, bb_ref, pwt_ref, cc_ref, p_ref, y_ref,
             toep_s, w_s, v_s):
    L, C = S5_CHUNK, S5_GROUP
    n = u_ref.shape[1]
    two_p = 2 * S5_STATE
    e_r, e_i = pws_ref[0, 0], pws_ref[0, 1]
    et_r, et_i = pwt_ref[0, 0], pwt_ref[0, 1]
    c_r, c_i = cc_ref[0, 0], cc_ref[0, 1]
    for c in range(C):
        b_r = bb_ref[0, 0, c:c + 1, :]
        b_i = bb_ref[0, 1, c:c + 1, :]
        w_s[c * L:(c + 1) * L, :two_p] = (e_r * b_r - e_i * b_i).astype(BF16)
        w_s[c * L:(c + 1) * L, two_p:] = (e_r * b_i + e_i * b_r).astype(BF16)
        k_r = c_r[:, c:c + 1]
        k_i = c_i[:, c:c + 1]
        v_s[:two_p, c * L:(c + 1) * L] = (k_r * et_r - k_i * et_i).astype(BF16)
        v_s[two_p:, c * L:(c + 1) * L] = (-(k_r * et_i + k_i * et_r)).astype(BF16)
    s_i = lax.broadcasted_iota(jnp.int32, (L, L), 0)
    t_i = lax.broadcasted_iota(jnp.int32, (L, L), 1)
    non_positive_lag = t_i <= s_i

    def build_block_row(cp, carry):
        r0 = pl.multiple_of(cp * L, L)
        for c in range(C):
            k = lag_ref[0, pl.ds(cp * C + c, 1), :]
            r = pltpu.roll(jnp.broadcast_to(k, (L, L)), 1, 1, stride=1, stride_axis=0)
            bits = jnp.where(non_positive_lag, lax.shift_left(r, jnp.uint32(16)), r & jnp.uint32(0xFFFF0000))
            toep_s[pl.ds(r0, L), c * L:(c + 1) * L] = pltpu.bitcast(bits, F32).astype(BF16)
        return carry

    lax.fori_loop(0, C, build_block_row, 0)
    u = jnp.concatenate([u_ref[c] for c in range(C)], axis=1).astype(BF16)
    y = jnp.dot(u, toep_s[...], preferred_element_type=F32)
    s = jnp.dot(u, w_s[...], preferred_element_type=F32)
    xr = s[:, :two_p]
    xi = s[:, two_p:]
    row = lax.broadcasted_iota(jnp.int32, (n, two_p), 0) % n_chunks
    is_fwd = lax.broadcasted_iota(jnp.int32, (n, two_p), 1) < S5_STATE
    pim0 = p_ref.shape[1] // 2

    def shifted(a, k):
        down = jnp.where(row >= k, pltpu.roll(a, k, 0), 0.0)
        up = jnp.where(row < n_chunks - k, pltpu.roll(a, n - k, 0), 0.0)
        return jnp.where(is_fwd, down, up)

    for j in range(n_steps):
        k = 1 << j
        pr = p_ref[0, j:j + 1, :]
        pi = p_ref[0, pim0 + j:pim0 + j + 1, :]
        sr = shifted(xr, k)
        si = shifted(xi, k)
        xr, xi = xr + pr * sr - pi * si, xi + pr * si + pi * sr
    carry = jnp.concatenate([shifted(xr, 1), shifted(xi, 1)], axis=1).astype(BF16)
    y = y + jnp.dot(carry, v_s[...], preferred_element_type=F32)
    for c in range(C):
        y_ref[c] = y[:, c * L:(c + 1) * L]


def _s5_core(zbt, batch, seq, tables):
    L = S5_CHUNK
    n_chunks = seq // L
    n_steps = _s5_scan_steps(n_chunks)
    n = batch * n_chunks
    per_g = lambda a: pl.BlockSpec((1,) + a.shape[1:], lambda g: (g,) + (0,) * (a.ndim - 1))
    grp = pl.BlockSpec((S5_GROUP, n, L), lambda g: (g, 0, 0))
    return pl.pallas_call(
        functools.partial(_s5_body, n_chunks, n_steps),
        grid=(S5_NGROUPS,),
        in_specs=[grp] + [per_g(a) for a in tables],
        out_specs=grp,
        out_shape=jax.ShapeDtypeStruct((D_S5, n, L), F32),
        scratch_shapes=[pltpu.VMEM((S5_GROUP * L, S5_GROUP * L), BF16),
                        pltpu.VMEM((S5_GROUP * L, 4 * S5_STATE), BF16),
                        pltpu.VMEM((4 * S5_STATE, S5_GROUP * L), BF16)],
        compiler_params=_cparams(("parallel",)), name="s5_core")(zbt, *tables)


def _lru_body(reverse, prev_ref, cur_ref, next_ref, cw_ref, cb_ref, wg_ref, bg_ref, sp_ref, *rest):
    if reverse:
        hf_ref, gate_ref, gain_ref, o_ref, x_s, a_s, b_s, h_s, carry_s = rest
    else:
        o_ref, x_s, a_s, b_s, h_s, carry_s = rest
    nb, tl, _ = cur_ref.shape
    step = pl.program_id(0)
    nt = pl.num_programs(0)
    ti = nt - 1 - step if reverse else step

    @pl.when(step == 0)
    def _():
        carry_s[...] = jnp.zeros_like(carry_s)

    cw = cw_ref[...]
    for b in range(nb):
        x_s[0:HALO, :] = jnp.where(ti > 0, prev_ref[b], 0.0)
        x_s[HALO:HALO + tl, :] = cur_ref[b]
        x_s[HALO + tl:2 * HALO + tl, :] = jnp.where(ti < nt - 1, next_ref[b], 0.0)
        xc = cb_ref[...]
        for k in range(4):
            xc = xc + cw[k:k + 1, :] * x_s[HALO - 1 + k:HALO - 1 + k + tl, :]
        gates = jnp.dot(xc.astype(BF16), wg_ref[...], preferred_element_type=F32) + bg_ref[...]
        r = _sigmoid(gates[:, :D_LRU])
        ig = _sigmoid(gates[:, D_LRU:])
        log_a = (-RG_C) * r * sp_ref[...]
        a = jnp.exp(log_a)
        a_s[b] = a
        om = 1.0 - a * a
        b_s[b] = (om * lax.rsqrt(jnp.maximum(om, 1e-30))) * (ig * xc)

    def scan_step(s, hs):
        t = tl - 1 - s if reverse else s
        out = []
        for b in range(nb):
            h = a_s[b, pl.ds(t, 1), :] * hs[b] + b_s[b, pl.ds(t, 1), :]
            h_s[b, pl.ds(t, 1), :] = h
            out.append(h)
        return tuple(out)

    hs = lax.fori_loop(0, tl, scan_step, tuple(carry_s[b:b + 1, :] for b in range(nb)), unroll=8)
    for b in range(nb):
        carry_s[b:b + 1, :] = hs[b]

    if reverse:
        for b in range(nb):
            y = (hf_ref[b] + h_s[b]) * _gelu(gate_ref[b])
            o_ref[b] = _rms(y, gain_ref[...])
    else:
        o_ref[...] = h_s[...]


def _lru_pass(reverse, zc3, params, extra):
    nb, seq, _ = zc3.shape
    tl = LRU_TILE
    nt = seq // tl
    hb = tl // HALO
    nhb = seq // HALO
    tix = (lambda i: nt - 1 - i) if reverse else (lambda i: i)
    cur = pl.BlockSpec((nb, tl, D_LRU), lambda i: (0, tix(i), 0))
    prev = pl.BlockSpec((nb, HALO, D_LRU), lambda i: (0, jnp.maximum(tix(i) * hb - 1, 0), 0))
    nxt = pl.BlockSpec((nb, HALO, D_LRU), lambda i: (0, jnp.minimum((tix(i) + 1) * hb, nhb - 1), 0))
    full = lambda a: pl.BlockSpec(a.shape, lambda i: (0,) * a.ndim)
    ins = [zc3, zc3, zc3, *params]
    in_specs = [prev, cur, nxt] + [full(a) for a in params]
    if reverse:
        hf, gate, gain = extra
        ins += [hf, gate, gain]
        in_specs += [cur, cur, full(gain)]
    return pl.pallas_call(
        functools.partial(_lru_body, reverse),
        grid=(nt,), in_specs=in_specs, out_specs=cur,
        out_shape=jax.ShapeDtypeStruct((nb, seq, D_LRU), F32),
        scratch_shapes=[pltpu.VMEM((tl + 2 * HALO, D_LRU), F32),
                        pltpu.VMEM((nb, tl, D_LRU), F32),
                        pltpu.VMEM((nb, tl, D_LRU), F32),
                        pltpu.VMEM((nb, tl, D_LRU), F32),
                        pltpu.VMEM((8, D_LRU), F32)],
        compiler_params=_cparams(("arbitrary",)),
        name="lru_bwd" if reverse else "lru_fwd")(*ins)


def _block_diag(w):
    h, d, _ = w.shape
    eye = jnp.eye(h, dtype=w.dtype)
    return (eye[:, None, :, None] * w[:, :, None, :]).reshape(h * d, h * d)


def _lru_params(w_a, b_a, w_x, b_x, lam):
    bd = jax.vmap(_block_diag)
    wg = jnp.concatenate([bd(w_a), bd(w_x)], axis=-1).astype(BF16)
    bg = jnp.concatenate([b_a, b_x], axis=-1)[:, None, :]
    sp = jax.nn.softplus(-lam.astype(F32))[:, None, :]
    return wg, bg, sp


def _lru_mixer(zc, zg, batch, seq, conv_w, conv_b, wg, bg, sp, gain):
    zc3 = zc.reshape(batch, seq, D_LRU)
    zg3 = zg.reshape(batch, seq, D_LRU)
    outs = None
    for d in (0, 1):
        params = [conv_w, conv_b[None, :], wg[d], bg[d], sp[d]]
        extra = None if d == 0 else (outs, zg3, gain[None, :])
        outs = _lru_pass(d == 1, zc3, params, extra)
    return outs.reshape(batch * seq, D_LRU)


def _mixout_body(x_ref, oa_ref, y3_ref, u3_ref, oc_ref, sd_ref, wglu_ref, bglu_ref, gb_ref,
                 wout_ref, g2_ref, rw_ref, rb_ref, xo_ref, h_ref, ti_ref, gt_ref):
    tm = x_ref.shape[0]
    per_step = tm // S5_CHUNK
    c0 = pl.program_id(1) * per_step
    yt = jnp.concatenate([y3_ref[:, c0 + n, :] for n in range(per_step)], axis=1)
    ut = jnp.concatenate([u3_ref[:, c0 + n, :] for n in range(per_step)], axis=1)
    vt = _gelu(yt + sd_ref[...] * ut).astype(BF16)
    gl = lax.dot_general(vt, wglu_ref[...], (((0,), (0,)), ((), ())), preferred_element_type=F32) + bglu_ref[...]
    ob = _rms(gl[:, :D_S5] * _sigmoid(gl[:, D_S5:]), gb_ref[...])
    o = jnp.concatenate([oa_ref[...], ob, oc_ref[...]], axis=1).astype(BF16)
    x = x_ref[...] + jnp.dot(o, wout_ref[...], preferred_element_type=F32)
    xo_ref[...] = x
    h = _rms(x, g2_ref[...])
    h_hi = h.astype(BF16)
    bits = pltpu.bitcast(h_hi.astype(F32), jnp.uint32)
    h_ref[...] = lax.shift_right_logical(bits[:, :D_MODEL // 2], jnp.uint32(16)) | (
        bits[:, D_MODEL // 2:] & jnp.uint32(0xFFFF0000))
    h_lo = (h - h_hi.astype(F32)).astype(BF16)
    nt_dims = (((1,), (1,)), ((), ()))
    la = lax.dot_general(rw_ref[...], h_hi, nt_dims, preferred_element_type=F32)
    lb = lax.dot_general(rw_ref[0:N_EXPERTS, :], h_lo, nt_dims, preferred_element_type=F32)
    logits = la[:N_EXPERTS] + la[N_EXPERTS:] + lb + rb_ref[...]
    eidx = lax.broadcasted_iota(jnp.int32, (N_EXPERTS, tm), 0).astype(F32)
    idx_rows, val_rows = [], []
    for k in range(TOP_K):
        m = jnp.max(logits, axis=0, keepdims=True)
        idx = jnp.min(jnp.where(logits == m, eidx, float(N_EXPERTS)), axis=0, keepdims=True)
        idx_rows.append(idx)
        val_rows.append(jnp.exp(m - val_rows[0]) if k else m)
        logits = jnp.where(eidx == idx, -jnp.inf, logits)
    top = val_rows[0]
    vals = jnp.concatenate([jnp.ones_like(top)] + [v for v in val_rows[1:]], axis=0)
    ti_ref[...] = jnp.concatenate(idx_rows, axis=0).astype(jnp.int32)
    gt_ref[...] = vals / jnp.sum(vals, axis=0, keepdims=True)


def _router_split(rw):
    rw_hi = rw.astype(BF16)
    rw_lo = (rw - rw_hi.astype(F32)).astype(BF16)
    return jnp.concatenate([jnp.swapaxes(rw_hi, -1, -2), jnp.swapaxes(rw_lo, -1, -2)], axis=-2)


def _mixout(x, oa, yt3, zbt3, oc, s5_d, wglu, bglu, gain_b, wout, g2, rw2, rb):
    t = x.shape[0]
    tm = ROW_TILE
    nc = tm * ROW_SPLIT // S5_CHUNK
    row = lambda n: pl.BlockSpec((tm, n), lambda i, j: (i * ROW_SPLIT + j, 0))
    col = pl.BlockSpec((TOP_K, tm), lambda i, j: (0, i * ROW_SPLIT + j))
    chunks = pl.BlockSpec((D_S5, nc, S5_CHUNK), lambda i, j: (0, i, 0))
    full = lambda a: pl.BlockSpec(a.shape, lambda i, j: (0,) * a.ndim)
    params = [s5_d.reshape(D_S5, 1), wglu, bglu, gain_b, wout, g2, rw2, rb.reshape(N_EXPERTS, 1)]
    return pl.pallas_call(
        _mixout_body,
        grid=(t // (tm * ROW_SPLIT), ROW_SPLIT),
        in_specs=[row(D_MODEL), row(D_POOL), chunks, chunks, row(D_LRU)] + [full(a) for a in params],
        out_specs=[row(D_MODEL), row(D_MODEL // 2), col, col],
        out_shape=[jax.ShapeDtypeStruct((t, D_MODEL), F32), jax.ShapeDtypeStruct((t, D_MODEL // 2), jnp.uint32),
                   jax.ShapeDtypeStruct((TOP_K, t), jnp.int32), jax.ShapeDtypeStruct((TOP_K, t), F32)],
        compiler_params=_cparams(("parallel", "arbitrary")), name="mix_out")(x, oa, yt3, zbt3, oc, *params)


def _expert_body(be_ref, nu_ref, first_ref, xl_ref, xh_ref, wgu_ref, bg_ref, bu_ref, wd_ref, bd_ref, p_ref, xs_ref,
                 y_ref, wg_s, wu_s, wd_s, ws_s):
    i = pl.program_id(0)
    live = i < nu_ref[0]
    x_scale = xs_ref[...]
    act_scale = FP8_TARGET / ((SWIGLU_LIMIT + 1.0) * SWIGLU_LIMIT)

    @pl.when(jnp.logical_and(live, first_ref[i] == 1))
    def _():
        p = p_ref[...]
        half = MXU_TILE // 2
        n_groups = wgu_ref.shape[3] // MXU_TILE
        amax = functools.reduce(jnp.maximum, [
            jnp.max(jnp.abs(wgu_ref[0, 0, :, MXU_TILE * j:MXU_TILE * (j + 1)]), axis=(0, 1), keepdims=True)
            for j in range(n_groups)])
        s_gu = FP8_TARGET / jnp.maximum(amax, FP8_TINY)
        for j in range(n_groups):
            w = wgu_ref[0, 0, :, MXU_TILE * j:MXU_TILE * (j + 1)].astype(BF16)
            r = jnp.dot(w, p, preferred_element_type=F32) * s_gu
            wg_s[:, half * j:half * (j + 1)] = r[:, :half].astype(FP8)
            wu_s[:, half * j:half * (j + 1)] = r[:, half:].astype(FP8)
        wd = wd_ref[0, 0]
        dmax = jnp.max(jnp.abs(wd), axis=(0, 1), keepdims=True)
        s_d = FP8_TARGET / jnp.maximum(dmax, FP8_TINY)
        wd_s[...] = (wd * s_d).astype(FP8)
        ws_s[0:1, :] = jnp.broadcast_to(1.0 / (s_gu * x_scale), (1, ws_s.shape[1]))
        ws_s[1:2, :] = jnp.broadcast_to(1.0 / (s_d * act_scale), (1, ws_s.shape[1]))

    @pl.when(live)
    def _():
        words = jnp.concatenate([xl_ref[...], xh_ref[...]], axis=1)
        lo = pltpu.bitcast(lax.shift_left(words, jnp.uint32(16)), F32)
        hi = pltpu.bitcast(words & jnp.uint32(0xFFFF0000), F32)
        x = (jnp.concatenate([lo, hi], axis=1) * x_scale).astype(FP8)
        gu_inv = ws_s[0:1, 0:1]
        g = jnp.dot(x, wg_s[...], preferred_element_type=F32) * gu_inv + bg_ref[0, 0]
        u = jnp.dot(x, wu_s[...], preferred_element_type=F32) * gu_inv + bu_ref[0, 0]
        g = jnp.minimum(g, SWIGLU_LIMIT)
        u = jnp.clip(u, -SWIGLU_LIMIT, SWIGLU_LIMIT)
        act = ((u + 1.0) * (g * _sigmoid(g * SWIGLU_ALPHA)) * act_scale).astype(FP8)
        y = jnp.dot(act, wd_s[...], preferred_element_type=F32) * ws_s[1:2, 0:1] + bd_ref[0, 0]
        y_ref[...] = y.astype(y_ref.dtype)

    @pl.when(jnp.logical_not(live))
    def _():
        y_ref[...] = jnp.zeros_like(y_ref)


def _expert_mlp(layer, xs, x_bound, blk_e, n_used, w_gate_up, b_gate, b_up, w_down, b_down):
    n_rows, dh = xs[0].shape
    bm = MOE_ROWS
    x_scale = (FP8_TARGET / jnp.maximum(x_bound, FP8_TINY)).astype(F32).reshape(1, 1)
    first = jnp.concatenate([jnp.ones((1,), jnp.int32), (blk_e[1:] != blk_e[:-1]).astype(jnp.int32)])
    idx = jnp.arange(MXU_TILE)
    perm = jax.nn.one_hot(jnp.where(idx % 2 == 0, idx // 2, MXU_TILE // 2 + idx // 2), MXU_TILE, dtype=BF16)
    rows = pl.BlockSpec((bm, dh), lambda i, be, nu, fi: (jnp.minimum(i, nu[0] - 1), 0))
    per_e = lambda a: pl.BlockSpec((1, 1) + a.shape[2:], lambda i, be, nu, fi: (layer, be[i], 0, 0))
    grid_spec = pltpu.PrefetchScalarGridSpec(
        num_scalar_prefetch=3, grid=(n_rows // bm,),
        in_specs=[rows, rows, per_e(w_gate_up), per_e(b_gate), per_e(b_up), per_e(w_down), per_e(b_down),
                  pl.BlockSpec(perm.shape, lambda i, be, nu, fi: (0, 0)),
                  pl.BlockSpec((1, 1), lambda i, be, nu, fi: (0, 0))],
        out_specs=pl.BlockSpec((bm, D_MODEL), lambda i, be, nu, fi: (i, 0)),
        scratch_shapes=[pltpu.VMEM((D_MODEL, D_FF), FP8), pltpu.VMEM((D_MODEL, D_FF), FP8),
                        pltpu.VMEM((D_FF, D_MODEL), FP8), pltpu.VMEM((8, 128), F32)])
    return pl.pallas_call(
        _expert_body, grid_spec=grid_spec,
        out_shape=jax.ShapeDtypeStruct((n_rows, D_MODEL), BF16),
        compiler_params=pltpu.CompilerParams(dimension_semantics=("arbitrary",),
                                             vmem_limit_bytes=EXPERT_VMEM_LIMIT),
        name="expert_mlp")(blk_e, n_used, first, *xs, w_gate_up, b_gate, b_up, w_down, b_down, perm, x_scale)


def _sc_dispatch(h, dest, n_rows):
    t, d = h.shape
    win = SC_WINDOW
    dh = d // SC_COL_SPLIT
    idx = [dest[:, k].reshape(1, t) for k in range(TOP_K)]
    mesh = plsc.VectorSubcoreMesh(core_axis_name="core", subcore_axis_name="subcore")

    def scatter_cols(j):
        @functools.partial(pl.kernel, out_type=jax.ShapeDtypeStruct((n_rows, dh), h.dtype), mesh=mesh,
                           scratch_types=[], name="sc_dispatch")
        def scatter_rows(x_hbm, i0_hbm, i1_hbm, i2_hbm, i3_hbm, o_hbm):
            def body(x_vmem, *idx_vmem):
                for iv in idx_vmem:
                    pltpu.sync_copy(x_vmem, o_hbm.at[iv.at[0]])

            pltpu.emit_pipeline(
                body, grid=(t // win,),
                in_specs=[pl.BlockSpec((win, dh), lambda i: (i, j))]
                         + [pl.BlockSpec((1, win), lambda i: (0, i))] * TOP_K,
                out_specs=[],
                core_axis_name=("core", "subcore"),
                dimension_semantics=(pltpu.PARALLEL,),
            )(x_hbm, i0_hbm, i1_hbm, i2_hbm, i3_hbm)

        return scatter_rows(h, *idx)

    return [scatter_cols(j) for j in range(SC_COL_SPLIT)]


def _moe_dispatch(top_i, t):
    bm = MOE_ROWS
    n_blocks = (t * TOP_K) // bm + N_EXPERTS
    onehot = jax.nn.one_hot(top_i, N_EXPERTS, dtype=jnp.int32)
    sel = jnp.sum(onehot, axis=1)
    csum = jnp.cumsum(sel, axis=0)
    counts = csum[-1]
    padded = ((counts + bm - 1) // bm) * bm
    pad_end = jnp.cumsum(padded)
    pad_start = pad_end - padded
    base = (csum - sel) + pad_start[None, :]
    dest = jnp.sum(onehot * base[:, None, :], axis=-1)
    blk_start = jnp.arange(n_blocks, dtype=jnp.int32) * bm
    blk_e = jnp.minimum(jnp.sum((pad_end[None, :] <= blk_start[:, None]).astype(jnp.int32), axis=1),
                        N_EXPERTS - 1)
    n_used = (pad_end[-1] // bm).astype(jnp.int32).reshape(1)
    return dest, n_blocks * bm, blk_e, n_used


def _final_body(x_ref, y0, y1, y2, y3, gt_ref, g_ref, *rest):
    o_ref = rest[-1]
    o_ref[...] = _rms(_combine(x_ref, (y0, y1, y2, y3), gt_ref), g_ref[...])


def _final(x, add, g):
    t = x.shape[0]
    tm = ROW_TILE
    parts, gates = add
    per_part = t // tm // len(parts)
    out = None
    for p, ys in enumerate(parts):
        off = p * per_part
        row_p = lambda n, off=off: pl.BlockSpec((tm, n), lambda i: (i + off, 0))
        row_y = pl.BlockSpec((tm, D_MODEL), lambda i: (i, 0))
        prev = [] if out is None else [out]
        out = pl.pallas_call(
            _final_body, grid=(per_part,),
            in_specs=[row_p(D_MODEL)] + [row_y] * TOP_K + [row_p(TOP_K), pl.BlockSpec(g.shape, lambda i: (0, 0))]
                     + [pl.BlockSpec(memory_space=pl.ANY)] * len(prev),
            out_specs=row_p(D_MODEL), out_shape=jax.ShapeDtypeStruct((t, D_MODEL), F32),
            input_output_aliases={2 + TOP_K + 1: 0} if prev else {},
            compiler_params=_cparams(("parallel",)), name="final_norm")(x, *ys, gates, g, *prev)
    return out


def kernel(x, norm1_g, w_in, pool_w, pool_scale, s5_lam_re, s5_lam_im, s5_log_step, s5_b_re, s5_b_im, s5_c_re, s5_c_im, s5_d, s5_w_glu, s5_b_glu, lru_conv_w, lru_conv_b, lru_w_a, lru_b_a, lru_w_x, lru_b_x, lru_lam, mix_gain, w_out, norm2_g, router_w, router_b, w_gate_up, b_gate_up, w_down, b_down, final_g):
    batch, seq, d = x.shape
    t = batch * seq
    depth = norm1_g.shape[0]
    xt = x.reshape(t, d)
    add = None
    o1, o2 = D_POOL, D_POOL + D_S5
    b_gate = b_gate_up[:, :, None, 0::2]
    b_up = b_gate_up[:, :, None, 1::2]
    b_dn = b_down[:, :, None, :]
    for l in range(depth):
        w_rest = jnp.concatenate([w_in[l][:, :o1], w_in[l][:, o2:]], axis=1).astype(BF16)
        w_s5t = w_in[l][:, o1:o2].T.astype(BF16)
        s5_tabs = _s5_tables(s5_lam_re[l], s5_lam_im[l], s5_log_step[l], s5_b_re[l], s5_b_im[l],
                             s5_c_re[l], s5_c_im[l], seq // S5_CHUNK)
        lru_wg, lru_bg, lru_sp = _lru_params(lru_w_a[l], lru_b_a[l], lru_w_x[l], lru_b_x[l], lru_lam[l])
        if add is None:
            za, zbt, zc, zg = _inproj(xt, None, norm1_g[l][None, :], w_rest, w_s5t)
        else:
            xt, za, zbt, zc, zg = _inproj(xt, add, norm1_g[l][None, :], w_rest, w_s5t)
        oa = _pool_mixer(za, batch, seq, _block_diag(pool_w[l]).astype(BF16), pool_scale[l][None, :],
                         mix_gain[l][None, :o1])
        yt = _s5_core(zbt, batch, seq, s5_tabs)
        oc = _lru_mixer(zc, zg, batch, seq, lru_conv_w[l], lru_conv_b[l], lru_wg, lru_bg, lru_sp,
                        mix_gain[l][o2:])
        xt, h2, top_i, gates = _mixout(
            xt, oa, yt, zbt, oc, s5_d[l], s5_w_glu[l].astype(BF16), s5_b_glu[l][None, :],
            mix_gain[l][None, o1:o2], w_out[l].astype(BF16), norm2_g[l][None, :], _router_split(router_w[l]),
            router_b[l])
        top_i, gates = top_i.T, gates.T
        dest, n_rows, blk_e, n_used = _moe_dispatch(top_i, t)
        xs = _sc_dispatch(h2, dest, n_rows)
        x_bound = math.sqrt(D_MODEL) * jnp.max(jnp.abs(norm2_g[l]))
        ys = _expert_mlp(l, xs, x_bound, blk_e, n_used, w_gate_up, b_gate, b_up, w_down, b_dn)
        tp = t // TOKEN_PARTS
        add = ([[ys.at[dest[p * tp:(p + 1) * tp, k]].get(mode="promise_in_bounds") for k in range(TOP_K)]
                for p in range(TOKEN_PARTS)], gates)
    out = _final(xt, add, final_g[None, :])
    return out.reshape(batch, seq, d)
```

```python
import functools
import math

import jax
import jax.numpy as jnp
from jax import lax
from jax.experimental import pallas as pl
from jax.experimental.pallas import tpu as pltpu
from jax.experimental.pallas import tpu_sc as plsc

F32 = jnp.float32
BF16 = jnp.bfloat16
FP8 = jnp.float8_e4m3fn
FP8_TARGET = 416.0
FP8_TINY = 1e-30

D_MODEL = 1024
D_POOL = 256
D_S5 = 384
D_LRU = 384
POOL_WINDOWS = (2, 4, 8, 16)
POOL_GROUP = 64
S5_GROUP = 16
S5_NGROUPS = 24
S5_STATE = 64
LRU_HEADS = 6
LRU_HEAD_DIM = 64
RG_C = 8.0
N_EXPERTS = 32
TOP_K = 4
D_FF = 1024
SWIGLU_LIMIT = 7.0
SWIGLU_ALPHA = 1.702
EPS = 1e-5

ROW_TILE = 512
ROW_SPLIT = 2
S5_CHUNK = 128
POOL_TILE = 512
LRU_TILE = 256
HALO = 8
MOE_ROWS = 512
TOKEN_PARTS = 1
SC_WINDOW = 128
SC_COL_SPLIT = 2
MXU_TILE = 256
EXPERT_VMEM_LIMIT = 56 * 1024 * 1024
VMEM_LIMIT = 48 * 1024 * 1024


def _cparams(sem):
    return pltpu.CompilerParams(dimension_semantics=sem, vmem_limit_bytes=VMEM_LIMIT)


def _rms(x, g):
    return x * lax.rsqrt(jnp.mean(x * x, axis=-1, keepdims=True) + EPS) * g


def _gelu(x):
    return 0.5 * x * (1.0 + jnp.tanh(0.7978845608028654 * (x + 0.044715 * (x * x * x))))


def _sigmoid(x):
    return 1.0 / (1.0 + jnp.exp(-x))


def _combine(x_ref, y_refs, gt_ref):
    x = x_ref[...]
    gt = gt_ref[...]
    for k in range(TOP_K):
        x = x + gt[:, k:k + 1] * y_refs[k][...].astype(F32)
    return x


def _inproj_body(has_add, *refs):
    if has_add:
        x_ref, y0, y1, y2, y3, gt_ref, g_ref, w_ref, wbt_ref = refs[:9]
        xo_ref, za, zbt, zc, zg = refs[-5:]
        x = _combine(x_ref, (y0, y1, y2, y3), gt_ref)
        xo_ref[...] = x
    else:
        x_ref, g_ref, w_ref, wbt_ref, za, zbt, zc, zg = refs
        x = x_ref[...]
    h = _rms(x, g_ref[...]).astype(BF16)
    z = jnp.dot(h, w_ref[...], preferred_element_type=F32)
    za[...] = z[:, :D_POOL]
    zc[...] = z[:, D_POOL:D_POOL + D_LRU]
    zg[...] = z[:, D_POOL + D_LRU:]
    zt = lax.dot_general(wbt_ref[...], h, (((1,), (1,)), ((), ())), preferred_element_type=F32)
    per_step = zt.shape[1] // S5_CHUNK
    for n in range(per_step):
        zbt[:, pl.program_id(1) * per_step + n, :] = zt[:, n * S5_CHUNK:(n + 1) * S5_CHUNK]


def _inproj(x, add, g, w_rest, w_s5t):
    t = x.shape[0]
    tm = ROW_TILE
    nc = tm * ROW_SPLIT // S5_CHUNK
    row = lambda n: pl.BlockSpec((tm, n), lambda i, j: (i * ROW_SPLIT + j, 0))
    full = lambda a: pl.BlockSpec(a.shape, lambda i, j: (0,) * a.ndim)
    z_shapes = [jax.ShapeDtypeStruct((t, D_POOL), F32), jax.ShapeDtypeStruct((D_S5, t // S5_CHUNK, S5_CHUNK), F32),
                jax.ShapeDtypeStruct((t, D_LRU), F32), jax.ShapeDtypeStruct((t, D_LRU), F32)]
    z_specs = [row(D_POOL), pl.BlockSpec((D_S5, nc, S5_CHUNK), lambda i, j: (0, i, 0)), row(D_LRU), row(D_LRU)]
    n_outer = t // (tm * ROW_SPLIT)
    if add is None:
        return pl.pallas_call(
            functools.partial(_inproj_body, False),
            grid=(n_outer, ROW_SPLIT), in_specs=[row(D_MODEL), full(g), full(w_rest), full(w_s5t)],
            out_specs=z_specs, out_shape=z_shapes,
            compiler_params=_cparams(("parallel", "arbitrary")), name="inproj")(x, g, w_rest, w_s5t)
    parts, gates = add
    per_part = n_outer // len(parts)
    out_shape = [jax.ShapeDtypeStruct((t, D_MODEL), F32)] + z_shapes
    outs = []
    for p, ys in enumerate(parts):
        off = p * per_part
        row_p = lambda n, off=off: pl.BlockSpec((tm, n), lambda i, j: ((i + off) * ROW_SPLIT + j, 0))
        row_y = pl.BlockSpec((tm, D_MODEL), lambda i, j: (i * ROW_SPLIT + j, 0))
        zbt_p = pl.BlockSpec((D_S5, nc, S5_CHUNK), lambda i, j, off=off: (0, i + off, 0))
        n_in = 1 + TOP_K + 4
        outs = pl.pallas_call(
            functools.partial(_inproj_body, True),
            grid=(per_part, ROW_SPLIT),
            in_specs=[row_p(D_MODEL)] + [row_y] * TOP_K + [row_p(TOP_K), full(g), full(w_rest), full(w_s5t)]
                     + [pl.BlockSpec(memory_space=pl.ANY)] * len(outs),
            out_specs=[row_p(D_MODEL), row_p(D_POOL), zbt_p, row_p(D_LRU), row_p(D_LRU)],
            out_shape=out_shape,
            input_output_aliases={n_in + k: k for k in range(len(outs))},
            compiler_params=_cparams(("parallel", "arbitrary")), name="inproj")(
                x, *ys, gates, g, w_rest, w_s5t, *outs)
    return outs


def _pool_body(seq, prev_ref, cur_ref, next_ref, w_ref, sc_ref, gain_ref, o_ref, u_s, s2_s, s4_s, s8_s):
    tl = cur_ref.shape[0]
    i = pl.program_id(1)
    nt = pl.num_programs(1)
    zero8 = jnp.zeros((HALO, D_POOL), F32)
    for buf in (u_s, s2_s, s4_s, s8_s):
        buf[0:HALO, :] = zero8
        buf[tl + 3 * HALO:tl + 4 * HALO, :] = zero8
    u_s[HALO:2 * HALO, :] = jnp.where(i > 0, prev_ref[...], 0.0)
    u_s[2 * HALO:2 * HALO + tl, :] = cur_ref[...]
    u_s[2 * HALO + tl:3 * HALO + tl, :] = jnp.where(i < nt - 1, next_ref[...], 0.0)
    r = tl + 2 * HALO
    s2_s[HALO:HALO + r, :] = u_s[HALO - 1:HALO - 1 + r, :] + u_s[HALO:HALO + r, :]
    s4_s[HALO:HALO + r, :] = s2_s[HALO - 1:HALO - 1 + r, :] + s2_s[HALO + 1:HALO + 1 + r, :]
    s8_s[HALO:HALO + r, :] = s4_s[HALO - 2:HALO - 2 + r, :] + s4_s[HALO + 2:HALO + 2 + r, :]
    o = 2 * HALO
    s16 = s8_s[o - 4:o - 4 + tl, :] + s8_s[o + 4:o + 4 + tl, :]
    s8 = s8_s[o:o + tl, :]
    s4 = s4_s[o:o + tl, :]
    s2 = s2_s[o:o + tl, :]
    u = u_s[o:o + tl, :]
    lane = lax.broadcasted_iota(jnp.int32, (tl, D_POOL), 1)
    tpos = lax.broadcasted_iota(jnp.int32, (tl, D_POOL), 0) + i * tl
    g0, g1, g2 = lane < POOL_GROUP, lane < 2 * POOL_GROUP, lane < 3 * POOL_GROUP
    half = jnp.where(g0, 1, jnp.where(g1, 2, jnp.where(g2, 4, 8)))
    wsum = jnp.where(g0, s2, jnp.where(g1, s4, jnp.where(g2, s8, s16)))
    cnt = (jnp.minimum(tpos + half, seq) - jnp.maximum(tpos - half, 0)).astype(F32)
    d = wsum / cnt - u
    y = jnp.dot(d.astype(BF16), w_ref[...], preferred_element_type=F32) * sc_ref[...]
    o_ref[...] = _rms(y, gain_ref[...])


def _pool_mixer(za, batch, seq, w_bd, scale, gain):
    t = za.shape[0]
    tl = POOL_TILE
    nt = seq // tl
    hb = tl // HALO
    nhb = t // HALO
    cur = pl.BlockSpec((tl, D_POOL), lambda b, i: (b * nt + i, 0))
    prev = pl.BlockSpec((HALO, D_POOL), lambda b, i: (jnp.maximum((b * nt + i) * hb - 1, 0), 0))
    nxt = pl.BlockSpec((HALO, D_POOL), lambda b, i: (jnp.minimum((b * nt + i + 1) * hb, nhb - 1), 0))
    full = lambda a: pl.BlockSpec(a.shape, lambda b, i: (0,) * a.ndim)
    return pl.pallas_call(
        functools.partial(_pool_body, seq),
        grid=(batch, nt),
        in_specs=[prev, cur, nxt, full(w_bd), full(scale), full(gain)],
        out_specs=cur,
        out_shape=jax.ShapeDtypeStruct((t, D_POOL), F32),
        scratch_shapes=[pltpu.VMEM((tl + 4 * HALO, D_POOL), F32)] * 4,
        compiler_params=_cparams(("parallel", "parallel")), name="pool_mixer")(za, za, za, w_bd, scale, gain)


def _s5_scan_steps(n_chunks):
    return max(1, int(math.ceil(math.log2(n_chunks))))


def _s5_tables(lam_re, lam_im, log_step, b_re, b_im, c_re, c_im, n_chunks):
    L, C, P, G = S5_CHUNK, S5_GROUP, S5_STATE, S5_NGROUPS
    hp = lax.Precision.HIGHEST
    lr = lam_re.astype(F32)
    li = lam_im.astype(F32)
    dt = jnp.exp(log_step.astype(F32))[..., None]
    ar = lr * dt
    ai = li * dt
    jj = jnp.arange(L + 1, dtype=F32)

    def powers(j, a_r, a_i):
        mag = jnp.exp(j * a_r)
        return mag * jnp.cos(j * a_i), mag * jnp.sin(j * a_i)

    e_r, e_i = powers(jj[None, None, :, None], ar[:, :, None, :], ai[:, :, None, :])
    et_r, et_i = powers(jj[None, None, None, :], ar[..., None], ai[..., None])
    den = lr * lr + li * li
    nr = e_r[:, :, 1, :] - 1.0
    ni = e_i[:, :, 1, :]
    q_r = (nr * lr + ni * li) / den
    q_i = (ni * lr - nr * li) / den
    br = b_re.astype(F32)
    bi = b_im.astype(F32)
    bbt_r = (q_r[..., None] * br - q_i[..., None] * bi).transpose(0, 1, 3, 2)
    bbt_i = (q_r[..., None] * bi + q_i[..., None] * br).transpose(0, 1, 3, 2)
    cr = c_re.astype(F32)
    ci = c_im.astype(F32)
    m_r = cr[:, :, None] * bbt_r[:, :, :, None] - ci[:, :, None] * bbt_i[:, :, :, None]
    m_i = cr[:, :, None] * bbt_i[:, :, :, None] + ci[:, :, None] * bbt_r[:, :, :, None]
    kk = (jnp.einsum('dgxcp,dgpj->dgxcj', m_r, et_r[..., :L], precision=hp)
          - jnp.einsum('dgxcp,dgpj->dgxcj', m_i, et_i[..., :L], precision=hp))
    kf, kb = kk[0], kk[1]
    lagtab = jnp.concatenate([kb[..., :0:-1], kf[..., :1] + kb[..., :1], kf[..., 1:],
                              jnp.zeros_like(kf[..., :1])], axis=-1)
    bits = lax.bitcast_convert_type(lagtab.astype(BF16).astype(F32), jnp.uint32).reshape(G, C * C, 2 * L)
    lagtab = lax.shift_right_logical(bits[..., :L], jnp.uint32(16)) | (bits[..., L:] & jnp.uint32(0xFFFF0000))
    fb = lambda f, b, axis: jnp.concatenate([f, b], axis=axis)
    pw_s = jnp.stack([fb(e_r[0, :, L - 1::-1], e_r[1, :, :L], -1),
                      fb(e_i[0, :, L - 1::-1], e_i[1, :, :L], -1)], axis=1)
    bb = jnp.stack([fb(bbt_r[0], bbt_r[1], -1), fb(bbt_i[0], bbt_i[1], -1)], axis=1)
    pw_t = jnp.stack([fb(et_r[0, :, :, 1:L + 1], et_r[1, :, :, L:0:-1], 1),
                      fb(et_i[0, :, :, 1:L + 1], et_i[1, :, :, L:0:-1], 1)], axis=1)
    crt = cr.transpose(0, 1, 3, 2)
    cit = ci.transpose(0, 1, 3, 2)
    cc = jnp.stack([fb(crt[0], crt[1], 1), fb(cit[0], cit[1], 1)], axis=1)
    n_steps = _s5_scan_steps(n_chunks)
    kpow =(L * (2 ** jnp.arange(n_steps, dtype=F32)))[None, None, :, None]
    p_r, p_i = powers(kpow, ar[:, :, None, :], ai[:, :, None, :])
    pad = jnp.zeros((G, (-n_steps) % 8, 2 * P), F32)
    p_r = jnp.concatenate([jnp.concatenate([p_r[0], p_r[1]], axis=-1), pad], axis=1)
    p_i = jnp.concatenate([jnp.concatenate([p_i[0], p_i[1]], axis=-1), pad], axis=1)
    ptab = jnp.concatenate([p_r, p_i], axis=1)
    return lagtab, pw_s, bb, pw_t, cc, ptab


def _s5_body(n_chunks, n_steps, u_ref, lag_ref, pws_ref, bb_ref, pwt_ref, cc_ref, p_ref, y_ref,
             toep_s, w_s, v_s):
    L, C = S5_CHUNK, S5_GROUP
    n = u_ref.shape[1]
    two_p = 2 * S5_STATE
    e_r, e_i = pws_ref[0, 0], pws_ref[0, 1]
    et_r, et_i = pwt_ref[0, 0], pwt_ref[0, 1]
    c_r, c_i = cc_ref[0, 0], cc_ref[0, 1]
    for c in range(C):
        b_r = bb_ref[0, 0, c:c + 1, :]
        b_i = bb_ref[0, 1, c:c + 1, :]
        w_s[c * L:(c + 1) * L, :two_p] = (e_r * b_r - e_i * b_i).astype(BF16)
        w_s[c * L:(c + 1) * L, two_p:] = (e_r * b_i + e_i * b_r).astype(BF16)
        k_r = c_r[:, c:c + 1]
        k_i = c_i[:, c:c + 1]
        v_s[:two_p, c * L:(c + 1) * L] = (k_r * et_r - k_i * et_i).astype(BF16)
        v_s[two_p:, c * L:(c + 1) * L] = (-(k_r * et_i + k_i * et_r)).astype(BF16)
    s_i = lax.broadcasted_iota(jnp.int32, (L, L), 0)
    t_i = lax.broadcasted_iota(jnp.int32, (L, L), 1)
    non_positive_lag = t_i <= s_i

    def build_block_row(cp, carry):
        r0 = pl.multiple_of(cp * L, L)
        for c in range(C):
            k = lag_ref[0, pl.ds(cp * C + c, 1), :]
            r = pltpu.roll(jnp.broadcast_to(k, (L, L)), 1, 1, stride=1, stride_axis=0)
            bits = jnp.where(non_positive_lag, lax.shift_left(r, jnp.uint32(16)), r & jnp.uint32(0xFFFF0000))
            toep_s[pl.ds(r0, L), c * L:(c + 1) * L] = pltpu.bitcast(bits, F32).astype(BF16)
        return carry

    lax.fori_loop(0, C, build_block_row, 0)
    u = jnp.concatenate([u_ref[c] for c in range(C)], axis=1).astype(BF16)
    y = jnp.dot(u, toep_s[...], preferred_element_type=F32)
    s = jnp.dot(u, w_s[...], preferred_element_type=F32)
    xr = s[:, :two_p]
    xi = s[:, two_p:]
    row = lax.broadcasted_iota(jnp.int32, (n, two_p), 0) % n_chunks
    is_fwd = lax.broadcasted_iota(jnp.int32, (n, two_p), 1) < S5_STATE
    pim0 = p_ref.shape[1] // 2

    def shifted(a, k):
        down = jnp.where(row >= k, pltpu.roll(a, k, 0), 0.0)
        up = jnp.where(row < n_chunks - k, pltpu.roll(a, n - k, 0), 0.0)
        return jnp.where(is_fwd, down, up)

    for j in range(n_steps):
        k = 1 << j
        pr = p_ref[0, j:j + 1, :]
        pi = p_ref[0, pim0 + j:pim0 + j + 1, :]
        sr = shifted(xr, k)
        si = shifted(xi, k)
        xr, xi = xr + pr * sr - pi * si, xi + pr * si + pi * sr
    carry = jnp.concatenate([shifted(xr, 1), shifted(xi, 1)], axis=1).astype(BF16)
    y = y + jnp.dot(carry, v_s[...], preferred_element_type=F32)
    for c in range(C):
        y_ref[c] = y[:, c * L:(c + 1) * L]


def _s5_core(zbt, batch, seq, tables):
    L = S5_CHUNK
    n_chunks = seq // L
    n_steps = _s5_scan_steps(n_chunks)
    n = batch * n_chunks
    per_g = lambda a: pl.BlockSpec((1,) + a.shape[1:], lambda g: (g,) + (0,) * (a.ndim - 1))
    grp = pl.BlockSpec((S5_GROUP, n, L), lambda g: (g, 0, 0))
    return pl.pallas_call(
        functools.partial(_s5_body, n_chunks, n_steps),
        grid=(S5_NGROUPS,),
        in_specs=[grp] + [per_g(a) for a in tables],
        out_specs=grp,
        out_shape=jax.ShapeDtypeStruct((D_S5, n, L), F32),
        scratch_shapes=[pltpu.VMEM((S5_GROUP * L, S5_GROUP * L), BF16),
                        pltpu.VMEM((S5_GROUP * L, 4 * S5_STATE), BF16),
                        pltpu.VMEM((4 * S5_STATE, S5_GROUP * L), BF16)],
        compiler_params=_cparams(("parallel",)), name="s5_core")(zbt, *tables)


def _lru_body(reverse, prev_ref, cur_ref, next_ref, cw_ref, cb_ref, wg_ref, bg_ref, sp_ref, *rest):
    if reverse:
        hf_ref, gate_ref, gain_ref, o_ref, a_s, b_s, h_s, carry_s = rest
    else:
        o_ref, a_s, b_s, h_s, carry_s = rest
    nb, tl, _ = cur_ref.shape
    step = pl.program_id(0)
    nt = pl.num_programs(0)
    ti = nt - 1 - step if reverse else step

    @pl.when(step == 0)
    def _():
        carry_s[...] = jnp.zeros_like(carry_s)

    cw = cw_ref[...]
    k_row = sp_ref[...] * (-0.5 * RG_C * math.log2(math.e))
    sub = lax.broadcasted_iota(jnp.int32, (HALO, D_LRU), 0)

    def shifted(x, d, edge):
        y = pltpu.roll(x, d % tl, 0)
        if d > 0:
            return jnp.concatenate([jnp.where(sub < d, pltpu.roll(edge, d, 0), y[:HALO]), y[HALO:]], axis=0)
        return jnp.concatenate([y[:-HALO], jnp.where(sub >= HALO + d, pltpu.roll(edge, HALO + d, 0), y[-HALO:])],
                               axis=0)

    for b in range(nb):
        x = cur_ref[b]
        before = jnp.where(ti > 0, prev_ref[b], 0.0)
        after = jnp.where(ti < nt - 1, next_ref[b], 0.0)
        xc = (cb_ref[...] + cw[0:1, :] * shifted(x, 1, before) + cw[1:2, :] * x
              + cw[2:3, :] * shifted(x, -1, after) + cw[3:4, :] * shifted(x, -2, after))
        gates = jnp.dot(xc.astype(BF16), wg_ref[...], preferred_element_type=F32) + bg_ref[...]
        a = jnp.exp2(k_row * (jnp.tanh(0.5 * gates[:, :D_LRU]) + 1.0))
        ig = 0.5 * jnp.tanh(0.5 * gates[:, D_LRU:]) + 0.5
        a_s[b] = a
        om = 1.0 - a * a
        b_s[b] = (om * lax.rsqrt(jnp.maximum(om, 1e-30))) * (ig * xc)

    def scan_step(s, hs):
        t = tl - 1 - s if reverse else s
        out = []
        for b in range(nb):
            h = a_s[b, pl.ds(t, 1), :] * hs[b] + b_s[b, pl.ds(t, 1), :]
            h_s[b, pl.ds(t, 1), :] = h
            out.append(h)
        return tuple(out)

    hs = lax.fori_loop(0, tl, scan_step, tuple(carry_s[b:b + 1, :] for b in range(nb)), unroll=8)
    for b in range(nb):
        carry_s[b:b + 1, :] = hs[b]

    if reverse:
        for b in range(nb):
            y = (hf_ref[b] + h_s[b]) * _gelu(gate_ref[b])
            o_ref[b] = _rms(y, gain_ref[...])
    else:
        o_ref[...] = h_s[...]


def _lru_pass(reverse, zc3, params, extra):
    nb, seq, _ = zc3.shape
    tl = LRU_TILE
    nt = seq // tl
    hb = tl // HALO
    nhb = seq // HALO
    tix = (lambda i: nt - 1 - i) if reverse else (lambda i: i)
    cur = pl.BlockSpec((nb, tl, D_LRU), lambda i: (0, tix(i), 0))
    prev = pl.BlockSpec((nb, HALO, D_LRU), lambda i: (0, jnp.maximum(tix(i) * hb - 1, 0), 0))
    nxt = pl.BlockSpec((nb, HALO, D_LRU), lambda i: (0, jnp.minimum((tix(i) + 1) * hb, nhb - 1), 0))
    full = lambda a: pl.BlockSpec(a.shape, lambda i: (0,) * a.ndim)
    ins = [zc3, zc3, zc3, *params]
    in_specs = [prev, cur, nxt] + [full(a) for a in params]
    if reverse:
        hf, gate, gain = extra
        ins += [hf, gate, gain]
        in_specs += [cur, cur, full(gain)]
    return pl.pallas_call(
        functools.partial(_lru_body, reverse),
        grid=(nt,), in_specs=in_specs, out_specs=cur,
        out_shape=jax.ShapeDtypeStruct((nb, seq, D_LRU), F32),
        scratch_shapes=[pltpu.VMEM((nb, tl, D_LRU), F32),
                        pltpu.VMEM((nb, tl, D_LRU), F32),
                        pltpu.VMEM((nb, tl, D_LRU), F32),
                        pltpu.VMEM((8, D_LRU), F32)],
        compiler_params=_cparams(("arbitrary",)),
        name="lru_bwd" if reverse else "lru_fwd")(*ins)


def _block_diag(w):
    h, d, _ = w.shape
    eye = jnp.eye(h, dtype=w.dtype)
    return (eye[:, None, :, None] * w[:, :, None, :]).reshape(h * d, h * d)


def _lru_params(w_a, b_a, w_x, b_x, lam):
    bd = jax.vmap(_block_diag)
    wg = jnp.concatenate([bd(w_a), bd(w_x)], axis=-1).astype(BF16)
    bg = jnp.concatenate([b_a, b_x], axis=-1)[:, None, :]
    sp = jax.nn.softplus(-lam.astype(F32))[:, None, :]
    return wg, bg, sp


def _lru_mixer(zc, zg, batch, seq, conv_w, conv_b, wg, bg, sp, gain):
    zc3 = zc.reshape(batch, seq, D_LRU)
    zg3 = zg.reshape(batch, seq, D_LRU)
    outs = None
    for d in (0, 1):
        params = [conv_w, conv_b[None, :], wg[d], bg[d], sp[d]]
        extra = None if d == 0 else (outs, zg3, gain[None, :])
        outs = _lru_pass(d == 1, zc3, params, extra)
    return outs.reshape(batch * seq, D_LRU)


def _mixout_body(x_ref, oa_ref, y3_ref, u3_ref, oc_ref, sd_ref, wglu_ref, bglu_ref, gb_ref,
                 wout_ref, g2_ref, rw_ref, rb_ref, xo_ref, h_ref, ti_ref, gt_ref):
    tm = x_ref.shape[0]
    per_step = tm // S5_CHUNK
    c0 = pl.program_id(1) * per_step
    yt = jnp.concatenate([y3_ref[:, c0 + n, :] for n in range(per_step)], axis=1)
    ut = jnp.concatenate([u3_ref[:, c0 + n, :] for n in range(per_step)], axis=1)
    vt = _gelu(yt + sd_ref[...] * ut).astype(BF16)
    gl = lax.dot_general(vt, wglu_ref[...], (((0,), (0,)), ((), ())), preferred_element_type=F32) + bglu_ref[...]
    ob = _rms(gl[:, :D_S5] * _sigmoid(gl[:, D_S5:]), gb_ref[...])
    o = jnp.concatenate([oa_ref[...], ob, oc_ref[...]], axis=1).astype(BF16)
    x = x_ref[...] + jnp.dot(o, wout_ref[...], preferred_element_type=F32)
    xo_ref[...] = x
    h = _rms(x, g2_ref[...])
    h_hi = h.astype(BF16)
    bits = pltpu.bitcast(h_hi.astype(F32), jnp.uint32)
    h_ref[...] = lax.shift_right_logical(bits[:, :D_MODEL // 2], jnp.uint32(16)) | (
        bits[:, D_MODEL // 2:] & jnp.uint32(0xFFFF0000))
    h_lo = (h - h_hi.astype(F32)).astype(BF16)
    nt_dims = (((1,), (1,)), ((), ()))
    la = lax.dot_general(rw_ref[...], h_hi, nt_dims, preferred_element_type=F32)
    lb = lax.dot_general(rw_ref[0:N_EXPERTS, :], h_lo, nt_dims, preferred_element_type=F32)
    logits = la[:N_EXPERTS] + la[N_EXPERTS:] + lb + rb_ref[...]
    eidx = lax.broadcasted_iota(jnp.int32, (N_EXPERTS, tm), 0).astype(F32)
    idx_rows, val_rows = [], []
    for k in range(TOP_K):
        m = jnp.max(logits, axis=0, keepdims=True)
        idx = jnp.min(jnp.where(logits == m, eidx, float(N_EXPERTS)), axis=0, keepdims=True)
        idx_rows.append(idx)
        val_rows.append(jnp.exp(m - val_rows[0]) if k else m)
        logits = jnp.where(eidx == idx, -jnp.inf, logits)
    top = val_rows[0]
    vals = jnp.concatenate([jnp.ones_like(top)] + [v for v in val_rows[1:]], axis=0)
    ti_ref[...] = jnp.concatenate(idx_rows, axis=0).astype(jnp.int32)
    gt_ref[...] = vals / jnp.sum(vals, axis=0, keepdims=True)


def _router_split(rw):
    rw_hi = rw.astype(BF16)
    rw_lo = (rw - rw_hi.astype(F32)).astype(BF16)
    return jnp.concatenate([jnp.swapaxes(rw_hi, -1, -2), jnp.swapaxes(rw_lo, -1, -2)], axis=-2)


def _mixout(x, oa, yt3, zbt3, oc, s5_d, wglu, bglu, gain_b, wout, g2, rw2, rb):
    t = x.shape[0]
    tm = ROW_TILE
    nc = tm * ROW_SPLIT // S5_CHUNK
    row = lambda n: pl.BlockSpec((tm, n), lambda i, j: (i * ROW_SPLIT + j, 0))
    col = pl.BlockSpec((TOP_K, tm), lambda i, j: (0, i * ROW_SPLIT + j))
    chunks = pl.BlockSpec((D_S5, nc, S5_CHUNK), lambda i, j: (0, i, 0))
    full = lambda a: pl.BlockSpec(a.shape, lambda i, j: (0,) * a.ndim)
    params = [s5_d.reshape(D_S5, 1), wglu, bglu, gain_b, wout, g2, rw2, rb.reshape(N_EXPERTS, 1)]
    return pl.pallas_call(
        _mixout_body,
        grid=(t // (tm * ROW_SPLIT), ROW_SPLIT),
        in_specs=[row(D_MODEL), row(D_POOL), chunks, chunks, row(D_LRU)] + [full(a) for a in params],
        out_specs=[row(D_MODEL), row(D_MODEL // 2), col, col],
        out_shape=[jax.ShapeDtypeStruct((t, D_MODEL), F32), jax.ShapeDtypeStruct((t, D_MODEL // 2), jnp.uint32),
                   jax.ShapeDtypeStruct((TOP_K, t), jnp.int32), jax.ShapeDtypeStruct((TOP_K, t), F32)],
        compiler_params=_cparams(("parallel", "arbitrary")), name="mix_out")(x, oa, yt3, zbt3, oc, *params)


def _expert_body(be_ref, nu_ref, first_ref, xl_ref, xh_ref, wgu_ref, bg_ref, bu_ref, wd_ref, bd_ref, p_ref, xs_ref,
                 y_ref, wg_s, wu_s, wd_s, ws_s):
    i = pl.program_id(0)
    live = i < nu_ref[0]
    x_scale = xs_ref[...]
    act_scale = FP8_TARGET / ((SWIGLU_LIMIT + 1.0) * SWIGLU_LIMIT)

    @pl.when(jnp.logical_and(live, first_ref[i] == 1))
    def _():
        p = p_ref[...]
        half = MXU_TILE // 2
        n_groups = wgu_ref.shape[3] // MXU_TILE
        amax = functools.reduce(jnp.maximum, [
            jnp.max(jnp.abs(wgu_ref[0, 0, :, MXU_TILE * j:MXU_TILE * (j + 1)]), axis=(0, 1), keepdims=True)
            for j in range(n_groups)])
        s_gu = FP8_TARGET / jnp.maximum(amax, FP8_TINY)
        for j in range(n_groups):
            w = wgu_ref[0, 0, :, MXU_TILE * j:MXU_TILE * (j + 1)].astype(BF16)
            r = jnp.dot(w, p, preferred_element_type=F32) * s_gu
            wg_s[:, half * j:half * (j + 1)] = r[:, :half].astype(FP8)
            wu_s[:, half * j:half * (j + 1)] = r[:, half:].astype(FP8)
        wd = wd_ref[0, 0]
        dmax = jnp.max(jnp.abs(wd), axis=(0, 1), keepdims=True)
        s_d = FP8_TARGET / jnp.maximum(dmax, FP8_TINY)
        wd_s[...] = (wd * s_d).astype(FP8)
        ws_s[0:1, :] = jnp.broadcast_to(1.0 / (s_gu * x_scale), (1, ws_s.shape[1]))
        ws_s[1:2, :] = jnp.broadcast_to(1.0 / (s_d * act_scale), (1, ws_s.shape[1]))

    @pl.when(live)
    def _():
        words = jnp.concatenate([xl_ref[...], xh_ref[...]], axis=1)
        lo = pltpu.bitcast(lax.shift_left(words, jnp.uint32(16)), F32)
        hi = pltpu.bitcast(words & jnp.uint32(0xFFFF0000), F32)
        x = (jnp.concatenate([lo, hi], axis=1) * x_scale).astype(FP8)
        gu_inv = ws_s[0:1, 0:1]
        g = jnp.dot(x, wg_s[...], preferred_element_type=F32) * gu_inv + bg_ref[0, 0]
        g = jnp.minimum(g, SWIGLU_LIMIT)
        u1 = (jnp.dot(x, wu_s[...], preferred_element_type=F32) * (gu_inv * act_scale)
              + (bu_ref[0, 0] + 1.0) * act_scale)
        u1 = jnp.clip(u1, (1.0 - SWIGLU_LIMIT) * act_scale, (1.0 + SWIGLU_LIMIT) * act_scale)
        glu = g / (1.0 + jnp.exp2(g * (-SWIGLU_ALPHA * math.log2(math.e))))
        act = (u1 * glu).astype(FP8)
        y = jnp.dot(act, wd_s[...], preferred_element_type=F32) * ws_s[1:2, 0:1] + bd_ref[0, 0]
        y_ref[...] = y.astype(y_ref.dtype)

    @pl.when(jnp.logical_not(live))
    def _():
        y_ref[...] = jnp.zeros_like(y_ref)


def _expert_mlp(layer, xs, x_bound, blk_e, n_used, w_gate_up, b_gate, b_up, w_down, b_down):
    n_rows, dh = xs[0].shape
    bm = MOE_ROWS
    x_scale = (FP8_TARGET / jnp.maximum(x_bound, FP8_TINY)).astype(F32).reshape(1, 1)
    first = jnp.concatenate([jnp.ones((1,), jnp.int32), (blk_e[1:] != blk_e[:-1]).astype(jnp.int32)])
    idx = jnp.arange(MXU_TILE)
    perm = jax.nn.one_hot(jnp.where(idx % 2 == 0, idx // 2, MXU_TILE // 2 + idx // 2), MXU_TILE, dtype=BF16)
    rows = pl.BlockSpec((bm, dh), lambda i, be, nu, fi: (jnp.minimum(i, nu[0] - 1), 0))
    per_e = lambda a: pl.BlockSpec((1, 1) + a.shape[2:], lambda i, be, nu, fi: (layer, be[i], 0, 0))
    grid_spec = pltpu.PrefetchScalarGridSpec(
        num_scalar_prefetch=3, grid=(n_rows // bm,),
        in_specs=[rows, rows, per_e(w_gate_up), per_e(b_gate), per_e(b_up), per_e(w_down), per_e(b_down),
                  pl.BlockSpec(perm.shape, lambda i, be, nu, fi: (0, 0)),
                  pl.BlockSpec((1, 1), lambda i, be, nu, fi: (0, 0))],
        out_specs=pl.BlockSpec((bm, D_MODEL), lambda i, be, nu, fi: (i, 0)),
        scratch_shapes=[pltpu.VMEM((D_MODEL, D_FF), FP8), pltpu.VMEM((D_MODEL, D_FF), FP8),
                        pltpu.VMEM((D_FF, D_MODEL), FP8), pltpu.VMEM((8, 128), F32)])
    return pl.pallas_call(
        _expert_body, grid_spec=grid_spec,
        out_shape=jax.ShapeDtypeStruct((n_rows, D_MODEL), BF16),
        compiler_params=pltpu.CompilerParams(dimension_semantics=("arbitrary",),
                                             vmem_limit_bytes=EXPERT_VMEM_LIMIT),
        name="expert_mlp")(blk_e, n_used, first, *xs, w_gate_up, b_gate, b_up, w_down, b_down, perm, x_scale)


def _sc_dispatch(h, dest, n_rows):
    t, d = h.shape
    win = SC_WINDOW
    dh = d // SC_COL_SPLIT
    idx = [dest[:, k].reshape(1, t) for k in range(TOP_K)]
    mesh = plsc.VectorSubcoreMesh(core_axis_name="core", subcore_axis_name="subcore")

    def scatter_cols(j):
        @functools.partial(pl.kernel, out_type=jax.ShapeDtypeStruct((n_rows, dh), h.dtype), mesh=mesh,
                           scratch_types=[], name="sc_dispatch")
        def scatter_rows(x_hbm, i0_hbm, i1_hbm, i2_hbm, i3_hbm, o_hbm):
            def body(x_vmem, *idx_vmem):
                for iv in idx_vmem:
                    pltpu.sync_copy(x_vmem, o_hbm.at[iv.at[0]])

            pltpu.emit_pipeline(
                body, grid=(t // win,),
                in_specs=[pl.BlockSpec((win, dh), lambda i: (i, j))]
                         + [pl.BlockSpec((1, win), lambda i: (0, i))] * TOP_K,
                out_specs=[],
                core_axis_name=("core", "subcore"),
                dimension_semantics=(pltpu.PARALLEL,),
            )(x_hbm, i0_hbm, i1_hbm, i2_hbm, i3_hbm)

        return scatter_rows(h, *idx)

    return [scatter_cols(j) for j in range(SC_COL_SPLIT)]


def _moe_dispatch(top_i, t):
    bm = MOE_ROWS
    n_blocks = (t * TOP_K) // bm + N_EXPERTS
    onehot = jax.nn.one_hot(top_i, N_EXPERTS, dtype=jnp.int32)
    sel = jnp.sum(onehot, axis=1)
    csum = jnp.cumsum(sel, axis=0)
    counts = csum[-1]
    padded = ((counts + bm - 1) // bm) * bm
    pad_end = jnp.cumsum(padded)
    pad_start = pad_end - padded
    base = (csum - sel) + pad_start[None, :]
    dest = jnp.sum(onehot * base[:, None, :], axis=-1)
    blk_start = jnp.arange(n_blocks, dtype=jnp.int32) * bm
    blk_e = jnp.minimum(jnp.sum((pad_end[None, :] <= blk_start[:, None]).astype(jnp.int32), axis=1),
                        N_EXPERTS - 1)
    n_used = (pad_end[-1] // bm).astype(jnp.int32).reshape(1)
    return dest, n_blocks * bm, blk_e, n_used


def _final_body(x_ref, y0, y1, y2, y3, gt_ref, g_ref, *rest):
    o_ref = rest[-1]
    o_ref[...] = _rms(_combine(x_ref, (y0, y1, y2, y3), gt_ref), g_ref[...])


def _final(x, add, g):
    t = x.shape[0]
    tm = ROW_TILE
    parts, gates = add
    per_part = t // tm // len(parts)
    out = None
    for p, ys in enumerate(parts):
        off = p * per_part
        row_p = lambda n, off=off: pl.BlockSpec((tm, n), lambda i: (i + off, 0))
        row_y = pl.BlockSpec((tm, D_MODEL), lambda i: (i, 0))
        prev = [] if out is None else [out]
        out = pl.pallas_call(
            _final_body, grid=(per_part,),
            in_specs=[row_p(D_MODEL)] + [row_y] * TOP_K + [row_p(TOP_K), pl.BlockSpec(g.shape, lambda i: (0, 0))]
                     + [pl.BlockSpec(memory_space=pl.ANY)] * len(prev),
            out_specs=row_p(D_MODEL), out_shape=jax.ShapeDtypeStruct((t, D_MODEL), F32),
            input_output_aliases={2 + TOP_K + 1: 0} if prev else {},
            compiler_params=_cparams(("parallel",)), name="final_norm")(x, *ys, gates, g, *prev)
    return out


def kernel(x, norm1_g, w_in, pool_w, pool_scale, s5_lam_re, s5_lam_im, s5_log_step, s5_b_re, s5_b_im, s5_c_re, s5_c_im, s5_d, s5_w_glu, s5_b_glu, lru_conv_w, lru_conv_b, lru_w_a, lru_b_a, lru_w_x, lru_b_x, lru_lam, mix_gain, w_out, norm2_g, router_w, router_b, w_gate_up, b_gate_up, w_down, b_down, final_g):
    batch, seq, d = x.shape
    t = batch * seq
    depth = norm1_g.shape[0]
    xt = x.reshape(t, d)
    add = None
    o1, o2 = D_POOL, D_POOL + D_S5
    b_gate = b_gate_up[:, :, None, 0::2]
    b_up = b_gate_up[:, :, None, 1::2]
    b_dn = b_down[:, :, None, :]
    for l in range(depth):
        w_rest = jnp.concatenate([w_in[l][:, :o1], w_in[l][:, o2:]], axis=1).astype(BF16)
        w_s5t = w_in[l][:, o1:o2].T.astype(BF16)
        s5_tabs = _s5_tables(s5_lam_re[l], s5_lam_im[l], s5_log_step[l], s5_b_re[l], s5_b_im[l],
                             s5_c_re[l], s5_c_im[l], seq // S5_CHUNK)
        lru_wg, lru_bg, lru_sp = _lru_params(lru_w_a[l], lru_b_a[l], lru_w_x[l], lru_b_x[l], lru_lam[l])
        if add is None:
            za, zbt, zc, zg = _inproj(xt, None, norm1_g[l][None, :], w_rest, w_s5t)
        else:
            xt, za, zbt, zc, zg = _inproj(xt, add, norm1_g[l][None, :], w_rest, w_s5t)
        oa = _pool_mixer(za, batch, seq, _block_diag(pool_w[l]).astype(BF16), pool_scale[l][None, :],
                         mix_gain[l][None, :o1])
        yt = _s5_core(zbt, batch, seq, s5_tabs)
        oc = _lru_mixer(zc, zg, batch, seq, lru_conv_w[l], lru_conv_b[l], lru_wg, lru_bg, lru_sp,
                        mix_gain[l][o2:])
        xt, h2, top_i, gates = _mixout(
            xt, oa, yt, zbt, oc, s5_d[l], s5_w_glu[l].astype(BF16), s5_b_glu[l][None, :],
            mix_gain[l][None, o1:o2], w_out[l].astype(BF16), norm2_g[l][None, :], _router_split(router_w[l]),
            router_b[l])
        top_i, gates = top_i.T, gates.T
        dest, n_rows, blk_e, n_used = _moe_dispatch(top_i, t)
        xs = _sc_dispatch(h2, dest, n_rows)
        x_bound = math.sqrt(D_MODEL) * jnp.max(jnp.abs(norm2_g[l]))
        ys = _expert_mlp(l, xs, x_bound, blk_e, n_used, w_gate_up, b_gate, b_up, w_down, b_dn)
        tp = t // TOKEN_PARTS
        add = ([[ys.at[dest[p * tp:(p + 1) * tp, k]].get(mode="promise_in_bounds") for k in range(TOP_K)]
                for p in range(TOKEN_PARTS)], gates)
    out = _final(xt, add, final_g[None, :])
    return out.reshape(batch, seq, d)
```

```python
import functools
import math

import jax
import jax.numpy as jnp
from jax import lax
from jax.experimental import pallas as pl
from jax.experimental.pallas import tpu as pltpu
from jax.experimental.pallas import tpu_sc as plsc

F32 = jnp.float32
BF16 = jnp.bfloat16
FP8 = jnp.float8_e4m3fn
FP8_TARGET = 416.0
FP8_TINY = 1e-30

D_MODEL = 1024
D_POOL = 256
D_S5 = 384
D_LRU = 384
POOL_WINDOWS = (2, 4, 8, 16)
POOL_GROUP = 64
S5_GROUP = 16
S5_NGROUPS = 24
S5_STATE = 64
LRU_HEADS = 6
LRU_HEAD_DIM = 64
RG_C = 8.0
N_EXPERTS = 32
TOP_K = 4
D_FF = 1024
SWIGLU_LIMIT = 7.0
SWIGLU_ALPHA = 1.702
EPS = 1e-5

ROW_TILE = 512
ROW_SPLIT = 2
S5_CHUNK = 128
POOL_TILE = 1024
LRU_TILE = 512
HALO = 8
MOE_ROWS = 512
TOKEN_PARTS = 1
SC_WINDOW = 128
SC_COL_SPLIT = 2
MXU_TILE = 256
EXPERT_VMEM_LIMIT = 56 * 1024 * 1024
VMEM_LIMIT = 48 * 1024 * 1024


def _cparams(sem):
    return pltpu.CompilerParams(dimension_semantics=sem, vmem_limit_bytes=VMEM_LIMIT)


def _rms(x, g):
    return x * lax.rsqrt(jnp.mean(x * x, axis=-1, keepdims=True) + EPS) * g


def _gelu(x):
    return 0.5 * x * (1.0 + jnp.tanh(0.7978845608028654 * (x + 0.044715 * (x * x * x))))


def _sigmoid(x):
    return 1.0 / (1.0 + jnp.exp(-x))


def _combine(x_ref, y_refs, gt_ref):
    x = x_ref[...]
    gt = gt_ref[...]
    for k in range(TOP_K):
        x = x + gt[:, k:k + 1] * y_refs[k][...].astype(F32)
    return x


def _inproj_body(has_add, *refs):
    if has_add:
        x_ref, y0, y1, y2, y3, gt_ref, g_ref, w_ref, wbt_ref = refs[:9]
        xo_ref, za, zbt, zc, zg = refs[-5:]
        x = _combine(x_ref, (y0, y1, y2, y3), gt_ref)
        xo_ref[...] = x
    else:
        x_ref, g_ref, w_ref, wbt_ref, za, zbt, zc, zg = refs
        x = x_ref[...]
    h = _rms(x, g_ref[...]).astype(BF16)
    z = jnp.dot(h, w_ref[...], preferred_element_type=F32)
    za[...] = z[:, :D_POOL]
    zc[...] = z[:, D_POOL:D_POOL + D_LRU]
    zg[...] = z[:, D_POOL + D_LRU:]
    zt = lax.dot_general(wbt_ref[...], h, (((1,), (1,)), ((), ())), preferred_element_type=F32)
    per_step = zt.shape[1] // S5_CHUNK
    for n in range(per_step):
        zbt[:, pl.program_id(1) * per_step + n, :] = zt[:, n * S5_CHUNK:(n + 1) * S5_CHUNK]


def _inproj(x, add, g, w_rest, w_s5t):
    t = x.shape[0]
    tm = ROW_TILE
    nc = tm * ROW_SPLIT // S5_CHUNK
    row = lambda n: pl.BlockSpec((tm, n), lambda i, j: (i * ROW_SPLIT + j, 0))
    full = lambda a: pl.BlockSpec(a.shape, lambda i, j: (0,) * a.ndim)
    z_shapes = [jax.ShapeDtypeStruct((t, D_POOL), F32), jax.ShapeDtypeStruct((D_S5, t // S5_CHUNK, S5_CHUNK), F32),
                jax.ShapeDtypeStruct((t, D_LRU), F32), jax.ShapeDtypeStruct((t, D_LRU), F32)]
    z_specs = [row(D_POOL), pl.BlockSpec((D_S5, nc, S5_CHUNK), lambda i, j: (0, i, 0)), row(D_LRU), row(D_LRU)]
    n_outer = t // (tm * ROW_SPLIT)
    if add is None:
        return pl.pallas_call(
            functools.partial(_inproj_body, False),
            grid=(n_outer, ROW_SPLIT), in_specs=[row(D_MODEL), full(g), full(w_rest), full(w_s5t)],
            out_specs=z_specs, out_shape=z_shapes,
            compiler_params=_cparams(("parallel", "arbitrary")), name="inproj")(x, g, w_rest, w_s5t)
    parts, gates = add
    per_part = n_outer // len(parts)
    out_shape = [jax.ShapeDtypeStruct((t, D_MODEL), F32)] + z_shapes
    outs = []
    for p, ys in enumerate(parts):
        off = p * per_part
        row_p = lambda n, off=off: pl.BlockSpec((tm, n), lambda i, j: ((i + off) * ROW_SPLIT + j, 0))
        row_y = pl.BlockSpec((tm, D_MODEL), lambda i, j: (i * ROW_SPLIT + j, 0))
        zbt_p = pl.BlockSpec((D_S5, nc, S5_CHUNK), lambda i, j, off=off: (0, i + off, 0))
        n_in = 1 + TOP_K + 4
        outs = pl.pallas_call(
            functools.partial(_inproj_body, True),
            grid=(per_part, ROW_SPLIT),
            in_specs=[row_p(D_MODEL)] + [row_y] * TOP_K + [row_p(TOP_K), full(g), full(w_rest), full(w_s5t)]
                     + [pl.BlockSpec(memory_space=pl.ANY)] * len(outs),
            out_specs=[row_p(D_MODEL), row_p(D_POOL), zbt_p, row_p(D_LRU), row_p(D_LRU)],
            out_shape=out_shape,
            input_output_aliases={n_in + k: k for k in range(len(outs))},
            compiler_params=_cparams(("parallel", "arbitrary")), name="inproj")(
                x, *ys, gates, g, w_rest, w_s5t, *outs)
    return outs


def _pool_body(seq, prev_ref, cur_ref, next_ref, w_ref, sc_ref, gain_ref, o_ref, u_s, s2_s, s4_s, s8_s):
    tl = cur_ref.shape[0]
    i = pl.program_id(1)
    nt = pl.num_programs(1)
    zero8 = jnp.zeros((HALO, D_POOL), F32)
    for buf in (u_s, s2_s, s4_s, s8_s):
        buf[0:HALO, :] = zero8
        buf[tl + 3 * HALO:tl + 4 * HALO, :] = zero8
    u_s[HALO:2 * HALO, :] = jnp.where(i > 0, prev_ref[...], 0.0)
    u_s[2 * HALO:2 * HALO + tl, :] = cur_ref[...]
    u_s[2 * HALO + tl:3 * HALO + tl, :] = jnp.where(i < nt - 1, next_ref[...], 0.0)
    r = tl + 2 * HALO
    s2_s[HALO:HALO + r, :] = u_s[HALO - 1:HALO - 1 + r, :] + u_s[HALO:HALO + r, :]
    s4_s[HALO:HALO + r, :] = s2_s[HALO - 1:HALO - 1 + r, :] + s2_s[HALO + 1:HALO + 1 + r, :]
    s8_s[HALO:HALO + r, :] = s4_s[HALO - 2:HALO - 2 + r, :] + s4_s[HALO + 2:HALO + 2 + r, :]
    o = 2 * HALO
    s16 = s8_s[o - 4:o - 4 + tl, :] + s8_s[o + 4:o + 4 + tl, :]
    s8 = s8_s[o:o + tl, :]
    s4 = s4_s[o:o + tl, :]
    s2 = s2_s[o:o + tl, :]
    u = u_s[o:o + tl, :]
    lane = lax.broadcasted_iota(jnp.int32, (tl, D_POOL), 1)
    tpos = lax.broadcasted_iota(jnp.int32, (tl, D_POOL), 0) + i * tl
    g0, g1, g2 = lane < POOL_GROUP, lane < 2 * POOL_GROUP, lane < 3 * POOL_GROUP
    half = jnp.where(g0, 1, jnp.where(g1, 2, jnp.where(g2, 4, 8)))
    wsum = jnp.where(g0, s2, jnp.where(g1, s4, jnp.where(g2, s8, s16)))
    cnt = (jnp.minimum(tpos + half, seq) - jnp.maximum(tpos - half, 0)).astype(F32)
    d = wsum / cnt - u
    y = jnp.dot(d.astype(BF16), w_ref[...], preferred_element_type=F32) * sc_ref[...]
    o_ref[...] = _rms(y, gain_ref[...])


def _pool_mixer(za, batch, seq, w_bd, scale, gain):
    t = za.shape[0]
    tl = POOL_TILE
    nt = seq // tl
    hb = tl // HALO
    nhb = t // HALO
    cur = pl.BlockSpec((tl, D_POOL), lambda b, i: (b * nt + i, 0))
    prev = pl.BlockSpec((HALO, D_POOL), lambda b, i: (jnp.maximum((b * nt + i) * hb - 1, 0), 0))
    nxt = pl.BlockSpec((HALO, D_POOL), lambda b, i: (jnp.minimum((b * nt + i + 1) * hb, nhb - 1), 0))
    full = lambda a: pl.BlockSpec(a.shape, lambda b, i: (0,) * a.ndim)
    return pl.pallas_call(
        functools.partial(_pool_body, seq),
        grid=(batch, nt),
        in_specs=[prev, cur, nxt, full(w_bd), full(scale), full(gain)],
        out_specs=cur,
        out_shape=jax.ShapeDtypeStruct((t, D_POOL), F32),
        scratch_shapes=[pltpu.VMEM((tl + 4 * HALO, D_POOL), F32)] * 4,
        compiler_params=_cparams(("parallel", "parallel")), name="pool_mixer")(za, za, za, w_bd, scale, gain)


def _s5_scan_steps(n_chunks):
    return max(1, int(math.ceil(math.log2(n_chunks))))


def _s5_tables(lam_re, lam_im, log_step, b_re, b_im, c_re, c_im, n_chunks):
    L, C, P, G = S5_CHUNK, S5_GROUP, S5_STATE, S5_NGROUPS
    hp = lax.Precision.HIGHEST
    lr = lam_re.astype(F32)
    li = lam_im.astype(F32)
    dt = jnp.exp(log_step.astype(F32))[..., None]
    ar = lr * dt
    ai = li * dt
    jj = jnp.arange(L + 1, dtype=F32)

    def powers(j, a_r, a_i):
        mag = jnp.exp(j * a_r)
        return mag * jnp.cos(j * a_i), mag * jnp.sin(j * a_i)

    e_r, e_i = powers(jj[None, None, :, None], ar[:, :, None, :], ai[:, :, None, :])
    et_r, et_i = powers(jj[None, None, None, :], ar[..., None], ai[..., None])
    den = lr * lr + li * li
    nr = e_r[:, :, 1, :] - 1.0
    ni = e_i[:, :, 1, :]
    q_r = (nr * lr + ni * li) / den
    q_i = (ni * lr - nr * li) / den
    br = b_re.astype(F32)
    bi = b_im.astype(F32)
    bbt_r = (q_r[..., None] * br - q_i[..., None] * bi).transpose(0, 1, 3, 2)
    bbt_i = (q_r[..., None] * bi + q_i[..., None] * br).transpose(0, 1, 3, 2)
    cr = c_re.astype(F32)
    ci = c_im.astype(F32)
    m_r = cr[:, :, None] * bbt_r[:, :, :, None] - ci[:, :, None] * bbt_i[:, :, :, None]
    m_i = cr[:, :, None] * bbt_i[:, :, :, None] + ci[:, :, None] * bbt_r[:, :, :, None]
    kk = (jnp.einsum('dgxcp,dgpj->dgxcj', m_r, et_r[..., :L], precision=hp)
          - jnp.einsum('dgxcp,dgpj->dgxcj', m_i, et_i[..., :L], precision=hp))
    kf, kb = kk[0], kk[1]
    lagtab = jnp.concatenate([kb[..., :0:-1], kf[..., :1] + kb[..., :1], kf[..., 1:],
                              jnp.zeros_like(kf[..., :1])], axis=-1)
    bits = lax.bitcast_convert_type(lagtab.astype(BF16).astype(F32), jnp.uint32).reshape(G, C * C, 2 * L)
    lagtab = lax.shift_right_logical(bits[..., :L], jnp.uint32(16)) | (bits[..., L:] & jnp.uint32(0xFFFF0000))
    fb = lambda f, b, axis: jnp.concatenate([f, b], axis=axis)
    pw_s = jnp.stack([fb(e_r[0, :, L - 1::-1], e_r[1, :, :L], -1),
                      fb(e_i[0, :, L - 1::-1], e_i[1, :, :L], -1)], axis=1)
    bb = jnp.stack([fb(bbt_r[0], bbt_r[1], -1), fb(bbt_i[0], bbt_i[1], -1)], axis=1)
    pw_t = jnp.stack([fb(et_r[0, :, :, 1:L + 1], et_r[1, :, :, L:0:-1], 1),
                      fb(et_i[0, :, :, 1:L + 1], et_i[1, :, :, L:0:-1], 1)], axis=1)
    crt = cr.transpose(0, 1, 3, 2)
    cit = ci.transpose(0, 1, 3, 2)
    cc = jnp.stack([fb(crt[0], crt[1], 1), fb(cit[0], cit[1], 1)], axis=1)
    n_steps = _s5_scan_steps(n_chunks)
    kpow =(L * (2 ** jnp.arange(n_steps, dtype=F32)))[None, None, :, None]
    p_r, p_i = powers(kpow, ar[:, :, None, :], ai[:, :, None, :])
    pad = jnp.zeros((G, (-n_steps) % 8, 2 * P), F32)
    p_r = jnp.concatenate([jnp.concatenate([p_r[0], p_r[1]], axis=-1), pad], axis=1)
    p_i = jnp.concatenate([jnp.concatenate([p_i[0], p_i[1]], axis=-1), pad], axis=1)
    ptab = jnp.concatenate([p_r, p_i], axis=1)
    return lagtab, pw_s, bb, pw_t, cc, ptab


def _s5_body(n_chunks, n_steps, u_ref, lag_ref, pws_ref, bb_ref, pwt_ref, cc_ref, p_ref, y_ref,
             toep_s, w_s, v_s):
    L, C = S5_CHUNK, S5_GROUP
    n = u_ref.shape[1]
    two_p = 2 * S5_STATE
    e_r, e_i = pws_ref[0, 0], pws_ref[0, 1]
    et_r, et_i = pwt_ref[0, 0], pwt_ref[0, 1]
    c_r, c_i = cc_ref[0, 0], cc_ref[0, 1]
    for c in range(C):
        b_r = bb_ref[0, 0, c:c + 1, :]
        b_i = bb_ref[0, 1, c:c + 1, :]
        w_s[c * L:(c + 1) * L, :two_p] = (e_r * b_r - e_i * b_i).astype(BF16)
        w_s[c * L:(c + 1) * L, two_p:] = (e_r * b_i + e_i * b_r).astype(BF16)
        k_r = c_r[:, c:c + 1]
        k_i = c_i[:, c:c + 1]
        v_s[:two_p, c * L:(c + 1) * L] = (k_r * et_r - k_i * et_i).astype(BF16)
        v_s[two_p:, c * L:(c + 1) * L] = (-(k_r * et_i + k_i * et_r)).astype(BF16)
    s_i = lax.broadcasted_iota(jnp.int32, (L, L), 0)
    t_i = lax.broadcasted_iota(jnp.int32, (L, L), 1)
    non_positive_lag = t_i <= s_i

    def build_block_row(cp, carry):
        r0 = pl.multiple_of(cp * L, L)
        for c in range(C):
            k = lag_ref[0, pl.ds(cp * C + c, 1), :]
            r = pltpu.roll(jnp.broadcast_to(k, (L, L)), 1, 1, stride=1, stride_axis=0)
            bits = jnp.where(non_positive_lag, lax.shift_left(r, jnp.uint32(16)), r & jnp.uint32(0xFFFF0000))
            toep_s[pl.ds(r0, L), c * L:(c + 1) * L] = pltpu.bitcast(bits, F32).astype(BF16)
        return carry

    lax.fori_loop(0, C, build_block_row, 0)
    u = jnp.concatenate([u_ref[c] for c in range(C)], axis=1).astype(BF16)
    y = jnp.dot(u, toep_s[...], preferred_element_type=F32)
    s = jnp.dot(u, w_s[...], preferred_element_type=F32)
    xr = s[:, :two_p]
    xi = s[:, two_p:]
    row = lax.broadcasted_iota(jnp.int32, (n, two_p), 0) % n_chunks
    is_fwd = lax.broadcasted_iota(jnp.int32, (n, two_p), 1) < S5_STATE
    pim0 = p_ref.shape[1] // 2

    def shifted(a, k):
        down = jnp.where(row >= k, pltpu.roll(a, k, 0), 0.0)
        up = jnp.where(row < n_chunks - k, pltpu.roll(a, n - k, 0), 0.0)
        return jnp.where(is_fwd, down, up)

    for j in range(n_steps):
        k = 1 << j
        pr = p_ref[0, j:j + 1, :]
        pi = p_ref[0, pim0 + j:pim0 + j + 1, :]
        sr = shifted(xr, k)
        si = shifted(xi, k)
        xr, xi = xr + pr * sr - pi * si, xi + pr * si + pi * sr
    carry = jnp.concatenate([shifted(xr, 1), shifted(xi, 1)], axis=1).astype(BF16)
    y = y + jnp.dot(carry, v_s[...], preferred_element_type=F32)
    for c in range(C):
        y_ref[c] = y[:, c * L:(c + 1) * L]


def _s5_core(zbt, batch, seq, tables):
    L = S5_CHUNK
    n_chunks = seq // L
    n_steps = _s5_scan_steps(n_chunks)
    n = batch * n_chunks
    per_g = lambda a: pl.BlockSpec((1,) + a.shape[1:], lambda g: (g,) + (0,) * (a.ndim - 1))
    grp = pl.BlockSpec((S5_GROUP, n, L), lambda g: (g, 0, 0))
    return pl.pallas_call(
        functools.partial(_s5_body, n_chunks, n_steps),
        grid=(S5_NGROUPS,),
        in_specs=[grp] + [per_g(a) for a in tables],
        out_specs=grp,
        out_shape=jax.ShapeDtypeStruct((D_S5, n, L), F32),
        scratch_shapes=[pltpu.VMEM((S5_GROUP * L, S5_GROUP * L), BF16),
                        pltpu.VMEM((S5_GROUP * L, 4 * S5_STATE), BF16),
                        pltpu.VMEM((4 * S5_STATE, S5_GROUP * L), BF16)],
        compiler_params=_cparams(("parallel",)), name="s5_core")(zbt, *tables)


def _lru_body(reverse, prev_ref, cur_ref, next_ref, cw_ref, cb_ref, wg_ref, bg_ref, sp_ref, *rest):
    if reverse:
        hf_ref, gate_ref, gain_ref, o_ref, a_s, b_s, h_s, carry_s = rest
    else:
        o_ref, a_s, b_s, h_s, carry_s = rest
    nb, tl, _ = cur_ref.shape
    step = pl.program_id(0)
    nt = pl.num_programs(0)
    ti = nt - 1 - step if reverse else step

    @pl.when(step == 0)
    def _():
        carry_s[...] = jnp.zeros_like(carry_s)

    cw = cw_ref[...]
    k_row = sp_ref[...] * (-0.5 * RG_C * math.log2(math.e))
    sub = lax.broadcasted_iota(jnp.int32, (HALO, D_LRU), 0)

    def shifted(x, d, edge):
        y = pltpu.roll(x, d % tl, 0)
        if d > 0:
            return jnp.concatenate([jnp.where(sub < d, pltpu.roll(edge, d, 0), y[:HALO]), y[HALO:]], axis=0)
        return jnp.concatenate([y[:-HALO], jnp.where(sub >= HALO + d, pltpu.roll(edge, HALO + d, 0), y[-HALO:])],
                               axis=0)

    for b in range(nb):
        x = cur_ref[b]
        before = jnp.where(ti > 0, prev_ref[b], 0.0)
        after = jnp.where(ti < nt - 1, next_ref[b], 0.0)
        xc = (cb_ref[...] + cw[0:1, :] * shifted(x, 1, before) + cw[1:2, :] * x
              + cw[2:3, :] * shifted(x, -1, after) + cw[3:4, :] * shifted(x, -2, after))
        gates = jnp.dot(xc.astype(BF16), wg_ref[...], preferred_element_type=F32) + bg_ref[...]
        a = jnp.exp2(k_row * (jnp.tanh(0.5 * gates[:, :D_LRU]) + 1.0))
        ig = 0.5 * jnp.tanh(0.5 * gates[:, D_LRU:]) + 0.5
        a_s[b] = a
        om = 1.0 - a * a
        b_s[b] = (om * lax.rsqrt(jnp.maximum(om, 1e-30))) * (ig * xc)

    def scan_step(s, hs):
        t = tl - 1 - s if reverse else s
        out = []
        for b in range(nb):
            h = a_s[b, pl.ds(t, 1), :] * hs[b] + b_s[b, pl.ds(t, 1), :]
            h_s[b, pl.ds(t, 1), :] = h
            out.append(h)
        return tuple(out)

    hs = lax.fori_loop(0, tl, scan_step, tuple(carry_s[b:b + 1, :] for b in range(nb)), unroll=8)
    for b in range(nb):
        carry_s[b:b + 1, :] = hs[b]

    if reverse:
        for b in range(nb):
            y = (hf_ref[b] + h_s[b]) * _gelu(gate_ref[b])
            o_ref[b] = _rms(y, gain_ref[...])
    else:
        o_ref[...] = h_s[...]


def _lru_pass(reverse, zc3, params, extra):
    nb, seq, _ = zc3.shape
    tl = LRU_TILE
    nt = seq // tl
    hb = tl // HALO
    nhb = seq // HALO
    tix = (lambda i: nt - 1 - i) if reverse else (lambda i: i)
    cur = pl.BlockSpec((nb, tl, D_LRU), lambda i: (0, tix(i), 0))
    prev = pl.BlockSpec((nb, HALO, D_LRU), lambda i: (0, jnp.maximum(tix(i) * hb - 1, 0), 0))
    nxt = pl.BlockSpec((nb, HALO, D_LRU), lambda i: (0, jnp.minimum((tix(i) + 1) * hb, nhb - 1), 0))
    full = lambda a: pl.BlockSpec(a.shape, lambda i: (0,) * a.ndim)
    ins = [zc3, zc3, zc3, *params]
    in_specs = [prev, cur, nxt] + [full(a) for a in params]
    if reverse:
        hf, gate, gain = extra
        ins += [hf, gate, gain]
        in_specs += [cur, cur, full(gain)]
    return pl.pallas_call(
        functools.partial(_lru_body, reverse),
        grid=(nt,), in_specs=in_specs, out_specs=cur,
        out_shape=jax.ShapeDtypeStruct((nb, seq, D_LRU), F32),
        scratch_shapes=[pltpu.VMEM((nb, tl, D_LRU), F32),
                        pltpu.VMEM((nb, tl, D_LRU), F32),
                        pltpu.VMEM((nb, tl, D_LRU), F32),
                        pltpu.VMEM((8, D_LRU), F32)],
        compiler_params=_cparams(("arbitrary",)),
        name="lru_bwd" if reverse else "lru_fwd")(*ins)


def _block_diag(w):
    h, d, _ = w.shape
    eye = jnp.eye(h, dtype=w.dtype)
    return (eye[:, None, :, None] * w[:, :, None, :]).reshape(h * d, h * d)


def _lru_params(w_a, b_a, w_x, b_x, lam):
    bd = jax.vmap(_block_diag)
    wg = jnp.concatenate([bd(w_a), bd(w_x)], axis=-1).astype(BF16)
    bg = jnp.concatenate([b_a, b_x], axis=-1)[:, None, :]
    sp = jax.nn.softplus(-lam.astype(F32))[:, None, :]
    return wg, bg, sp


def _lru_mixer(zc, zg, batch, seq, conv_w, conv_b, wg, bg, sp, gain):
    zc3 = zc.reshape(batch, seq, D_LRU)
    zg3 = zg.reshape(batch, seq, D_LRU)
    outs = None
    for d in (0, 1):
        params = [conv_w, conv_b[None, :], wg[d], bg[d], sp[d]]
        extra = None if d == 0 else (outs, zg3, gain[None, :])
        outs = _lru_pass(d == 1, zc3, params, extra)
    return outs.reshape(batch * seq, D_LRU)


def _mixout_body(x_ref, oa_ref, y3_ref, u3_ref, oc_ref, sd_ref, wglu_ref, bglu_ref, gb_ref,
                 wout_ref, g2_ref, rw_ref, rb_ref, xo_ref, h_ref, ti_ref, gt_ref):
    tm = x_ref.shape[0]
    per_step = tm // S5_CHUNK
    c0 = pl.program_id(1) * per_step
    yt = jnp.concatenate([y3_ref[:, c0 + n, :] for n in range(per_step)], axis=1)
    ut = jnp.concatenate([u3_ref[:, c0 + n, :] for n in range(per_step)], axis=1)
    vt = _gelu(yt + sd_ref[...] * ut).astype(BF16)
    gl = lax.dot_general(vt, wglu_ref[...], (((0,), (0,)), ((), ())), preferred_element_type=F32) + bglu_ref[...]
    ob = _rms(gl[:, :D_S5] * _sigmoid(gl[:, D_S5:]), gb_ref[...])
    o = jnp.concatenate([oa_ref[...], ob, oc_ref[...]], axis=1).astype(BF16)
    x = x_ref[...] + jnp.dot(o, wout_ref[...], preferred_element_type=F32)
    xo_ref[...] = x
    h = _rms(x, g2_ref[...])
    h_hi = h.astype(BF16)
    bits = pltpu.bitcast(h_hi.astype(F32), jnp.uint32)
    h_ref[...] = lax.shift_right_logical(bits[:, :D_MODEL // 2], jnp.uint32(16)) | (
        bits[:, D_MODEL // 2:] & jnp.uint32(0xFFFF0000))
    h_lo = (h - h_hi.astype(F32)).astype(BF16)
    nt_dims = (((1,), (1,)), ((), ()))
    la = lax.dot_general(rw_ref[...], h_hi, nt_dims, preferred_element_type=F32)
    lb = lax.dot_general(rw_ref[0:N_EXPERTS, :], h_lo, nt_dims, preferred_element_type=F32)
    logits = la[:N_EXPERTS] + la[N_EXPERTS:] + lb + rb_ref[...]
    eidx = lax.broadcasted_iota(jnp.int32, (N_EXPERTS, tm), 0).astype(F32)
    idx_rows, val_rows = [], []
    for k in range(TOP_K):
        m = jnp.max(logits, axis=0, keepdims=True)
        idx = jnp.min(jnp.where(logits == m, eidx, float(N_EXPERTS)), axis=0, keepdims=True)
        idx_rows.append(idx)
        val_rows.append(jnp.exp(m - val_rows[0]) if k else m)
        logits = jnp.where(eidx == idx, -jnp.inf, logits)
    top = val_rows[0]
    vals = jnp.concatenate([jnp.ones_like(top)] + [v for v in val_rows[1:]], axis=0)
    ti_ref[...] = jnp.concatenate(idx_rows, axis=0).astype(jnp.int32)
    gt_ref[...] = vals / jnp.sum(vals, axis=0, keepdims=True)


def _router_split(rw):
    rw_hi = rw.astype(BF16)
    rw_lo = (rw - rw_hi.astype(F32)).astype(BF16)
    return jnp.concatenate([jnp.swapaxes(rw_hi, -1, -2), jnp.swapaxes(rw_lo, -1, -2)], axis=-2)


def _mixout(x, oa, yt3, zbt3, oc, s5_d, wglu, bglu, gain_b, wout, g2, rw2, rb):
    t = x.shape[0]
    tm = ROW_TILE
    nc = tm * ROW_SPLIT // S5_CHUNK
    row = lambda n: pl.BlockSpec((tm, n), lambda i, j: (i * ROW_SPLIT + j, 0))
    col = pl.BlockSpec((TOP_K, tm), lambda i, j: (0, i * ROW_SPLIT + j))
    chunks = pl.BlockSpec((D_S5, nc, S5_CHUNK), lambda i, j: (0, i, 0))
    full = lambda a: pl.BlockSpec(a.shape, lambda i, j: (0,) * a.ndim)
    params = [s5_d.reshape(D_S5, 1), wglu, bglu, gain_b, wout, g2, rw2, rb.reshape(N_EXPERTS, 1)]
    return pl.pallas_call(
        _mixout_body,
        grid=(t // (tm * ROW_SPLIT), ROW_SPLIT),
        in_specs=[row(D_MODEL), row(D_POOL), chunks, chunks, row(D_LRU)] + [full(a) for a in params],
        out_specs=[row(D_MODEL), row(D_MODEL // 2), col, col],
        out_shape=[jax.ShapeDtypeStruct((t, D_MODEL), F32), jax.ShapeDtypeStruct((t, D_MODEL // 2), jnp.uint32),
                   jax.ShapeDtypeStruct((TOP_K, t), jnp.int32), jax.ShapeDtypeStruct((TOP_K, t), F32)],
        compiler_params=_cparams(("parallel", "arbitrary")), name="mix_out")(x, oa, yt3, zbt3, oc, *params)


def _expert_body(be_ref, nu_ref, first_ref, xl_ref, xh_ref, wgu_ref, bg_ref, bu_ref, wd_ref, bd_ref, p_ref, xs_ref,
                 y_ref, wg_s, wu_s, wd_s, ws_s):
    i = pl.program_id(0)
    live = i < nu_ref[0]
    x_scale = xs_ref[...]
    act_scale = FP8_TARGET / ((SWIGLU_LIMIT + 1.0) * SWIGLU_LIMIT)

    @pl.when(jnp.logical_and(live, first_ref[i] == 1))
    def _():
        p = p_ref[...]
        half = MXU_TILE // 2
        n_groups = wgu_ref.shape[3] // MXU_TILE
        amax = functools.reduce(jnp.maximum, [
            jnp.max(jnp.abs(wgu_ref[0, 0, :, MXU_TILE * j:MXU_TILE * (j + 1)]), axis=(0, 1), keepdims=True)
            for j in range(n_groups)])
        s_gu = FP8_TARGET / jnp.maximum(amax, FP8_TINY)
        for j in range(n_groups):
            w = wgu_ref[0, 0, :, MXU_TILE * j:MXU_TILE * (j + 1)].astype(BF16)
            r = jnp.dot(w, p, preferred_element_type=F32) * s_gu
            wg_s[:, half * j:half * (j + 1)] = r[:, :half].astype(FP8)
            wu_s[:, half * j:half * (j + 1)] = r[:, half:].astype(FP8)
        wd = wd_ref[0, 0]
        dmax = jnp.max(jnp.abs(wd), axis=(0, 1), keepdims=True)
        s_d = FP8_TARGET / jnp.maximum(dmax, FP8_TINY)
        wd_s[...] = (wd * s_d).astype(FP8)
        ws_s[0:1, :] = jnp.broadcast_to(1.0 / (s_gu * x_scale), (1, ws_s.shape[1]))
        ws_s[1:2, :] = jnp.broadcast_to(1.0 / (s_d * act_scale), (1, ws_s.shape[1]))

    @pl.when(live)
    def _():
        words = jnp.concatenate([xl_ref[...], xh_ref[...]], axis=1)
        lo = pltpu.bitcast(lax.shift_left(words, jnp.uint32(16)), F32)
        hi = pltpu.bitcast(words & jnp.uint32(0xFFFF0000), F32)
        x = (jnp.concatenate([lo, hi], axis=1) * x_scale).astype(FP8)
        gu_inv = ws_s[0:1, 0:1]
        g = jnp.dot(x, wg_s[...], preferred_element_type=F32) * gu_inv + bg_ref[0, 0]
        g = jnp.minimum(g, SWIGLU_LIMIT)
        u1 = (jnp.dot(x, wu_s[...], preferred_element_type=F32) * (gu_inv * act_scale)
              + (bu_ref[0, 0] + 1.0) * act_scale)
        u1 = jnp.clip(u1, (1.0 - SWIGLU_LIMIT) * act_scale, (1.0 + SWIGLU_LIMIT) * act_scale)
        glu = g / (1.0 + jnp.exp2(g * (-SWIGLU_ALPHA * math.log2(math.e))))
        act = (u1 * glu).astype(FP8)
        y = jnp.dot(act, wd_s[...], preferred_element_type=F32) * ws_s[1:2, 0:1] + bd_ref[0, 0]
        y_ref[...] = y.astype(y_ref.dtype)

    @pl.when(jnp.logical_not(live))
    def _():
        y_ref[...] = jnp.zeros_like(y_ref)


def _expert_mlp(layer, xs, x_bound, blk_e, n_used, w_gate_up, b_gate, b_up, w_down, b_down):
    n_rows, dh = xs[0].shape
    bm = MOE_ROWS
    x_scale = (FP8_TARGET / jnp.maximum(x_bound, FP8_TINY)).astype(F32).reshape(1, 1)
    first = jnp.concatenate([jnp.ones((1,), jnp.int32), (blk_e[1:] != blk_e[:-1]).astype(jnp.int32)])
    idx = jnp.arange(MXU_TILE)
    perm = jax.nn.one_hot(jnp.where(idx % 2 == 0, idx // 2, MXU_TILE // 2 + idx // 2), MXU_TILE, dtype=BF16)
    rows = pl.BlockSpec((bm, dh), lambda i, be, nu, fi: (jnp.minimum(i, nu[0] - 1), 0))
    per_e = lambda a: pl.BlockSpec((1, 1) + a.shape[2:], lambda i, be, nu, fi: (layer, be[i], 0, 0))
    grid_spec = pltpu.PrefetchScalarGridSpec(
        num_scalar_prefetch=3, grid=(n_rows // bm,),
        in_specs=[rows, rows, per_e(w_gate_up), per_e(b_gate), per_e(b_up), per_e(w_down), per_e(b_down),
                  pl.BlockSpec(perm.shape, lambda i, be, nu, fi: (0, 0)),
                  pl.BlockSpec((1, 1), lambda i, be, nu, fi: (0, 0))],
        out_specs=pl.BlockSpec((bm, D_MODEL), lambda i, be, nu, fi: (i, 0)),
        scratch_shapes=[pltpu.VMEM((D_MODEL, D_FF), FP8), pltpu.VMEM((D_MODEL, D_FF), FP8),
                        pltpu.VMEM((D_FF, D_MODEL), FP8), pltpu.VMEM((8, 128), F32)])
    return pl.pallas_call(
        _expert_body, grid_spec=grid_spec,
        out_shape=jax.ShapeDtypeStruct((n_rows, D_MODEL), BF16),
        compiler_params=pltpu.CompilerParams(dimension_semantics=("arbitrary",),
                                             vmem_limit_bytes=EXPERT_VMEM_LIMIT),
        name="expert_mlp")(blk_e, n_used, first, *xs, w_gate_up, b_gate, b_up, w_down, b_down, perm, x_scale)


def _sc_dispatch(h, dest, n_rows):
    t, d = h.shape
    win = SC_WINDOW
    dh = d // SC_COL_SPLIT
    idx = [dest[:, k].reshape(1, t) for k in range(TOP_K)]
    mesh = plsc.VectorSubcoreMesh(core_axis_name="core", subcore_axis_name="subcore")

    def scatter_cols(j):
        @functools.partial(pl.kernel, out_type=jax.ShapeDtypeStruct((n_rows, dh), h.dtype), mesh=mesh,
                           scratch_types=[], name="sc_dispatch")
        def scatter_rows(x_hbm, i0_hbm, i1_hbm, i2_hbm, i3_hbm, o_hbm):
            def body(x_vmem, *idx_vmem):
                for iv in idx_vmem:
                    pltpu.sync_copy(x_vmem, o_hbm.at[iv.at[0]])

            pltpu.emit_pipeline(
                body, grid=(t // win,),
                in_specs=[pl.BlockSpec((win, dh), lambda i: (i, j))]
                         + [pl.BlockSpec((1, win), lambda i: (0, i))] * TOP_K,
                out_specs=[],
                core_axis_name=("core", "subcore"),
                dimension_semantics=(pltpu.PARALLEL,),
            )(x_hbm, i0_hbm, i1_hbm, i2_hbm, i3_hbm)

        return scatter_rows(h, *idx)

    return [scatter_cols(j) for j in range(SC_COL_SPLIT)]


def _moe_dispatch(top_i, t):
    bm = MOE_ROWS
    n_blocks = (t * TOP_K) // bm + N_EXPERTS
    onehot = jax.nn.one_hot(top_i, N_EXPERTS, dtype=jnp.int32)
    sel = jnp.sum(onehot, axis=1)
    csum = jnp.cumsum(sel, axis=0)
    counts = csum[-1]
    padded = ((counts + bm - 1) // bm) * bm
    pad_end = jnp.cumsum(padded)
    pad_start = pad_end - padded
    base = (csum - sel) + pad_start[None, :]
    dest = jnp.sum(onehot * base[:, None, :], axis=-1)
    blk_start = jnp.arange(n_blocks, dtype=jnp.int32) * bm
    blk_e = jnp.minimum(jnp.sum((pad_end[None, :] <= blk_start[:, None]).astype(jnp.int32), axis=1),
                        N_EXPERTS - 1)
    n_used = (pad_end[-1] // bm).astype(jnp.int32).reshape(1)
    return dest, n_blocks * bm, blk_e, n_used


def _final_body(x_ref, y0, y1, y2, y3, gt_ref, g_ref, *rest):
    o_ref = rest[-1]
    o_ref[...] = _rms(_combine(x_ref, (y0, y1, y2, y3), gt_ref), g_ref[...])


def _final(x, add, g):
    t = x.shape[0]
    tm = ROW_TILE
    parts, gates = add
    per_part = t // tm // len(parts)
    out = None
    for p, ys in enumerate(parts):
        off = p * per_part
        row_p = lambda n, off=off: pl.BlockSpec((tm, n), lambda i: (i + off, 0))
        row_y = pl.BlockSpec((tm, D_MODEL), lambda i: (i, 0))
        prev = [] if out is None else [out]
        out = pl.pallas_call(
            _final_body, grid=(per_part,),
            in_specs=[row_p(D_MODEL)] + [row_y] * TOP_K + [row_p(TOP_K), pl.BlockSpec(g.shape, lambda i: (0, 0))]
                     + [pl.BlockSpec(memory_space=pl.ANY)] * len(prev),
            out_specs=row_p(D_MODEL), out_shape=jax.ShapeDtypeStruct((t, D_MODEL), F32),
            input_output_aliases={2 + TOP_K + 1: 0} if prev else {},
            compiler_params=_cparams(("parallel",)), name="final_norm")(x, *ys, gates, g, *prev)
    return out


def kernel(x, norm1_g, w_in, pool_w, pool_scale, s5_lam_re, s5_lam_im, s5_log_step, s5_b_re, s5_b_im, s5_c_re, s5_c_im, s5_d, s5_w_glu, s5_b_glu, lru_conv_w, lru_conv_b, lru_w_a, lru_b_a, lru_w_x, lru_b_x, lru_lam, mix_gain, w_out, norm2_g, router_w, router_b, w_gate_up, b_gate_up, w_down, b_down, final_g):
    batch, seq, d = x.shape
    t = batch * seq
    depth = norm1_g.shape[0]
    xt = x.reshape(t, d)
    add = None
    o1, o2 = D_POOL, D_POOL + D_S5
    b_gate = b_gate_up[:, :, None, 0::2]
    b_up = b_gate_up[:, :, None, 1::2]
    b_dn = b_down[:, :, None, :]
    for l in range(depth):
        w_rest = jnp.concatenate([w_in[l][:, :o1], w_in[l][:, o2:]], axis=1).astype(BF16)
        w_s5t = w_in[l][:, o1:o2].T.astype(BF16)
        s5_tabs = _s5_tables(s5_lam_re[l], s5_lam_im[l], s5_log_step[l], s5_b_re[l], s5_b_im[l],
                             s5_c_re[l], s5_c_im[l], seq // S5_CHUNK)
        lru_wg, lru_bg, lru_sp = _lru_params(lru_w_a[l], lru_b_a[l], lru_w_x[l], lru_b_x[l], lru_lam[l])
        if add is None:
            za, zbt, zc, zg = _inproj(xt, None, norm1_g[l][None, :], w_rest, w_s5t)
        else:
            xt, za, zbt, zc, zg = _inproj(xt, add, norm1_g[l][None, :], w_rest, w_s5t)
        oa = _pool_mixer(za, batch, seq, _block_diag(pool_w[l]).astype(BF16), pool_scale[l][None, :],
                         mix_gain[l][None, :o1])
        yt = _s5_core(zbt, batch, seq, s5_tabs)
        oc = _lru_mixer(zc, zg, batch, seq, lru_conv_w[l], lru_conv_b[l], lru_wg, lru_bg, lru_sp,
                        mix_gain[l][o2:])
        xt, h2, top_i, gates = _mixout(
            xt, oa, yt, zbt, oc, s5_d[l], s5_w_glu[l].astype(BF16), s5_b_glu[l][None, :],
            mix_gain[l][None, o1:o2], w_out[l].astype(BF16), norm2_g[l][None, :], _router_split(router_w[l]),
            router_b[l])
        top_i, gates = top_i.T, gates.T
        dest, n_rows, blk_e, n_used = _moe_dispatch(top_i, t)
        xs = _sc_dispatch(h2, dest, n_rows)
        x_bound = math.sqrt(D_MODEL) * jnp.max(jnp.abs(norm2_g[l]))
        ys = _expert_mlp(l, xs, x_bound, blk_e, n_used, w_gate_up, b_gate, b_up, w_down, b_dn)
        tp = t // TOKEN_PARTS
        add = ([[ys.at[dest[p * tp:(p + 1) * tp, k]].get(mode="promise_in_bounds") for k in range(TOP_K)]
                for p in range(TOKEN_PARTS)], gates)
    out = _final(xt, add, final_g[None, :])
    return out.reshape(batch, seq, d)
```

```python
import functools
import math

import jax
import jax.numpy as jnp
from jax import lax
from jax.experimental import pallas as pl
from jax.experimental.pallas import tpu as pltpu
from jax.experimental.pallas import tpu_sc as plsc

F32 = jnp.float32
BF16 = jnp.bfloat16
FP8 = jnp.float8_e4m3fn
FP8_TARGET = 416.0
FP8_TINY = 1e-30

D_MODEL = 1024
D_POOL = 256
D_S5 = 384
D_LRU = 384
POOL_WINDOWS = (2, 4, 8, 16)
POOL_GROUP = 64
S5_GROUP = 16
S5_NGROUPS = 24
S5_STATE = 64
LRU_HEADS = 6
LRU_HEAD_DIM = 64
RG_C = 8.0
N_EXPERTS = 32
TOP_K = 4
D_FF = 1024
SWIGLU_LIMIT = 7.0
SWIGLU_ALPHA = 1.702
EPS = 1e-5

ROW_TILE = 512
ROW_SPLIT = 2
S5_CHUNK = 128
POOL_TILE = 512
LRU_TILE = 256
HALO = 8
MOE_ROWS = 1024
TOKEN_PARTS = 1
SC_WINDOW = 128
SC_COL_SPLIT = 2
MXU_TILE = 256
EXPERT_VMEM_LIMIT = 56 * 1024 * 1024
VMEM_LIMIT = 48 * 1024 * 1024


def _cparams(sem):
    return pltpu.CompilerParams(dimension_semantics=sem, vmem_limit_bytes=VMEM_LIMIT)


def _rms(x, g):
    return x * lax.rsqrt(jnp.mean(x * x, axis=-1, keepdims=True) + EPS) * g


def _gelu(x):
    return 0.5 * x * (1.0 + jnp.tanh(0.7978845608028654 * (x + 0.044715 * (x * x * x))))


def _sigmoid(x):
    return 1.0 / (1.0 + jnp.exp(-x))


def _combine(x_ref, y_refs, gt_ref):
    x = x_ref[...]
    gt = gt_ref[...]
    for k in range(TOP_K):
        x = x + gt[:, k:k + 1] * y_refs[k][...].astype(F32)
    return x


def _inproj_body(has_add, *refs):
    if has_add:
        x_ref, y0, y1, y2, y3, gt_ref, g_ref, w_ref, wbt_ref = refs[:9]
        xo_ref, za, zbt, zc, zg = refs[-5:]
        x = _combine(x_ref, (y0, y1, y2, y3), gt_ref)
        xo_ref[...] = x
    else:
        x_ref, g_ref, w_ref, wbt_ref, za, zbt, zc, zg = refs
        x = x_ref[...]
    h = _rms(x, g_ref[...]).astype(BF16)
    z = jnp.dot(h, w_ref[...], preferred_element_type=F32)
    za[...] = z[:, :D_POOL]
    zc[...] = z[:, D_POOL:D_POOL + D_LRU]
    zg[...] = z[:, D_POOL + D_LRU:]
    zt = lax.dot_general(wbt_ref[...], h, (((1,), (1,)), ((), ())), preferred_element_type=F32)
    per_step = zt.shape[1] // S5_CHUNK
    for n in range(per_step):
        zbt[:, pl.program_id(1) * per_step + n, :] = zt[:, n * S5_CHUNK:(n + 1) * S5_CHUNK]


def _inproj(x, add, g, w_rest, w_s5t):
    t = x.shape[0]
    tm = ROW_TILE
    nc = tm * ROW_SPLIT // S5_CHUNK
    row = lambda n: pl.BlockSpec((tm, n), lambda i, j: (i * ROW_SPLIT + j, 0))
    full = lambda a: pl.BlockSpec(a.shape, lambda i, j: (0,) * a.ndim)
    z_shapes = [jax.ShapeDtypeStruct((t, D_POOL), F32), jax.ShapeDtypeStruct((D_S5, t // S5_CHUNK, S5_CHUNK), F32),
                jax.ShapeDtypeStruct((t, D_LRU), F32), jax.ShapeDtypeStruct((t, D_LRU), F32)]
    z_specs = [row(D_POOL), pl.BlockSpec((D_S5, nc, S5_CHUNK), lambda i, j: (0, i, 0)), row(D_LRU), row(D_LRU)]
    n_outer = t // (tm * ROW_SPLIT)
    if add is None:
        return pl.pallas_call(
            functools.partial(_inproj_body, False),
            grid=(n_outer, ROW_SPLIT), in_specs=[row(D_MODEL), full(g), full(w_rest), full(w_s5t)],
            out_specs=z_specs, out_shape=z_shapes,
            compiler_params=_cparams(("parallel", "arbitrary")), name="inproj")(x, g, w_rest, w_s5t)
    parts, gates = add
    per_part = n_outer // len(parts)
    out_shape = [jax.ShapeDtypeStruct((t, D_MODEL), F32)] + z_shapes
    outs = []
    for p, ys in enumerate(parts):
        off = p * per_part
        row_p = lambda n, off=off: pl.BlockSpec((tm, n), lambda i, j: ((i + off) * ROW_SPLIT + j, 0))
        row_y = pl.BlockSpec((tm, D_MODEL), lambda i, j: (i * ROW_SPLIT + j, 0))
        zbt_p = pl.BlockSpec((D_S5, nc, S5_CHUNK), lambda i, j, off=off: (0, i + off, 0))
        n_in = 1 + TOP_K + 4
        outs = pl.pallas_call(
            functools.partial(_inproj_body, True),
            grid=(per_part, ROW_SPLIT),
            in_specs=[row_p(D_MODEL)] + [row_y] * TOP_K + [row_p(TOP_K), full(g), full(w_rest), full(w_s5t)]
                     + [pl.BlockSpec(memory_space=pl.ANY)] * len(outs),
            out_specs=[row_p(D_MODEL), row_p(D_POOL), zbt_p, row_p(D_LRU), row_p(D_LRU)],
            out_shape=out_shape,
            input_output_aliases={n_in + k: k for k in range(len(outs))},
            compiler_params=_cparams(("parallel", "arbitrary")), name="inproj")(
                x, *ys, gates, g, w_rest, w_s5t, *outs)
    return outs


def _pool_body(seq, prev_ref, cur_ref, next_ref, w_ref, sc_ref, gain_ref, o_ref, u_s, s2_s, s4_s, s8_s):
    tl = cur_ref.shape[0]
    i = pl.program_id(1)
    nt = pl.num_programs(1)
    zero8 = jnp.zeros((HALO, D_POOL), F32)
    for buf in (u_s, s2_s, s4_s, s8_s):
        buf[0:HALO, :] = zero8
        buf[tl + 3 * HALO:tl + 4 * HALO, :] = zero8
    u_s[HALO:2 * HALO, :] = jnp.where(i > 0, prev_ref[...], 0.0)
    u_s[2 * HALO:2 * HALO + tl, :] = cur_ref[...]
    u_s[2 * HALO + tl:3 * HALO + tl, :] = jnp.where(i < nt - 1, next_ref[...], 0.0)
    r = tl + 2 * HALO
    s2_s[HALO:HALO + r, :] = u_s[HALO - 1:HALO - 1 + r, :] + u_s[HALO:HALO + r, :]
    s4_s[HALO:HALO + r, :] = s2_s[HALO - 1:HALO - 1 + r, :] + s2_s[HALO + 1:HALO + 1 + r, :]
    s8_s[HALO:HALO + r, :] = s4_s[HALO - 2:HALO - 2 + r, :] + s4_s[HALO + 2:HALO + 2 + r, :]
    o = 2 * HALO
    s16 = s8_s[o - 4:o - 4 + tl, :] + s8_s[o + 4:o + 4 + tl, :]
    s8 = s8_s[o:o + tl, :]
    s4 = s4_s[o:o + tl, :]
    s2 = s2_s[o:o + tl, :]
    u = u_s[o:o + tl, :]
    lane = lax.broadcasted_iota(jnp.int32, (tl, D_POOL), 1)
    tpos = lax.broadcasted_iota(jnp.int32, (tl, D_POOL), 0) + i * tl
    g0, g1, g2 = lane < POOL_GROUP, lane < 2 * POOL_GROUP, lane < 3 * POOL_GROUP
    half = jnp.where(g0, 1, jnp.where(g1, 2, jnp.where(g2, 4, 8)))
    wsum = jnp.where(g0, s2, jnp.where(g1, s4, jnp.where(g2, s8, s16)))
    cnt = (jnp.minimum(tpos + half, seq) - jnp.maximum(tpos - half, 0)).astype(F32)
    d = wsum / cnt - u
    y = jnp.dot(d.astype(BF16), w_ref[...], preferred_element_type=F32) * sc_ref[...]
    o_ref[...] = _rms(y, gain_ref[...])


def _pool_mixer(za, batch, seq, w_bd, scale, gain):
    t = za.shape[0]
    tl = POOL_TILE
    nt = seq // tl
    hb = tl // HALO
    nhb = t // HALO
    cur = pl.BlockSpec((tl, D_POOL), lambda b, i: (b * nt + i, 0))
    prev = pl.BlockSpec((HALO, D_POOL), lambda b, i: (jnp.maximum((b * nt + i) * hb - 1, 0), 0))
    nxt = pl.BlockSpec((HALO, D_POOL), lambda b, i: (jnp.minimum((b * nt + i + 1) * hb, nhb - 1), 0))
    full = lambda a: pl.BlockSpec(a.shape, lambda b, i: (0,) * a.ndim)
    return pl.pallas_call(
        functools.partial(_pool_body, seq),
        grid=(batch, nt),
        in_specs=[prev, cur, nxt, full(w_bd), full(scale), full(gain)],
        out_specs=cur,
        out_shape=jax.ShapeDtypeStruct((t, D_POOL), F32),
        scratch_shapes=[pltpu.VMEM((tl + 4 * HALO, D_POOL), F32)] * 4,
        compiler_params=_cparams(("parallel", "parallel")), name="pool_mixer")(za, za, za, w_bd, scale, gain)


def _s5_scan_steps(n_chunks):
    return max(1, int(math.ceil(math.log2(n_chunks))))


def _s5_tables(lam_re, lam_im, log_step, b_re, b_im, c_re, c_im, n_chunks):
    L, C, P, G = S5_CHUNK, S5_GROUP, S5_STATE, S5_NGROUPS
    hp = lax.Precision.HIGHEST
    lr = lam_re.astype(F32)
    li = lam_im.astype(F32)
    dt = jnp.exp(log_step.astype(F32))[..., None]
    ar = lr * dt
    ai = li * dt
    jj = jnp.arange(L + 1, dtype=F32)

    def powers(j, a_r, a_i):
        mag = jnp.exp(j * a_r)
        return mag * jnp.cos(j * a_i), mag * jnp.sin(j * a_i)

    e_r, e_i = powers(jj[None, None, :, None], ar[:, :, None, :], ai[:, :, None, :])
    et_r, et_i = powers(jj[None, None, None, :], ar[..., None], ai[..., None])
    den = lr * lr + li * li
    nr = e_r[:, :, 1, :] - 1.0
    ni = e_i[:, :, 1, :]
    q_r = (nr * lr + ni * li) / den
    q_i = (ni * lr - nr * li) / den
    br = b_re.astype(F32)
    bi = b_im.astype(F32)
    bbt_r = (q_r[..., None] * br - q_i[..., None] * bi).transpose(0, 1, 3, 2)
    bbt_i = (q_r[..., None] * bi + q_i[..., None] * br).transpose(0, 1, 3, 2)
    cr = c_re.astype(F32)
    ci = c_im.astype(F32)
    m_r = cr[:, :, None] * bbt_r[:, :, :, None] - ci[:, :, None] * bbt_i[:, :, :, None]
    m_i = cr[:, :, None] * bbt_i[:, :, :, None] + ci[:, :, None] * bbt_r[:, :, :, None]
    kk = (jnp.einsum('dgxcp,dgpj->dgxcj', m_r, et_r[..., :L], precision=hp)
          - jnp.einsum('dgxcp,dgpj->dgxcj', m_i, et_i[..., :L], precision=hp))
    kf, kb = kk[0], kk[1]
    lagtab = jnp.concatenate([kb[..., :0:-1], kf[..., :1] + kb[..., :1], kf[..., 1:],
                              jnp.zeros_like(kf[..., :1])], axis=-1)
    bits = lax.bitcast_convert_type(lagtab.astype(BF16).astype(F32), jnp.uint32).reshape(G, C * C, 2 * L)
    lagtab = lax.shift_right_logical(bits[..., :L], jnp.uint32(16)) | (bits[..., L:] & jnp.uint32(0xFFFF0000))
    fb = lambda f, b, axis: jnp.concatenate([f, b], axis=axis)
    pw_s = jnp.stack([fb(e_r[0, :, L - 1::-1], e_r[1, :, :L], -1),
                      fb(e_i[0, :, L - 1::-1], e_i[1, :, :L], -1)], axis=1)
    bb = jnp.stack([fb(bbt_r[0], bbt_r[1], -1), fb(bbt_i[0], bbt_i[1], -1)], axis=1)
    pw_t = jnp.stack([fb(et_r[0, :, :, 1:L + 1], et_r[1, :, :, L:0:-1], 1),
                      fb(et_i[0, :, :, 1:L + 1], et_i[1, :, :, L:0:-1], 1)], axis=1)
    crt = cr.transpose(0, 1, 3, 2)
    cit = ci.transpose(0, 1, 3, 2)
    cc = jnp.stack([fb(crt[0], crt[1], 1), fb(cit[0], cit[1], 1)], axis=1)
    n_steps = _s5_scan_steps(n_chunks)
    kpow =(L * (2 ** jnp.arange(n_steps, dtype=F32)))[None, None, :, None]
    p_r, p_i = powers(kpow, ar[:, :, None, :], ai[:, :, None, :])
    pad = jnp.zeros((G, (-n_steps) % 8, 2 * P), F32)
    p_r = jnp.concatenate([jnp.concatenate([p_r[0], p_r[1]], axis=-1), pad], axis=1)
    p_i = jnp.concatenate([jnp.concatenate([p_i[0], p_i[1]], axis=-1), pad], axis=1)
    ptab = jnp.concatenate([p_r, p_i], axis=1)
    return lagtab, pw_s, bb, pw_t, cc, ptab


def _s5_body(n_chunks, n_steps, u_ref, lag_ref, pws_ref, bb_ref, pwt_ref, cc_ref, p_ref, y_ref,
             toep_s, w_s, v_s):
    L, C = S5_CHUNK, S5_GROUP
    n = u_ref.shape[1]
    two_p = 2 * S5_STATE
    e_r, e_i = pws_ref[0, 0], pws_ref[0, 1]
    et_r, et_i = pwt_ref[0, 0], pwt_ref[0, 1]
    c_r, c_i = cc_ref[0, 0], cc_ref[0, 1]
    for c in range(C):
        b_r = bb_ref[0, 0, c:c + 1, :]
        b_i = bb_ref[0, 1, c:c + 1, :]
        w_s[c * L:(c + 1) * L, :two_p] = (e_r * b_r - e_i * b_i).astype(BF16)
        w_s[c * L:(c + 1) * L, two_p:] = (e_r * b_i + e_i * b_r).astype(BF16)
        k_r = c_r[:, c:c + 1]
        k_i = c_i[:, c:c + 1]
        v_s[:two_p, c * L:(c + 1) * L] = (k_r * et_r - k_i * et_i).astype(BF16)
        v_s[two_p:, c * L:(c + 1) * L] = (-(k_r * et_i + k_i * et_r)).astype(BF16)
    s_i = lax.broadcasted_iota(jnp.int32, (L, L), 0)
    t_i = lax.broadcasted_iota(jnp.int32, (L, L), 1)
    non_positive_lag = t_i <= s_i

    def build_block_row(cp, carry):
        r0 = pl.multiple_of(cp * L, L)
        for c in range(C):
            k = lag_ref[0, pl.ds(cp * C + c, 1), :]
            r = pltpu.roll(jnp.broadcast_to(k, (L, L)), 1, 1, stride=1, stride_axis=0)
            bits = jnp.where(non_positive_lag, lax.shift_left(r, jnp.uint32(16)), r & jnp.uint32(0xFFFF0000))
            toep_s[pl.ds(r0, L), c * L:(c + 1) * L] = pltpu.bitcast(bits, F32).astype(BF16)
        return carry

    lax.fori_loop(0, C, build_block_row, 0)
    u = jnp.concatenate([u_ref[c] for c in range(C)], axis=1).astype(BF16)
    y = jnp.dot(u, toep_s[...], preferred_element_type=F32)
    s = jnp.dot(u, w_s[...], preferred_element_type=F32)
    xr = s[:, :two_p]
    xi = s[:, two_p:]
    row = lax.broadcasted_iota(jnp.int32, (n, two_p), 0) % n_chunks
    is_fwd = lax.broadcasted_iota(jnp.int32, (n, two_p), 1) < S5_STATE
    pim0 = p_ref.shape[1] // 2

    def shifted(a, k):
        down = jnp.where(row >= k, pltpu.roll(a, k, 0), 0.0)
        up = jnp.where(row < n_chunks - k, pltpu.roll(a, n - k, 0), 0.0)
        return jnp.where(is_fwd, down, up)

    for j in range(n_steps):
        k = 1 << j
        pr = p_ref[0, j:j + 1, :]
        pi = p_ref[0, pim0 + j:pim0 + j + 1, :]
        sr = shifted(xr, k)
        si = shifted(xi, k)
        xr, xi = xr + pr * sr - pi * si, xi + pr * si + pi * sr
    carry = jnp.concatenate([shifted(xr, 1), shifted(xi, 1)], axis=1).astype(BF16)
    y = y + jnp.dot(carry, v_s[...], preferred_element_type=F32)
    for c in range(C):
        y_ref[c] = y[:, c * L:(c + 1) * L]


def _s5_core(zbt, batch, seq, tables):
    L = S5_CHUNK
    n_chunks = seq // L
    n_steps = _s5_scan_steps(n_chunks)
    n = batch * n_chunks
    per_g = lambda a: pl.BlockSpec((1,) + a.shape[1:], lambda g: (g,) + (0,) * (a.ndim - 1))
    grp = pl.BlockSpec((S5_GROUP, n, L), lambda g: (g, 0, 0))
    return pl.pallas_call(
        functools.partial(_s5_body, n_chunks, n_steps),
        grid=(S5_NGROUPS,),
        in_specs=[grp] + [per_g(a) for a in tables],
        out_specs=grp,
        out_shape=jax.ShapeDtypeStruct((D_S5, n, L), F32),
        scratch_shapes=[pltpu.VMEM((S5_GROUP * L, S5_GROUP * L), BF16),
                        pltpu.VMEM((S5_GROUP * L, 4 * S5_STATE), BF16),
                        pltpu.VMEM((4 * S5_STATE, S5_GROUP * L), BF16)],
        compiler_params=_cparams(("parallel",)), name="s5_core")(zbt, *tables)


def _lru_body(reverse, prev_ref, cur_ref, next_ref, cw_ref, cb_ref, wg_ref, bg_ref, sp_ref, *rest):
    if reverse:
        hf_ref, gate_ref, gain_ref, o_ref, a_s, b_s, h_s, carry_s = rest
    else:
        o_ref, a_s, b_s, h_s, carry_s = rest
    nb, tl, _ = cur_ref.shape
    step = pl.program_id(0)
    nt = pl.num_programs(0)
    ti = nt - 1 - step if reverse else step

    @pl.when(step == 0)
    def _():
        carry_s[...] = jnp.zeros_like(carry_s)

    cw = cw_ref[...]
    k_row = sp_ref[...] * (-0.5 * RG_C * math.log2(math.e))
    sub = lax.broadcasted_iota(jnp.int32, (HALO, D_LRU), 0)

    def shifted(x, d, edge):
        y = pltpu.roll(x, d % tl, 0)
        if d > 0:
            return jnp.concatenate([jnp.where(sub < d, pltpu.roll(edge, d, 0), y[:HALO]), y[HALO:]], axis=0)
        return jnp.concatenate([y[:-HALO], jnp.where(sub >= HALO + d, pltpu.roll(edge, HALO + d, 0), y[-HALO:])],
                               axis=0)

    for b in range(nb):
        x = cur_ref[b]
        before = jnp.where(ti > 0, prev_ref[b], 0.0)
        after = jnp.where(ti < nt - 1, next_ref[b], 0.0)
        xc = (cb_ref[...] + cw[0:1, :] * shifted(x, 1, before) + cw[1:2, :] * x
              + cw[2:3, :] * shifted(x, -1, after) + cw[3:4, :] * shifted(x, -2, after))
        gates = jnp.dot(xc.astype(BF16), wg_ref[...], preferred_element_type=F32) + bg_ref[...]
        a = jnp.exp2(k_row * (jnp.tanh(0.5 * gates[:, :D_LRU]) + 1.0))
        ig = 0.5 * jnp.tanh(0.5 * gates[:, D_LRU:]) + 0.5
        a_s[b] = a
        om = 1.0 - a * a
        b_s[b] = (om * lax.rsqrt(jnp.maximum(om, 1e-30))) * (ig * xc)

    def scan_step(s, hs):
        t = tl - 1 - s if reverse else s
        out = []
        for b in range(nb):
            h = a_s[b, pl.ds(t, 1), :] * hs[b] + b_s[b, pl.ds(t, 1), :]
            h_s[b, pl.ds(t, 1), :] = h
            out.append(h)
        return tuple(out)

    hs = lax.fori_loop(0, tl, scan_step, tuple(carry_s[b:b + 1, :] for b in range(nb)), unroll=8)
    for b in range(nb):
        carry_s[b:b + 1, :] = hs[b]

    if reverse:
        for b in range(nb):
            y = (hf_ref[b] + h_s[b]) * _gelu(gate_ref[b])
            o_ref[b] = _rms(y, gain_ref[...])
    else:
        o_ref[...] = h_s[...]


def _lru_pass(reverse, zc3, params, extra):
    nb, seq, _ = zc3.shape
    tl = LRU_TILE
    nt = seq // tl
    hb = tl // HALO
    nhb = seq // HALO
    tix = (lambda i: nt - 1 - i) if reverse else (lambda i: i)
    cur = pl.BlockSpec((nb, tl, D_LRU), lambda i: (0, tix(i), 0))
    prev = pl.BlockSpec((nb, HALO, D_LRU), lambda i: (0, jnp.maximum(tix(i) * hb - 1, 0), 0))
    nxt = pl.BlockSpec((nb, HALO, D_LRU), lambda i: (0, jnp.minimum((tix(i) + 1) * hb, nhb - 1), 0))
    full = lambda a: pl.BlockSpec(a.shape, lambda i: (0,) * a.ndim)
    ins = [zc3, zc3, zc3, *params]
    in_specs = [prev, cur, nxt] + [full(a) for a in params]
    if reverse:
        hf, gate, gain = extra
        ins += [hf, gate, gain]
        in_specs += [cur, cur, full(gain)]
    return pl.pallas_call(
        functools.partial(_lru_body, reverse),
        grid=(nt,), in_specs=in_specs, out_specs=cur,
        out_shape=jax.ShapeDtypeStruct((nb, seq, D_LRU), F32),
        scratch_shapes=[pltpu.VMEM((nb, tl, D_LRU), F32),
                        pltpu.VMEM((nb, tl, D_LRU), F32),
                        pltpu.VMEM((nb, tl, D_LRU), F32),
                        pltpu.VMEM((8, D_LRU), F32)],
        compiler_params=_cparams(("arbitrary",)),
        name="lru_bwd" if reverse else "lru_fwd")(*ins)


def _block_diag(w):
    h, d, _ = w.shape
    eye = jnp.eye(h, dtype=w.dtype)
    return (eye[:, None, :, None] * w[:, :, None, :]).reshape(h * d, h * d)


def _lru_params(w_a, b_a, w_x, b_x, lam):
    bd = jax.vmap(_block_diag)
    wg = jnp.concatenate([bd(w_a), bd(w_x)], axis=-1).astype(BF16)
    bg = jnp.concatenate([b_a, b_x], axis=-1)[:, None, :]
    sp = jax.nn.softplus(-lam.astype(F32))[:, None, :]
    return wg, bg, sp


def _lru_mixer(zc, zg, batch, seq, conv_w, conv_b, wg, bg, sp, gain):
    zc3 = zc.reshape(batch, seq, D_LRU)
    zg3 = zg.reshape(batch, seq, D_LRU)
    outs = None
    for d in (0, 1):
        params = [conv_w, conv_b[None, :], wg[d], bg[d], sp[d]]
        extra = None if d == 0 else (outs, zg3, gain[None, :])
        outs = _lru_pass(d == 1, zc3, params, extra)
    return outs.reshape(batch * seq, D_LRU)


def _mixout_body(x_ref, oa_ref, y3_ref, u3_ref, oc_ref, sd_ref, wglu_ref, bglu_ref, gb_ref,
                 wout_ref, g2_ref, rw_ref, rb_ref, xo_ref, h_ref, ti_ref, gt_ref):
    tm = x_ref.shape[0]
    per_step = tm // S5_CHUNK
    c0 = pl.program_id(1) * per_step
    yt = jnp.concatenate([y3_ref[:, c0 + n, :] for n in range(per_step)], axis=1)
    ut = jnp.concatenate([u3_ref[:, c0 + n, :] for n in range(per_step)], axis=1)
    vt = _gelu(yt + sd_ref[...] * ut).astype(BF16)
    gl = lax.dot_general(vt, wglu_ref[...], (((0,), (0,)), ((), ())), preferred_element_type=F32) + bglu_ref[...]
    ob = _rms(gl[:, :D_S5] * _sigmoid(gl[:, D_S5:]), gb_ref[...])
    o = jnp.concatenate([oa_ref[...], ob, oc_ref[...]], axis=1).astype(BF16)
    x = x_ref[...] + jnp.dot(o, wout_ref[...], preferred_element_type=F32)
    xo_ref[...] = x
    h = _rms(x, g2_ref[...])
    h_hi = h.astype(BF16)
    bits = pltpu.bitcast(h_hi.astype(F32), jnp.uint32)
    h_ref[...] = lax.shift_right_logical(bits[:, :D_MODEL // 2], jnp.uint32(16)) | (
        bits[:, D_MODEL // 2:] & jnp.uint32(0xFFFF0000))
    h_lo = (h - h_hi.astype(F32)).astype(BF16)
    nt_dims = (((1,), (1,)), ((), ()))
    la = lax.dot_general(rw_ref[...], h_hi, nt_dims, preferred_element_type=F32)
    lb = lax.dot_general(rw_ref[0:N_EXPERTS, :], h_lo, nt_dims, preferred_element_type=F32)
    logits = la[:N_EXPERTS] + la[N_EXPERTS:] + lb + rb_ref[...]
    eidx = lax.broadcasted_iota(jnp.int32, (N_EXPERTS, tm), 0).astype(F32)
    idx_rows, val_rows = [], []
    for k in range(TOP_K):
        m = jnp.max(logits, axis=0, keepdims=True)
        idx = jnp.min(jnp.where(logits == m, eidx, float(N_EXPERTS)), axis=0, keepdims=True)
        idx_rows.append(idx)
        val_rows.append(jnp.exp(m - val_rows[0]) if k else m)
        logits = jnp.where(eidx == idx, -jnp.inf, logits)
    top = val_rows[0]
    vals = jnp.concatenate([jnp.ones_like(top)] + [v for v in val_rows[1:]], axis=0)
    ti_ref[...] = jnp.concatenate(idx_rows, axis=0).astype(jnp.int32)
    gt_ref[...] = vals / jnp.sum(vals, axis=0, keepdims=True)


def _router_split(rw):
    rw_hi = rw.astype(BF16)
    rw_lo = (rw - rw_hi.astype(F32)).astype(BF16)
    return jnp.concatenate([jnp.swapaxes(rw_hi, -1, -2), jnp.swapaxes(rw_lo, -1, -2)], axis=-2)


def _mixout(x, oa, yt3, zbt3, oc, s5_d, wglu, bglu, gain_b, wout, g2, rw2, rb):
    t = x.shape[0]
    tm = ROW_TILE
    nc = tm * ROW_SPLIT // S5_CHUNK
    row = lambda n: pl.BlockSpec((tm, n), lambda i, j: (i * ROW_SPLIT + j, 0))
    col = pl.BlockSpec((TOP_K, tm), lambda i, j: (0, i * ROW_SPLIT + j))
    chunks = pl.BlockSpec((D_S5, nc, S5_CHUNK), lambda i, j: (0, i, 0))
    full = lambda a: pl.BlockSpec(a.shape, lambda i, j: (0,) * a.ndim)
    params = [s5_d.reshape(D_S5, 1), wglu, bglu, gain_b, wout, g2, rw2, rb.reshape(N_EXPERTS, 1)]
    return pl.pallas_call(
        _mixout_body,
        grid=(t // (tm * ROW_SPLIT), ROW_SPLIT),
        in_specs=[row(D_MODEL), row(D_POOL), chunks, chunks, row(D_LRU)] + [full(a) for a in params],
        out_specs=[row(D_MODEL), row(D_MODEL // 2), col, col],
        out_shape=[jax.ShapeDtypeStruct((t, D_MODEL), F32), jax.ShapeDtypeStruct((t, D_MODEL // 2), jnp.uint32),
                   jax.ShapeDtypeStruct((TOP_K, t), jnp.int32), jax.ShapeDtypeStruct((TOP_K, t), F32)],
        compiler_params=_cparams(("parallel", "arbitrary")), name="mix_out")(x, oa, yt3, zbt3, oc, *params)


def _expert_body(be_ref, nu_ref, first_ref, xl_ref, xh_ref, wgu_ref, bg_ref, bu_ref, wd_ref, bd_ref, p_ref, xs_ref,
                 y_ref, wg_s, wu_s, wd_s, ws_s):
    i = pl.program_id(0)
    live = i < nu_ref[0]
    x_scale = xs_ref[...]
    act_scale = FP8_TARGET / ((SWIGLU_LIMIT + 1.0) * SWIGLU_LIMIT)

    @pl.when(jnp.logical_and(live, first_ref[i] == 1))
    def _():
        p = p_ref[...]
        half = MXU_TILE // 2
        n_groups = wgu_ref.shape[3] // MXU_TILE
        amax = functools.reduce(jnp.maximum, [
            jnp.max(jnp.abs(wgu_ref[0, 0, :, MXU_TILE * j:MXU_TILE * (j + 1)]), axis=(0, 1), keepdims=True)
            for j in range(n_groups)])
        s_gu = FP8_TARGET / jnp.maximum(amax, FP8_TINY)
        for j in range(n_groups):
            w = wgu_ref[0, 0, :, MXU_TILE * j:MXU_TILE * (j + 1)].astype(BF16)
            r = jnp.dot(w, p, preferred_element_type=F32) * s_gu
            wg_s[:, half * j:half * (j + 1)] = r[:, :half].astype(FP8)
            wu_s[:, half * j:half * (j + 1)] = r[:, half:].astype(FP8)
        wd = wd_ref[0, 0]
        dmax = jnp.max(jnp.abs(wd), axis=(0, 1), keepdims=True)
        s_d = FP8_TARGET / jnp.maximum(dmax, FP8_TINY)
        wd_s[...] = (wd * s_d).astype(FP8)
        ws_s[0:1, :] = jnp.broadcast_to(1.0 / (s_gu * x_scale), (1, ws_s.shape[1]))
        ws_s[1:2, :] = jnp.broadcast_to(1.0 / (s_d * act_scale), (1, ws_s.shape[1]))

    @pl.when(live)
    def _():
        words = jnp.concatenate([xl_ref[...], xh_ref[...]], axis=1)
        lo = pltpu.bitcast(lax.shift_left(words, jnp.uint32(16)), F32)
        hi = pltpu.bitcast(words & jnp.uint32(0xFFFF0000), F32)
        x = (jnp.concatenate([lo, hi], axis=1) * x_scale).astype(FP8)
        gu_inv = ws_s[0:1, 0:1]
        g = jnp.dot(x, wg_s[...], preferred_element_type=F32) * gu_inv + bg_ref[0, 0]
        g = jnp.minimum(g, SWIGLU_LIMIT)
        u1 = (jnp.dot(x, wu_s[...], preferred_element_type=F32) * (gu_inv * act_scale)
              + (bu_ref[0, 0] + 1.0) * act_scale)
        u1 = jnp.clip(u1, (1.0 - SWIGLU_LIMIT) * act_scale, (1.0 + SWIGLU_LIMIT) * act_scale)
        glu = g / (1.0 + jnp.exp2(g * (-SWIGLU_ALPHA * math.log2(math.e))))
        act = (u1 * glu).astype(FP8)
        y = jnp.dot(act, wd_s[...], preferred_element_type=F32) * ws_s[1:2, 0:1] + bd_ref[0, 0]
        y_ref[...] = y.astype(y_ref.dtype)

    @pl.when(jnp.logical_not(live))
    def _():
        y_ref[...] = jnp.zeros_like(y_ref)


def _expert_mlp(layer, xs, x_bound, blk_e, n_used, w_gate_up, b_gate, b_up, w_down, b_down):
    n_rows, dh = xs[0].shape
    bm = MOE_ROWS
    x_scale = (FP8_TARGET / jnp.maximum(x_bound, FP8_TINY)).astype(F32).reshape(1, 1)
    first = jnp.concatenate([jnp.ones((1,), jnp.int32), (blk_e[1:] != blk_e[:-1]).astype(jnp.int32)])
    idx = jnp.arange(MXU_TILE)
    perm = jax.nn.one_hot(jnp.where(idx % 2 == 0, idx // 2, MXU_TILE // 2 + idx // 2), MXU_TILE, dtype=BF16)
    rows = pl.BlockSpec((bm, dh), lambda i, be, nu, fi: (jnp.minimum(i, nu[0] - 1), 0))
    per_e = lambda a: pl.BlockSpec((1, 1) + a.shape[2:], lambda i, be, nu, fi: (layer, be[i], 0, 0))
    grid_spec = pltpu.PrefetchScalarGridSpec(
        num_scalar_prefetch=3, grid=(n_rows // bm,),
        in_specs=[rows, rows, per_e(w_gate_up), per_e(b_gate), per_e(b_up), per_e(w_down), per_e(b_down),
                  pl.BlockSpec(perm.shape, lambda i, be, nu, fi: (0, 0)),
                  pl.BlockSpec((1, 1), lambda i, be, nu, fi: (0, 0))],
        out_specs=pl.BlockSpec((bm, D_MODEL), lambda i, be, nu, fi: (i, 0)),
        scratch_shapes=[pltpu.VMEM((D_MODEL, D_FF), FP8), pltpu.VMEM((D_MODEL, D_FF), FP8),
                        pltpu.VMEM((D_FF, D_MODEL), FP8), pltpu.VMEM((8, 128), F32)])
    return pl.pallas_call(
        _expert_body, grid_spec=grid_spec,
        out_shape=jax.ShapeDtypeStruct((n_rows, D_MODEL), BF16),
        compiler_params=pltpu.CompilerParams(dimension_semantics=("arbitrary",),
                                             vmem_limit_bytes=EXPERT_VMEM_LIMIT),
        name="expert_mlp")(blk_e, n_used, first, *xs, w_gate_up, b_gate, b_up, w_down, b_down, perm, x_scale)


def _sc_dispatch(h, dest, n_rows):
    t, d = h.shape
    win = SC_WINDOW
    dh = d // SC_COL_SPLIT
    idx = [dest[:, k].reshape(1, t) for k in range(TOP_K)]
    mesh = plsc.VectorSubcoreMesh(core_axis_name="core", subcore_axis_name="subcore")

    def scatter_cols(j):
        @functools.partial(pl.kernel, out_type=jax.ShapeDtypeStruct((n_rows, dh), h.dtype), mesh=mesh,
                           scratch_types=[], name="sc_dispatch")
        def scatter_rows(x_hbm, i0_hbm, i1_hbm, i2_hbm, i3_hbm, o_hbm):
            def body(x_vmem, *idx_vmem):
                for iv in idx_vmem:
                    pltpu.sync_copy(x_vmem, o_hbm.at[iv.at[0]])

            pltpu.emit_pipeline(
                body, grid=(t // win,),
                in_specs=[pl.BlockSpec((win, dh), lambda i: (i, j))]
                         + [pl.BlockSpec((1, win), lambda i: (0, i))] * TOP_K,
                out_specs=[],
                core_axis_name=("core", "subcore"),
                dimension_semantics=(pltpu.PARALLEL,),
            )(x_hbm, i0_hbm, i1_hbm, i2_hbm, i3_hbm)

        return scatter_rows(h, *idx)

    return [scatter_cols(j) for j in range(SC_COL_SPLIT)]


def _moe_dispatch(top_i, t):
    bm = MOE_ROWS
    n_blocks = (t * TOP_K) // bm + N_EXPERTS
    onehot = jax.nn.one_hot(top_i, N_EXPERTS, dtype=jnp.int32)
    sel = jnp.sum(onehot, axis=1)
    csum = jnp.cumsum(sel, axis=0)
    counts = csum[-1]
    padded = ((counts + bm - 1) // bm) * bm
    pad_end = jnp.cumsum(padded)
    pad_start = pad_end - padded
    base = (csum - sel) + pad_start[None, :]
    dest = jnp.sum(onehot * base[:, None, :], axis=-1)
    blk_start = jnp.arange(n_blocks, dtype=jnp.int32) * bm
    blk_e = jnp.minimum(jnp.sum((pad_end[None, :] <= blk_start[:, None]).astype(jnp.int32), axis=1),
                        N_EXPERTS - 1)
    n_used = (pad_end[-1] // bm).astype(jnp.int32).reshape(1)
    return dest, n_blocks * bm, blk_e, n_used


def _final_body(x_ref, y0, y1, y2, y3, gt_ref, g_ref, *rest):
    o_ref = rest[-1]
    o_ref[...] = _rms(_combine(x_ref, (y0, y1, y2, y3), gt_ref), g_ref[...])


def _final(x, add, g):
    t = x.shape[0]
    tm = ROW_TILE
    parts, gates = add
    per_part = t // tm // len(parts)
    out = None
    for p, ys in enumerate(parts):
        off = p * per_part
        row_p = lambda n, off=off: pl.BlockSpec((tm, n), lambda i: (i + off, 0))
        row_y = pl.BlockSpec((tm, D_MODEL), lambda i: (i, 0))
        prev = [] if out is None else [out]
        out = pl.pallas_call(
            _final_body, grid=(per_part,),
            in_specs=[row_p(D_MODEL)] + [row_y] * TOP_K + [row_p(TOP_K), pl.BlockSpec(g.shape, lambda i: (0, 0))]
                     + [pl.BlockSpec(memory_space=pl.ANY)] * len(prev),
            out_specs=row_p(D_MODEL), out_shape=jax.ShapeDtypeStruct((t, D_MODEL), F32),
            input_output_aliases={2 + TOP_K + 1: 0} if prev else {},
            compiler_params=_cparams(("parallel",)), name="final_norm")(x, *ys, gates, g, *prev)
    return out


def kernel(x, norm1_g, w_in, pool_w, pool_scale, s5_lam_re, s5_lam_im, s5_log_step, s5_b_re, s5_b_im, s5_c_re, s5_c_im, s5_d, s5_w_glu, s5_b_glu, lru_conv_w, lru_conv_b, lru_w_a, lru_b_a, lru_w_x, lru_b_x, lru_lam, mix_gain, w_out, norm2_g, router_w, router_b, w_gate_up, b_gate_up, w_down, b_down, final_g):
    batch, seq, d = x.shape
    t = batch * seq
    depth = norm1_g.shape[0]
    xt = x.reshape(t, d)
    add = None
    o1, o2 = D_POOL, D_POOL + D_S5
    b_gate = b_gate_up[:, :, None, 0::2]
    b_up = b_gate_up[:, :, None, 1::2]
    b_dn = b_down[:, :, None, :]
    for l in range(depth):
        w_rest = jnp.concatenate([w_in[l][:, :o1], w_in[l][:, o2:]], axis=1).astype(BF16)
        w_s5t = w_in[l][:, o1:o2].T.astype(BF16)
        s5_tabs = _s5_tables(s5_lam_re[l], s5_lam_im[l], s5_log_step[l], s5_b_re[l], s5_b_im[l],
                             s5_c_re[l], s5_c_im[l], seq // S5_CHUNK)
        lru_wg, lru_bg, lru_sp = _lru_params(lru_w_a[l], lru_b_a[l], lru_w_x[l], lru_b_x[l], lru_lam[l])
        if add is None:
            za, zbt, zc, zg = _inproj(xt, None, norm1_g[l][None, :], w_rest, w_s5t)
        else:
            xt, za, zbt, zc, zg = _inproj(xt, add, norm1_g[l][None, :], w_rest, w_s5t)
        oa = _pool_mixer(za, batch, seq, _block_diag(pool_w[l]).astype(BF16), pool_scale[l][None, :],
                         mix_gain[l][None, :o1])
        yt = _s5_core(zbt, batch, seq, s5_tabs)
        oc = _lru_mixer(zc, zg, batch, seq, lru_conv_w[l], lru_conv_b[l], lru_wg, lru_bg, lru_sp,
                        mix_gain[l][o2:])
        xt, h2, top_i, gates = _mixout(
            xt, oa, yt, zbt, oc, s5_d[l], s5_w_glu[l].astype(BF16), s5_b_glu[l][None, :],
            mix_gain[l][None, o1:o2], w_out[l].astype(BF16), norm2_g[l][None, :], _router_split(router_w[l]),
            router_b[l])
        top_i, gates = top_i.T, gates.T
        dest, n_rows, blk_e, n_used = _moe_dispatch(top_i, t)
        xs = _sc_dispatch(h2, dest, n_rows)
        x_bound = math.sqrt(D_MODEL) * jnp.max(jnp.abs(norm2_g[l]))
        ys = _expert_mlp(l, xs, x_bound, blk_e, n_used, w_gate_up, b_gate, b_up, w_down, b_dn)
        tp = t // TOKEN_PARTS
        add = ([[ys.at[dest[p * tp:(p + 1) * tp, k]].get(mode="promise_in_bounds") for k in range(TOP_K)]
                for p in range(TOKEN_PARTS)], gates)
    out = _final(xt, add, final_g[None, :])
    return out.reshape(batch, seq, d)
```

```python
import functools
import math

import jax
import jax.numpy as jnp
from jax import lax
from jax.experimental import pallas as pl
from jax.experimental.pallas import tpu as pltpu
from jax.experimental.pallas import tpu_sc as plsc

F32 = jnp.float32
BF16 = jnp.bfloat16
FP8 = jnp.float8_e4m3fn
FP8_TARGET = 416.0
FP8_TINY = 1e-30

D_MODEL = 1024
D_POOL = 256
D_S5 = 384
D_LRU = 384
POOL_WINDOWS = (2, 4, 8, 16)
POOL_GROUP = 64
S5_GROUP = 16
S5_NGROUPS = 24
S5_STATE = 64
LRU_HEADS = 6
LRU_HEAD_DIM = 64
RG_C = 8.0
N_EXPERTS = 32
TOP_K = 4
D_FF = 1024
SWIGLU_LIMIT = 7.0
SWIGLU_ALPHA = 1.702
EPS = 1e-5

ROW_TILE = 512
MIX_TILE = 1024
ROW_SPLIT = 2
S5_CHUNK = 128
POOL_TILE = 512
LRU_TILE = 256
HALO = 8
MOE_ROWS = 1024
TOKEN_PARTS = 1
SC_WINDOW = 128
SC_COL_SPLIT = 2
MXU_TILE = 256
EXPERT_VMEM_LIMIT = 56 * 1024 * 1024
VMEM_LIMIT = 48 * 1024 * 1024


def _cparams(sem):
    return pltpu.CompilerParams(dimension_semantics=sem, vmem_limit_bytes=VMEM_LIMIT)


def _rms(x, g):
    return x * lax.rsqrt(jnp.mean(x * x, axis=-1, keepdims=True) + EPS) * g


def _gelu(x):
    return 0.5 * x * (1.0 + jnp.tanh(0.7978845608028654 * (x + 0.044715 * (x * x * x))))


def _sigmoid(x):
    return 1.0 / (1.0 + jnp.exp(-x))


def _combine(x_ref, y_refs, gt_ref):
    x = x_ref[...]
    gt = gt_ref[...]
    for k in range(TOP_K):
        x = x + gt[:, k:k + 1] * y_refs[k][...].astype(F32)
    return x


def _inproj_body(has_add, *refs):
    if has_add:
        x_ref, y0, y1, y2, y3, gt_ref, g_ref, w_ref, wbt_ref = refs[:9]
        xo_ref, za, zbt, zc, zg = refs[-5:]
        x = _combine(x_ref, (y0, y1, y2, y3), gt_ref)
        xo_ref[...] = x
    else:
        x_ref, g_ref, w_ref, wbt_ref, za, zbt, zc, zg = refs
        x = x_ref[...]
    h = _rms(x, g_ref[...]).astype(BF16)
    z = jnp.dot(h, w_ref[...], preferred_element_type=F32)
    za[...] = z[:, :D_POOL]
    zc[...] = z[:, D_POOL:D_POOL + D_LRU]
    zg[...] = z[:, D_POOL + D_LRU:]
    zt = lax.dot_general(wbt_ref[...], h, (((1,), (1,)), ((), ())), preferred_element_type=F32)
    per_step = zt.shape[1] // S5_CHUNK
    for n in range(per_step):
        zbt[:, pl.program_id(1) * per_step + n, :] = zt[:, n * S5_CHUNK:(n + 1) * S5_CHUNK]


def _inproj(x, add, g, w_rest, w_s5t):
    t = x.shape[0]
    tm = ROW_TILE
    nc = tm * ROW_SPLIT // S5_CHUNK
    row = lambda n: pl.BlockSpec((tm, n), lambda i, j: (i * ROW_SPLIT + j, 0))
    full = lambda a: pl.BlockSpec(a.shape, lambda i, j: (0,) * a.ndim)
    z_shapes = [jax.ShapeDtypeStruct((t, D_POOL), F32), jax.ShapeDtypeStruct((D_S5, t // S5_CHUNK, S5_CHUNK), F32),
                jax.ShapeDtypeStruct((t, D_LRU), F32), jax.ShapeDtypeStruct((t, D_LRU), F32)]
    z_specs = [row(D_POOL), pl.BlockSpec((D_S5, nc, S5_CHUNK), lambda i, j: (0, i, 0)), row(D_LRU), row(D_LRU)]
    n_outer = t // (tm * ROW_SPLIT)
    if add is None:
        return pl.pallas_call(
            functools.partial(_inproj_body, False),
            grid=(n_outer, ROW_SPLIT), in_specs=[row(D_MODEL), full(g), full(w_rest), full(w_s5t)],
            out_specs=z_specs, out_shape=z_shapes,
            compiler_params=_cparams(("parallel", "arbitrary")), name="inproj")(x, g, w_rest, w_s5t)
    parts, gates = add
    per_part = n_outer // len(parts)
    out_shape = [jax.ShapeDtypeStruct((t, D_MODEL), F32)] + z_shapes
    outs = []
    for p, ys in enumerate(parts):
        off = p * per_part
        row_p = lambda n, off=off: pl.BlockSpec((tm, n), lambda i, j: ((i + off) * ROW_SPLIT + j, 0))
        row_y = pl.BlockSpec((tm, D_MODEL), lambda i, j: (i * ROW_SPLIT + j, 0))
        zbt_p = pl.BlockSpec((D_S5, nc, S5_CHUNK), lambda i, j, off=off: (0, i + off, 0))
        n_in = 1 + TOP_K + 4
        outs = pl.pallas_call(
            functools.partial(_inproj_body, True),
            grid=(per_part, ROW_SPLIT),
            in_specs=[row_p(D_MODEL)] + [row_y] * TOP_K + [row_p(TOP_K), full(g), full(w_rest), full(w_s5t)]
                     + [pl.BlockSpec(memory_space=pl.ANY)] * len(outs),
            out_specs=[row_p(D_MODEL), row_p(D_POOL), zbt_p, row_p(D_LRU), row_p(D_LRU)],
            out_shape=out_shape,
            input_output_aliases={n_in + k: k for k in range(len(outs))},
            compiler_params=_cparams(("parallel", "arbitrary")), name="inproj")(
                x, *ys, gates, g, w_rest, w_s5t, *outs)
    return outs


def _pool_body(seq, prev_ref, cur_ref, next_ref, w_ref, sc_ref, gain_ref, o_ref, u_s, s2_s, s4_s, s8_s):
    tl = cur_ref.shape[0]
    i = pl.program_id(1)
    nt = pl.num_programs(1)
    zero8 = jnp.zeros((HALO, D_POOL), F32)
    for buf in (u_s, s2_s, s4_s, s8_s):
        buf[0:HALO, :] = zero8
        buf[tl + 3 * HALO:tl + 4 * HALO, :] = zero8
    u_s[HALO:2 * HALO, :] = jnp.where(i > 0, prev_ref[...], 0.0)
    u_s[2 * HALO:2 * HALO + tl, :] = cur_ref[...]
    u_s[2 * HALO + tl:3 * HALO + tl, :] = jnp.where(i < nt - 1, next_ref[...], 0.0)
    r = tl + 2 * HALO
    s2_s[HALO:HALO + r, :] = u_s[HALO - 1:HALO - 1 + r, :] + u_s[HALO:HALO + r, :]
    s4_s[HALO:HALO + r, :] = s2_s[HALO - 1:HALO - 1 + r, :] + s2_s[HALO + 1:HALO + 1 + r, :]
    s8_s[HALO:HALO + r, :] = s4_s[HALO - 2:HALO - 2 + r, :] + s4_s[HALO + 2:HALO + 2 + r, :]
    o = 2 * HALO
    s16 = s8_s[o - 4:o - 4 + tl, :] + s8_s[o + 4:o + 4 + tl, :]
    s8 = s8_s[o:o + tl, :]
    s4 = s4_s[o:o + tl, :]
    s2 = s2_s[o:o + tl, :]
    u = u_s[o:o + tl, :]
    lane = lax.broadcasted_iota(jnp.int32, (tl, D_POOL), 1)
    tpos = lax.broadcasted_iota(jnp.int32, (tl, D_POOL), 0) + i * tl
    g0, g1, g2 = lane < POOL_GROUP, lane < 2 * POOL_GROUP, lane < 3 * POOL_GROUP
    half = jnp.where(g0, 1, jnp.where(g1, 2, jnp.where(g2, 4, 8)))
    wsum = jnp.where(g0, s2, jnp.where(g1, s4, jnp.where(g2, s8, s16)))
    cnt = (jnp.minimum(tpos + half, seq) - jnp.maximum(tpos - half, 0)).astype(F32)
    d = wsum / cnt - u
    y = jnp.dot(d.astype(BF16), w_ref[...], preferred_element_type=F32) * sc_ref[...]
    o_ref[...] = _rms(y, gain_ref[...])


def _pool_mixer(za, batch, seq, w_bd, scale, gain):
    t = za.shape[0]
    tl = POOL_TILE
    nt = seq // tl
    hb = tl // HALO
    nhb = t // HALO
    cur = pl.BlockSpec((tl, D_POOL), lambda b, i: (b * nt + i, 0))
    prev = pl.BlockSpec((HALO, D_POOL), lambda b, i: (jnp.maximum((b * nt + i) * hb - 1, 0), 0))
    nxt = pl.BlockSpec((HALO, D_POOL), lambda b, i: (jnp.minimum((b * nt + i + 1) * hb, nhb - 1), 0))
    full = lambda a: pl.BlockSpec(a.shape, lambda b, i: (0,) * a.ndim)
    return pl.pallas_call(
        functools.partial(_pool_body, seq),
        grid=(batch, nt),
        in_specs=[prev, cur, nxt, full(w_bd), full(scale), full(gain)],
        out_specs=cur,
        out_shape=jax.ShapeDtypeStruct((t, D_POOL), F32),
        scratch_shapes=[pltpu.VMEM((tl + 4 * HALO, D_POOL), F32)] * 4,
        compiler_params=_cparams(("parallel", "parallel")), name="pool_mixer")(za, za, za, w_bd, scale, gain)


def _s5_scan_steps(n_chunks):
    return max(1, int(math.ceil(math.log2(n_chunks))))


def _s5_tables(lam_re, lam_im, log_step, b_re, b_im, c_re, c_im, n_chunks):
    L, C, P, G = S5_CHUNK, S5_GROUP, S5_STATE, S5_NGROUPS
    hp = lax.Precision.HIGHEST
    lr = lam_re.astype(F32)
    li = lam_im.astype(F32)
    dt = jnp.exp(log_step.astype(F32))[..., None]
    ar = lr * dt
    ai = li * dt
    jj = jnp.arange(L + 1, dtype=F32)

    def powers(j, a_r, a_i):
        mag = jnp.exp(j * a_r)
        return mag * jnp.cos(j * a_i), mag * jnp.sin(j * a_i)

    e_r, e_i = powers(jj[None, None, :, None], ar[:, :, None, :], ai[:, :, None, :])
    et_r, et_i = powers(jj[None, None, None, :], ar[..., None], ai[..., None])
    den = lr * lr + li * li
    nr = e_r[:, :, 1, :] - 1.0
    ni = e_i[:, :, 1, :]
    q_r = (nr * lr + ni * li) / den
    q_i = (ni * lr - nr * li) / den
    br = b_re.astype(F32)
    bi = b_im.astype(F32)
    bbt_r = (q_r[..., None] * br - q_i[..., None] * bi).transpose(0, 1, 3, 2)
    bbt_i = (q_r[..., None] * bi + q_i[..., None] * br).transpose(0, 1, 3, 2)
    cr = c_re.astype(F32)
    ci = c_im.astype(F32)
    m_r = cr[:, :, None] * bbt_r[:, :, :, None] - ci[:, :, None] * bbt_i[:, :, :, None]
    m_i = cr[:, :, None] * bbt_i[:, :, :, None] + ci[:, :, None] * bbt_r[:, :, :, None]
    kk = (jnp.einsum('dgxcp,dgpj->dgxcj', m_r, et_r[..., :L], precision=hp)
          - jnp.einsum('dgxcp,dgpj->dgxcj', m_i, et_i[..., :L], precision=hp))
    kf, kb = kk[0], kk[1]
    lagtab = jnp.concatenate([kb[..., :0:-1], kf[..., :1] + kb[..., :1], kf[..., 1:],
                              jnp.zeros_like(kf[..., :1])], axis=-1)
    bits = lax.bitcast_convert_type(lagtab.astype(BF16).astype(F32), jnp.uint32).reshape(G, C * C, 2 * L)
    lagtab = lax.shift_right_logical(bits[..., :L], jnp.uint32(16)) | (bits[..., L:] & jnp.uint32(0xFFFF0000))
    fb = lambda f, b, axis: jnp.concatenate([f, b], axis=axis)
    pw_s = jnp.stack([fb(e_r[0, :, L - 1::-1], e_r[1, :, :L], -1),
                      fb(e_i[0, :, L - 1::-1], e_i[1, :, :L], -1)], axis=1)
    bb = jnp.stack([fb(bbt_r[0], bbt_r[1], -1), fb(bbt_i[0], bbt_i[1], -1)], axis=1)
    pw_t = jnp.stack([fb(et_r[0, :, :, 1:L + 1], et_r[1, :, :, L:0:-1], 1),
                      fb(et_i[0, :, :, 1:L + 1], et_i[1, :, :, L:0:-1], 1)], axis=1)
    crt = cr.transpose(0, 1, 3, 2)
    cit = ci.transpose(0, 1, 3, 2)
    cc = jnp.stack([fb(crt[0], crt[1], 1), fb(cit[0], cit[1], 1)], axis=1)
    n_steps = _s5_scan_steps(n_chunks)
    kpow =(L * (2 ** jnp.arange(n_steps, dtype=F32)))[None, None, :, None]
    p_r, p_i = powers(kpow, ar[:, :, None, :], ai[:, :, None, :])
    pad = jnp.zeros((G, (-n_steps) % 8, 2 * P), F32)
    p_r = jnp.concatenate([jnp.concatenate([p_r[0], p_r[1]], axis=-1), pad], axis=1)
    p_i = jnp.concatenate([jnp.concatenate([p_i[0], p_i[1]], axis=-1), pad], axis=1)
    ptab = jnp.concatenate([p_r, p_i], axis=1)
    return lagtab, pw_s, bb, pw_t, cc, ptab


def _s5_body(n_chunks, n_steps, u_ref, lag_ref, pws_ref, bb_ref, pwt_ref, cc_ref, p_ref, y_ref,
             toep_s, w_s, v_s):
    L, C = S5_CHUNK, S5_GROUP
    n = u_ref.shape[1]
    two_p = 2 * S5_STATE
    e_r, e_i = pws_ref[0, 0], pws_ref[0, 1]
    et_r, et_i = pwt_ref[0, 0], pwt_ref[0, 1]
    c_r, c_i = cc_ref[0, 0], cc_ref[0, 1]
    for c in range(C):
        b_r = bb_ref[0, 0, c:c + 1, :]
        b_i = bb_ref[0, 1, c:c + 1, :]
        w_s[c * L:(c + 1) * L, :two_p] = (e_r * b_r - e_i * b_i).astype(BF16)
        w_s[c * L:(c + 1) * L, two_p:] = (e_r * b_i + e_i * b_r).astype(BF16)
        k_r = c_r[:, c:c + 1]
        k_i = c_i[:, c:c + 1]
        v_s[:two_p, c * L:(c + 1) * L] = (k_r * et_r - k_i * et_i).astype(BF16)
        v_s[two_p:, c * L:(c + 1) * L] = (-(k_r * et_i + k_i * et_r)).astype(BF16)
    s_i = lax.broadcasted_iota(jnp.int32, (L, L), 0)
    t_i = lax.broadcasted_iota(jnp.int32, (L, L), 1)
    non_positive_lag = t_i <= s_i

    def build_block_row(cp, carry):
        r0 = pl.multiple_of(cp * L, L)
        for c in range(C):
            k = lag_ref[0, pl.ds(cp * C + c, 1), :]
            r = pltpu.roll(jnp.broadcast_to(k, (L, L)), 1, 1, stride=1, stride_axis=0)
            bits = jnp.where(non_positive_lag, lax.shift_left(r, jnp.uint32(16)), r & jnp.uint32(0xFFFF0000))
            toep_s[pl.ds(r0, L), c * L:(c + 1) * L] = pltpu.bitcast(bits, F32).astype(BF16)
        return carry

    lax.fori_loop(0, C, build_block_row, 0)
    u = jnp.concatenate([u_ref[c] for c in range(C)], axis=1).astype(BF16)
    y = jnp.dot(u, toep_s[...], preferred_element_type=F32)
    s = jnp.dot(u, w_s[...], preferred_element_type=F32)
    xr = s[:, :two_p]
    xi = s[:, two_p:]
    row = lax.broadcasted_iota(jnp.int32, (n, two_p), 0) % n_chunks
    is_fwd = lax.broadcasted_iota(jnp.int32, (n, two_p), 1) < S5_STATE
    pim0 = p_ref.shape[1] // 2

    def shifted(a, k):
        down = jnp.where(row >= k, pltpu.roll(a, k, 0), 0.0)
        up = jnp.where(row < n_chunks - k, pltpu.roll(a, n - k, 0), 0.0)
        return jnp.where(is_fwd, down, up)

    for j in range(n_steps):
        k = 1 << j
        pr = p_ref[0, j:j + 1, :]
        pi = p_ref[0, pim0 + j:pim0 + j + 1, :]
        sr = shifted(xr, k)
        si = shifted(xi, k)
        xr, xi = xr + pr * sr - pi * si, xi + pr * si + pi * sr
    carry = jnp.concatenate([shifted(xr, 1), shifted(xi, 1)], axis=1).astype(BF16)
    y = y + jnp.dot(carry, v_s[...], preferred_element_type=F32)
    for c in range(C):
        y_ref[c] = y[:, c * L:(c + 1) * L]


def _s5_core(zbt, batch, seq, tables):
    L = S5_CHUNK
    n_chunks = seq // L
    n_steps = _s5_scan_steps(n_chunks)
    n = batch * n_chunks
    per_g = lambda a: pl.BlockSpec((1,) + a.shape[1:], lambda g: (g,) + (0,) * (a.ndim - 1))
    grp = pl.BlockSpec((S5_GROUP, n, L), lambda g: (g, 0, 0))
    return pl.pallas_call(
        functools.partial(_s5_body, n_chunks, n_steps),
        grid=(S5_NGROUPS,),
        in_specs=[grp] + [per_g(a) for a in tables],
        out_specs=grp,
        out_shape=jax.ShapeDtypeStruct((D_S5, n, L), F32),
        scratch_shapes=[pltpu.VMEM((S5_GROUP * L, S5_GROUP * L), BF16),
                        pltpu.VMEM((S5_GROUP * L, 4 * S5_STATE), BF16),
                        pltpu.VMEM((4 * S5_STATE, S5_GROUP * L), BF16)],
        compiler_params=_cparams(("parallel",)), name="s5_core")(zbt, *tables)


def _lru_body(reverse, prev_ref, cur_ref, next_ref, cw_ref, cb_ref, wg_ref, bg_ref, sp_ref, *rest):
    if reverse:
        hf_ref, gate_ref, gain_ref, o_ref, a_s, b_s, h_s, carry_s = rest
    else:
        o_ref, a_s, b_s, h_s, carry_s = rest
    nb, tl, _ = cur_ref.shape
    step = pl.program_id(0)
    nt = pl.num_programs(0)
    ti = nt - 1 - step if reverse else step

    @pl.when(step == 0)
    def _():
        carry_s[...] = jnp.zeros_like(carry_s)

    cw = cw_ref[...]
    k_row = sp_ref[...] * (-0.5 * RG_C * math.log2(math.e))
    sub = lax.broadcasted_iota(jnp.int32, (HALO, D_LRU), 0)

    def shifted(x, d, edge):
        y = pltpu.roll(x, d % tl, 0)
        if d > 0:
            return jnp.concatenate([jnp.where(sub < d, pltpu.roll(edge, d, 0), y[:HALO]), y[HALO:]], axis=0)
        return jnp.concatenate([y[:-HALO], jnp.where(sub >= HALO + d, pltpu.roll(edge, HALO + d, 0), y[-HALO:])],
                               axis=0)

    for b in range(nb):
        x = cur_ref[b]
        before = jnp.where(ti > 0, prev_ref[b], 0.0)
        after = jnp.where(ti < nt - 1, next_ref[b], 0.0)
        xc = (cb_ref[...] + cw[0:1, :] * shifted(x, 1, before) + cw[1:2, :] * x
              + cw[2:3, :] * shifted(x, -1, after) + cw[3:4, :] * shifted(x, -2, after))
        gates = jnp.dot(xc.astype(BF16), wg_ref[...], preferred_element_type=F32) + bg_ref[...]
        a = jnp.exp2(k_row * (jnp.tanh(0.5 * gates[:, :D_LRU]) + 1.0))
        ig = 0.5 * jnp.tanh(0.5 * gates[:, D_LRU:]) + 0.5
        a_s[b] = a
        om = 1.0 - a * a
        b_s[b] = (om * lax.rsqrt(jnp.maximum(om, 1e-30))) * (ig * xc)

    def scan_step(s, hs):
        t = tl - 1 - s if reverse else s
        out = []
        for b in range(nb):
            h = a_s[b, pl.ds(t, 1), :] * hs[b] + b_s[b, pl.ds(t, 1), :]
            h_s[b, pl.ds(t, 1), :] = h
            out.append(h)
        return tuple(out)

    hs = lax.fori_loop(0, tl, scan_step, tuple(carry_s[b:b + 1, :] for b in range(nb)), unroll=8)
    for b in range(nb):
        carry_s[b:b + 1, :] = hs[b]

    if reverse:
        for b in range(nb):
            y = (hf_ref[b] + h_s[b]) * _gelu(gate_ref[b])
            o_ref[b] = _rms(y, gain_ref[...])
    else:
        o_ref[...] = h_s[...]


def _lru_pass(reverse, zc3, params, extra):
    nb, seq, _ = zc3.shape
    tl = LRU_TILE
    nt = seq // tl
    hb = tl // HALO
    nhb = seq // HALO
    tix = (lambda i: nt - 1 - i) if reverse else (lambda i: i)
    cur = pl.BlockSpec((nb, tl, D_LRU), lambda i: (0, tix(i), 0))
    prev = pl.BlockSpec((nb, HALO, D_LRU), lambda i: (0, jnp.maximum(tix(i) * hb - 1, 0), 0))
    nxt = pl.BlockSpec((nb, HALO, D_LRU), lambda i: (0, jnp.minimum((tix(i) + 1) * hb, nhb - 1), 0))
    full = lambda a: pl.BlockSpec(a.shape, lambda i: (0,) * a.ndim)
    ins = [zc3, zc3, zc3, *params]
    in_specs = [prev, cur, nxt] + [full(a) for a in params]
    if reverse:
        hf, gate, gain = extra
        ins += [hf, gate, gain]
        in_specs += [cur, cur, full(gain)]
    return pl.pallas_call(
        functools.partial(_lru_body, reverse),
        grid=(nt,), in_specs=in_specs, out_specs=cur,
        out_shape=jax.ShapeDtypeStruct((nb, seq, D_LRU), F32),
        scratch_shapes=[pltpu.VMEM((nb, tl, D_LRU), F32),
                        pltpu.VMEM((nb, tl, D_LRU), F32),
                        pltpu.VMEM((nb, tl, D_LRU), F32),
                        pltpu.VMEM((8, D_LRU), F32)],
        compiler_params=_cparams(("arbitrary",)),
        name="lru_bwd" if reverse else "lru_fwd")(*ins)


def _block_diag(w):
    h, d, _ = w.shape
    eye = jnp.eye(h, dtype=w.dtype)
    return (eye[:, None, :, None] * w[:, :, None, :]).reshape(h * d, h * d)


def _lru_params(w_a, b_a, w_x, b_x, lam):
    bd = jax.vmap(_block_diag)
    wg = jnp.concatenate([bd(w_a), bd(w_x)], axis=-1).astype(BF16)
    bg = jnp.concatenate([b_a, b_x], axis=-1)[:, None, :]
    sp = jax.nn.softplus(-lam.astype(F32))[:, None, :]
    return wg, bg, sp


def _lru_mixer(zc, zg, batch, seq, conv_w, conv_b, wg, bg, sp, gain):
    zc3 = zc.reshape(batch, seq, D_LRU)
    zg3 = zg.reshape(batch, seq, D_LRU)
    outs = None
    for d in (0, 1):
        params = [conv_w, conv_b[None, :], wg[d], bg[d], sp[d]]
        extra = None if d == 0 else (outs, zg3, gain[None, :])
        outs = _lru_pass(d == 1, zc3, params, extra)
    return outs.reshape(batch * seq, D_LRU)


def _mixout_body(x_ref, oa_ref, y3_ref, u3_ref, oc_ref, sd_ref, wglu_ref, bglu_ref, gb_ref,
                 wout_ref, g2_ref, rw_ref, rb_ref, xo_ref, h_ref, ti_ref, gt_ref):
    tm = x_ref.shape[0]
    per_step = tm // S5_CHUNK
    c0 = pl.program_id(1) * per_step
    yt = jnp.concatenate([y3_ref[:, c0 + n, :] for n in range(per_step)], axis=1)
    ut = jnp.concatenate([u3_ref[:, c0 + n, :] for n in range(per_step)], axis=1)
    vt = _gelu(yt + sd_ref[...] * ut).astype(BF16)
    gl = lax.dot_general(vt, wglu_ref[...], (((0,), (0,)), ((), ())), preferred_element_type=F32) + bglu_ref[...]
    ob = _rms(gl[:, :D_S5] * _sigmoid(gl[:, D_S5:]), gb_ref[...])
    o = jnp.concatenate([oa_ref[...], ob, oc_ref[...]], axis=1).astype(BF16)
    x = x_ref[...] + jnp.dot(o, wout_ref[...], preferred_element_type=F32)
    xo_ref[...] = x
    h = _rms(x, g2_ref[...])
    h_hi = h.astype(BF16)
    bits = pltpu.bitcast(h_hi.astype(F32), jnp.uint32)
    h_ref[...] = lax.shift_right_logical(bits[:, :D_MODEL // 2], jnp.uint32(16)) | (
        bits[:, D_MODEL // 2:] & jnp.uint32(0xFFFF0000))
    h_lo = (h - h_hi.astype(F32)).astype(BF16)
    nt_dims = (((1,), (1,)), ((), ()))
    la = lax.dot_general(rw_ref[...], h_hi, nt_dims, preferred_element_type=F32)
    lb = lax.dot_general(rw_ref[0:N_EXPERTS, :], h_lo, nt_dims, preferred_element_type=F32)
    logits = la[:N_EXPERTS] + la[N_EXPERTS:] + lb + rb_ref[...]
    eidx = lax.broadcasted_iota(jnp.int32, (N_EXPERTS, tm), 0).astype(F32)
    idx_rows, val_rows = [], []
    for k in range(TOP_K):
        m = jnp.max(logits, axis=0, keepdims=True)
        idx = jnp.min(jnp.where(logits == m, eidx, float(N_EXPERTS)), axis=0, keepdims=True)
        idx_rows.append(idx)
        val_rows.append(jnp.exp(m - val_rows[0]) if k else m)
        logits = jnp.where(eidx == idx, -jnp.inf, logits)
    top = val_rows[0]
    vals = jnp.concatenate([jnp.ones_like(top)] + [v for v in val_rows[1:]], axis=0)
    ti_ref[...] = jnp.concatenate(idx_rows, axis=0).astype(jnp.int32)
    gt_ref[...] = vals / jnp.sum(vals, axis=0, keepdims=True)


def _router_split(rw):
    rw_hi = rw.astype(BF16)
    rw_lo = (rw - rw_hi.astype(F32)).astype(BF16)
    return jnp.concatenate([jnp.swapaxes(rw_hi, -1, -2), jnp.swapaxes(rw_lo, -1, -2)], axis=-2)


def _mixout(x, oa, yt3, zbt3, oc, s5_d, wglu, bglu, gain_b, wout, g2, rw2, rb):
    t = x.shape[0]
    tm = MIX_TILE
    nc = ROW_TILE * ROW_SPLIT // S5_CHUNK
    split = nc * S5_CHUNK // tm
    row = lambda n: pl.BlockSpec((tm, n), lambda i, j: (i * split + j, 0))
    col = pl.BlockSpec((TOP_K, tm), lambda i, j: (0, i * split + j))
    chunks = pl.BlockSpec((D_S5, nc, S5_CHUNK), lambda i, j: (0, i, 0))
    full = lambda a: pl.BlockSpec(a.shape, lambda i, j: (0,) * a.ndim)
    params = [s5_d.reshape(D_S5, 1), wglu, bglu, gain_b, wout, g2, rw2, rb.reshape(N_EXPERTS, 1)]
    return pl.pallas_call(
        _mixout_body,
        grid=(t // (tm * split), split),
        in_specs=[row(D_MODEL), row(D_POOL), chunks, chunks, row(D_LRU)] + [full(a) for a in params],
        out_specs=[row(D_MODEL), row(D_MODEL // 2), col, col],
        out_shape=[jax.ShapeDtypeStruct((t, D_MODEL), F32), jax.ShapeDtypeStruct((t, D_MODEL // 2), jnp.uint32),
                   jax.ShapeDtypeStruct((TOP_K, t), jnp.int32), jax.ShapeDtypeStruct((TOP_K, t), F32)],
        compiler_params=pltpu.CompilerParams(dimension_semantics=("parallel", "arbitrary"),
                                             vmem_limit_bytes=EXPERT_VMEM_LIMIT),
        name="mix_out")(x, oa, yt3, zbt3, oc, *params)


def _expert_body(be_ref, nu_ref, first_ref, xl_ref, xh_ref, wgu_ref, bg_ref, bu_ref, wd_ref, bd_ref, p_ref, xs_ref,
                 y_ref, wg_s, wu_s, wd_s, ws_s):
    i = pl.program_id(0)
    live = i < nu_ref[0]
    x_scale = xs_ref[...]
    act_scale = FP8_TARGET / ((SWIGLU_LIMIT + 1.0) * SWIGLU_LIMIT)

    @pl.when(jnp.logical_and(live, first_ref[i] == 1))
    def _():
        p = p_ref[...]
        half = MXU_TILE // 2
        n_groups = wgu_ref.shape[3] // MXU_TILE
        amax = functools.reduce(jnp.maximum, [
            jnp.max(jnp.abs(wgu_ref[0, 0, :, MXU_TILE * j:MXU_TILE * (j + 1)]), axis=(0, 1), keepdims=True)
            for j in range(n_groups)])
        s_gu = FP8_TARGET / jnp.maximum(amax, FP8_TINY)
        for j in range(n_groups):
            w = wgu_ref[0, 0, :, MXU_TILE * j:MXU_TILE * (j + 1)].astype(BF16)
            r = jnp.dot(w, p, preferred_element_type=F32) * s_gu
            wg_s[:, half * j:half * (j + 1)] = r[:, :half].astype(FP8)
            wu_s[:, half * j:half * (j + 1)] = r[:, half:].astype(FP8)
        wd = wd_ref[0, 0]
        dmax = jnp.max(jnp.abs(wd), axis=(0, 1), keepdims=True)
        s_d = FP8_TARGET / jnp.maximum(dmax, FP8_TINY)
        wd_s[...] = (wd * s_d).astype(FP8)
        ws_s[0:1, :] = jnp.broadcast_to(1.0 / (s_gu * x_scale), (1, ws_s.shape[1]))
        ws_s[1:2, :] = jnp.broadcast_to(1.0 / (s_d * act_scale), (1, ws_s.shape[1]))

    @pl.when(live)
    def _():
        words = jnp.concatenate([xl_ref[...], xh_ref[...]], axis=1)
        lo = pltpu.bitcast(lax.shift_left(words, jnp.uint32(16)), F32)
        hi = pltpu.bitcast(words & jnp.uint32(0xFFFF0000), F32)
        x = (jnp.concatenate([lo, hi], axis=1) * x_scale).astype(FP8)
        gu_inv = ws_s[0:1, 0:1]
        g = jnp.dot(x, wg_s[...], preferred_element_type=F32) * gu_inv + bg_ref[0, 0]
        g = jnp.minimum(g, SWIGLU_LIMIT)
        u1 = (jnp.dot(x, wu_s[...], preferred_element_type=F32) * (gu_inv * act_scale)
              + (bu_ref[0, 0] + 1.0) * act_scale)
        u1 = jnp.clip(u1, (1.0 - SWIGLU_LIMIT) * act_scale, (1.0 + SWIGLU_LIMIT) * act_scale)
        glu = g / (1.0 + jnp.exp2(g * (-SWIGLU_ALPHA * math.log2(math.e))))
        act = (u1 * glu).astype(FP8)
        y = jnp.dot(act, wd_s[...], preferred_element_type=F32) * ws_s[1:2, 0:1] + bd_ref[0, 0]
        y_ref[...] = y.astype(y_ref.dtype)

    @pl.when(jnp.logical_not(live))
    def _():
        y_ref[...] = jnp.zeros_like(y_ref)


def _expert_mlp(layer, xs, x_bound, blk_e, n_used, w_gate_up, b_gate, b_up, w_down, b_down):
    n_rows, dh = xs[0].shape
    bm = MOE_ROWS
    x_scale = (FP8_TARGET / jnp.maximum(x_bound, FP8_TINY)).astype(F32).reshape(1, 1)
    first = jnp.concatenate([jnp.ones((1,), jnp.int32), (blk_e[1:] != blk_e[:-1]).astype(jnp.int32)])
    idx = jnp.arange(MXU_TILE)
    perm = jax.nn.one_hot(jnp.where(idx % 2 == 0, idx // 2, MXU_TILE // 2 + idx // 2), MXU_TILE, dtype=BF16)
    rows = pl.BlockSpec((bm, dh), lambda i, be, nu, fi: (jnp.minimum(i, nu[0] - 1), 0))
    per_e = lambda a: pl.BlockSpec((1, 1) + a.shape[2:], lambda i, be, nu, fi: (layer, be[i], 0, 0))
    grid_spec = pltpu.PrefetchScalarGridSpec(
        num_scalar_prefetch=3, grid=(n_rows // bm,),
        in_specs=[rows, rows, per_e(w_gate_up), per_e(b_gate), per_e(b_up), per_e(w_down), per_e(b_down),
                  pl.BlockSpec(perm.shape, lambda i, be, nu, fi: (0, 0)),
                  pl.BlockSpec((1, 1), lambda i, be, nu, fi: (0, 0))],
        out_specs=pl.BlockSpec((bm, D_MODEL), lambda i, be, nu, fi: (i, 0)),
        scratch_shapes=[pltpu.VMEM((D_MODEL, D_FF), FP8), pltpu.VMEM((D_MODEL, D_FF), FP8),
                        pltpu.VMEM((D_FF, D_MODEL), FP8), pltpu.VMEM((8, 128), F32)])
    return pl.pallas_call(
        _expert_body, grid_spec=grid_spec,
        out_shape=jax.ShapeDtypeStruct((n_rows, D_MODEL), BF16),
        compiler_params=pltpu.CompilerParams(dimension_semantics=("arbitrary",),
                                             vmem_limit_bytes=EXPERT_VMEM_LIMIT),
        name="expert_mlp")(blk_e, n_used, first, *xs, w_gate_up, b_gate, b_up, w_down, b_down, perm, x_scale)


def _sc_dispatch(h, dest, n_rows):
    t, d = h.shape
    win = SC_WINDOW
    dh = d // SC_COL_SPLIT
    idx = [dest[:, k].reshape(1, t) for k in range(TOP_K)]
    mesh = plsc.VectorSubcoreMesh(core_axis_name="core", subcore_axis_name="subcore")

    def scatter_cols(j):
        @functools.partial(pl.kernel, out_type=jax.ShapeDtypeStruct((n_rows, dh), h.dtype), mesh=mesh,
                           scratch_types=[], name="sc_dispatch")
        def scatter_rows(x_hbm, i0_hbm, i1_hbm, i2_hbm, i3_hbm, o_hbm):
            def body(x_vmem, *idx_vmem):
                for iv in idx_vmem:
                    pltpu.sync_copy(x_vmem, o_hbm.at[iv.at[0]])

            pltpu.emit_pipeline(
                body, grid=(t // win,),
                in_specs=[pl.BlockSpec((win, dh), lambda i: (i, j))]
                         + [pl.BlockSpec((1, win), lambda i: (0, i))] * TOP_K,
                out_specs=[],
                core_axis_name=("core", "subcore"),
                dimension_semantics=(pltpu.PARALLEL,),
            )(x_hbm, i0_hbm, i1_hbm, i2_hbm, i3_hbm)

        return scatter_rows(h, *idx)

    return [scatter_cols(j) for j in range(SC_COL_SPLIT)]


def _moe_dispatch(top_i, t):
    bm = MOE_ROWS
    n_blocks = (t * TOP_K) // bm + N_EXPERTS
    onehot = jax.nn.one_hot(top_i, N_EXPERTS, dtype=jnp.int32)
    sel = jnp.sum(onehot, axis=1)
    csum = jnp.cumsum(sel, axis=0)
    counts = csum[-1]
    padded = ((counts + bm - 1) // bm) * bm
    pad_end = jnp.cumsum(padded)
    pad_start = pad_end - padded
    base = (csum - sel) + pad_start[None, :]
    dest = jnp.sum(onehot * base[:, None, :], axis=-1)
    blk_start = jnp.arange(n_blocks, dtype=jnp.int32) * bm
    blk_e = jnp.minimum(jnp.sum((pad_end[None, :] <= blk_start[:, None]).astype(jnp.int32), axis=1),
                        N_EXPERTS - 1)
    n_used = (pad_end[-1] // bm).astype(jnp.int32).reshape(1)
    return dest, n_blocks * bm, blk_e, n_used


def _final_body(x_ref, y0, y1, y2, y3, gt_ref, g_ref, *rest):
    o_ref = rest[-1]
    o_ref[...] = _rms(_combine(x_ref, (y0, y1, y2, y3), gt_ref), g_ref[...])


def _final(x, add, g):
    t = x.shape[0]
    tm = ROW_TILE
    parts, gates = add
    per_part = t // tm // len(parts)
    out = None
    for p, ys in enumerate(parts):
        off = p * per_part
        row_p = lambda n, off=off: pl.BlockSpec((tm, n), lambda i: (i + off, 0))
        row_y = pl.BlockSpec((tm, D_MODEL), lambda i: (i, 0))
        prev = [] if out is None else [out]
        out = pl.pallas_call(
            _final_body, grid=(per_part,),
            in_specs=[row_p(D_MODEL)] + [row_y] * TOP_K + [row_p(TOP_K), pl.BlockSpec(g.shape, lambda i: (0, 0))]
                     + [pl.BlockSpec(memory_space=pl.ANY)] * len(prev),
            out_specs=row_p(D_MODEL), out_shape=jax.ShapeDtypeStruct((t, D_MODEL), F32),
            input_output_aliases={2 + TOP_K + 1: 0} if prev else {},
            compiler_params=_cparams(("parallel",)), name="final_norm")(x, *ys, gates, g, *prev)
    return out


def kernel(x, norm1_g, w_in, pool_w, pool_scale, s5_lam_re, s5_lam_im, s5_log_step, s5_b_re, s5_b_im, s5_c_re, s5_c_im, s5_d, s5_w_glu, s5_b_glu, lru_conv_w, lru_conv_b, lru_w_a, lru_b_a, lru_w_x, lru_b_x, lru_lam, mix_gain, w_out, norm2_g, router_w, router_b, w_gate_up, b_gate_up, w_down, b_down, final_g):
    batch, seq, d = x.shape
    t = batch * seq
    depth = norm1_g.shape[0]
    xt = x.reshape(t, d)
    add = None
    o1, o2 = D_POOL, D_POOL + D_S5
    b_gate = b_gate_up[:, :, None, 0::2]
    b_up = b_gate_up[:, :, None, 1::2]
    b_dn = b_down[:, :, None, :]
    for l in range(depth):
        w_rest = jnp.concatenate([w_in[l][:, :o1], w_in[l][:, o2:]], axis=1).astype(BF16)
        w_s5t = w_in[l][:, o1:o2].T.astype(BF16)
        s5_tabs = _s5_tables(s5_lam_re[l], s5_lam_im[l], s5_log_step[l], s5_b_re[l], s5_b_im[l],
                             s5_c_re[l], s5_c_im[l], seq // S5_CHUNK)
        lru_wg, lru_bg, lru_sp = _lru_params(lru_w_a[l], lru_b_a[l], lru_w_x[l], lru_b_x[l], lru_lam[l])
        if add is None:
            za, zbt, zc, zg = _inproj(xt, None, norm1_g[l][None, :], w_rest, w_s5t)
        else:
            xt, za, zbt, zc, zg = _inproj(xt, add, norm1_g[l][None, :], w_rest, w_s5t)
        oa = _pool_mixer(za, batch, seq, _block_diag(pool_w[l]).astype(BF16), pool_scale[l][None, :],
                         mix_gain[l][None, :o1])
        yt = _s5_core(zbt, batch, seq, s5_tabs)
        oc = _lru_mixer(zc, zg, batch, seq, lru_conv_w[l], lru_conv_b[l], lru_wg, lru_bg, lru_sp,
                        mix_gain[l][o2:])
        xt, h2, top_i, gates = _mixout(
            xt, oa, yt, zbt, oc, s5_d[l], s5_w_glu[l].astype(BF16), s5_b_glu[l][None, :],
            mix_gain[l][None, o1:o2], w_out[l].astype(BF16), norm2_g[l][None, :], _router_split(router_w[l]),
            router_b[l])
        top_i, gates = top_i.T, gates.T
        dest, n_rows, blk_e, n_used = _moe_dispatch(top_i, t)
        xs = _sc_dispatch(h2, dest, n_rows)
        x_bound = math.sqrt(D_MODEL) * jnp.max(jnp.abs(norm2_g[l]))
        ys = _expert_mlp(l, xs, x_bound, blk_e, n_used, w_gate_up, b_gate, b_up, w_down, b_dn)
        tp = t // TOKEN_PARTS
        add = ([[ys.at[dest[p * tp:(p + 1) * tp, k]].get(mode="promise_in_bounds") for k in range(TOP_K)]
                for p in range(TOKEN_PARTS)], gates)
    out = _final(xt, add, final_g[None, :])
    return out.reshape(batch, seq, d)
```

```python
import functools
import math

import jax
import jax.numpy as jnp
from jax import lax
from jax.experimental import pallas as pl
from jax.experimental.pallas import tpu as pltpu
from jax.experimental.pallas import tpu_sc as plsc

F32 = jnp.float32
BF16 = jnp.bfloat16
FP8 = jnp.float8_e4m3fn
FP8_TARGET = 416.0
FP8_TINY = 1e-30

D_MODEL = 1024
D_POOL = 256
D_S5 = 384
D_LRU = 384
POOL_WINDOWS = (2, 4, 8, 16)
POOL_GROUP = 64
S5_GROUP = 16
S5_NGROUPS = 24
S5_STATE = 64
LRU_HEADS = 6
LRU_HEAD_DIM = 64
RG_C = 8.0
N_EXPERTS = 32
TOP_K = 4
D_FF = 1024
SWIGLU_LIMIT = 7.0
SWIGLU_ALPHA = 1.702
EPS = 1e-5

ROW_TILE = 512
ROW_SPLIT = 2
S5_CHUNK = 128
POOL_TILE = 512
LRU_TILE = 256
HALO = 8
MOE_ROWS = 1024
SC_WINDOW = 128
SC_COL_SPLIT = 2
MXU_TILE = 256
EXPERT_VMEM_LIMIT = 56 * 1024 * 1024
VMEM_LIMIT = 48 * 1024 * 1024


def _cparams(sem):
    return pltpu.CompilerParams(dimension_semantics=sem, vmem_limit_bytes=VMEM_LIMIT)


def _rms(x, g):
    return x * lax.rsqrt(jnp.mean(x * x, axis=-1, keepdims=True) + EPS) * g


def _gelu(x):
    return 0.5 * x * (1.0 + jnp.tanh(0.7978845608028654 * (x + 0.044715 * (x * x * x))))


def _sigmoid(x):
    return 1.0 / (1.0 + jnp.exp(-x))


def _combine(x_ref, y_refs, gt_ref):
    x = x_ref[...]
    gt = gt_ref[...]
    for k in range(TOP_K):
        x = x + gt[:, k:k + 1] * y_refs[k][...].astype(F32)
    return x


def _inproj_body(has_add, *refs):
    if has_add:
        x_ref, y0, y1, y2, y3, gt_ref, g_ref, w_ref, wbt_ref, xo_ref, za, zbt, zc, zg = refs
        x = _combine(x_ref, (y0, y1, y2, y3), gt_ref)
        xo_ref[...] = x
    else:
        x_ref, g_ref, w_ref, wbt_ref, za, zbt, zc, zg = refs
        x = x_ref[...]
    h = _rms(x, g_ref[...]).astype(BF16)
    z = jnp.dot(h, w_ref[...], preferred_element_type=F32)
    za[...] = z[:, :D_POOL]
    zc[...] = z[:, D_POOL:D_POOL + D_LRU]
    zg[...] = z[:, D_POOL + D_LRU:]
    zt = lax.dot_general(wbt_ref[...], h, (((1,), (1,)), ((), ())), preferred_element_type=F32)
    per_step = zt.shape[1] // S5_CHUNK
    for n in range(per_step):
        zbt[:, pl.program_id(1) * per_step + n, :] = zt[:, n * S5_CHUNK:(n + 1) * S5_CHUNK]


def _inproj(x, add, g, w_rest, w_s5t):
    t = x.shape[0]
    tm = ROW_TILE
    nc = tm * ROW_SPLIT // S5_CHUNK
    row = lambda n: pl.BlockSpec((tm, n), lambda i, j: (i * ROW_SPLIT + j, 0))
    full = lambda a: pl.BlockSpec(a.shape, lambda i, j: (0,) * a.ndim)
    z_shapes = [jax.ShapeDtypeStruct((t, D_POOL), F32), jax.ShapeDtypeStruct((D_S5, t // S5_CHUNK, S5_CHUNK), F32),
                jax.ShapeDtypeStruct((t, D_LRU), F32), jax.ShapeDtypeStruct((t, D_LRU), F32)]
    z_specs = [row(D_POOL), pl.BlockSpec((D_S5, nc, S5_CHUNK), lambda i, j: (0, i, 0)), row(D_LRU), row(D_LRU)]
    n_outer = t // (tm * ROW_SPLIT)
    if add is None:
        return pl.pallas_call(
            functools.partial(_inproj_body, False),
            grid=(n_outer, ROW_SPLIT), in_specs=[row(D_MODEL), full(g), full(w_rest), full(w_s5t)],
            out_specs=z_specs, out_shape=z_shapes,
            compiler_params=_cparams(("parallel", "arbitrary")), name="inproj")(x, g, w_rest, w_s5t)
    ys, gates = add
    return pl.pallas_call(
        functools.partial(_inproj_body, True),
        grid=(n_outer, ROW_SPLIT),
        in_specs=[row(D_MODEL)] * (1 + TOP_K) + [row(TOP_K), full(g), full(w_rest), full(w_s5t)],
        out_specs=[row(D_MODEL)] + z_specs,
        out_shape=[jax.ShapeDtypeStruct((t, D_MODEL), F32)] + z_shapes,
        compiler_params=_cparams(("parallel", "arbitrary")), name="inproj")(x, *ys, gates, g, w_rest, w_s5t)


def _pool_body(seq, prev_ref, cur_ref, next_ref, w_ref, sc_ref, gain_ref, o_ref, u_s, s2_s, s4_s, s8_s):
    tl = cur_ref.shape[0]
    i = pl.program_id(1)
    nt = pl.num_programs(1)
    zero8 = jnp.zeros((HALO, D_POOL), F32)
    for buf in (u_s, s2_s, s4_s, s8_s):
        buf[0:HALO, :] = zero8
        buf[tl + 3 * HALO:tl + 4 * HALO, :] = zero8
    u_s[HALO:2 * HALO, :] = jnp.where(i > 0, prev_ref[...], 0.0)
    u_s[2 * HALO:2 * HALO + tl, :] = cur_ref[...]
    u_s[2 * HALO + tl:3 * HALO + tl, :] = jnp.where(i < nt - 1, next_ref[...], 0.0)
    r = tl + 2 * HALO
    s2_s[HALO:HALO + r, :] = u_s[HALO - 1:HALO - 1 + r, :] + u_s[HALO:HALO + r, :]
    s4_s[HALO:HALO + r, :] = s2_s[HALO - 1:HALO - 1 + r, :] + s2_s[HALO + 1:HALO + 1 + r, :]
    s8_s[HALO:HALO + r, :] = s4_s[HALO - 2:HALO - 2 + r, :] + s4_s[HALO + 2:HALO + 2 + r, :]
    o = 2 * HALO
    s16 = s8_s[o - 4:o - 4 + tl, :] + s8_s[o + 4:o + 4 + tl, :]
    s8 = s8_s[o:o + tl, :]
    s4 = s4_s[o:o + tl, :]
    s2 = s2_s[o:o + tl, :]
    u = u_s[o:o + tl, :]
    lane = lax.broadcasted_iota(jnp.int32, (tl, D_POOL), 1)
    tpos = lax.broadcasted_iota(jnp.int32, (tl, D_POOL), 0) + i * tl
    g0, g1, g2 = lane < POOL_GROUP, lane < 2 * POOL_GROUP, lane < 3 * POOL_GROUP
    half = jnp.where(g0, 1, jnp.where(g1, 2, jnp.where(g2, 4, 8)))
    wsum = jnp.where(g0, s2, jnp.where(g1, s4, jnp.where(g2, s8, s16)))
    cnt = (jnp.minimum(tpos + half, seq) - jnp.maximum(tpos - half, 0)).astype(F32)
    d = wsum / cnt - u
    y = jnp.dot(d.astype(BF16), w_ref[...], preferred_element_type=F32) * sc_ref[...]
    o_ref[...] = _rms(y, gain_ref[...])


def _pool_mixer(za, batch, seq, w_bd, scale, gain):
    t = za.shape[0]
    tl = POOL_TILE
    nt = seq // tl
    hb = tl // HALO
    nhb = t // HALO
    cur = pl.BlockSpec((tl, D_POOL), lambda b, i: (b * nt + i, 0))
    prev = pl.BlockSpec((HALO, D_POOL), lambda b, i: (jnp.maximum((b * nt + i) * hb - 1, 0), 0))
    nxt = pl.BlockSpec((HALO, D_POOL), lambda b, i: (jnp.minimum((b * nt + i + 1) * hb, nhb - 1), 0))
    full = lambda a: pl.BlockSpec(a.shape, lambda b, i: (0,) * a.ndim)
    return pl.pallas_call(
        functools.partial(_pool_body, seq),
        grid=(batch, nt),
        in_specs=[prev, cur, nxt, full(w_bd), full(scale), full(gain)],
        out_specs=cur,
        out_shape=jax.ShapeDtypeStruct((t, D_POOL), F32),
        scratch_shapes=[pltpu.VMEM((tl + 4 * HALO, D_POOL), F32)] * 4,
        compiler_params=_cparams(("parallel", "parallel")), name="pool_mixer")(za, za, za, w_bd, scale, gain)


def _s5_scan_steps(n_chunks):
    return max(1, int(math.ceil(math.log2(n_chunks))))


def _s5_tables(lam_re, lam_im, log_step, b_re, b_im, c_re, c_im, n_chunks):
    L, C, P, G = S5_CHUNK, S5_GROUP, S5_STATE, S5_NGROUPS
    hp = lax.Precision.HIGHEST
    lr = lam_re.astype(F32)
    li = lam_im.astype(F32)
    dt = jnp.exp(log_step.astype(F32))[..., None]
    ar = lr * dt
    ai = li * dt
    jj = jnp.arange(L + 1, dtype=F32)

    def powers(j, a_r, a_i):
        mag = jnp.exp(j * a_r)
        return mag * jnp.cos(j * a_i), mag * jnp.sin(j * a_i)

    e_r, e_i = powers(jj[None, None, :, None], ar[:, :, None, :], ai[:, :, None, :])
    et_r, et_i = powers(jj[None, None, None, :], ar[..., None], ai[..., None])
    den = lr * lr + li * li
    nr = e_r[:, :, 1, :] - 1.0
    ni = e_i[:, :, 1, :]
    q_r = (nr * lr + ni * li) / den
    q_i = (ni * lr - nr * li) / den
    br = b_re.astype(F32)
    bi = b_im.astype(F32)
    bbt_r = (q_r[..., None] * br - q_i[..., None] * bi).transpose(0, 1, 3, 2)
    bbt_i = (q_r[..., None] * bi + q_i[..., None] * br).transpose(0, 1, 3, 2)
    cr = c_re.astype(F32)
    ci = c_im.astype(F32)
    m_r = cr[:, :, None] * bbt_r[:, :, :, None] - ci[:, :, None] * bbt_i[:, :, :, None]
    m_i = cr[:, :, None] * bbt_i[:, :, :, None] + ci[:, :, None] * bbt_r[:, :, :, None]
    kk = (jnp.einsum('dgxcp,dgpj->dgxcj', m_r, et_r[..., :L], precision=hp)
          - jnp.einsum('dgxcp,dgpj->dgxcj', m_i, et_i[..., :L], precision=hp))
    kf, kb = kk[0], kk[1]
    lagtab = jnp.concatenate([kb[..., :0:-1], kf[..., :1] + kb[..., :1], kf[..., 1:],
                              jnp.zeros_like(kf[..., :1])], axis=-1)
    bits = lax.bitcast_convert_type(lagtab.astype(BF16).astype(F32), jnp.uint32).reshape(G, C * C, 2 * L)
    lagtab = lax.shift_right_logical(bits[..., :L], jnp.uint32(16)) | (bits[..., L:] & jnp.uint32(0xFFFF0000))
    fb = lambda f, b, axis: jnp.concatenate([f, b], axis=axis)
    pw_s = jnp.stack([fb(e_r[0, :, L - 1::-1], e_r[1, :, :L], -1),
                      fb(e_i[0, :, L - 1::-1], e_i[1, :, :L], -1)], axis=1)
    bb = jnp.stack([fb(bbt_r[0], bbt_r[1], -1), fb(bbt_i[0], bbt_i[1], -1)], axis=1)
    pw_t = jnp.stack([fb(et_r[0, :, :, 1:L + 1], et_r[1, :, :, L:0:-1], 1),
                      fb(et_i[0, :, :, 1:L + 1], et_i[1, :, :, L:0:-1], 1)], axis=1)
    crt = cr.transpose(0, 1, 3, 2)
    cit = ci.transpose(0, 1, 3, 2)
    cc = jnp.stack([fb(crt[0], crt[1], 1), fb(cit[0], cit[1], 1)], axis=1)
    n_steps = _s5_scan_steps(n_chunks)
    kpow =(L * (2 ** jnp.arange(n_steps, dtype=F32)))[None, None, :, None]
    p_r, p_i = powers(kpow, ar[:, :, None, :], ai[:, :, None, :])
    pad = jnp.zeros((G, (-n_steps) % 8, 2 * P), F32)
    p_r = jnp.concatenate([jnp.concatenate([p_r[0], p_r[1]], axis=-1), pad], axis=1)
    p_i = jnp.concatenate([jnp.concatenate([p_i[0], p_i[1]], axis=-1), pad], axis=1)
    ptab = jnp.concatenate([p_r, p_i], axis=1)
    return lagtab, pw_s, bb, pw_t, cc, ptab


def _s5_body(n_chunks, n_steps, u_ref, lag_ref, pws_ref, bb_ref, pwt_ref, cc_ref, p_ref, y_ref,
             toep_s, w_s, v_s):
    L, C = S5_CHUNK, S5_GROUP
    n = u_ref.shape[1]
    two_p = 2 * S5_STATE
    e_r, e_i = pws_ref[0, 0], pws_ref[0, 1]
    et_r, et_i = pwt_ref[0, 0], pwt_ref[0, 1]
    c_r, c_i = cc_ref[0, 0], cc_ref[0, 1]
    for c in range(C):
        b_r = bb_ref[0, 0, c:c + 1, :]
        b_i = bb_ref[0, 1, c:c + 1, :]
        w_s[c * L:(c + 1) * L, :two_p] = (e_r * b_r - e_i * b_i).astype(BF16)
        w_s[c * L:(c + 1) * L, two_p:] = (e_r * b_i + e_i * b_r).astype(BF16)
        k_r = c_r[:, c:c + 1]
        k_i = c_i[:, c:c + 1]
        v_s[:two_p, c * L:(c + 1) * L] = (k_r * et_r - k_i * et_i).astype(BF16)
        v_s[two_p:, c * L:(c + 1) * L] = (-(k_r * et_i + k_i * et_r)).astype(BF16)
    s_i = lax.broadcasted_iota(jnp.int32, (L, L), 0)
    t_i = lax.broadcasted_iota(jnp.int32, (L, L), 1)
    non_positive_lag = t_i <= s_i

    def build_block_row(cp, carry):
        r0 = pl.multiple_of(cp * L, L)
        for c in range(C):
            k = lag_ref[0, pl.ds(cp * C + c, 1), :]
            r = pltpu.roll(jnp.broadcast_to(k, (L, L)), 1, 1, stride=1, stride_axis=0)
            bits = jnp.where(non_positive_lag, lax.shift_left(r, jnp.uint32(16)), r & jnp.uint32(0xFFFF0000))
            toep_s[pl.ds(r0, L), c * L:(c + 1) * L] = pltpu.bitcast(bits, F32).astype(BF16)
        return carry

    lax.fori_loop(0, C, build_block_row, 0)
    u = jnp.concatenate([u_ref[c] for c in range(C)], axis=1).astype(BF16)
    y = jnp.dot(u, toep_s[...], preferred_element_type=F32)
    s = jnp.dot(u, w_s[...], preferred_element_type=F32)
    xr = s[:, :two_p]
    xi = s[:, two_p:]
    row = lax.broadcasted_iota(jnp.int32, (n, two_p), 0) % n_chunks
    is_fwd = lax.broadcasted_iota(jnp.int32, (n, two_p), 1) < S5_STATE
    pim0 = p_ref.shape[1] // 2

    def shifted(a, k):
        down = jnp.where(row >= k, pltpu.roll(a, k, 0), 0.0)
        up = jnp.where(row < n_chunks - k, pltpu.roll(a, n - k, 0), 0.0)
        return jnp.where(is_fwd, down, up)

    for j in range(n_steps):
        k = 1 << j
        pr = p_ref[0, j:j + 1, :]
        pi = p_ref[0, pim0 + j:pim0 + j + 1, :]
        sr = shifted(xr, k)
        si = shifted(xi, k)
        xr, xi = xr + pr * sr - pi * si, xi + pr * si + pi * sr
    carry = jnp.concatenate([shifted(xr, 1), shifted(xi, 1)], axis=1).astype(BF16)
    y = y + jnp.dot(carry, v_s[...], preferred_element_type=F32)
    for c in range(C):
        y_ref[c] = y[:, c * L:(c + 1) * L]


def _s5_core(zbt, batch, seq, tables):
    L = S5_CHUNK
    n_chunks = seq // L
    n_steps = _s5_scan_steps(n_chunks)
    n = batch * n_chunks
    per_g = lambda a: pl.BlockSpec((1,) + a.shape[1:], lambda g: (g,) + (0,) * (a.ndim - 1))
    grp = pl.BlockSpec((S5_GROUP, n, L), lambda g: (g, 0, 0))
    return pl.pallas_call(
        functools.partial(_s5_body, n_chunks, n_steps),
        grid=(S5_NGROUPS,),
        in_specs=[grp] + [per_g(a) for a in tables],
        out_specs=grp,
        out_shape=jax.ShapeDtypeStruct((D_S5, n, L), F32),
        scratch_shapes=[pltpu.VMEM((S5_GROUP * L, S5_GROUP * L), BF16),
                        pltpu.VMEM((S5_GROUP * L, 4 * S5_STATE), BF16),
                        pltpu.VMEM((4 * S5_STATE, S5_GROUP * L), BF16)],
        compiler_params=_cparams(("parallel",)), name="s5_core")(zbt, *tables)


def _lru_body(reverse, prev_ref, cur_ref, next_ref, cw_ref, cb_ref, wg_ref, bg_ref, sp_ref, *rest):
    if reverse:
        hf_ref, gate_ref, gain_ref, o_ref, a_s, b_s, h_s, carry_s = rest
    else:
        o_ref, a_s, b_s, h_s, carry_s = rest
    nb, tl, _ = cur_ref.shape
    step = pl.program_id(0)
    nt = pl.num_programs(0)
    ti = nt - 1 - step if reverse else step

    @pl.when(step == 0)
    def _():
        carry_s[...] = jnp.zeros_like(carry_s)

    cw = cw_ref[...]
    k_row = sp_ref[...] * (-0.5 * RG_C * math.log2(math.e))
    sub = lax.broadcasted_iota(jnp.int32, (HALO, D_LRU), 0)

    def shifted(x, d, edge):
        y = pltpu.roll(x, d % tl, 0)
        if d > 0:
            return jnp.concatenate([jnp.where(sub < d, pltpu.roll(edge, d, 0), y[:HALO]), y[HALO:]], axis=0)
        return jnp.concatenate([y[:-HALO], jnp.where(sub >= HALO + d, pltpu.roll(edge, HALO + d, 0), y[-HALO:])],
                               axis=0)

    for b in range(nb):
        x = cur_ref[b]
        before = jnp.where(ti > 0, prev_ref[b], 0.0)
        after = jnp.where(ti < nt - 1, next_ref[b], 0.0)
        xc = (cb_ref[...] + cw[0:1, :] * shifted(x, 1, before) + cw[1:2, :] * x
              + cw[2:3, :] * shifted(x, -1, after) + cw[3:4, :] * shifted(x, -2, after))
        gates = jnp.dot(xc.astype(BF16), wg_ref[...], preferred_element_type=F32) + bg_ref[...]
        a = jnp.exp2(k_row * (jnp.tanh(0.5 * gates[:, :D_LRU]) + 1.0))
        ig = 0.5 * jnp.tanh(0.5 * gates[:, D_LRU:]) + 0.5
        a_s[b] = a
        om = 1.0 - a * a
        b_s[b] = (om * lax.rsqrt(jnp.maximum(om, 1e-30))) * (ig * xc)

    def scan_step(s, hs):
        t = tl - 1 - s if reverse else s
        out = []
        for b in range(nb):
            h = a_s[b, pl.ds(t, 1), :] * hs[b] + b_s[b, pl.ds(t, 1), :]
            h_s[b, pl.ds(t, 1), :] = h
            out.append(h)
        return tuple(out)

    hs = lax.fori_loop(0, tl, scan_step, tuple(carry_s[b:b + 1, :] for b in range(nb)), unroll=8)
    for b in range(nb):
        carry_s[b:b + 1, :] = hs[b]

    if reverse:
        for b in range(nb):
            y = (hf_ref[b] + h_s[b]) * _gelu(gate_ref[b])
            o_ref[b] = _rms(y, gain_ref[...])
    else:
        o_ref[...] = h_s[...]


def _lru_pass(reverse, zc3, params, extra):
    nb, seq, _ = zc3.shape
    tl = LRU_TILE
    nt = seq // tl
    hb = tl // HALO
    nhb = seq // HALO
    tix = (lambda i: nt - 1 - i) if reverse else (lambda i: i)
    cur = pl.BlockSpec((nb, tl, D_LRU), lambda i: (0, tix(i), 0))
    prev = pl.BlockSpec((nb, HALO, D_LRU), lambda i: (0, jnp.maximum(tix(i) * hb - 1, 0), 0))
    nxt = pl.BlockSpec((nb, HALO, D_LRU), lambda i: (0, jnp.minimum((tix(i) + 1) * hb, nhb - 1), 0))
    full = lambda a: pl.BlockSpec(a.shape, lambda i: (0,) * a.ndim)
    ins = [zc3, zc3, zc3, *params]
    in_specs = [prev, cur, nxt] + [full(a) for a in params]
    if reverse:
        hf, gate, gain = extra
        ins += [hf, gate, gain]
        in_specs += [cur, cur, full(gain)]
    return pl.pallas_call(
        functools.partial(_lru_body, reverse),
        grid=(nt,), in_specs=in_specs, out_specs=cur,
        out_shape=jax.ShapeDtypeStruct((nb, seq, D_LRU), F32),
        scratch_shapes=[pltpu.VMEM((nb, tl, D_LRU), F32),
                        pltpu.VMEM((nb, tl, D_LRU), F32),
                        pltpu.VMEM((nb, tl, D_LRU), F32),
                        pltpu.VMEM((8, D_LRU), F32)],
        compiler_params=_cparams(("arbitrary",)),
        name="lru_bwd" if reverse else "lru_fwd")(*ins)


def _block_diag(w):
    h, d, _ = w.shape
    eye = jnp.eye(h, dtype=w.dtype)
    return (eye[:, None, :, None] * w[:, :, None, :]).reshape(h * d, h * d)


def _lru_params(w_a, b_a, w_x, b_x, lam):
    bd = jax.vmap(_block_diag)
    wg = jnp.concatenate([bd(w_a), bd(w_x)], axis=-1).astype(BF16)
    bg = jnp.concatenate([b_a, b_x], axis=-1)[:, None, :]
    sp = jax.nn.softplus(-lam.astype(F32))[:, None, :]
    return wg, bg, sp


def _lru_mixer(zc, zg, batch, seq, conv_w, conv_b, wg, bg, sp, gain):
    zc3 = zc.reshape(batch, seq, D_LRU)
    zg3 = zg.reshape(batch, seq, D_LRU)
    outs = None
    for d in (0, 1):
        params = [conv_w, conv_b[None, :], wg[d], bg[d], sp[d]]
        extra = None if d == 0 else (outs, zg3, gain[None, :])
        outs = _lru_pass(d == 1, zc3, params, extra)
    return outs.reshape(batch * seq, D_LRU)


def _mixout_body(x_ref, oa_ref, y3_ref, u3_ref, oc_ref, sd_ref, wglu_ref, bglu_ref, gb_ref,
                 wout_ref, g2_ref, rw_ref, rb_ref, xo_ref, h_ref, ti_ref, gt_ref):
    tm = x_ref.shape[0]
    per_step = tm // S5_CHUNK
    c0 = pl.program_id(1) * per_step
    yt = jnp.concatenate([y3_ref[:, c0 + n, :] for n in range(per_step)], axis=1)
    ut = jnp.concatenate([u3_ref[:, c0 + n, :] for n in range(per_step)], axis=1)
    vt = _gelu(yt + sd_ref[...] * ut).astype(BF16)
    gl = lax.dot_general(vt, wglu_ref[...], (((0,), (0,)), ((), ())), preferred_element_type=F32) + bglu_ref[...]
    ob = _rms(gl[:, :D_S5] * _sigmoid(gl[:, D_S5:]), gb_ref[...])
    o = jnp.concatenate([oa_ref[...], ob, oc_ref[...]], axis=1).astype(BF16)
    x = x_ref[...] + jnp.dot(o, wout_ref[...], preferred_element_type=F32)
    xo_ref[...] = x
    h = _rms(x, g2_ref[...])
    h_hi = h.astype(BF16)
    bits = pltpu.bitcast(h_hi.astype(F32), jnp.uint32)
    h_ref[...] = lax.shift_right_logical(bits[:, :D_MODEL // 2], jnp.uint32(16)) | (
        bits[:, D_MODEL // 2:] & jnp.uint32(0xFFFF0000))
    h_lo = (h - h_hi.astype(F32)).astype(BF16)
    nt_dims = (((1,), (1,)), ((), ()))
    la = lax.dot_general(rw_ref[...], h_hi, nt_dims, preferred_element_type=F32)
    lb = lax.dot_general(rw_ref[0:N_EXPERTS, :], h_lo, nt_dims, preferred_element_type=F32)
    logits = la[:N_EXPERTS] + la[N_EXPERTS:] + lb + rb_ref[...]
    eidx = lax.broadcasted_iota(jnp.int32, (N_EXPERTS, tm), 0).astype(F32)
    idx_rows, val_rows = [], []
    for k in range(TOP_K):
        m = jnp.max(logits, axis=0, keepdims=True)
        idx = jnp.min(jnp.where(logits == m, eidx, float(N_EXPERTS)), axis=0, keepdims=True)
        idx_rows.append(idx)
        val_rows.append(jnp.exp(m - val_rows[0]) if k else m)
        logits = jnp.where(eidx == idx, -jnp.inf, logits)
    top = val_rows[0]
    vals = jnp.concatenate([jnp.ones_like(top)] + [v for v in val_rows[1:]], axis=0)
    ti_ref[...] = jnp.concatenate(idx_rows, axis=0).astype(jnp.int32)
    gt_ref[...] = vals / jnp.sum(vals, axis=0, keepdims=True)


def _router_split(rw):
    rw_hi = rw.astype(BF16)
    rw_lo = (rw - rw_hi.astype(F32)).astype(BF16)
    return jnp.concatenate([jnp.swapaxes(rw_hi, -1, -2), jnp.swapaxes(rw_lo, -1, -2)], axis=-2)


def _mixout(x, oa, yt3, zbt3, oc, s5_d, wglu, bglu, gain_b, wout, g2, rw2, rb):
    t = x.shape[0]
    tm = ROW_TILE
    nc = tm * ROW_SPLIT // S5_CHUNK
    row = lambda n: pl.BlockSpec((tm, n), lambda i, j: (i * ROW_SPLIT + j, 0))
    col = pl.BlockSpec((TOP_K, tm), lambda i, j: (0, i * ROW_SPLIT + j))
    chunks = pl.BlockSpec((D_S5, nc, S5_CHUNK), lambda i, j: (0, i, 0))
    full = lambda a: pl.BlockSpec(a.shape, lambda i, j: (0,) * a.ndim)
    params = [s5_d.reshape(D_S5, 1), wglu, bglu, gain_b, wout, g2, rw2, rb.reshape(N_EXPERTS, 1)]
    return pl.pallas_call(
        _mixout_body,
        grid=(t // (tm * ROW_SPLIT), ROW_SPLIT),
        in_specs=[row(D_MODEL), row(D_POOL), chunks, chunks, row(D_LRU)] + [full(a) for a in params],
        out_specs=[row(D_MODEL), row(D_MODEL // 2), col, col],
        out_shape=[jax.ShapeDtypeStruct((t, D_MODEL), F32), jax.ShapeDtypeStruct((t, D_MODEL // 2), jnp.uint32),
                   jax.ShapeDtypeStruct((TOP_K, t), jnp.int32), jax.ShapeDtypeStruct((TOP_K, t), F32)],
        compiler_params=_cparams(("parallel", "arbitrary")), name="mix_out")(x, oa, yt3, zbt3, oc, *params)


def _expert_body(be_ref, nu_ref, first_ref, xl_ref, xh_ref, wgu_ref, bg_ref, bu_ref, wd_ref, bd_ref, p_ref, xs_ref,
                 y_ref, wg_s, wu_s, wd_s, ws_s):
    i = pl.program_id(0)
    live = i < nu_ref[0]
    x_scale = xs_ref[...]
    act_scale = FP8_TARGET / ((SWIGLU_LIMIT + 1.0) * SWIGLU_LIMIT)

    @pl.when(jnp.logical_and(live, first_ref[i] == 1))
    def _():
        p = p_ref[...]
        half = MXU_TILE // 2
        n_groups = wgu_ref.shape[3] // MXU_TILE
        amax = functools.reduce(jnp.maximum, [
            jnp.max(jnp.abs(wgu_ref[0, 0, :, MXU_TILE * j:MXU_TILE * (j + 1)]), axis=(0, 1), keepdims=True)
            for j in range(n_groups)])
        s_gu = FP8_TARGET / jnp.maximum(amax, FP8_TINY)
        for j in range(n_groups):
            w = wgu_ref[0, 0, :, MXU_TILE * j:MXU_TILE * (j + 1)].astype(BF16)
            r = jnp.dot(w, p, preferred_element_type=F32) * s_gu
            wg_s[:, half * j:half * (j + 1)] = r[:, :half].astype(FP8)
            wu_s[:, half * j:half * (j + 1)] = r[:, half:].astype(FP8)
        wd = wd_ref[0, 0]
        dmax = jnp.max(jnp.abs(wd), axis=(0, 1), keepdims=True)
        s_d = FP8_TARGET / jnp.maximum(dmax, FP8_TINY)
        wd_s[...] = (wd * s_d).astype(FP8)
        ws_s[0:1, :] = jnp.broadcast_to(1.0 / (s_gu * x_scale), (1, ws_s.shape[1]))
        ws_s[1:2, :] = jnp.broadcast_to(1.0 / (s_d * act_scale), (1, ws_s.shape[1]))

    @pl.when(live)
    def _():
        words = jnp.concatenate([xl_ref[...], xh_ref[...]], axis=1)
        lo = pltpu.bitcast(lax.shift_left(words, jnp.uint32(16)), F32)
        hi = pltpu.bitcast(words & jnp.uint32(0xFFFF0000), F32)
        x = (jnp.concatenate([lo, hi], axis=1) * x_scale).astype(FP8)
        gu_inv = ws_s[0:1, 0:1]
        g = jnp.dot(x, wg_s[...], preferred_element_type=F32) * gu_inv + bg_ref[0, 0]
        g = jnp.minimum(g, SWIGLU_LIMIT)
        u1 = (jnp.dot(x, wu_s[...], preferred_element_type=F32) * (gu_inv * act_scale)
              + (bu_ref[0, 0] + 1.0) * act_scale)
        u1 = jnp.clip(u1, (1.0 - SWIGLU_LIMIT) * act_scale, (1.0 + SWIGLU_LIMIT) * act_scale)
        glu = g / (1.0 + jnp.exp2(g * (-SWIGLU_ALPHA * math.log2(math.e))))
        act = (u1 * glu).astype(FP8)
        y = jnp.dot(act, wd_s[...], preferred_element_type=F32) * ws_s[1:2, 0:1] + bd_ref[0, 0]
        y_ref[...] = y.astype(y_ref.dtype)

    @pl.when(jnp.logical_not(live))
    def _():
        y_ref[...] = jnp.zeros_like(y_ref)


def _expert_mlp(layer, xs, x_bound, blk_e, n_used, w_gate_up, b_gate, b_up, w_down, b_down):
    n_rows, dh = xs[0].shape
    bm = MOE_ROWS
    x_scale = (FP8_TARGET / jnp.maximum(x_bound, FP8_TINY)).astype(F32).reshape(1, 1)
    first = jnp.concatenate([jnp.ones((1,), jnp.int32), (blk_e[1:] != blk_e[:-1]).astype(jnp.int32)])
    idx = jnp.arange(MXU_TILE)
    perm = jax.nn.one_hot(jnp.where(idx % 2 == 0, idx // 2, MXU_TILE // 2 + idx // 2), MXU_TILE, dtype=BF16)
    rows = pl.BlockSpec((bm, dh), lambda i, be, nu, fi: (jnp.minimum(i, nu[0] - 1), 0))
    per_e = lambda a: pl.BlockSpec((1, 1) + a.shape[2:], lambda i, be, nu, fi: (layer, be[i], 0, 0))
    grid_spec = pltpu.PrefetchScalarGridSpec(
        num_scalar_prefetch=3, grid=(n_rows // bm,),
        in_specs=[rows, rows, per_e(w_gate_up), per_e(b_gate), per_e(b_up), per_e(w_down), per_e(b_down),
                  pl.BlockSpec(perm.shape, lambda i, be, nu, fi: (0, 0)),
                  pl.BlockSpec((1, 1), lambda i, be, nu, fi: (0, 0))],
        out_specs=pl.BlockSpec((bm, D_MODEL), lambda i, be, nu, fi: (i, 0)),
        scratch_shapes=[pltpu.VMEM((D_MODEL, D_FF), FP8), pltpu.VMEM((D_MODEL, D_FF), FP8),
                        pltpu.VMEM((D_FF, D_MODEL), FP8), pltpu.VMEM((8, 128), F32)])
    return pl.pallas_call(
        _expert_body, grid_spec=grid_spec,
        out_shape=jax.ShapeDtypeStruct((n_rows, D_MODEL), BF16),
        compiler_params=pltpu.CompilerParams(dimension_semantics=("arbitrary",),
                                             vmem_limit_bytes=EXPERT_VMEM_LIMIT),
        name="expert_mlp")(blk_e, n_used, first, *xs, w_gate_up, b_gate, b_up, w_down, b_down, perm, x_scale)


def _sc_dispatch(h, dest, n_rows):
    t, d = h.shape
    win = SC_WINDOW
    dh = d // SC_COL_SPLIT
    idx = [dest[:, k].reshape(1, t) for k in range(TOP_K)]
    mesh = plsc.VectorSubcoreMesh(core_axis_name="core", subcore_axis_name="subcore")

    def scatter_cols(j):
        @functools.partial(pl.kernel, out_type=jax.ShapeDtypeStruct((n_rows, dh), h.dtype), mesh=mesh,
                           scratch_types=[], name="sc_dispatch")
        def scatter_rows(x_hbm, i0_hbm, i1_hbm, i2_hbm, i3_hbm, o_hbm):
            def body(x_vmem, *idx_vmem):
                for iv in idx_vmem:
                    pltpu.sync_copy(x_vmem, o_hbm.at[iv.at[0]])

            pltpu.emit_pipeline(
                body, grid=(t // win,),
                in_specs=[pl.BlockSpec((win, dh), lambda i: (i, j))]
                         + [pl.BlockSpec((1, win), lambda i: (0, i))] * TOP_K,
                out_specs=[],
                core_axis_name=("core", "subcore"),
                dimension_semantics=(pltpu.PARALLEL,),
            )(x_hbm, i0_hbm, i1_hbm, i2_hbm, i3_hbm)

        return scatter_rows(h, *idx)

    return [scatter_cols(j) for j in range(SC_COL_SPLIT)]


def _moe_dispatch(top_i, t):
    bm = MOE_ROWS
    n_blocks = (t * TOP_K) // bm + N_EXPERTS
    onehot = jax.nn.one_hot(top_i, N_EXPERTS, dtype=jnp.int32)
    sel = jnp.sum(onehot, axis=1)
    csum = jnp.cumsum(sel, axis=0)
    counts = csum[-1]
    padded = ((counts + bm - 1) // bm) * bm
    pad_end = jnp.cumsum(padded)
    pad_start = pad_end - padded
    base = (csum - sel) + pad_start[None, :]
    dest = jnp.sum(onehot * base[:, None, :], axis=-1)
    blk_start = jnp.arange(n_blocks, dtype=jnp.int32) * bm
    blk_e = jnp.minimum(jnp.sum((pad_end[None, :] <= blk_start[:, None]).astype(jnp.int32), axis=1),
                        N_EXPERTS - 1)
    n_used = (pad_end[-1] // bm).astype(jnp.int32).reshape(1)
    return dest, n_blocks * bm, blk_e, n_used


def _final_body(x_ref, y0, y1, y2, y3, gt_ref, g_ref, o_ref):
    o_ref[...] = _rms(_combine(x_ref, (y0, y1, y2, y3), gt_ref), g_ref[...])


def _final(x, add, g):
    t = x.shape[0]
    tm = ROW_TILE
    ys, gates = add
    row = lambda n: pl.BlockSpec((tm, n), lambda i: (i, 0))
    return pl.pallas_call(
        _final_body, grid=(t // tm,),
        in_specs=[row(D_MODEL)] * (1 + TOP_K) + [row(TOP_K), pl.BlockSpec(g.shape, lambda i: (0, 0))],
        out_specs=row(D_MODEL), out_shape=jax.ShapeDtypeStruct((t, D_MODEL), F32),
        compiler_params=_cparams(("parallel",)), name="final_norm")(x, *ys, gates, g)


def kernel(x, norm1_g, w_in, pool_w, pool_scale, s5_lam_re, s5_lam_im, s5_log_step, s5_b_re, s5_b_im, s5_c_re, s5_c_im, s5_d, s5_w_glu, s5_b_glu, lru_conv_w, lru_conv_b, lru_w_a, lru_b_a, lru_w_x, lru_b_x, lru_lam, mix_gain, w_out, norm2_g, router_w, router_b, w_gate_up, b_gate_up, w_down, b_down, final_g):
    batch, seq, d = x.shape
    t = batch * seq
    depth = norm1_g.shape[0]
    xt = x.reshape(t, d)
    add = None
    o1, o2 = D_POOL, D_POOL + D_S5
    b_gate = b_gate_up[:, :, None, 0::2]
    b_up = b_gate_up[:, :, None, 1::2]
    b_dn = b_down[:, :, None, :]
    for l in range(depth):
        w_rest = jnp.concatenate([w_in[l][:, :o1], w_in[l][:, o2:]], axis=1).astype(BF16)
        w_s5t = w_in[l][:, o1:o2].T.astype(BF16)
        s5_tabs = _s5_tables(s5_lam_re[l], s5_lam_im[l], s5_log_step[l], s5_b_re[l], s5_b_im[l],
                             s5_c_re[l], s5_c_im[l], seq // S5_CHUNK)
        lru_wg, lru_bg, lru_sp = _lru_params(lru_w_a[l], lru_b_a[l], lru_w_x[l], lru_b_x[l], lru_lam[l])
        if add is None:
            za, zbt, zc, zg = _inproj(xt, None, norm1_g[l][None, :], w_rest, w_s5t)
        else:
            xt, za, zbt, zc, zg = _inproj(xt, add, norm1_g[l][None, :], w_rest, w_s5t)
        oa = _pool_mixer(za, batch, seq, _block_diag(pool_w[l]).astype(BF16), pool_scale[l][None, :],
                         mix_gain[l][None, :o1])
        yt = _s5_core(zbt, batch, seq, s5_tabs)
        oc = _lru_mixer(zc, zg, batch, seq, lru_conv_w[l], lru_conv_b[l], lru_wg, lru_bg, lru_sp,
                        mix_gain[l][o2:])
        xt, h2, top_i, gates = _mixout(
            xt, oa, yt, zbt, oc, s5_d[l], s5_w_glu[l].astype(BF16), s5_b_glu[l][None, :],
            mix_gain[l][None, o1:o2], w_out[l].astype(BF16), norm2_g[l][None, :], _router_split(router_w[l]),
            router_b[l])
        top_i, gates = top_i.T, gates.T
        dest, n_rows, blk_e, n_used = _moe_dispatch(top_i, t)
        xs = _sc_dispatch(h2, dest, n_rows)
        x_bound = math.sqrt(D_MODEL) * jnp.max(jnp.abs(norm2_g[l]))
        ys = _expert_mlp(l, xs, x_bound, blk_e, n_used, w_gate_up, b_gate, b_up, w_down, b_dn)
        add = ([ys.at[dest[:, k]].get(mode="promise_in_bounds") for k in range(TOP_K)], gates)
    out = _final(xt, add, final_g[None, :])
    return out.reshape(batch, seq, d)
```

```python
import functools
import math

import jax
import jax.numpy as jnp
from jax import lax
from jax.experimental import pallas as pl
from jax.experimental.pallas import tpu as pltpu
from jax.experimental.pallas import tpu_sc as plsc

F32 = jnp.float32
BF16 = jnp.bfloat16
FP8 = jnp.float8_e4m3fn
FP8_TARGET = 416.0
FP8_TINY = 1e-30

D_MODEL = 1024
D_POOL = 256
D_S5 = 384
D_LRU = 384
POOL_WINDOWS = (2, 4, 8, 16)
POOL_GROUP = 64
S5_GROUP = 16
S5_NGROUPS = 24
S5_STATE = 64
LRU_HEADS = 6
LRU_HEAD_DIM = 64
RG_C = 8.0
N_EXPERTS = 32
TOP_K = 4
D_FF = 1024
SWIGLU_LIMIT = 7.0
SWIGLU_ALPHA = 1.702
EPS = 1e-5

ROW_TILE = 512
ROW_SPLIT = 2
S5_CHUNK = 128
POOL_TILE = 512
LRU_TILE = 256
HALO = 8
MOE_ROWS = 1024
SC_WINDOW = 128
SC_COL_SPLIT = 2
MXU_TILE = 256
EXPERT_VMEM_LIMIT = 56 * 1024 * 1024
VMEM_LIMIT = 48 * 1024 * 1024


def _cparams(sem):
    return pltpu.CompilerParams(dimension_semantics=sem, vmem_limit_bytes=VMEM_LIMIT)


def _rms(x, g):
    return x * lax.rsqrt(jnp.mean(x * x, axis=-1, keepdims=True) + EPS) * g


def _gelu(x):
    return 0.5 * x * (1.0 + jnp.tanh(0.7978845608028654 * (x + 0.044715 * (x * x * x))))


def _sigmoid(x):
    return 1.0 / (1.0 + jnp.exp(-x))


def _combine(x_ref, y_refs, gt_ref):
    x = x_ref[...]
    gt = gt_ref[...]
    for k in range(TOP_K):
        x = x + gt[:, k:k + 1] * y_refs[k][...].astype(F32)
    return x


def _inproj_body(has_add, *refs):
    if has_add:
        x_ref, y0, y1, y2, y3, gt_ref, g_ref, w_ref, wbt_ref, xo_ref, za, zbt, zc, zg = refs
        x = _combine(x_ref, (y0, y1, y2, y3), gt_ref)
        xo_ref[...] = x
    else:
        x_ref, g_ref, w_ref, wbt_ref, za, zbt, zc, zg = refs
        x = x_ref[...]
    h = _rms(x, g_ref[...]).astype(BF16)
    z = jnp.dot(h, w_ref[...], preferred_element_type=F32)
    za[...] = z[:, :D_POOL]
    zc[...] = z[:, D_POOL:D_POOL + D_LRU]
    zg[...] = z[:, D_POOL + D_LRU:]
    zt = lax.dot_general(wbt_ref[...], h, (((1,), (1,)), ((), ())), preferred_element_type=F32)
    per_step = zt.shape[1] // S5_CHUNK
    for n in range(per_step):
        zbt[:, pl.program_id(1) * per_step + n, :] = zt[:, n * S5_CHUNK:(n + 1) * S5_CHUNK]


def _inproj(x, add, g, w_rest, w_s5t):
    t = x.shape[0]
    tm = ROW_TILE
    nc = tm * ROW_SPLIT // S5_CHUNK
    row = lambda n: pl.BlockSpec((tm, n), lambda i, j: (i * ROW_SPLIT + j, 0))
    full = lambda a: pl.BlockSpec(a.shape, lambda i, j: (0,) * a.ndim)
    z_shapes = [jax.ShapeDtypeStruct((t, D_POOL), F32), jax.ShapeDtypeStruct((D_S5, t // S5_CHUNK, S5_CHUNK), F32),
                jax.ShapeDtypeStruct((t, D_LRU), F32), jax.ShapeDtypeStruct((t, D_LRU), F32)]
    z_specs = [row(D_POOL), pl.BlockSpec((D_S5, nc, S5_CHUNK), lambda i, j: (0, i, 0)), row(D_LRU), row(D_LRU)]
    n_outer = t // (tm * ROW_SPLIT)
    if add is None:
        return pl.pallas_call(
            functools.partial(_inproj_body, False),
            grid=(n_outer, ROW_SPLIT), in_specs=[row(D_MODEL), full(g), full(w_rest), full(w_s5t)],
            out_specs=z_specs, out_shape=z_shapes,
            compiler_params=_cparams(("parallel", "arbitrary")), name="inproj")(x, g, w_rest, w_s5t)
    ys, gates = add
    return pl.pallas_call(
        functools.partial(_inproj_body, True),
        grid=(n_outer, ROW_SPLIT),
        in_specs=[row(D_MODEL)] * (1 + TOP_K) + [row(TOP_K), full(g), full(w_rest), full(w_s5t)],
        out_specs=[row(D_MODEL)] + z_specs,
        out_shape=[jax.ShapeDtypeStruct((t, D_MODEL), F32)] + z_shapes,
        compiler_params=_cparams(("parallel", "arbitrary")), name="inproj")(x, *ys, gates, g, w_rest, w_s5t)


def _pool_body(seq, prev_ref, cur_ref, next_ref, w_ref, sc_ref, gain_ref, o_ref, u_s, s2_s, s4_s, s8_s):
    tl = cur_ref.shape[0]
    i = pl.program_id(1)
    nt = pl.num_programs(1)
    zero8 = jnp.zeros((HALO, D_POOL), F32)
    for buf in (u_s, s2_s, s4_s, s8_s):
        buf[0:HALO, :] = zero8
        buf[tl + 3 * HALO:tl + 4 * HALO, :] = zero8
    u_s[HALO:2 * HALO, :] = jnp.where(i > 0, prev_ref[...], 0.0)
    u_s[2 * HALO:2 * HALO + tl, :] = cur_ref[...]
    u_s[2 * HALO + tl:3 * HALO + tl, :] = jnp.where(i < nt - 1, next_ref[...], 0.0)
    r = tl + 2 * HALO
    s2_s[HALO:HALO + r, :] = u_s[HALO - 1:HALO - 1 + r, :] + u_s[HALO:HALO + r, :]
    s4_s[HALO:HALO + r, :] = s2_s[HALO - 1:HALO - 1 + r, :] + s2_s[HALO + 1:HALO + 1 + r, :]
    s8_s[HALO:HALO + r, :] = s4_s[HALO - 2:HALO - 2 + r, :] + s4_s[HALO + 2:HALO + 2 + r, :]
    o = 2 * HALO
    s16 = s8_s[o - 4:o - 4 + tl, :] + s8_s[o + 4:o + 4 + tl, :]
    s8 = s8_s[o:o + tl, :]
    s4 = s4_s[o:o + tl, :]
    s2 = s2_s[o:o + tl, :]
    u = u_s[o:o + tl, :]
    lane = lax.broadcasted_iota(jnp.int32, (tl, D_POOL), 1)
    tpos = lax.broadcasted_iota(jnp.int32, (tl, D_POOL), 0) + i * tl
    g0, g1, g2 = lane < POOL_GROUP, lane < 2 * POOL_GROUP, lane < 3 * POOL_GROUP
    half = jnp.where(g0, 1, jnp.where(g1, 2, jnp.where(g2, 4, 8)))
    wsum = jnp.where(g0, s2, jnp.where(g1, s4, jnp.where(g2, s8, s16)))
    cnt = (jnp.minimum(tpos + half, seq) - jnp.maximum(tpos - half, 0)).astype(F32)
    d = wsum / cnt - u
    y = jnp.dot(d.astype(BF16), w_ref[...], preferred_element_type=F32) * sc_ref[...]
    o_ref[...] = _rms(y, gain_ref[...])


def _pool_mixer(za, batch, seq, w_bd, scale, gain):
    t = za.shape[0]
    tl = POOL_TILE
    nt = seq // tl
    hb = tl // HALO
    nhb = t // HALO
    cur = pl.BlockSpec((tl, D_POOL), lambda b, i: (b * nt + i, 0))
    prev = pl.BlockSpec((HALO, D_POOL), lambda b, i: (jnp.maximum((b * nt + i) * hb - 1, 0), 0))
    nxt = pl.BlockSpec((HALO, D_POOL), lambda b, i: (jnp.minimum((b * nt + i + 1) * hb, nhb - 1), 0))
    full = lambda a: pl.BlockSpec(a.shape, lambda b, i: (0,) * a.ndim)
    return pl.pallas_call(
        functools.partial(_pool_body, seq),
        grid=(batch, nt),
        in_specs=[prev, cur, nxt, full(w_bd), full(scale), full(gain)],
        out_specs=cur,
        out_shape=jax.ShapeDtypeStruct((t, D_POOL), F32),
        scratch_shapes=[pltpu.VMEM((tl + 4 * HALO, D_POOL), F32)] * 4,
        compiler_params=_cparams(("parallel", "parallel")), name="pool_mixer")(za, za, za, w_bd, scale, gain)


def _s5_scan_steps(n_chunks):
    return max(1, int(math.ceil(math.log2(n_chunks))))


def _s5_tables(lam_re, lam_im, log_step, b_re, b_im, c_re, c_im, n_chunks):
    L, C, P, G = S5_CHUNK, S5_GROUP, S5_STATE, S5_NGROUPS
    hp = lax.Precision.HIGHEST
    lr = lam_re.astype(F32)
    li = lam_im.astype(F32)
    dt = jnp.exp(log_step.astype(F32))[..., None]
    ar = lr * dt
    ai = li * dt
    jj = jnp.arange(L + 1, dtype=F32)

    def powers(j, a_r, a_i):
        mag = jnp.exp(j * a_r)
        return mag * jnp.cos(j * a_i), mag * jnp.sin(j * a_i)

    e_r, e_i = powers(jj[None, None, :, None], ar[:, :, None, :], ai[:, :, None, :])
    et_r, et_i = powers(jj[None, None, None, :], ar[..., None], ai[..., None])
    den = lr * lr + li * li
    nr = e_r[:, :, 1, :] - 1.0
    ni = e_i[:, :, 1, :]
    q_r = (nr * lr + ni * li) / den
    q_i = (ni * lr - nr * li) / den
    br = b_re.astype(F32)
    bi = b_im.astype(F32)
    bbt_r = (q_r[..., None] * br - q_i[..., None] * bi).transpose(0, 1, 3, 2)
    bbt_i = (q_r[..., None] * bi + q_i[..., None] * br).transpose(0, 1, 3, 2)
    cr = c_re.astype(F32)
    ci = c_im.astype(F32)
    m_r = cr[:, :, None] * bbt_r[:, :, :, None] - ci[:, :, None] * bbt_i[:, :, :, None]
    m_i = cr[:, :, None] * bbt_i[:, :, :, None] + ci[:, :, None] * bbt_r[:, :, :, None]
    kk = (jnp.einsum('dgxcp,dgpj->dgxcj', m_r, et_r[..., :L], precision=hp)
          - jnp.einsum('dgxcp,dgpj->dgxcj', m_i, et_i[..., :L], precision=hp))
    kf, kb = kk[0], kk[1]
    lagtab = jnp.concatenate([kb[..., :0:-1], kf[..., :1] + kb[..., :1], kf[..., 1:],
                              jnp.zeros_like(kf[..., :1])], axis=-1)
    bits = lax.bitcast_convert_type(lagtab.astype(BF16).astype(F32), jnp.uint32).reshape(G, C * C, 2 * L)
    lagtab = lax.shift_right_logical(bits[..., :L], jnp.uint32(16)) | (bits[..., L:] & jnp.uint32(0xFFFF0000))
    fb = lambda f, b, axis: jnp.concatenate([f, b], axis=axis)
    pw_s = jnp.stack([fb(e_r[0, :, L - 1::-1], e_r[1, :, :L], -1),
                      fb(e_i[0, :, L - 1::-1], e_i[1, :, :L], -1)], axis=1)
    bb = jnp.stack([fb(bbt_r[0], bbt_r[1], -1), fb(bbt_i[0], bbt_i[1], -1)], axis=1)
    pw_t = jnp.stack([fb(et_r[0, :, :, 1:L + 1], et_r[1, :, :, L:0:-1], 1),
                      fb(et_i[0, :, :, 1:L + 1], et_i[1, :, :, L:0:-1], 1)], axis=1)
    crt = cr.transpose(0, 1, 3, 2)
    cit = ci.transpose(0, 1, 3, 2)
    cc = jnp.stack([fb(crt[0], crt[1], 1), fb(cit[0], cit[1], 1)], axis=1)
    n_steps = _s5_scan_steps(n_chunks)
    kpow =(L * (2 ** jnp.arange(n_steps, dtype=F32)))[None, None, :, None]
    p_r, p_i = powers(kpow, ar[:, :, None, :], ai[:, :, None, :])
    pad = jnp.zeros((G, (-n_steps) % 8, 2 * P), F32)
    p_r = jnp.concatenate([jnp.concatenate([p_r[0], p_r[1]], axis=-1), pad], axis=1)
    p_i = jnp.concatenate([jnp.concatenate([p_i[0], p_i[1]], axis=-1), pad], axis=1)
    ptab = jnp.concatenate([p_r, p_i], axis=1)
    return lagtab, pw_s, bb, pw_t, cc, ptab


def _s5_body(n_chunks, n_steps, u_ref, lag_ref, pws_ref, bb_ref, pwt_ref, cc_ref, p_ref, y_ref,
             toep_s, w_s, v_s):
    L, C = S5_CHUNK, S5_GROUP
    n = u_ref.shape[1]
    two_p = 2 * S5_STATE
    e_r, e_i = pws_ref[0, 0], pws_ref[0, 1]
    et_r, et_i = pwt_ref[0, 0], pwt_ref[0, 1]
    c_r, c_i = cc_ref[0, 0], cc_ref[0, 1]
    for c in range(C):
        b_r = bb_ref[0, 0, c:c + 1, :]
        b_i = bb_ref[0, 1, c:c + 1, :]
        w_s[c * L:(c + 1) * L, :two_p] = (e_r * b_r - e_i * b_i).astype(BF16)
        w_s[c * L:(c + 1) * L, two_p:] = (e_r * b_i + e_i * b_r).astype(BF16)
        k_r = c_r[:, c:c + 1]
        k_i = c_i[:, c:c + 1]
        v_s[:two_p, c * L:(c + 1) * L] = (k_r * et_r - k_i * et_i).astype(BF16)
        v_s[two_p:, c * L:(c + 1) * L] = (-(k_r * et_i + k_i * et_r)).astype(BF16)
    s_i = lax.broadcasted_iota(jnp.int32, (L, L), 0)
    t_i = lax.broadcasted_iota(jnp.int32, (L, L), 1)
    non_positive_lag = t_i <= s_i

    def build_block_row(cp, carry):
        r0 = pl.multiple_of(cp * L, L)
        for c in range(C):
            k = lag_ref[0, pl.ds(cp * C + c, 1), :]
            r = pltpu.roll(jnp.broadcast_to(k, (L, L)), 1, 1, stride=1, stride_axis=0)
            bits = jnp.where(non_positive_lag, lax.shift_left(r, jnp.uint32(16)), r & jnp.uint32(0xFFFF0000))
            toep_s[pl.ds(r0, L), c * L:(c + 1) * L] = pltpu.bitcast(bits, F32).astype(BF16)
        return carry

    lax.fori_loop(0, C, build_block_row, 0)
    u = jnp.concatenate([u_ref[c] for c in range(C)], axis=1).astype(BF16)
    y = jnp.dot(u, toep_s[...], preferred_element_type=F32)
    s = jnp.dot(u, w_s[...], preferred_element_type=F32)
    xr = s[:, :two_p]
    xi = s[:, two_p:]
    row = lax.broadcasted_iota(jnp.int32, (n, two_p), 0) % n_chunks
    is_fwd = lax.broadcasted_iota(jnp.int32, (n, two_p), 1) < S5_STATE
    pim0 = p_ref.shape[1] // 2

    def shifted(a, k):
        down = jnp.where(row >= k, pltpu.roll(a, k, 0), 0.0)
        up = jnp.where(row < n_chunks - k, pltpu.roll(a, n - k, 0), 0.0)
        return jnp.where(is_fwd, down, up)

    for j in range(n_steps):
        k = 1 << j
        pr = p_ref[0, j:j + 1, :]
        pi = p_ref[0, pim0 + j:pim0 + j + 1, :]
        sr = shifted(xr, k)
        si = shifted(xi, k)
        xr, xi = xr + pr * sr - pi * si, xi + pr * si + pi * sr
    carry = jnp.concatenate([shifted(xr, 1), shifted(xi, 1)], axis=1).astype(BF16)
    y = y + jnp.dot(carry, v_s[...], preferred_element_type=F32)
    for c in range(C):
        y_ref[c] = y[:, c * L:(c + 1) * L]


def _s5_core(zbt, batch, seq, tables):
    L = S5_CHUNK
    n_chunks = seq // L
    n_steps = _s5_scan_steps(n_chunks)
    n = batch * n_chunks
    per_g = lambda a: pl.BlockSpec((1,) + a.shape[1:], lambda g: (g,) + (0,) * (a.ndim - 1))
    grp = pl.BlockSpec((S5_GROUP, n, L), lambda g: (g, 0, 0))
    return pl.pallas_call(
        functools.partial(_s5_body, n_chunks, n_steps),
        grid=(S5_NGROUPS,),
        in_specs=[grp] + [per_g(a) for a in tables],
        out_specs=grp,
        out_shape=jax.ShapeDtypeStruct((D_S5, n, L), F32),
        scratch_shapes=[pltpu.VMEM((S5_GROUP * L, S5_GROUP * L), BF16),
                        pltpu.VMEM((S5_GROUP * L, 4 * S5_STATE), BF16),
                        pltpu.VMEM((4 * S5_STATE, S5_GROUP * L), BF16)],
        compiler_params=_cparams(("parallel",)), name="s5_core")(zbt, *tables)


def _lru_body(reverse, prev_ref, cur_ref, next_ref, cw_ref, cb_ref, wg_ref, bg_ref, sp_ref, *rest):
    if reverse:
        hf_ref, gate_ref, gain_ref, o_ref, a_s, b_s, h_s, carry_s = rest
    else:
        o_ref, a_s, b_s, h_s, carry_s = rest
    nb, tl, _ = cur_ref.shape
    step = pl.program_id(0)
    nt = pl.num_programs(0)
    ti = nt - 1 - step if reverse else step

    @pl.when(step == 0)
    def _():
        carry_s[...] = jnp.zeros_like(carry_s)

    cw = cw_ref[...]
    k_row = sp_ref[...] * (-0.5 * RG_C * math.log2(math.e))
    sub = lax.broadcasted_iota(jnp.int32, (HALO, D_LRU), 0)

    def shifted(x, d, edge):
        y = pltpu.roll(x, d % tl, 0)
        if d > 0:
            return jnp.concatenate([jnp.where(sub < d, pltpu.roll(edge, d, 0), y[:HALO]), y[HALO:]], axis=0)
        return jnp.concatenate([y[:-HALO], jnp.where(sub >= HALO + d, pltpu.roll(edge, HALO + d, 0), y[-HALO:])],
                               axis=0)

    for b in range(nb):
        x = cur_ref[b]
        before = jnp.where(ti > 0, prev_ref[b], 0.0)
        after = jnp.where(ti < nt - 1, next_ref[b], 0.0)
        xc = (cb_ref[...] + cw[0:1, :] * shifted(x, 1, before) + cw[1:2, :] * x
              + cw[2:3, :] * shifted(x, -1, after) + cw[3:4, :] * shifted(x, -2, after))
        gates = jnp.dot(xc.astype(BF16), wg_ref[...], preferred_element_type=F32) + bg_ref[...]
        a = jnp.exp2(k_row * (jnp.tanh(0.5 * gates[:, :D_LRU]) + 1.0))
        ig = 0.5 * jnp.tanh(0.5 * gates[:, D_LRU:]) + 0.5
        a_s[b] = a
        om = 1.0 - a * a
        b_s[b] = (om * lax.rsqrt(jnp.maximum(om, 1e-30))) * (ig * xc)

    def scan_step(s, hs):
        t = tl - 1 - s if reverse else s
        out = []
        for b in range(nb):
            h = a_s[b, pl.ds(t, 1), :] * hs[b] + b_s[b, pl.ds(t, 1), :]
            h_s[b, pl.ds(t, 1), :] = h
            out.append(h)
        return tuple(out)

    hs = lax.fori_loop(0, tl, scan_step, tuple(carry_s[b:b + 1, :] for b in range(nb)), unroll=8)
    for b in range(nb):
        carry_s[b:b + 1, :] = hs[b]

    if reverse:
        for b in range(nb):
            y = (hf_ref[b] + h_s[b]) * _gelu(gate_ref[b])
            o_ref[b] = _rms(y, gain_ref[...])
    else:
        o_ref[...] = h_s[...]


def _lru_pass(reverse, zc3, params, extra):
    nb, seq, _ = zc3.shape
    tl = LRU_TILE
    nt = seq // tl
    hb = tl // HALO
    nhb = seq // HALO
    tix = (lambda i: nt - 1 - i) if reverse else (lambda i: i)
    cur = pl.BlockSpec((nb, tl, D_LRU), lambda i: (0, tix(i), 0))
    prev = pl.BlockSpec((nb, HALO, D_LRU), lambda i: (0, jnp.maximum(tix(i) * hb - 1, 0), 0))
    nxt = pl.BlockSpec((nb, HALO, D_LRU), lambda i: (0, jnp.minimum((tix(i) + 1) * hb, nhb - 1), 0))
    full = lambda a: pl.BlockSpec(a.shape, lambda i: (0,) * a.ndim)
    ins = [zc3, zc3, zc3, *params]
    in_specs = [prev, cur, nxt] + [full(a) for a in params]
    if reverse:
        hf, gate, gain = extra
        ins += [hf, gate, gain]
        in_specs += [cur, cur, full(gain)]
    return pl.pallas_call(
        functools.partial(_lru_body, reverse),
        grid=(nt,), in_specs=in_specs, out_specs=cur,
        out_shape=jax.ShapeDtypeStruct((nb, seq, D_LRU), F32),
        scratch_shapes=[pltpu.VMEM((nb, tl, D_LRU), F32),
                        pltpu.VMEM((nb, tl, D_LRU), F32),
                        pltpu.VMEM((nb, tl, D_LRU), F32),
                        pltpu.VMEM((8, D_LRU), F32)],
        compiler_params=_cparams(("arbitrary",)),
        name="lru_bwd" if reverse else "lru_fwd")(*ins)


def _block_diag(w):
    h, d, _ = w.shape
    eye = jnp.eye(h, dtype=w.dtype)
    return (eye[:, None, :, None] * w[:, :, None, :]).reshape(h * d, h * d)


def _lru_params(w_a, b_a, w_x, b_x, lam):
    bd = jax.vmap(_block_diag)
    wg = jnp.concatenate([bd(w_a), bd(w_x)], axis=-1).astype(BF16)
    bg = jnp.concatenate([b_a, b_x], axis=-1)[:, None, :]
    sp = jax.nn.softplus(-lam.astype(F32))[:, None, :]
    return wg, bg, sp


def _lru_mixer(zc, zg, batch, seq, conv_w, conv_b, wg, bg, sp, gain):
    zc3 = zc.reshape(batch, seq, D_LRU)
    zg3 = zg.reshape(batch, seq, D_LRU)
    outs = None
    for d in (0, 1):
        params = [conv_w, conv_b[None, :], wg[d], bg[d], sp[d]]
        extra = None if d == 0 else (outs, zg3, gain[None, :])
        outs = _lru_pass(d == 1, zc3, params, extra)
    return outs.reshape(batch * seq, D_LRU)


def _mixout_body(x_ref, oa_ref, y3_ref, u3_ref, oc_ref, sd_ref, wglu_ref, bglu_ref, gb_ref,
                 wout_ref, g2_ref, rw_ref, rb_ref, xo_ref, h_ref, ti_ref, gt_ref):
    tm = x_ref.shape[0]
    per_step = tm // S5_CHUNK
    c0 = pl.program_id(1) * per_step
    yt = jnp.concatenate([y3_ref[:, c0 + n, :] for n in range(per_step)], axis=1)
    ut = jnp.concatenate([u3_ref[:, c0 + n, :] for n in range(per_step)], axis=1)
    vt = _gelu(yt + sd_ref[...] * ut).astype(BF16)
    gl = lax.dot_general(vt, wglu_ref[...], (((0,), (0,)), ((), ())), preferred_element_type=F32) + bglu_ref[...]
    ob = _rms(gl[:, :D_S5] * _sigmoid(gl[:, D_S5:]), gb_ref[...])
    o = jnp.concatenate([oa_ref[...], ob, oc_ref[...]], axis=1).astype(BF16)
    x = x_ref[...] + jnp.dot(o, wout_ref[...], preferred_element_type=F32)
    xo_ref[...] = x
    h = _rms(x, g2_ref[...])
    h_hi = h.astype(BF16)
    bits = pltpu.bitcast(h_hi.astype(F32), jnp.uint32)
    h_ref[...] = lax.shift_right_logical(bits[:, :D_MODEL // 2], jnp.uint32(16)) | (
        bits[:, D_MODEL // 2:] & jnp.uint32(0xFFFF0000))
    h_lo = (h - h_hi.astype(F32)).astype(BF16)
    nt_dims = (((1,), (1,)), ((), ()))
    la = lax.dot_general(rw_ref[...], h_hi, nt_dims, preferred_element_type=F32)
    lb = lax.dot_general(rw_ref[0:N_EXPERTS, :], h_lo, nt_dims, preferred_element_type=F32)
    logits = la[:N_EXPERTS] + la[N_EXPERTS:] + lb + rb_ref[...]
    eidx = lax.broadcasted_iota(jnp.int32, (N_EXPERTS, tm), 0).astype(F32)
    idx_rows, val_rows = [], []
    for k in range(TOP_K):
        m = jnp.max(logits, axis=0, keepdims=True)
        idx = jnp.min(jnp.where(logits == m, eidx, float(N_EXPERTS)), axis=0, keepdims=True)
        idx_rows.append(idx)
        val_rows.append(jnp.exp(m - val_rows[0]) if k else m)
        logits = jnp.where(eidx == idx, -jnp.inf, logits)
    top = val_rows[0]
    vals = jnp.concatenate([jnp.ones_like(top)] + [v for v in val_rows[1:]], axis=0)
    ti_ref[...] = jnp.concatenate(idx_rows, axis=0).astype(jnp.int32)
    gt_ref[...] = vals / jnp.sum(vals, axis=0, keepdims=True)


def _router_split(rw):
    rw_hi = rw.astype(BF16)
    rw_lo = (rw - rw_hi.astype(F32)).astype(BF16)
    return jnp.concatenate([jnp.swapaxes(rw_hi, -1, -2), jnp.swapaxes(rw_lo, -1, -2)], axis=-2)


def _mixout(x, oa, yt3, zbt3, oc, s5_d, wglu, bglu, gain_b, wout, g2, rw2, rb):
    t = x.shape[0]
    tm = ROW_TILE
    nc = tm * ROW_SPLIT // S5_CHUNK
    row = lambda n: pl.BlockSpec((tm, n), lambda i, j: (i * ROW_SPLIT + j, 0))
    col = pl.BlockSpec((TOP_K, tm), lambda i, j: (0, i * ROW_SPLIT + j))
    chunks = pl.BlockSpec((D_S5, nc, S5_CHUNK), lambda i, j: (0, i, 0))
    full = lambda a: pl.BlockSpec(a.shape, lambda i, j: (0,) * a.ndim)
    params = [s5_d.reshape(D_S5, 1), wglu, bglu, gain_b, wout, g2, rw2, rb.reshape(N_EXPERTS, 1)]
    return pl.pallas_call(
        _mixout_body,
        grid=(t // (tm * ROW_SPLIT), ROW_SPLIT),
        in_specs=[row(D_MODEL), row(D_POOL), chunks, chunks, row(D_LRU)] + [full(a) for a in params],
        out_specs=[row(D_MODEL), row(D_MODEL // 2), col, col],
        out_shape=[jax.ShapeDtypeStruct((t, D_MODEL), F32), jax.ShapeDtypeStruct((t, D_MODEL // 2), jnp.uint32),
                   jax.ShapeDtypeStruct((TOP_K, t), jnp.int32), jax.ShapeDtypeStruct((TOP_K, t), F32)],
        compiler_params=_cparams(("parallel", "arbitrary")), name="mix_out")(x, oa, yt3, zbt3, oc, *params)


def _expert_body(layer, be_ref, nu_ref, first_ref, seg_ref, nxt_ref, xl_ref, xh_ref, wgu_hbm, bg_ref, bu_ref, wd_hbm,
                 bd_ref, p_ref, xs_ref, y_ref, wg_s, wu_s, wd_s, ws_s, wgu_buf, wd_buf, sem):
    i = pl.program_id(0)
    live = i < nu_ref[0]

    def weight_copies(e, slot):
        return (pltpu.make_async_copy(wgu_hbm.at[layer, e], wgu_buf.at[slot], sem.at[slot, 0]),
                pltpu.make_async_copy(wd_hbm.at[layer, e], wd_buf.at[slot], sem.at[slot, 1]))
    x_scale = xs_ref[...]
    act_scale = FP8_TARGET / ((SWIGLU_LIMIT + 1.0) * SWIGLU_LIMIT)

    @pl.when(jnp.logical_and(live, first_ref[i] == 1))
    def _():
        seg = seg_ref[i]
        slot = seg % 2

        @pl.when(seg == 0)
        def _():
            for cp in weight_copies(be_ref[i], slot):
                cp.start()

        for cp in weight_copies(be_ref[i], slot):
            cp.wait()

        @pl.when(nxt_ref[i] >= 0)
        def _():
            for cp in weight_copies(nxt_ref[i], 1 - slot):
                cp.start()

        p = p_ref[...]
        half = MXU_TILE // 2
        n_groups = wgu_buf.shape[2] // MXU_TILE
        amax = functools.reduce(jnp.maximum, [
            jnp.max(jnp.abs(wgu_buf[slot, :, MXU_TILE * j:MXU_TILE * (j + 1)]), axis=(0, 1), keepdims=True)
            for j in range(n_groups)])
        s_gu = FP8_TARGET / jnp.maximum(amax, FP8_TINY)
        for j in range(n_groups):
            w = wgu_buf[slot, :, MXU_TILE * j:MXU_TILE * (j + 1)].astype(BF16)
            r = jnp.dot(w, p, preferred_element_type=F32) * s_gu
            wg_s[:, half * j:half * (j + 1)] = r[:, :half].astype(FP8)
            wu_s[:, half * j:half * (j + 1)] = r[:, half:].astype(FP8)
        wd = wd_buf[slot]
        dmax = jnp.max(jnp.abs(wd), axis=(0, 1), keepdims=True)
        s_d = FP8_TARGET / jnp.maximum(dmax, FP8_TINY)
        wd_s[...] = (wd * s_d).astype(FP8)
        ws_s[0:1, :] = jnp.broadcast_to(1.0 / (s_gu * x_scale), (1, ws_s.shape[1]))
        ws_s[1:2, :] = jnp.broadcast_to(1.0 / (s_d * act_scale), (1, ws_s.shape[1]))

    @pl.when(live)
    def _():
        words = jnp.concatenate([xl_ref[...], xh_ref[...]], axis=1)
        lo = pltpu.bitcast(lax.shift_left(words, jnp.uint32(16)), F32)
        hi = pltpu.bitcast(words & jnp.uint32(0xFFFF0000), F32)
        x = (jnp.concatenate([lo, hi], axis=1) * x_scale).astype(FP8)
        gu_inv = ws_s[0:1, 0:1]
        g = jnp.dot(x, wg_s[...], preferred_element_type=F32) * gu_inv + bg_ref[0, 0]
        g = jnp.minimum(g, SWIGLU_LIMIT)
        u1 = (jnp.dot(x, wu_s[...], preferred_element_type=F32) * (gu_inv * act_scale)
              + (bu_ref[0, 0] + 1.0) * act_scale)
        u1 = jnp.clip(u1, (1.0 - SWIGLU_LIMIT) * act_scale, (1.0 + SWIGLU_LIMIT) * act_scale)
        glu = g / (1.0 + jnp.exp2(g * (-SWIGLU_ALPHA * math.log2(math.e))))
        act = (u1 * glu).astype(FP8)
        y = jnp.dot(act, wd_s[...], preferred_element_type=F32) * ws_s[1:2, 0:1] + bd_ref[0, 0]
        y_ref[...] = y.astype(y_ref.dtype)

    @pl.when(jnp.logical_not(live))
    def _():
        y_ref[...] = jnp.zeros_like(y_ref)


def _expert_mlp(layer, xs, x_bound, blk_e, n_used, w_gate_up, b_gate, b_up, w_down, b_down):
    n_rows, dh = xs[0].shape
    bm = MOE_ROWS
    x_scale = (FP8_TARGET / jnp.maximum(x_bound, FP8_TINY)).astype(F32).reshape(1, 1)
    n_blocks = n_rows // bm
    pos = jnp.arange(n_blocks, dtype=jnp.int32)
    changed = jnp.concatenate([jnp.ones((1,), bool), blk_e[1:] != blk_e[:-1]])
    first = jnp.logical_and(changed, pos < n_used[0])
    seg = jnp.cumsum(first.astype(jnp.int32)) - 1
    starts = jnp.where(first, pos, n_blocks)
    later = jnp.concatenate([jnp.flip(lax.cummin(jnp.flip(starts)))[1:], jnp.full((1,), n_blocks, jnp.int32)])
    nxt = jnp.where(later < n_blocks, blk_e[jnp.minimum(later, n_blocks - 1)], -1).astype(jnp.int32)
    first = first.astype(jnp.int32)
    idx = jnp.arange(MXU_TILE)
    perm = jax.nn.one_hot(jnp.where(idx % 2 == 0, idx // 2, MXU_TILE // 2 + idx // 2), MXU_TILE, dtype=BF16)
    rows = pl.BlockSpec((bm, dh), lambda i, be, nu, fi, sg, nx: (jnp.minimum(i, nu[0] - 1), 0))
    per_e = lambda a: pl.BlockSpec((1, 1) + a.shape[2:], lambda i, be, nu, fi, sg, nx: (layer, be[i], 0, 0))
    hbm = pl.BlockSpec(memory_space=pl.ANY)
    grid_spec = pltpu.PrefetchScalarGridSpec(
        num_scalar_prefetch=5, grid=(n_blocks,),
        in_specs=[rows, rows, hbm, per_e(b_gate), per_e(b_up), hbm, per_e(b_down),
                  pl.BlockSpec(perm.shape, lambda i, be, nu, fi, sg, nx: (0, 0)),
                  pl.BlockSpec((1, 1), lambda i, be, nu, fi, sg, nx: (0, 0))],
        out_specs=pl.BlockSpec((bm, D_MODEL), lambda i, be, nu, fi, sg, nx: (i, 0)),
        scratch_shapes=[pltpu.VMEM((D_MODEL, D_FF), FP8), pltpu.VMEM((D_MODEL, D_FF), FP8),
                        pltpu.VMEM((D_FF, D_MODEL), FP8), pltpu.VMEM((8, 128), F32),
                        pltpu.VMEM((2,) + w_gate_up.shape[2:], F32), pltpu.VMEM((2,) + w_down.shape[2:], F32),
                        pltpu.SemaphoreType.DMA((2, 2))])
    return pl.pallas_call(
        functools.partial(_expert_body, layer), grid_spec=grid_spec,
        out_shape=jax.ShapeDtypeStruct((n_rows, D_MODEL), BF16),
        compiler_params=pltpu.CompilerParams(dimension_semantics=("arbitrary",),
                                             vmem_limit_bytes=EXPERT_VMEM_LIMIT),
        name="expert_mlp")(blk_e, n_used, first, seg, nxt, *xs, w_gate_up, b_gate, b_up, w_down, b_down, perm,
                           x_scale)


def _sc_dispatch(h, dest, n_rows):
    t, d = h.shape
    win = SC_WINDOW
    dh = d // SC_COL_SPLIT
    idx = [dest[:, k].reshape(1, t) for k in range(TOP_K)]
    mesh = plsc.VectorSubcoreMesh(core_axis_name="core", subcore_axis_name="subcore")

    def scatter_cols(j):
        @functools.partial(pl.kernel, out_type=jax.ShapeDtypeStruct((n_rows, dh), h.dtype), mesh=mesh,
                           scratch_types=[], name="sc_dispatch")
        def scatter_rows(x_hbm, i0_hbm, i1_hbm, i2_hbm, i3_hbm, o_hbm):
            def body(x_vmem, *idx_vmem):
                for iv in idx_vmem:
                    pltpu.sync_copy(x_vmem, o_hbm.at[iv.at[0]])

            pltpu.emit_pipeline(
                body, grid=(t // win,),
                in_specs=[pl.BlockSpec((win, dh), lambda i: (i, j))]
                         + [pl.BlockSpec((1, win), lambda i: (0, i))] * TOP_K,
                out_specs=[],
                core_axis_name=("core", "subcore"),
                dimension_semantics=(pltpu.PARALLEL,),
            )(x_hbm, i0_hbm, i1_hbm, i2_hbm, i3_hbm)

        return scatter_rows(h, *idx)

    return [scatter_cols(j) for j in range(SC_COL_SPLIT)]


def _moe_dispatch(top_i, t):
    bm = MOE_ROWS
    n_blocks = (t * TOP_K) // bm + N_EXPERTS
    onehot = jax.nn.one_hot(top_i, N_EXPERTS, dtype=jnp.int32)
    sel = jnp.sum(onehot, axis=1)
    csum = jnp.cumsum(sel, axis=0)
    counts = csum[-1]
    padded = ((counts + bm - 1) // bm) * bm
    pad_end = jnp.cumsum(padded)
    pad_start = pad_end - padded
    base = (csum - sel) + pad_start[None, :]
    dest = jnp.sum(onehot * base[:, None, :], axis=-1)
    blk_start = jnp.arange(n_blocks, dtype=jnp.int32) * bm
    blk_e = jnp.minimum(jnp.sum((pad_end[None, :] <= blk_start[:, None]).astype(jnp.int32), axis=1),
                        N_EXPERTS - 1)
    n_used = (pad_end[-1] // bm).astype(jnp.int32).reshape(1)
    return dest, n_blocks * bm, blk_e, n_used


def _final_body(x_ref, y0, y1, y2, y3, gt_ref, g_ref, o_ref):
    o_ref[...] = _rms(_combine(x_ref, (y0, y1, y2, y3), gt_ref), g_ref[...])


def _final(x, add, g):
    t = x.shape[0]
    tm = ROW_TILE
    ys, gates = add
    row = lambda n: pl.BlockSpec((tm, n), lambda i: (i, 0))
    return pl.pallas_call(
        _final_body, grid=(t // tm,),
        in_specs=[row(D_MODEL)] * (1 + TOP_K) + [row(TOP_K), pl.BlockSpec(g.shape, lambda i: (0, 0))],
        out_specs=row(D_MODEL), out_shape=jax.ShapeDtypeStruct((t, D_MODEL), F32),
        compiler_params=_cparams(("parallel",)), name="final_norm")(x, *ys, gates, g)


def kernel(x, norm1_g, w_in, pool_w, pool_scale, s5_lam_re, s5_lam_im, s5_log_step, s5_b_re, s5_b_im, s5_c_re, s5_c_im, s5_d, s5_w_glu, s5_b_glu, lru_conv_w, lru_conv_b, lru_w_a, lru_b_a, lru_w_x, lru_b_x, lru_lam, mix_gain, w_out, norm2_g, router_w, router_b, w_gate_up, b_gate_up, w_down, b_down, final_g):
    batch, seq, d = x.shape
    t = batch * seq
    depth = norm1_g.shape[0]
    xt = x.reshape(t, d)
    add = None
    o1, o2 = D_POOL, D_POOL + D_S5
    b_gate = b_gate_up[:, :, None, 0::2]
    b_up = b_gate_up[:, :, None, 1::2]
    b_dn = b_down[:, :, None, :]
    for l in range(depth):
        w_rest = jnp.concatenate([w_in[l][:, :o1], w_in[l][:, o2:]], axis=1).astype(BF16)
        w_s5t = w_in[l][:, o1:o2].T.astype(BF16)
        s5_tabs = _s5_tables(s5_lam_re[l], s5_lam_im[l], s5_log_step[l], s5_b_re[l], s5_b_im[l],
                             s5_c_re[l], s5_c_im[l], seq // S5_CHUNK)
        lru_wg, lru_bg, lru_sp = _lru_params(lru_w_a[l], lru_b_a[l], lru_w_x[l], lru_b_x[l], lru_lam[l])
        if add is None:
            za, zbt, zc, zg = _inproj(xt, None, norm1_g[l][None, :], w_rest, w_s5t)
        else:
            xt, za, zbt, zc, zg = _inproj(xt, add, norm1_g[l][None, :], w_rest, w_s5t)
        oa = _pool_mixer(za, batch, seq, _block_diag(pool_w[l]).astype(BF16), pool_scale[l][None, :],
                         mix_gain[l][None, :o1])
        yt = _s5_core(zbt, batch, seq, s5_tabs)
        oc = _lru_mixer(zc, zg, batch, seq, lru_conv_w[l], lru_conv_b[l], lru_wg, lru_bg, lru_sp,
                        mix_gain[l][o2:])
        xt, h2, top_i, gates = _mixout(
            xt, oa, yt, zbt, oc, s5_d[l], s5_w_glu[l].astype(BF16), s5_b_glu[l][None, :],
            mix_gain[l][None, o1:o2], w_out[l].astype(BF16), norm2_g[l][None, :], _router_split(router_w[l]),
            router_b[l])
        top_i, gates = top_i.T, gates.T
        dest, n_rows, blk_e, n_used = _moe_dispatch(top_i, t)
        xs = _sc_dispatch(h2, dest, n_rows)
        x_bound = math.sqrt(D_MODEL) * jnp.max(jnp.abs(norm2_g[l]))
        ys = _expert_mlp(l, xs, x_bound, blk_e, n_used, w_gate_up, b_gate, b_up, w_down, b_dn)
        add = ([ys.at[dest[:, k]].get(mode="promise_in_bounds") for k in range(TOP_K)], gates)
    out = _final(xt, add, final_g[None, :])
    return out.reshape(batch, seq, d)
```
